```python
import jax, jax.numpy as jnp
from jax import lax
import numpy as np

D_MODEL = 1024
BATCH = 2
SEQ = 16384
DEPTH = 1
DEC_BATCH = 16
DEC_SEQ = 32
PAST_LEN = 2048

CHUNK = 64
BAND_CHUNKS = 8
A_WINDOW = BAND_CHUNKS * CHUNK
N_HEADS_A = 8
HEAD_DIM_A = 64
WIDTH_A = N_HEADS_A * HEAD_DIM_A
REL_CLIP = 256
N_HEADS_B = 4
KEY_DIM_B = 64
VAL_DIM_B = 128
WIDTH_BK = N_HEADS_B * KEY_DIM_B
WIDTH_BV = N_HEADS_B * VAL_DIM_B
GATE_RANK = 16
GATE_TAU = 16.0
GLA_BLOCK = 16
N_MEM = 256
N_HEADS_MEM = 4
HEAD_DIM_MEM = D_MODEL // N_HEADS_MEM
D_FF = -(-(8 * D_MODEL) // (3 * 256)) * 256
MIX_WIDTH = WIDTH_A + WIDTH_BV
IN_SIZES = (WIDTH_A, WIDTH_A, WIDTH_A, WIDTH_BK, WIDTH_BK, WIDTH_BV, GATE_RANK, WIDTH_BV)
IN_OFFSETS = tuple(int(v) for v in np.cumsum(IN_SIZES)[:-1])
IN_WIDTH = sum(IN_SIZES)
EPS = 1e-6

kernel_name = "hybrid_chunkband_gla_stream_step"


def rmsnorm(x, g):
    xf = x.astype(jnp.float32)
    y = xf * lax.rsqrt(jnp.mean(xf * xf, axis=-1, keepdims=True) + EPS)
    return (y * g.astype(jnp.float32)).astype(x.dtype)


def mix_inputs(h, w_in, w_alpha2, b_alpha):
    n, t = h.shape[:2]
    qa, ka, va, qb, kb, vb, g_low, r = jnp.split(h @ w_in, IN_OFFSETS, axis=-1)
    heads_a = lambda z: z.reshape(n, t, N_HEADS_A, HEAD_DIM_A)
    z = (g_low @ w_alpha2 + b_alpha).astype(jnp.float32)
    log_a = (jax.nn.log_sigmoid(z) / GATE_TAU).reshape(n, t, N_HEADS_B, KEY_DIM_B)
    return (heads_a(qa), heads_a(ka), heads_a(va),
            qb.reshape(n, t, N_HEADS_B, KEY_DIM_B), kb.reshape(n, t, N_HEADS_B, KEY_DIM_B),
            vb.reshape(n, t, N_HEADS_B, VAL_DIM_B), log_a, r)


def rel_bias_block(table, n_q, n_k, offset):
    dist = offset + jnp.arange(n_q)[:, None] - jnp.arange(n_k)[None, :]
    idx = jnp.clip(dist, -REL_CLIP, REL_CLIP) + REL_CLIP
    return table[:, idx]


def band_attention(q, k, v, bias, valid):
    s = jnp.einsum('...qhd,...khd->...hqk', q, k).astype(jnp.float32) * (HEAD_DIM_A ** -0.5)
    s = jnp.where(valid, s + bias.astype(jnp.float32), -1e30)
    p = jax.nn.softmax(s, axis=-1).astype(v.dtype)
    return jnp.einsum('...hqk,...khd->...qhd', p, v)


def chunk_band_prompt(qa, ka, va, table):
    b, s = qa.shape[:2]
    nc = s // CHUNK
    shp = (b, nc, CHUNK, N_HEADS_A, HEAD_DIM_A)
    qc = qa.reshape(shp)
    pad = ((0, 0), (BAND_CHUNKS, 0), (0, 0), (0, 0), (0, 0))
    kp = jnp.pad(ka.reshape(shp), pad)
    vp = jnp.pad(va.reshape(shp), pad)
    n_band = (BAND_CHUNKS + 1) * CHUNK
    kband = jnp.stack([kp[:, o:o + nc] for o in range(BAND_CHUNKS + 1)], axis=2).reshape(b, nc, n_band, N_HEADS_A, HEAD_DIM_A)
    vband = jnp.stack([vp[:, o:o + nc] for o in range(BAND_CHUNKS + 1)], axis=2).reshape(b, nc, n_band, N_HEADS_A, HEAD_DIM_A)
    chunk_ok = (jnp.arange(nc)[:, None] + jnp.arange(BAND_CHUNKS + 1)[None, :] - BAND_CHUNKS) >= 0
    valid = jnp.repeat(chunk_ok, CHUNK, axis=1)[None, :, None, None, :]
    bias = rel_bias_block(table, CHUNK, n_band, A_WINDOW)
    o = band_attention(qc, kband, vband, bias, valid)
    return o.reshape(b, s, N_HEADS_A, HEAD_DIM_A)


def chunk_band_sample(qa, ka, va, cache_k, cache_v, table):
    t = qa.shape[1]
    l = cache_k.shape[1]
    k_all = jnp.concatenate([cache_k.astype(ka.dtype), ka], axis=1)
    v_all = jnp.concatenate([cache_v.astype(va.dtype), va], axis=1)
    bias = rel_bias_block(table, t, l + t, l)
    valid = jnp.ones((l + t,), dtype=bool)
    return band_attention(qa, k_all, v_all, bias, valid)


def gla_recurrence(q, k, v, log_a, s0, block):
    f32 = jnp.float32
    n, t, h, dk = q.shape
    dv = v.shape[-1]
    nb = t // block
    qb = (q.astype(f32) * (dk ** -0.5)).reshape(n, nb, block, h, dk)
    kb = k.astype(f32).reshape(n, nb, block, h, dk)
    vb = v.astype(f32).reshape(n, nb, block, h, dv)
    bcum = jnp.cumsum(log_a.astype(f32).reshape(n, nb, block, h, dk), axis=2)
    b_last = bcum[:, :, -1]
    causal = jnp.tril(jnp.ones((block, block), dtype=bool))
    diff = bcum[:, :, :, None] - bcum[:, :, None, :]
    w = jnp.exp(jnp.where(causal[:, :, None, None], diff, -jnp.inf))
    scores = jnp.einsum('nbihd,nbjhd,nbijhd->nbhij', qb, kb, w)
    o_intra = jnp.einsum('nbhij,nbjhv->nbihv', scores, vb)
    k_dec = kb * jnp.exp(b_last[:, :, None] - bcum)
    ds = jnp.einsum('nbjhd,nbjhv->nbhdv', k_dec, vb)
    decay = jnp.exp(b_last)

    def step(s, inp):
        dec, d = inp
        return dec[..., None] * s + d, s

    s_final, s_starts = lax.scan(step, s0.astype(f32), (jnp.moveaxis(decay, 1, 0), jnp.moveaxis(ds, 1, 0)))
    s_starts = jnp.moveaxis(s_starts, 0, 1)
    o_inter = jnp.einsum('nbihd,nbhdv->nbihv', qb * jnp.exp(bcum), s_starts)
    return (o_intra + o_inter).reshape(n, t, h, dv), s_final


def mix_output(o_a, o_b, r, g_gla, w_o):
    n, t = r.shape[:2]
    ob = o_b * lax.rsqrt(jnp.mean(o_b * o_b, axis=-1, keepdims=True) + EPS)
    y_b = ob.reshape(n, t, WIDTH_BV) * g_gla.astype(jnp.float32) * jax.nn.silu(r.astype(jnp.float32))
    y = jnp.concatenate([o_a.reshape(n, t, WIDTH_A), y_b.astype(o_a.dtype)], axis=-1)
    return y @ w_o


def mem_kv(mem, g_mem, w_mk, w_mv):
    n = mem.shape[0]
    m = rmsnorm(mem, g_mem)
    k = (m @ w_mk).reshape(n, N_MEM, N_HEADS_MEM, HEAD_DIM_MEM)
    v = (m @ w_mv).reshape(n, N_MEM, N_HEADS_MEM, HEAD_DIM_MEM)
    return k, v


def mem_attention(h, k, v, w_mq, w_mo):
    n, t = h.shape[:2]
    q = (h @ w_mq).reshape(n, t, N_HEADS_MEM, HEAD_DIM_MEM)
    s = jnp.einsum('nqhd,nkhd->nhqk', q, k.astype(q.dtype)).astype(jnp.float32) * (HEAD_DIM_MEM ** -0.5)
    p = jax.nn.softmax(s, axis=-1).astype(q.dtype)
    o = jnp.einsum('nhqk,nkhd->nqhd', p, v.astype(q.dtype)).reshape(n, t, D_MODEL)
    return o @ w_mo


def swiglu(h, w_gate, w_up, w_down):
    return (jax.nn.silu(h @ w_gate) * (h @ w_up)) @ w_down


def setup_inputs(seed: int = 0) -> dict:
    key = jax.random.key(seed)
    ks = iter(jax.random.split(key, 40))
    nrm = lambda shape, scale: jax.random.normal(next(ks), shape, jnp.float32) * scale
    gain = lambda shape: 1.0 + nrm(shape, 0.05)
    a_keep = min(A_WINDOW, PAST_LEN)
    L = DEPTH
    return {
        "x_prompt": nrm((BATCH, SEQ, D_MODEL), 1.0),
        "x_sample": nrm((DEC_BATCH, DEC_SEQ, D_MODEL), 1.0),
        "mem_prompt": nrm((BATCH, N_MEM, D_MODEL), 1.0),
        "cache_a_k": nrm((L, DEC_BATCH, a_keep, N_HEADS_A, HEAD_DIM_A), 1.0),
        "cache_a_v": nrm((L, DEC_BATCH, a_keep, N_HEADS_A, HEAD_DIM_A), 1.0),
        "state_gla": nrm((L, DEC_BATCH, N_HEADS_B, KEY_DIM_B, VAL_DIM_B), 1.0),
        "cache_mem_k": nrm((L, DEC_BATCH, N_MEM, N_HEADS_MEM, HEAD_DIM_MEM), 1.0),
        "cache_mem_v": nrm((L, DEC_BATCH, N_MEM, N_HEADS_MEM, HEAD_DIM_MEM), 1.0),
        "g_pre_mix": gain((L, D_MODEL)),
        "w_in": nrm((L, D_MODEL, IN_WIDTH), D_MODEL ** -0.5),
        "rel_bias": nrm((L, N_HEADS_A, 2 * REL_CLIP + 1), 0.2),
        "w_alpha2": nrm((L, GATE_RANK, WIDTH_BK), GATE_RANK ** -0.5),
        "b_alpha": nrm((L, WIDTH_BK), 0.1),
        "g_gla_out": gain((L, WIDTH_BV)),
        "w_o": nrm((L, MIX_WIDTH, D_MODEL), MIX_WIDTH ** -0.5),
        "g_post_mix": gain((L, D_MODEL)),
        "g_pre_mem": gain((L, D_MODEL)),
        "g_mem": gain((L, D_MODEL)),
        "w_mq": nrm((L, D_MODEL, D_MODEL), D_MODEL ** -0.5),
        "w_mk": nrm((L, D_MODEL, D_MODEL), D_MODEL ** -0.5),
        "w_mv": nrm((L, D_MODEL, D_MODEL), D_MODEL ** -0.5),
        "w_mo": nrm((L, D_MODEL, D_MODEL), D_MODEL ** -0.5),
        "g_post_mem": gain((L, D_MODEL)),
        "g_pre_ffn": gain((L, D_MODEL)),
        "w_ffn_gate": nrm((L, D_MODEL, D_FF), D_MODEL ** -0.5),
        "w_ffn_up": nrm((L, D_MODEL, D_FF), D_MODEL ** -0.5),
        "w_ffn_down": nrm((L, D_FF, D_MODEL), D_FF ** -0.5),
        "g_post_ffn": gain((L, D_MODEL)),
    }


def reference(x_prompt, x_sample, mem_prompt, cache_a_k, cache_a_v, state_gla, cache_mem_k, cache_mem_v,
              g_pre_mix, w_in, rel_bias, w_alpha2, b_alpha, g_gla_out, w_o, g_post_mix,
              g_pre_mem, g_mem, w_mq, w_mk, w_mv, w_mo, g_post_mem,
              g_pre_ffn, w_ffn_gate, w_ffn_up, w_ffn_down, g_post_ffn):
    xp, xs = x_prompt, x_sample
    pa_k, pa_v, p_gla, p_mk, p_mv, sa_k, sa_v, s_gla = [], [], [], [], [], [], [], []
    for l in range(DEPTH):
        hp = rmsnorm(xp, g_pre_mix[l])
        qa, ka, va, qb, kb, vb, la, r = mix_inputs(hp, w_in[l], w_alpha2[l], b_alpha[l])
        oa = chunk_band_prompt(qa, ka, va, rel_bias[l])
        s0 = jnp.zeros((xp.shape[0], N_HEADS_B, KEY_DIM_B, VAL_DIM_B), jnp.float32)
        ob, sp = gla_recurrence(qb, kb, vb, la, s0, GLA_BLOCK)
        xp = xp + rmsnorm(mix_output(oa, ob, r, g_gla_out[l], w_o[l]), g_post_mix[l])
        keep = min(A_WINDOW, ka.shape[1])
        pa_k.append(ka[:, -keep:])
        pa_v.append(va[:, -keep:])
        p_gla.append(sp)

        hs = rmsnorm(xs, g_pre_mix[l])
        qa_s, ka_s, va_s, qb_s, kb_s, vb_s, la_s, r_s = mix_inputs(hs, w_in[l], w_alpha2[l], b_alpha[l])
        oa_s = chunk_band_sample(qa_s, ka_s, va_s, cache_a_k[l], cache_a_v[l], rel_bias[l])
        ob_s, ss = gla_recurrence(qb_s, kb_s, vb_s, la_s, state_gla[l], xs.shape[1])
        xs = xs + rmsnorm(mix_output(oa_s, ob_s, r_s, g_gla_out[l], w_o[l]), g_post_mix[l])
        sa_k.append(ka_s)
        sa_v.append(va_s)
        s_gla.append(ss)

        mk, mv = mem_kv(mem_prompt, g_mem[l], w_mk[l], w_mv[l])
        xp = xp + rmsnorm(mem_attention(rmsnorm(xp, g_pre_mem[l]), mk, mv, w_mq[l], w_mo[l]), g_post_mem[l])
        xs = xs + rmsnorm(mem_attention(rmsnorm(xs, g_pre_mem[l]), cache_mem_k[l], cache_mem_v[l], w_mq[l], w_mo[l]), g_post_mem[l])
        p_mk.append(mk)
        p_mv.append(mv)

        xp = xp + rmsnorm(swiglu(rmsnorm(xp, g_pre_ffn[l]), w_ffn_gate[l], w_ffn_up[l], w_ffn_down[l]), g_post_ffn[l])
        xs = xs + rmsnorm(swiglu(rmsnorm(xs, g_pre_ffn[l]), w_ffn_gate[l], w_ffn_up[l], w_ffn_down[l]), g_post_ffn[l])

    return (xp, xs, jnp.stack(pa_k), jnp.stack(pa_v), jnp.stack(p_gla), jnp.stack(p_mk), jnp.stack(p_mv),
            jnp.stack(sa_k), jnp.stack(sa_v), jnp.stack(s_gla))
```

```python
import functools
import math

import jax
import jax.numpy as jnp
import numpy as np
from jax import lax
from jax.experimental import pallas as pl
from jax.experimental.pallas import tpu as pltpu

F32 = jnp.float32
BF16 = jnp.bfloat16

D_MODEL = 1024
CHUNK = 64
BAND_CHUNKS = 8
A_WINDOW = BAND_CHUNKS * CHUNK
N_HEADS_A = 8
HEAD_DIM_A = 64
WIDTH_A = 512
REL_CLIP = 256
N_HEADS_B = 4
KEY_DIM_B = 64
VAL_DIM_B = 128
WIDTH_BK = 256
WIDTH_BV = 512
GATE_RANK = 16
GATE_TAU = 16.0
N_MEM = 256
N_HEADS_MEM = 4
HEAD_DIM_MEM = 256
D_FF = 2816
EPS = 1e-6
NEG_BIG = -1e30

LANES = 128
BIAS_W = 640
FF_BLOCK = 256
VMEM_LIMIT = 56 * 1024 * 1024

OFF_QA, OFF_KA, OFF_VA, OFF_QB, OFF_KB, OFF_VB, OFF_R, OFF_G, W_CAT = 0, 512, 1024, 1536, 1792, 2048, 2560, 3072, 3200


def _rms(x, g):
    return x * lax.rsqrt(jnp.mean(x * x, axis=-1, keepdims=True) + EPS) * g


def _silu(x):
    return x / (1.0 + jnp.exp(-x))


def _log_sigmoid(z):
    return jnp.minimum(z, 0.0) - jnp.log(1.0 + jnp.exp(-jnp.abs(z)))


def _const_spec(shape):
    nd = len(shape)
    return pl.BlockSpec(shape, lambda *_: (0,) * nd, pipeline_mode=pl.Buffered(1))


def _params(n_axes, vmem=None):
    return pltpu.CompilerParams(dimension_semantics=("arbitrary",) * n_axes, vmem_limit_bytes=vmem)


def _proj_kernel(tiles_per_seq, x_ref, g_ref, w_ref, wa2_ref, ba_ref,
                 qa_ref, ka_ref, va_ref, qb_ref, kb_ref, vb_ref, r_ref, la_ref, kt_ref, vt_ref):
    h = _rms(x_ref[...], g_ref[...]).astype(BF16)

    def proj(lo, hi):
        return jnp.dot(h, w_ref[:, lo:hi], preferred_element_type=F32)

    qa_ref[...] = (proj(OFF_QA, OFF_KA) * (HEAD_DIM_A ** -0.5)).astype(BF16)
    ka = proj(OFF_KA, OFF_VA)
    va = proj(OFF_VA, OFF_QB)
    ka_ref[...] = ka.astype(BF16)
    va_ref[...] = va.astype(BF16)
    qb_ref[...] = proj(OFF_QB, OFF_KB) * (KEY_DIM_B ** -0.5)
    kb_ref[...] = proj(OFF_KB, OFF_VB)
    vb_ref[...] = proj(OFF_VB, OFF_R).astype(BF16)
    r_ref[...] = proj(OFF_R, OFF_G)
    g_low = proj(OFF_G, W_CAT).astype(BF16)
    z = jnp.dot(g_low, wa2_ref[...], preferred_element_type=F32) + ba_ref[...]
    la_ref[...] = _log_sigmoid(z) * (1.0 / GATE_TAU)

    @pl.when(pl.program_id(0) % tiles_per_seq == tiles_per_seq - 1)
    def _():
        kt_ref[...] = ka
        vt_ref[...] = va


def _proj(x, g, w_cat, wa2, ba, tm, tiles_per_seq):
    ntok = x.shape[0]
    nt = ntok // tm
    nseq = nt // tiles_per_seq
    row = lambda w: pl.BlockSpec((tm, w), lambda i: (i, 0))
    tail = pl.BlockSpec((tm, WIDTH_A), lambda i: (i // tiles_per_seq, 0))
    out_shape = (
        jax.ShapeDtypeStruct((ntok, WIDTH_A), BF16),
        jax.ShapeDtypeStruct((ntok, WIDTH_A), BF16),
        jax.ShapeDtypeStruct((ntok, WIDTH_A), BF16),
        jax.ShapeDtypeStruct((ntok, WIDTH_BK), F32),
        jax.ShapeDtypeStruct((ntok, WIDTH_BK), F32),
        jax.ShapeDtypeStruct((ntok, WIDTH_BV), BF16),
        jax.ShapeDtypeStruct((ntok, WIDTH_BV), F32),
        jax.ShapeDtypeStruct((ntok, WIDTH_BK), F32),
        jax.ShapeDtypeStruct((nseq * tm, WIDTH_A), F32),
        jax.ShapeDtypeStruct((nseq * tm, WIDTH_A), F32),
    )
    return pl.pallas_call(
        functools.partial(_proj_kernel, tiles_per_seq),
        grid=(nt,),
        in_specs=[row(D_MODEL), _const_spec((1, D_MODEL)), _const_spec((D_MODEL, W_CAT)),
                  _const_spec((LANES, WIDTH_BK)), _const_spec((1, WIDTH_BK))],
        out_specs=(row(WIDTH_A), row(WIDTH_A), row(WIDTH_A), row(WIDTH_BK), row(WIDTH_BK),
                   row(WIDTH_BV), row(WIDTH_BV), row(WIDTH_BK), tail, tail),
        out_shape=out_shape,
        compiler_params=_params(1, VMEM_LIMIT),
        name="proj",
    )(x, g, w_cat, wa2, ba)


def _build_bias(base_ref, bias_ref):
    for h in range(N_HEADS_A):
        rows = jnp.broadcast_to(base_ref[h:h + 1, :], (CHUNK, BIAS_W))
        bias_ref[h] = pltpu.roll(rows, 0, 1, stride=1, stride_axis=0)


def _attend(q_ref, q_row0, n_q, kcat_ref, vcat_ref, k_row0, n_keys, bias_ref, first_valid_col, o_ref):
    lane = lax.broadcasted_iota(jnp.int32, (n_q, LANES), 1)
    low_half = lane < HEAD_DIM_A
    if first_valid_col is not None:
        col = lax.broadcasted_iota(jnp.int32, (n_q, n_keys), 1)
        valid = col >= first_valid_col
    for hp in range(N_HEADS_A // 2):
        cols = slice(hp * LANES, (hp + 1) * LANES)
        qp = q_ref[q_row0:q_row0 + n_q, cols]
        kb = kcat_ref[k_row0:k_row0 + n_keys, cols]
        vb = vcat_ref[k_row0:k_row0 + n_keys, cols]
        outs = []
        for e in range(2):
            qm = jnp.where(low_half if e == 0 else ~low_half, qp, jnp.zeros_like(qp))
            s = lax.dot_general(qm, kb, (((1,), (1,)), ((), ())), preferred_element_type=F32)
            s = s + bias_ref[2 * hp + e, 0:n_q, 0:n_keys]
            if first_valid_col is not None:
                s = jnp.where(valid, s, NEG_BIG)
            m = jnp.max(s, axis=-1, keepdims=True)
            p = jnp.exp(s - m)
            l = jnp.sum(p, axis=-1, keepdims=True)
            outs.append(jnp.dot(p.astype(BF16), vb, preferred_element_type=F32) / l)
        o_ref[q_row0:q_row0 + n_q, cols] = jnp.where(low_half, outs[0], outs[1]).astype(o_ref.dtype)


def _attn_prompt_kernel(chunks_per_tile, q_ref, k2_ref, k1_ref, k0_ref, v2_ref, v1_ref, v0_ref, base_ref,
                        o_ref, kcat, vcat, bias):
    t = pl.program_id(1)
    tq = chunks_per_tile * CHUNK

    @pl.when((pl.program_id(0) == 0) & (t == 0))
    def _():
        _build_bias(base_ref, bias)

    kcat[0:tq] = k2_ref[...]
    kcat[tq:2 * tq] = k1_ref[...]
    kcat[2 * tq:3 * tq] = k0_ref[...]
    vcat[0:tq] = v2_ref[...]
    vcat[tq:2 * tq] = v1_ref[...]
    vcat[2 * tq:3 * tq] = v0_ref[...]
    n_band = (BAND_CHUNKS + 1) * CHUNK
    back = 2 * tq - A_WINDOW
    for qc in range(chunks_per_tile):
        c = t * chunks_per_tile + qc
        first_valid = (BAND_CHUNKS - c) * CHUNK
        _attend(q_ref, qc * CHUNK, CHUNK, kcat, vcat, back + qc * CHUNK, n_band, bias, first_valid, o_ref)


def _attn_prompt(q, k, v, base, n_batch, seq, chunks_per_tile=4):
    tq = chunks_per_tile * CHUNK
    assert 2 * tq >= A_WINDOW
    nt = seq // tq
    blk = lambda back: pl.BlockSpec((tq, WIDTH_A), lambda b, t: (b * nt + jnp.maximum(t - back, 0), 0))
    return pl.pallas_call(
        functools.partial(_attn_prompt_kernel, chunks_per_tile),
        grid=(n_batch, nt),
        in_specs=[blk(0), blk(2), blk(1), blk(0), blk(2), blk(1), blk(0), _const_spec((N_HEADS_A, BIAS_W))],
        out_specs=blk(0),
        out_shape=jax.ShapeDtypeStruct((n_batch * seq, WIDTH_A), BF16),
        scratch_shapes=[pltpu.VMEM((3 * tq, WIDTH_A), BF16), pltpu.VMEM((3 * tq, WIDTH_A), BF16),
                        pltpu.VMEM((N_HEADS_A, CHUNK, BIAS_W), F32)],
        compiler_params=_params(2, VMEM_LIMIT),
        name="band_attn_prompt",
    )(q, k, k, k, v, v, v, base)


def _attn_sample_kernel(n_new, q_ref, kn_ref, vn_ref, kc_ref, vc_ref, base_ref, o_ref, kcat, vcat, bias):
    @pl.when(pl.program_id(0) == 0)
    def _():
        _build_bias(base_ref, bias)

    n_cache = kc_ref.shape[0]
    kcat[0:n_cache] = kc_ref[...]
    kcat[n_cache:n_cache + n_new] = kn_ref[...]
    vcat[0:n_cache] = vc_ref[...]
    vcat[n_cache:n_cache + n_new] = vn_ref[...]
    _attend(q_ref, 0, n_new, kcat, vcat, 0, n_cache + n_new, bias, None, o_ref)


def _attn_sample(q, k_new, v_new, k_cache, v_cache, base, n_batch, n_new):
    n_cache = k_cache.shape[0] // n_batch
    assert n_cache == A_WINDOW and n_new <= CHUNK
    new = pl.BlockSpec((n_new, WIDTH_A), lambda b: (b, 0))
    old = pl.BlockSpec((n_cache, WIDTH_A), lambda b: (b, 0))
    return pl.pallas_call(
        functools.partial(_attn_sample_kernel, n_new),
        grid=(n_batch,),
        in_specs=[new, new, new, old, old, _const_spec((N_HEADS_A, BIAS_W))],
        out_specs=new,
        out_shape=jax.ShapeDtypeStruct((n_batch * n_new, WIDTH_A), BF16),
        scratch_shapes=[pltpu.VMEM((n_cache + n_new, WIDTH_A), BF16), pltpu.VMEM((n_cache + n_new, WIDTH_A), BF16),
                        pltpu.VMEM((N_HEADS_A, CHUNK, BIAS_W), F32)],
        compiler_params=_params(1, VMEM_LIMIT),
        name="band_attn_sample",
    )(q, k_new, v_new, k_cache, v_cache, base)


def _gla_levels(chunk):
    return [chunk >> i for i in range(int(math.log2(chunk)) + 1)]


def _segment_matrix(chunk):
    i = np.arange(chunk)[:, None]
    t = np.arange(chunk)[None, :]
    blocks = []
    for s in _gla_levels(chunk):
        start = (i // s) * s
        blocks.append((t >= start) & (t <= i))
        blocks.append((t > i) & (t <= start + s - 1))
    seg = np.concatenate(blocks, axis=0).astype(np.float32)
    return np.concatenate([seg, seg], axis=1)


def _gla_chunk(q, k, v, la, seg_ref, state_ref, chunk):
    levels = _gla_levels(chunk)
    la_hi = la.astype(BF16)
    la_lo = (la - la_hi.astype(F32)).astype(BF16)
    sums = jnp.dot(seg_ref[...], jnp.concatenate([la_hi, la_lo], axis=0), preferred_element_type=F32)

    def fwd(li):
        return jnp.minimum(sums[(2 * li) * chunk:(2 * li + 1) * chunk], 0.0)

    def rev(li):
        return jnp.minimum(sums[(2 * li + 1) * chunk:(2 * li + 2) * chunk], 0.0)

    ri = lax.broadcasted_iota(jnp.int32, (2 * chunk, chunk), 0) % chunk
    ci = lax.broadcasted_iota(jnp.int32, (2 * chunk, chunk), 1)
    lane = lax.broadcasted_iota(jnp.int32, (chunk, LANES), 1)
    low_half = lane < KEY_DIM_B

    def stack_heads(x_pair):
        zero = jnp.zeros_like(x_pair)
        return jnp.concatenate([jnp.where(low_half, x_pair, zero), jnp.where(low_half, zero, x_pair)], axis=0)

    nt_dims = (((1,), (1,)), ((), ()))
    outs = []
    for p in range(N_HEADS_B // 2):
        cols = slice(p * LANES, (p + 1) * LANES)
        qp, kp = q[:, cols], k[:, cols]
        a = jnp.where(ri == ci,
                      lax.dot_general(stack_heads(qp).astype(BF16), kp.astype(BF16), nt_dims,
                                      preferred_element_type=F32), 0.0)
        for li in range(1, len(levels)):
            s = levels[li]
            sh = int(math.log2(s))
            qd = (qp * jnp.exp(fwd(li)[:, cols])).astype(BF16)
            kd = (kp * jnp.exp(rev(li)[:, cols])).astype(BF16)
            term = lax.dot_general(stack_heads(qd), kd, nt_dims, preferred_element_type=F32)
            keep = (((ri >> sh) & 1) == 1) & ((ci >> sh) == (ri >> sh) - 1)
            a = a + jnp.where(keep, term, 0.0)
        a = a.astype(BF16)
        st = state_ref[p]
        q_in = (qp * jnp.exp(fwd(0)[:, cols])).astype(BF16)
        o_pair = lax.dot_general(q_in, st.astype(BF16), nt_dims, preferred_element_type=F32)
        k_out = (kp * jnp.exp(rev(0)[:, cols])).astype(BF16)
        vp = v[:, p * 2 * VAL_DIM_B:(p + 1) * 2 * VAL_DIM_B]
        upd = lax.dot_general(vp, k_out, (((0,), (0,)), ((), ())), preferred_element_type=F32)
        decay = jnp.exp(fwd(0)[chunk - 1:chunk, cols])
        own = (lax.broadcasted_iota(jnp.int32, (2 * VAL_DIM_B, LANES), 0) < VAL_DIM_B) == \
              (lax.broadcasted_iota(jnp.int32, (2 * VAL_DIM_B, LANES), 1) < KEY_DIM_B)
        state_ref[p] = st * decay + jnp.where(own, upd, 0.0)
        for e in range(2):
            ve = vp[:, e * VAL_DIM_B:(e + 1) * VAL_DIM_B]
            intra = jnp.dot(a[e * chunk:(e + 1) * chunk], ve, preferred_element_type=F32)
            outs.append(o_pair[:, e * VAL_DIM_B:(e + 1) * VAL_DIM_B] + intra)
    return jnp.concatenate(outs, axis=1)


def _gla_kernel(chunk, chunks_per_tile, has_init, *refs):
    if has_init:
        q_ref, k_ref, v_ref, la_ref, seg_ref, s0_ref, o_ref, sout_ref, state = refs
    else:
        q_ref, k_ref, v_ref, la_ref, seg_ref, o_ref, sout_ref, state = refs
    t = pl.program_id(1)

    @pl.when(t == 0)
    def _():
        state[...] = s0_ref[...] if has_init else jnp.zeros_like(state)

    for c in range(chunks_per_tile):
        rows = slice(c * chunk, (c + 1) * chunk)
        o_ref[rows, :] = _gla_chunk(q_ref[rows, :], k_ref[rows, :], v_ref[rows, :], la_ref[rows, :],
                                    seg_ref, state, chunk)

    @pl.when(t == pl.num_programs(1) - 1)
    def _():
        sout_ref[...] = state[...]


def _gla(q, k, v, la, s0, n_batch, seq, chunk, chunks_per_tile):
    tg = chunk * chunks_per_tile
    nt = seq // tg
    seg = jnp.asarray(_segment_matrix(chunk), BF16)
    row = lambda w: pl.BlockSpec((tg, w), lambda b, t: (b * nt + t, 0))
    st = pl.BlockSpec((None, 2, 2 * VAL_DIM_B, LANES), lambda b, t: (b, 0, 0, 0))
    in_specs = [row(WIDTH_BK), row(WIDTH_BK), row(WIDTH_BV), row(WIDTH_BK), _const_spec(seg.shape)]
    args = [q, k, v, la, seg]
    if s0 is not None:
        in_specs.append(st)
        args.append(s0)
    return pl.pallas_call(
        functools.partial(_gla_kernel, chunk, chunks_per_tile, s0 is not None),
        grid=(n_batch, nt),
        in_specs=in_specs,
        out_specs=(row(WIDTH_BV), st),
        out_shape=(jax.ShapeDtypeStruct((n_batch * seq, WIDTH_BV), F32),
                   jax.ShapeDtypeStruct((n_batch, 2, 2 * VAL_DIM_B, LANES), F32)),
        scratch_shapes=[pltpu.VMEM((2, 2 * VAL_DIM_B, LANES), F32)],
        compiler_params=_params(2, VMEM_LIMIT),
        name="gla_chunk%d" % chunk,
    )(*args)


def _state_to_pairs(s):
    n = s.shape[0]
    st = jnp.swapaxes(s, -1, -2).reshape(n, 2, 2, VAL_DIM_B, KEY_DIM_B)
    z = jnp.zeros_like(st[:, :, 0])
    top = jnp.concatenate([st[:, :, 0], z], axis=-1)
    bot = jnp.concatenate([z, st[:, :, 1]], axis=-1)
    return jnp.concatenate([top, bot], axis=-2)


def _pairs_to_state(sp):
    n = sp.shape[0]
    h0 = sp[:, :, :VAL_DIM_B, :KEY_DIM_B]
    h1 = sp[:, :, VAL_DIM_B:, KEY_DIM_B:]
    st = jnp.stack([h0, h1], axis=2).reshape(n, N_HEADS_B, VAL_DIM_B, KEY_DIM_B)
    return jnp.swapaxes(st, -1, -2)


def _mem_kv_kernel(m_ref, g_ref, wk_ref, wv_ref, k_ref, v_ref):
    m = _rms(m_ref[...], g_ref[...]).astype(BF16)
    k_ref[...] = jnp.dot(m, wk_ref[...], preferred_element_type=F32)
    v_ref[...] = jnp.dot(m, wv_ref[...], preferred_element_type=F32)


def _mem_kv(mem, g, wk, wv):
    n = mem.shape[0]
    tm = N_MEM
    row = pl.BlockSpec((tm, D_MODEL), lambda i: (i, 0))
    return pl.pallas_call(
        _mem_kv_kernel,
        grid=(n // tm,),
        in_specs=[row, _const_spec((1, D_MODEL)), _const_spec((D_MODEL, D_MODEL)), _const_spec((D_MODEL, D_MODEL))],
        out_specs=(row, row),
        out_shape=(jax.ShapeDtypeStruct((n, D_MODEL), F32), jax.ShapeDtypeStruct((n, D_MODEL), F32)),
        compiler_params=_params(1, VMEM_LIMIT),
        name="mem_kv",
    )(mem, g, wk, wv)


def _stack_kernel(n_seg, x_ref, oa_ref, ob_ref, r_ref, mk_ref, mv_ref,
                  g_gla_ref, wo_ref, g_post_mix_ref, g_pre_mem_ref, wq_ref, wmo_ref, g_post_mem_ref,
                  g_pre_ffn_ref, wg_ref, wu_ref, wd_ref, g_post_ffn_ref, y_ref):
    tm = x_ref.shape[0]
    ob = ob_ref[...]
    normed = []
    for h in range(N_HEADS_B):
        seg = ob[:, h * VAL_DIM_B:(h + 1) * VAL_DIM_B]
        normed.append(seg * lax.rsqrt(jnp.mean(seg * seg, axis=-1, keepdims=True) + EPS))
    yb = jnp.concatenate(normed, axis=1) * g_gla_ref[...] * _silu(r_ref[...])
    mix = (jnp.dot(oa_ref[...], wo_ref[0:WIDTH_A, :], preferred_element_type=F32)
           + jnp.dot(yb.astype(BF16), wo_ref[WIDTH_A:, :], preferred_element_type=F32))
    x1 = x_ref[...] + _rms(mix, g_post_mix_ref[...])

    hq = _rms(x1, g_pre_mem_ref[...]).astype(BF16)
    q = (jnp.dot(hq, wq_ref[...], preferred_element_type=F32) * (HEAD_DIM_MEM ** -0.5)).astype(BF16)
    rows_per_seg = tm // n_seg
    seg_outs = []
    for sg in range(n_seg):
        head_outs = []
        for h in range(N_HEADS_MEM):
            cols = slice(h * HEAD_DIM_MEM, (h + 1) * HEAD_DIM_MEM)
            qh = q[sg * rows_per_seg:(sg + 1) * rows_per_seg, cols]
            kh = mk_ref[sg * N_MEM:(sg + 1) * N_MEM, cols]
            vh = mv_ref[sg * N_MEM:(sg + 1) * N_MEM, cols]
            s = lax.dot_general(qh, kh, (((1,), (1,)), ((), ())), preferred_element_type=F32)
            p = jnp.exp(s - jnp.max(s, axis=-1, keepdims=True))
            l = jnp.sum(p, axis=-1, keepdims=True)
            head_outs.append(jnp.dot(p.astype(BF16), vh, preferred_element_type=F32) / l)
        seg_outs.append(jnp.concatenate(head_outs, axis=1))
    o = jnp.concatenate(seg_outs, axis=0) if n_seg > 1 else seg_outs[0]
    att = jnp.dot(o.astype(BF16), wmo_ref[...], preferred_element_type=F32)
    x2 = x1 + _rms(att, g_post_mem_ref[...])

    hf = _rms(x2, g_pre_ffn_ref[...]).astype(BF16)
    f = jnp.zeros((tm, D_MODEL), F32)
    for c in range(D_FF // FF_BLOCK):
        cols = slice(c * FF_BLOCK, (c + 1) * FF_BLOCK)
        gate = jnp.dot(hf, wg_ref[:, cols], preferred_element_type=F32)
        up = jnp.dot(hf, wu_ref[:, cols], preferred_element_type=F32)
        f = f + jnp.dot((_silu(gate) * up).astype(BF16), wd_ref[cols, :], preferred_element_type=F32)
    y_ref[...] = x2 + _rms(f, g_post_ffn_ref[...])


def _stack(x, oa, ob, r, mk, mv, weights, tm, n_seg, tiles_per_mem_block):
    ntok = x.shape[0]
    row = lambda w: pl.BlockSpec((tm, w), lambda i: (i, 0))
    mem = pl.BlockSpec((n_seg * N_MEM, D_MODEL), lambda i: (i // tiles_per_mem_block, 0))
    w_specs = [_const_spec(w.shape) for w in weights]
    return pl.pallas_call(
        functools.partial(_stack_kernel, n_seg),
        grid=(ntok // tm,),
        in_specs=[row(D_MODEL), row(WIDTH_A), row(WIDTH_BV), row(WIDTH_BV), mem, mem] + w_specs,
        out_specs=row(D_MODEL),
        out_shape=jax.ShapeDtypeStruct((ntok, D_MODEL), F32),
        compiler_params=_params(1, VMEM_LIMIT),
        name="token_stack",
    )(x, oa, ob, r, mk, mv, *weights)


def _bias_base(table):
    u = np.arange(BIAS_W)
    idx = np.where(u < (BAND_CHUNKS + 1) * CHUNK, np.clip(A_WINDOW - u, -REL_CLIP, REL_CLIP) + REL_CLIP, 2 * REL_CLIP)
    return table[:, idx]


def kernel(x_prompt, x_sample, mem_prompt, cache_a_k, cache_a_v, state_gla, cache_mem_k, cache_mem_v,
           g_pre_mix, w_in, rel_bias, w_alpha2, b_alpha, g_gla_out, w_o, g_post_mix,
           g_pre_mem, g_mem, w_mq, w_mk, w_mv, w_mo, g_post_mem,
           g_pre_ffn, w_ffn_gate, w_ffn_up, w_ffn_down, g_post_ffn):
    depth = w_in.shape[0]
    assert depth == 1
    l = 0
    n_p, seq, _ = x_prompt.shape
    n_s, seq_s, _ = x_sample.shape
    vec = lambda g: g[l].reshape(1, -1)

    wi = w_in[l]
    w_cat = jnp.concatenate(
        [wi[:, :2560], wi[:, 2576:3088], wi[:, 2560:2576], jnp.zeros((D_MODEL, W_CAT - 3088), F32)], axis=1).astype(BF16)
    wa2 = jnp.concatenate([w_alpha2[l], jnp.zeros((LANES - GATE_RANK, WIDTH_BK), F32)], axis=0).astype(BF16)
    base = _bias_base(rel_bias[l])
    stack_w = [vec(g_gla_out), w_o[l].astype(BF16), vec(g_post_mix), vec(g_pre_mem), w_mq[l].astype(BF16),
               w_mo[l].astype(BF16), vec(g_post_mem), vec(g_pre_ffn), w_ffn_gate[l].astype(BF16),
               w_ffn_up[l].astype(BF16), w_ffn_down[l].astype(BF16), vec(g_post_ffn)]

    xp = x_prompt.reshape(n_p * seq, D_MODEL)
    tm_p = 512
    qa, ka, va, qb, kb, vb, r, la, k_tail, v_tail = _proj(xp, vec(g_pre_mix), w_cat, wa2, vec(b_alpha), tm_p, seq // tm_p)
    oa = _attn_prompt(qa, ka, va, base, n_p, seq)
    ob, sp = _gla(qb, kb, vb, la, None, n_p, seq, CHUNK, 4)
    mk, mv = _mem_kv(mem_prompt.reshape(n_p * N_MEM, D_MODEL), vec(g_mem), w_mk[l].astype(BF16), w_mv[l].astype(BF16))
    tm = 256
    yp = _stack(xp, oa, ob, r, mk.astype(BF16), mv.astype(BF16), stack_w, tm, 1, seq // tm)

    xs = x_sample.reshape(n_s * seq_s, D_MODEL)
    ntok_s = n_s * seq_s
    qa_s, ka_s, va_s, qb_s, kb_s, vb_s, r_s, la_s, k_new, v_new = _proj(
        xs, vec(g_pre_mix), w_cat, wa2, vec(b_alpha), ntok_s, 1)
    n_cache = cache_a_k.shape[2]
    oa_s = _attn_sample(qa_s, ka_s, va_s,
                        cache_a_k[l].reshape(n_s * n_cache, WIDTH_A).astype(BF16),
                        cache_a_v[l].reshape(n_s * n_cache, WIDTH_A).astype(BF16), base, n_s, seq_s)
    ob_s, ss = _gla(qb_s, kb_s, vb_s, la_s, _state_to_pairs(state_gla[l]), n_s, seq_s, seq_s, 1)
    seg_s = 8
    tm_s = seg_s * seq_s
    ys = _stack(xs, oa_s, ob_s, r_s,
                cache_mem_k[l].reshape(n_s * N_MEM, D_MODEL).astype(BF16),
                cache_mem_v[l].reshape(n_s * N_MEM, D_MODEL).astype(BF16), stack_w, tm_s, seg_s, 1)

    keep = min(A_WINDOW, seq)
    heads = lambda z, n, t: z.reshape(1, n, t, N_HEADS_A, HEAD_DIM_A)
    return (yp.reshape(n_p, seq, D_MODEL), ys.reshape(n_s, seq_s, D_MODEL),
            heads(k_tail, n_p, keep), heads(v_tail, n_p, keep), _pairs_to_state(sp)[None],
            mk.reshape(1, n_p, N_MEM, N_HEADS_MEM, HEAD_DIM_MEM), mv.reshape(1, n_p, N_MEM, N_HEADS_MEM, HEAD_DIM_MEM),
            heads(k_new, n_s, seq_s), heads(v_new, n_s, seq_s), _pairs_to_state(ss)[None])
```

```python
import functools
import math

import jax
import jax.numpy as jnp
import numpy as np
from jax import lax
from jax.experimental import pallas as pl
from jax.experimental.pallas import tpu as pltpu

F32 = jnp.float32
BF16 = jnp.bfloat16

D_MODEL = 1024
CHUNK = 64
BAND_CHUNKS = 8
A_WINDOW = BAND_CHUNKS * CHUNK
N_HEADS_A = 8
HEAD_DIM_A = 64
WIDTH_A = 512
REL_CLIP = 256
N_HEADS_B = 4
KEY_DIM_B = 64
VAL_DIM_B = 128
WIDTH_BK = 256
WIDTH_BV = 512
GATE_RANK = 16
GATE_TAU = 16.0
N_MEM = 256
N_HEADS_MEM = 4
HEAD_DIM_MEM = 256
D_FF = 2816
EPS = 1e-6
NEG_BIG = -1e30

LANES = 128
BIAS_W = 640
FF_BLOCK = 256
VMEM_LIMIT = 56 * 1024 * 1024

OFF_QA, OFF_KA, OFF_VA, OFF_QB, OFF_KB, OFF_VB, OFF_R, OFF_G, W_CAT = 0, 512, 1024, 1536, 1792, 2048, 2560, 3072, 3200


def _rms(x, g):
    return x * lax.rsqrt(jnp.mean(x * x, axis=-1, keepdims=True) + EPS) * g


def _silu(x):
    return x / (1.0 + jnp.exp(-x))


def _log_sigmoid(z):
    return jnp.minimum(z, 0.0) - jnp.log(1.0 + jnp.exp(-jnp.abs(z)))


def _const_spec(shape):
    nd = len(shape)
    return pl.BlockSpec(shape, lambda *_: (0,) * nd, pipeline_mode=pl.Buffered(1))


def _params(n_axes, vmem=None):
    return pltpu.CompilerParams(dimension_semantics=("arbitrary",) * n_axes, vmem_limit_bytes=vmem)


def _proj_kernel(tiles_per_seq, x_ref, g_ref, w_ref, wa2_ref, ba_ref,
                 qa_ref, ka_ref, va_ref, qb_ref, kb_ref, vb_ref, r_ref, la_ref, kt_ref, vt_ref):
    h = _rms(x_ref[...], g_ref[...]).astype(BF16)

    def proj(lo, hi):
        return jnp.dot(h, w_ref[:, lo:hi], preferred_element_type=F32)

    qa_ref[...] = (proj(OFF_QA, OFF_KA) * (HEAD_DIM_A ** -0.5)).astype(BF16)
    ka = proj(OFF_KA, OFF_VA)
    va = proj(OFF_VA, OFF_QB)
    ka_ref[...] = ka.astype(BF16)
    va_ref[...] = va.astype(BF16)
    qb_ref[...] = proj(OFF_QB, OFF_KB) * (KEY_DIM_B ** -0.5)
    kb_ref[...] = proj(OFF_KB, OFF_VB)
    vb_ref[...] = proj(OFF_VB, OFF_R).astype(BF16)
    r_ref[...] = proj(OFF_R, OFF_G)
    g_low = proj(OFF_G, W_CAT).astype(BF16)
    z = jnp.dot(g_low, wa2_ref[...], preferred_element_type=F32) + ba_ref[...]
    la_ref[...] = _log_sigmoid(z) * (1.0 / GATE_TAU)

    @pl.when(pl.program_id(0) % tiles_per_seq == tiles_per_seq - 1)
    def _():
        kt_ref[...] = ka
        vt_ref[...] = va


def _proj(x, g, w_cat, wa2, ba, tm, tiles_per_seq):
    ntok = x.shape[0]
    nt = ntok // tm
    nseq = nt // tiles_per_seq
    row = lambda w: pl.BlockSpec((tm, w), lambda i: (i, 0))
    tail = pl.BlockSpec((tm, WIDTH_A), lambda i: (i // tiles_per_seq, 0))
    out_shape = (
        jax.ShapeDtypeStruct((ntok, WIDTH_A), BF16),
        jax.ShapeDtypeStruct((ntok, WIDTH_A), BF16),
        jax.ShapeDtypeStruct((ntok, WIDTH_A), BF16),
        jax.ShapeDtypeStruct((ntok, WIDTH_BK), F32),
        jax.ShapeDtypeStruct((ntok, WIDTH_BK), F32),
        jax.ShapeDtypeStruct((ntok, WIDTH_BV), BF16),
        jax.ShapeDtypeStruct((ntok, WIDTH_BV), F32),
        jax.ShapeDtypeStruct((ntok, WIDTH_BK), F32),
        jax.ShapeDtypeStruct((nseq * tm, WIDTH_A), F32),
        jax.ShapeDtypeStruct((nseq * tm, WIDTH_A), F32),
    )
    return pl.pallas_call(
        functools.partial(_proj_kernel, tiles_per_seq),
        grid=(nt,),
        in_specs=[row(D_MODEL), _const_spec((1, D_MODEL)), _const_spec((D_MODEL, W_CAT)),
                  _const_spec((LANES, WIDTH_BK)), _const_spec((1, WIDTH_BK))],
        out_specs=(row(WIDTH_A), row(WIDTH_A), row(WIDTH_A), row(WIDTH_BK), row(WIDTH_BK),
                   row(WIDTH_BV), row(WIDTH_BV), row(WIDTH_BK), tail, tail),
        out_shape=out_shape,
        compiler_params=_params(1, VMEM_LIMIT),
        name="proj",
    )(x, g, w_cat, wa2, ba)


def _build_bias(base_ref, bias_ref):
    for h in range(N_HEADS_A):
        rows = jnp.broadcast_to(base_ref[h:h + 1, :], (CHUNK, BIAS_W))
        bias_ref[h] = pltpu.roll(rows, 0, 1, stride=1, stride_axis=0)


def _attend(q_ref, q_row0, n_q, kcat_ref, vcat_ref, k_row0, n_keys, bias_ref, first_valid_col, o_ref):
    lane = lax.broadcasted_iota(jnp.int32, (n_q, LANES), 1)
    low_half = lane < HEAD_DIM_A
    if first_valid_col is not None:
        col = lax.broadcasted_iota(jnp.int32, (n_q, n_keys), 1)
        valid = col >= first_valid_col
    for hp in range(N_HEADS_A // 2):
        cols = slice(hp * LANES, (hp + 1) * LANES)
        qp = q_ref[q_row0:q_row0 + n_q, cols]
        kb = kcat_ref[k_row0:k_row0 + n_keys, cols]
        vb = vcat_ref[k_row0:k_row0 + n_keys, cols]
        outs = []
        for e in range(2):
            qm = jnp.where(low_half if e == 0 else ~low_half, qp, jnp.zeros_like(qp))
            s = lax.dot_general(qm, kb, (((1,), (1,)), ((), ())), preferred_element_type=F32)
            s = s + bias_ref[2 * hp + e, 0:n_q, 0:n_keys]
            if first_valid_col is not None:
                s = jnp.where(valid, s, NEG_BIG)
            m = jnp.max(s, axis=-1, keepdims=True)
            p = jnp.exp(s - m)
            l = jnp.sum(p, axis=-1, keepdims=True)
            outs.append(jnp.dot(p.astype(BF16), vb, preferred_element_type=F32) / l)
        o_ref[q_row0:q_row0 + n_q, cols] = jnp.where(low_half, outs[0], outs[1]).astype(o_ref.dtype)


TQ = 4 * CHUNK
KEYS = 3 * TQ
SOFT_W = 5 * LANES
assert 2 * TQ == A_WINDOW and SOFT_W >= (BAND_CHUNKS + 1) * CHUNK + CHUNK


def _build_band_bias(base_ref, bm_ref):
    n_band = (BAND_CHUNKS + 1) * CHUNK
    col = lax.broadcasted_iota(jnp.int32, (CHUNK, KEYS), 1)
    for h in range(N_HEADS_A):
        rows = jnp.broadcast_to(base_ref[h:h + 1, :], (CHUNK, KEYS))
        for qc in range(TQ // CHUNK):
            toeplitz = pltpu.roll(rows, qc * CHUNK, 1, stride=1, stride_axis=0)
            in_band = (col >= qc * CHUNK) & (col < qc * CHUNK + n_band)
            r0 = (qc * 2 + h % 2) * CHUNK
            bm_ref[h // 2, r0:r0 + CHUNK, :] = jnp.where(in_band, toeplitz, NEG_BIG)


def _band_tile(q_ref, k_refs, v_refs, bm_ref, o_ref, start_penalty):
    n_qc = TQ // CHUNK
    lane = lax.broadcasted_iota(jnp.int32, (CHUNK, LANES), 1)
    low_half = lane < HEAD_DIM_A
    nt_dims = (((1,), (1,)), ((), ()))
    for hp in range(N_HEADS_A // 2):
        cols = slice(hp * LANES, (hp + 1) * LANES)
        pieces = []
        for qc in range(n_qc):
            qp = q_ref[qc * CHUNK:(qc + 1) * CHUNK, cols]
            zero = jnp.zeros_like(qp)
            pieces += [jnp.where(low_half, qp, zero), jnp.where(low_half, zero, qp)]
        qs = jnp.concatenate(pieces, axis=0)
        s = jnp.concatenate([lax.dot_general(qs, k[:, cols], nt_dims, preferred_element_type=F32)
                             for k in k_refs], axis=1)
        probs, inv_l = [], []
        for qc in range(n_qc):
            rows = slice(qc * 2 * CHUNK, (qc + 1) * 2 * CHUNK)
            c0 = 0 if (qc + 1) * CHUNK + A_WINDOW <= SOFT_W else KEYS - SOFT_W
            sq = s[rows, c0:c0 + SOFT_W] + bm_ref[hp, rows, c0:c0 + SOFT_W]
            if start_penalty is not None:
                sq = sq + start_penalty[:, c0:c0 + SOFT_W]
            m = jnp.max(sq, axis=-1, keepdims=True)
            p = jnp.exp(sq - m)
            inv_l.append(1.0 / jnp.sum(p, axis=-1, keepdims=True))
            pad = jnp.zeros((2 * CHUNK, KEYS - SOFT_W), BF16)
            probs.append(jnp.concatenate([p.astype(BF16), pad] if c0 == 0 else [pad, p.astype(BF16)], axis=1))
        pm = jnp.concatenate(probs, axis=0)
        out = sum(jnp.dot(pm[:, i * TQ:(i + 1) * TQ], v[:, cols], preferred_element_type=F32)
                  for i, v in enumerate(v_refs))
        for qc in range(n_qc):
            r0 = qc * 2 * CHUNK
            o0 = out[r0:r0 + CHUNK] * inv_l[qc][0:CHUNK]
            o1 = out[r0 + CHUNK:r0 + 2 * CHUNK] * inv_l[qc][CHUNK:2 * CHUNK]
            o_ref[qc * CHUNK:(qc + 1) * CHUNK, cols] = jnp.where(low_half, o0, o1).astype(o_ref.dtype)


def _attn_prompt_kernel(q_ref, k2_ref, k1_ref, k0_ref, v2_ref, v1_ref, v0_ref, base_ref, o_ref, bm):
    t = pl.program_id(1)

    @pl.when((pl.program_id(0) == 0) & (t == 0))
    def _():
        _build_band_bias(base_ref, bm)

    k_refs = (k2_ref, k1_ref, k0_ref)
    v_refs = (v2_ref, v1_ref, v0_ref)

    @pl.when(t >= 2)
    def _():
        _band_tile(q_ref, k_refs, v_refs, bm, o_ref, None)

    @pl.when(t < 2)
    def _():
        col = lax.broadcasted_iota(jnp.int32, (1, KEYS), 1)
        penalty = jnp.where(col < (2 - t) * TQ, NEG_BIG, 0.0)
        _band_tile(q_ref, k_refs, v_refs, bm, o_ref, penalty)


def _attn_prompt(q, k, v, base, n_batch, seq):
    nt = seq // TQ
    blk = lambda back: pl.BlockSpec((TQ, WIDTH_A), lambda b, t: (b * nt + jnp.maximum(t - back, 0), 0))
    return pl.pallas_call(
        _attn_prompt_kernel,
        grid=(n_batch, nt),
        in_specs=[blk(0), blk(2), blk(1), blk(0), blk(2), blk(1), blk(0), _const_spec((N_HEADS_A, KEYS))],
        out_specs=blk(0),
        out_shape=jax.ShapeDtypeStruct((n_batch * seq, WIDTH_A), BF16),
        scratch_shapes=[pltpu.VMEM((N_HEADS_A // 2, 2 * TQ, KEYS), F32)],
        compiler_params=_params(2, VMEM_LIMIT),
        name="band_attn_prompt",
    )(q, k, k, k, v, v, v, base)


def _attn_sample_kernel(n_new, q_ref, kn_ref, vn_ref, kc_ref, vc_ref, base_ref, o_ref, kcat, vcat, bias):
    @pl.when(pl.program_id(0) == 0)
    def _():
        _build_bias(base_ref, bias)

    n_cache = kc_ref.shape[0]
    kcat[0:n_cache] = kc_ref[...]
    kcat[n_cache:n_cache + n_new] = kn_ref[...]
    vcat[0:n_cache] = vc_ref[...]
    vcat[n_cache:n_cache + n_new] = vn_ref[...]
    _attend(q_ref, 0, n_new, kcat, vcat, 0, n_cache + n_new, bias, None, o_ref)


def _attn_sample(q, k_new, v_new, k_cache, v_cache, base, n_batch, n_new):
    n_cache = k_cache.shape[0] // n_batch
    assert n_cache == A_WINDOW and n_new <= CHUNK
    new = pl.BlockSpec((n_new, WIDTH_A), lambda b: (b, 0))
    old = pl.BlockSpec((n_cache, WIDTH_A), lambda b: (b, 0))
    return pl.pallas_call(
        functools.partial(_attn_sample_kernel, n_new),
        grid=(n_batch,),
        in_specs=[new, new, new, old, old, _const_spec((N_HEADS_A, BIAS_W))],
        out_specs=new,
        out_shape=jax.ShapeDtypeStruct((n_batch * n_new, WIDTH_A), BF16),
        scratch_shapes=[pltpu.VMEM((n_cache + n_new, WIDTH_A), BF16), pltpu.VMEM((n_cache + n_new, WIDTH_A), BF16),
                        pltpu.VMEM((N_HEADS_A, CHUNK, BIAS_W), F32)],
        compiler_params=_params(1, VMEM_LIMIT),
        name="band_attn_sample",
    )(q, k_new, v_new, k_cache, v_cache, base)


def _gla_levels(chunk):
    return [chunk >> i for i in range(int(math.log2(chunk)) + 1)]


def _segment_matrix(chunk):
    i = np.arange(chunk)[:, None]
    t = np.arange(chunk)[None, :]
    blocks = []
    for s in _gla_levels(chunk):
        start = (i // s) * s
        blocks.append((t >= start) & (t <= i))
        blocks.append((t > i) & (t <= start + s - 1))
    seg = np.concatenate(blocks, axis=0).astype(np.float32)
    return np.concatenate([seg, seg], axis=1)


def _gla_chunk(q, k, v, la, seg_ref, state_ref, chunk):
    levels = _gla_levels(chunk)
    la_hi = la.astype(BF16)
    la_lo = (la - la_hi.astype(F32)).astype(BF16)
    sums = jnp.dot(seg_ref[...], jnp.concatenate([la_hi, la_lo], axis=0), preferred_element_type=F32)

    def fwd(li):
        return jnp.minimum(sums[(2 * li) * chunk:(2 * li + 1) * chunk], 0.0)

    def rev(li):
        return jnp.minimum(sums[(2 * li + 1) * chunk:(2 * li + 2) * chunk], 0.0)

    ri = lax.broadcasted_iota(jnp.int32, (2 * chunk, chunk), 0) % chunk
    ci = lax.broadcasted_iota(jnp.int32, (2 * chunk, chunk), 1)
    lane = lax.broadcasted_iota(jnp.int32, (chunk, LANES), 1)
    low_half = lane < KEY_DIM_B

    def stack_heads(x_pair):
        zero = jnp.zeros_like(x_pair)
        return jnp.concatenate([jnp.where(low_half, x_pair, zero), jnp.where(low_half, zero, x_pair)], axis=0)

    nt_dims = (((1,), (1,)), ((), ()))
    outs = []
    for p in range(N_HEADS_B // 2):
        cols = slice(p * LANES, (p + 1) * LANES)
        qp, kp = q[:, cols], k[:, cols]
        a = jnp.where(ri == ci,
                      lax.dot_general(stack_heads(qp).astype(BF16), kp.astype(BF16), nt_dims,
                                      preferred_element_type=F32), 0.0)
        for li in range(1, len(levels)):
            s = levels[li]
            sh = int(math.log2(s))
            qd = (qp * jnp.exp(fwd(li)[:, cols])).astype(BF16)
            kd = (kp * jnp.exp(rev(li)[:, cols])).astype(BF16)
            term = lax.dot_general(stack_heads(qd), kd, nt_dims, preferred_element_type=F32)
            keep = (((ri >> sh) & 1) == 1) & ((ci >> sh) == (ri >> sh) - 1)
            a = a + jnp.where(keep, term, 0.0)
        a = a.astype(BF16)
        st = state_ref[p]
        q_in = (qp * jnp.exp(fwd(0)[:, cols])).astype(BF16)
        o_pair = lax.dot_general(q_in, st.astype(BF16), nt_dims, preferred_element_type=F32)
        k_out = (kp * jnp.exp(rev(0)[:, cols])).astype(BF16)
        vp = v[:, p * 2 * VAL_DIM_B:(p + 1) * 2 * VAL_DIM_B]
        upd = lax.dot_general(vp, k_out, (((0,), (0,)), ((), ())), preferred_element_type=F32)
        decay = jnp.exp(fwd(0)[chunk - 1:chunk, cols])
        own = (lax.broadcasted_iota(jnp.int32, (2 * VAL_DIM_B, LANES), 0) < VAL_DIM_B) == \
              (lax.broadcasted_iota(jnp.int32, (2 * VAL_DIM_B, LANES), 1) < KEY_DIM_B)
        state_ref[p] = st * decay + jnp.where(own, upd, 0.0)
        for e in range(2):
            ve = vp[:, e * VAL_DIM_B:(e + 1) * VAL_DIM_B]
            intra = jnp.dot(a[e * chunk:(e + 1) * chunk], ve, preferred_element_type=F32)
            outs.append(o_pair[:, e * VAL_DIM_B:(e + 1) * VAL_DIM_B] + intra)
    return jnp.concatenate(outs, axis=1)


def _gla_kernel(chunk, chunks_per_tile, has_init, *refs):
    if has_init:
        q_ref, k_ref, v_ref, la_ref, seg_ref, s0_ref, o_ref, sout_ref, state = refs
    else:
        q_ref, k_ref, v_ref, la_ref, seg_ref, o_ref, sout_ref, state = refs
    t = pl.program_id(1)

    @pl.when(t == 0)
    def _():
        state[...] = s0_ref[...] if has_init else jnp.zeros_like(state)

    for c in range(chunks_per_tile):
        rows = slice(c * chunk, (c + 1) * chunk)
        o_ref[rows, :] = _gla_chunk(q_ref[rows, :], k_ref[rows, :], v_ref[rows, :], la_ref[rows, :],
                                    seg_ref, state, chunk)

    @pl.when(t == pl.num_programs(1) - 1)
    def _():
        sout_ref[...] = state[...]


def _gla(q, k, v, la, s0, n_batch, seq, chunk, chunks_per_tile):
    tg = chunk * chunks_per_tile
    nt = seq // tg
    seg = jnp.asarray(_segment_matrix(chunk), BF16)
    row = lambda w: pl.BlockSpec((tg, w), lambda b, t: (b * nt + t, 0))
    st = pl.BlockSpec((None, 2, 2 * VAL_DIM_B, LANES), lambda b, t: (b, 0, 0, 0))
    in_specs = [row(WIDTH_BK), row(WIDTH_BK), row(WIDTH_BV), row(WIDTH_BK), _const_spec(seg.shape)]
    args = [q, k, v, la, seg]
    if s0 is not None:
        in_specs.append(st)
        args.append(s0)
    return pl.pallas_call(
        functools.partial(_gla_kernel, chunk, chunks_per_tile, s0 is not None),
        grid=(n_batch, nt),
        in_specs=in_specs,
        out_specs=(row(WIDTH_BV), st),
        out_shape=(jax.ShapeDtypeStruct((n_batch * seq, WIDTH_BV), F32),
                   jax.ShapeDtypeStruct((n_batch, 2, 2 * VAL_DIM_B, LANES), F32)),
        scratch_shapes=[pltpu.VMEM((2, 2 * VAL_DIM_B, LANES), F32)],
        compiler_params=_params(2, VMEM_LIMIT),
        name="gla_chunk%d" % chunk,
    )(*args)


def _state_to_pairs(s):
    n = s.shape[0]
    st = jnp.swapaxes(s, -1, -2).reshape(n, 2, 2, VAL_DIM_B, KEY_DIM_B)
    z = jnp.zeros_like(st[:, :, 0])
    top = jnp.concatenate([st[:, :, 0], z], axis=-1)
    bot = jnp.concatenate([z, st[:, :, 1]], axis=-1)
    return jnp.concatenate([top, bot], axis=-2)


def _pairs_to_state(sp):
    n = sp.shape[0]
    h0 = sp[:, :, :VAL_DIM_B, :KEY_DIM_B]
    h1 = sp[:, :, VAL_DIM_B:, KEY_DIM_B:]
    st = jnp.stack([h0, h1], axis=2).reshape(n, N_HEADS_B, VAL_DIM_B, KEY_DIM_B)
    return jnp.swapaxes(st, -1, -2)


def _mem_kv_kernel(m_ref, g_ref, wk_ref, wv_ref, k_ref, v_ref):
    m = _rms(m_ref[...], g_ref[...]).astype(BF16)
    k_ref[...] = jnp.dot(m, wk_ref[...], preferred_element_type=F32)
    v_ref[...] = jnp.dot(m, wv_ref[...], preferred_element_type=F32)


def _mem_kv(mem, g, wk, wv):
    n = mem.shape[0]
    tm = N_MEM
    row = pl.BlockSpec((tm, D_MODEL), lambda i: (i, 0))
    return pl.pallas_call(
        _mem_kv_kernel,
        grid=(n // tm,),
        in_specs=[row, _const_spec((1, D_MODEL)), _const_spec((D_MODEL, D_MODEL)), _const_spec((D_MODEL, D_MODEL))],
        out_specs=(row, row),
        out_shape=(jax.ShapeDtypeStruct((n, D_MODEL), F32), jax.ShapeDtypeStruct((n, D_MODEL), F32)),
        compiler_params=_params(1, VMEM_LIMIT),
        name="mem_kv",
    )(mem, g, wk, wv)


def _stack_kernel(n_seg, x_ref, oa_ref, ob_ref, r_ref, mk_ref, mv_ref,
                  g_gla_ref, wo_ref, g_post_mix_ref, g_pre_mem_ref, wq_ref, wmo_ref, g_post_mem_ref,
                  g_pre_ffn_ref, wg_ref, wu_ref, wd_ref, g_post_ffn_ref, y_ref):
    tm = x_ref.shape[0]
    ob = ob_ref[...]
    normed = []
    for h in range(N_HEADS_B):
        seg = ob[:, h * VAL_DIM_B:(h + 1) * VAL_DIM_B]
        normed.append(seg * lax.rsqrt(jnp.mean(seg * seg, axis=-1, keepdims=True) + EPS))
    yb = jnp.concatenate(normed, axis=1) * g_gla_ref[...] * _silu(r_ref[...])
    mix = (jnp.dot(oa_ref[...], wo_ref[0:WIDTH_A, :], preferred_element_type=F32)
           + jnp.dot(yb.astype(BF16), wo_ref[WIDTH_A:, :], preferred_element_type=F32))
    x1 = x_ref[...] + _rms(mix, g_post_mix_ref[...])

    hq = _rms(x1, g_pre_mem_ref[...]).astype(BF16)
    q = (jnp.dot(hq, wq_ref[...], preferred_element_type=F32) * (HEAD_DIM_MEM ** -0.5)).astype(BF16)
    rows_per_seg = tm // n_seg
    seg_outs = []
    for sg in range(n_seg):
        head_outs = []
        for h in range(N_HEADS_MEM):
            cols = slice(h * HEAD_DIM_MEM, (h + 1) * HEAD_DIM_MEM)
            qh = q[sg * rows_per_seg:(sg + 1) * rows_per_seg, cols]
            kh = mk_ref[sg * N_MEM:(sg + 1) * N_MEM, cols]
            vh = mv_ref[sg * N_MEM:(sg + 1) * N_MEM, cols]
            s = lax.dot_general(qh, kh, (((1,), (1,)), ((), ())), preferred_element_type=F32)
            p = jnp.exp(s - jnp.max(s, axis=-1, keepdims=True))
            l = jnp.sum(p, axis=-1, keepdims=True)
            head_outs.append(jnp.dot(p.astype(BF16), vh, preferred_element_type=F32) / l)
        seg_outs.append(jnp.concatenate(head_outs, axis=1))
    o = jnp.concatenate(seg_outs, axis=0) if n_seg > 1 else seg_outs[0]
    att = jnp.dot(o.astype(BF16), wmo_ref[...], preferred_element_type=F32)
    x2 = x1 + _rms(att, g_post_mem_ref[...])

    hf = _rms(x2, g_pre_ffn_ref[...]).astype(BF16)
    f = jnp.zeros((tm, D_MODEL), F32)
    for c in range(D_FF // FF_BLOCK):
        cols = slice(c * FF_BLOCK, (c + 1) * FF_BLOCK)
        gate = jnp.dot(hf, wg_ref[:, cols], preferred_element_type=F32)
        up = jnp.dot(hf, wu_ref[:, cols], preferred_element_type=F32)
        f = f + jnp.dot((_silu(gate) * up).astype(BF16), wd_ref[cols, :], preferred_element_type=F32)
    y_ref[...] = x2 + _rms(f, g_post_ffn_ref[...])


def _stack(x, oa, ob, r, mk, mv, weights, tm, n_seg, tiles_per_mem_block):
    ntok = x.shape[0]
    row = lambda w: pl.BlockSpec((tm, w), lambda i: (i, 0))
    mem = pl.BlockSpec((n_seg * N_MEM, D_MODEL), lambda i: (i // tiles_per_mem_block, 0))
    w_specs = [_const_spec(w.shape) for w in weights]
    return pl.pallas_call(
        functools.partial(_stack_kernel, n_seg),
        grid=(ntok // tm,),
        in_specs=[row(D_MODEL), row(WIDTH_A), row(WIDTH_BV), row(WIDTH_BV), mem, mem] + w_specs,
        out_specs=row(D_MODEL),
        out_shape=jax.ShapeDtypeStruct((ntok, D_MODEL), F32),
        compiler_params=_params(1, VMEM_LIMIT),
        name="token_stack",
    )(x, oa, ob, r, mk, mv, *weights)


def _bias_base(table, width):
    u = np.arange(width)
    idx = np.where(u < (BAND_CHUNKS + 1) * CHUNK, np.clip(A_WINDOW - u, -REL_CLIP, REL_CLIP) + REL_CLIP, 2 * REL_CLIP)
    return table[:, idx]


def kernel(x_prompt, x_sample, mem_prompt, cache_a_k, cache_a_v, state_gla, cache_mem_k, cache_mem_v,
           g_pre_mix, w_in, rel_bias, w_alpha2, b_alpha, g_gla_out, w_o, g_post_mix,
           g_pre_mem, g_mem, w_mq, w_mk, w_mv, w_mo, g_post_mem,
           g_pre_ffn, w_ffn_gate, w_ffn_up, w_ffn_down, g_post_ffn):
    depth = w_in.shape[0]
    assert depth == 1
    l = 0
    n_p, seq, _ = x_prompt.shape
    n_s, seq_s, _ = x_sample.shape
    vec = lambda g: g[l].reshape(1, -1)

    wi = w_in[l]
    w_cat = jnp.concatenate(
        [wi[:, :2560], wi[:, 2576:3088], wi[:, 2560:2576], jnp.zeros((D_MODEL, W_CAT - 3088), F32)], axis=1).astype(BF16)
    wa2 = jnp.concatenate([w_alpha2[l], jnp.zeros((LANES - GATE_RANK, WIDTH_BK), F32)], axis=0).astype(BF16)
    base = _bias_base(rel_bias[l], BIAS_W)
    stack_w = [vec(g_gla_out), w_o[l].astype(BF16), vec(g_post_mix), vec(g_pre_mem), w_mq[l].astype(BF16),
               w_mo[l].astype(BF16), vec(g_post_mem), vec(g_pre_ffn), w_ffn_gate[l].astype(BF16),
               w_ffn_up[l].astype(BF16), w_ffn_down[l].astype(BF16), vec(g_post_ffn)]

    xp = x_prompt.reshape(n_p * seq, D_MODEL)
    tm_p = 512
    qa, ka, va, qb, kb, vb, r, la, k_tail, v_tail = _proj(xp, vec(g_pre_mix), w_cat, wa2, vec(b_alpha), tm_p, seq // tm_p)
    oa = _attn_prompt(qa, ka, va, _bias_base(rel_bias[l], KEYS), n_p, seq)
    ob, sp = _gla(qb, kb, vb, la, None, n_p, seq, CHUNK, 4)
    mk, mv = _mem_kv(mem_prompt.reshape(n_p * N_MEM, D_MODEL), vec(g_mem), w_mk[l].astype(BF16), w_mv[l].astype(BF16))
    tm = 256
    yp = _stack(xp, oa, ob, r, mk.astype(BF16), mv.astype(BF16), stack_w, tm, 1, seq // tm)

    xs = x_sample.reshape(n_s * seq_s, D_MODEL)
    ntok_s = n_s * seq_s
    qa_s, ka_s, va_s, qb_s, kb_s, vb_s, r_s, la_s, k_new, v_new = _proj(
        xs, vec(g_pre_mix), w_cat, wa2, vec(b_alpha), ntok_s, 1)
    n_cache = cache_a_k.shape[2]
    oa_s = _attn_sample(qa_s, ka_s, va_s,
                        cache_a_k[l].reshape(n_s * n_cache, WIDTH_A).astype(BF16),
                        cache_a_v[l].reshape(n_s * n_cache, WIDTH_A).astype(BF16), base, n_s, seq_s)
    ob_s, ss = _gla(qb_s, kb_s, vb_s, la_s, _state_to_pairs(state_gla[l]), n_s, seq_s, seq_s, 1)
    seg_s = 8
    tm_s = seg_s * seq_s
    ys = _stack(xs, oa_s, ob_s, r_s,
                cache_mem_k[l].reshape(n_s * N_MEM, D_MODEL).astype(BF16),
                cache_mem_v[l].reshape(n_s * N_MEM, D_MODEL).astype(BF16), stack_w, tm_s, seg_s, 1)

    keep = min(A_WINDOW, seq)
    heads = lambda z, n, t: z.reshape(1, n, t, N_HEADS_A, HEAD_DIM_A)
    return (yp.reshape(n_p, seq, D_MODEL), ys.reshape(n_s, seq_s, D_MODEL),
            heads(k_tail, n_p, keep), heads(v_tail, n_p, keep), _pairs_to_state(sp)[None],
            mk.reshape(1, n_p, N_MEM, N_HEADS_MEM, HEAD_DIM_MEM), mv.reshape(1, n_p, N_MEM, N_HEADS_MEM, HEAD_DIM_MEM),
            heads(k_new, n_s, seq_s), heads(v_new, n_s, seq_s), _pairs_to_state(ss)[None])
```

```python
import functools
import math

import jax
import jax.numpy as jnp
import numpy as np
from jax import lax
from jax.experimental import pallas as pl
from jax.experimental.pallas import tpu as pltpu

F32 = jnp.float32
BF16 = jnp.bfloat16

D_MODEL = 1024
CHUNK = 64
BAND_CHUNKS = 8
A_WINDOW = BAND_CHUNKS * CHUNK
N_HEADS_A = 8
HEAD_DIM_A = 64
WIDTH_A = 512
REL_CLIP = 256
N_HEADS_B = 4
KEY_DIM_B = 64
VAL_DIM_B = 128
WIDTH_BK = 256
WIDTH_BV = 512
GATE_RANK = 16
GATE_TAU = 16.0
N_MEM = 256
N_HEADS_MEM = 4
HEAD_DIM_MEM = 256
D_FF = 2816
EPS = 1e-6
NEG_BIG = -1e30

LANES = 128
BIAS_W = 640
FF_BLOCK = 256
VMEM_LIMIT = 56 * 1024 * 1024

OFF_QA, OFF_KA, OFF_VA, OFF_QB, OFF_KB, OFF_VB, OFF_R, OFF_G, W_CAT = 0, 512, 1024, 1536, 1792, 2048, 2560, 3072, 3200


def _rms(x, g):
    return x * lax.rsqrt(jnp.mean(x * x, axis=-1, keepdims=True) + EPS) * g


def _silu(x):
    return x / (1.0 + jnp.exp(-x))


def _log_sigmoid(z):
    return jnp.minimum(z, 0.0) - jnp.log(1.0 + jnp.exp(-jnp.abs(z)))


def _const_spec(shape):
    nd = len(shape)
    return pl.BlockSpec(shape, lambda *_: (0,) * nd, pipeline_mode=pl.Buffered(1))


def _params(n_axes, vmem=None):
    return pltpu.CompilerParams(dimension_semantics=("arbitrary",) * n_axes, vmem_limit_bytes=vmem)


def _proj_kernel(tiles_per_seq, x_ref, g_ref, w_ref, wa2_ref, ba_ref,
                 qa_ref, ka_ref, va_ref, qb_ref, kb_ref, vb_ref, r_ref, la_ref, kt_ref, vt_ref):
    h = _rms(x_ref[...], g_ref[...]).astype(BF16)

    def proj(lo, hi):
        return jnp.dot(h, w_ref[:, lo:hi], preferred_element_type=F32)

    qa_ref[...] = (proj(OFF_QA, OFF_KA) * (HEAD_DIM_A ** -0.5)).astype(BF16)
    ka = proj(OFF_KA, OFF_VA)
    va = proj(OFF_VA, OFF_QB)
    ka_ref[...] = ka.astype(BF16)
    va_ref[...] = va.astype(BF16)
    qb_ref[...] = proj(OFF_QB, OFF_KB) * (KEY_DIM_B ** -0.5)
    kb_ref[...] = proj(OFF_KB, OFF_VB)
    vb_ref[...] = proj(OFF_VB, OFF_R).astype(BF16)
    r_ref[...] = proj(OFF_R, OFF_G)
    g_low = proj(OFF_G, W_CAT).astype(BF16)
    z = jnp.dot(g_low, wa2_ref[...], preferred_element_type=F32) + ba_ref[...]
    la_ref[...] = _log_sigmoid(z) * (1.0 / GATE_TAU)

    @pl.when(pl.program_id(0) % tiles_per_seq == tiles_per_seq - 1)
    def _():
        kt_ref[...] = ka
        vt_ref[...] = va


def _proj(x, g, w_cat, wa2, ba, tm, tiles_per_seq):
    ntok = x.shape[0]
    nt = ntok // tm
    nseq = nt // tiles_per_seq
    row = lambda w: pl.BlockSpec((tm, w), lambda i: (i, 0))
    tail = pl.BlockSpec((tm, WIDTH_A), lambda i: (i // tiles_per_seq, 0))
    out_shape = (
        jax.ShapeDtypeStruct((ntok, WIDTH_A), BF16),
        jax.ShapeDtypeStruct((ntok, WIDTH_A), BF16),
        jax.ShapeDtypeStruct((ntok, WIDTH_A), BF16),
        jax.ShapeDtypeStruct((ntok, WIDTH_BK), F32),
        jax.ShapeDtypeStruct((ntok, WIDTH_BK), F32),
        jax.ShapeDtypeStruct((ntok, WIDTH_BV), BF16),
        jax.ShapeDtypeStruct((ntok, WIDTH_BV), F32),
        jax.ShapeDtypeStruct((ntok, WIDTH_BK), F32),
        jax.ShapeDtypeStruct((nseq * tm, WIDTH_A), F32),
        jax.ShapeDtypeStruct((nseq * tm, WIDTH_A), F32),
    )
    return pl.pallas_call(
        functools.partial(_proj_kernel, tiles_per_seq),
        grid=(nt,),
        in_specs=[row(D_MODEL), _const_spec((1, D_MODEL)), _const_spec((D_MODEL, W_CAT)),
                  _const_spec((LANES, WIDTH_BK)), _const_spec((1, WIDTH_BK))],
        out_specs=(row(WIDTH_A), row(WIDTH_A), row(WIDTH_A), row(WIDTH_BK), row(WIDTH_BK),
                   row(WIDTH_BV), row(WIDTH_BV), row(WIDTH_BK), tail, tail),
        out_shape=out_shape,
        compiler_params=_params(1, VMEM_LIMIT),
        name="proj",
    )(x, g, w_cat, wa2, ba)


def _build_bias(base_ref, bias_ref):
    for h in range(N_HEADS_A):
        rows = jnp.broadcast_to(base_ref[h:h + 1, :], (CHUNK, BIAS_W))
        bias_ref[h] = pltpu.roll(rows, 0, 1, stride=1, stride_axis=0)


def _attend(q_ref, q_row0, n_q, kcat_ref, vcat_ref, k_row0, n_keys, bias_ref, first_valid_col, o_ref):
    lane = lax.broadcasted_iota(jnp.int32, (n_q, LANES), 1)
    low_half = lane < HEAD_DIM_A
    if first_valid_col is not None:
        col = lax.broadcasted_iota(jnp.int32, (n_q, n_keys), 1)
        valid = col >= first_valid_col
    for hp in range(N_HEADS_A // 2):
        cols = slice(hp * LANES, (hp + 1) * LANES)
        qp = q_ref[q_row0:q_row0 + n_q, cols]
        kb = kcat_ref[k_row0:k_row0 + n_keys, cols]
        vb = vcat_ref[k_row0:k_row0 + n_keys, cols]
        outs = []
        for e in range(2):
            qm = jnp.where(low_half if e == 0 else ~low_half, qp, jnp.zeros_like(qp))
            s = lax.dot_general(qm, kb, (((1,), (1,)), ((), ())), preferred_element_type=F32)
            s = s + bias_ref[2 * hp + e, 0:n_q, 0:n_keys]
            if first_valid_col is not None:
                s = jnp.where(valid, s, NEG_BIG)
            m = jnp.max(s, axis=-1, keepdims=True)
            p = jnp.exp(s - m)
            l = jnp.sum(p, axis=-1, keepdims=True)
            outs.append(jnp.dot(p.astype(BF16), vb, preferred_element_type=F32) / l)
        o_ref[q_row0:q_row0 + n_q, cols] = jnp.where(low_half, outs[0], outs[1]).astype(o_ref.dtype)


TQ = 4 * CHUNK
KEYS = 3 * TQ
SOFT_W = 5 * LANES
assert 2 * TQ == A_WINDOW and SOFT_W >= (BAND_CHUNKS + 1) * CHUNK + CHUNK


def _build_band_bias(base_ref, bm_ref):
    n_band = (BAND_CHUNKS + 1) * CHUNK
    col = lax.broadcasted_iota(jnp.int32, (CHUNK, KEYS), 1)
    for h in range(N_HEADS_A):
        rows = jnp.broadcast_to(base_ref[h:h + 1, :], (CHUNK, KEYS))
        for qc in range(TQ // CHUNK):
            toeplitz = pltpu.roll(rows, qc * CHUNK, 1, stride=1, stride_axis=0)
            in_band = (col >= qc * CHUNK) & (col < qc * CHUNK + n_band)
            r0 = (qc * 2 + h % 2) * CHUNK
            bm_ref[h // 2, r0:r0 + CHUNK, :] = jnp.where(in_band, toeplitz, NEG_BIG)


def _band_tile(q_ref, k_refs, v_refs, bm_ref, o_ref, start_penalty):
    n_qc = TQ // CHUNK
    lane = lax.broadcasted_iota(jnp.int32, (CHUNK, LANES), 1)
    low_half = lane < HEAD_DIM_A
    nt_dims = (((1,), (1,)), ((), ()))
    for hp in range(N_HEADS_A // 2):
        cols = slice(hp * LANES, (hp + 1) * LANES)
        pieces = []
        for qc in range(n_qc):
            qp = q_ref[qc * CHUNK:(qc + 1) * CHUNK, cols]
            zero = jnp.zeros_like(qp)
            pieces += [jnp.where(low_half, qp, zero), jnp.where(low_half, zero, qp)]
        qs = jnp.concatenate(pieces, axis=0)
        s = jnp.concatenate([lax.dot_general(qs, k[:, cols], nt_dims, preferred_element_type=F32)
                             for k in k_refs], axis=1)
        probs, inv_l = [], []
        for qc in range(n_qc):
            rows = slice(qc * 2 * CHUNK, (qc + 1) * 2 * CHUNK)
            c0 = 0 if (qc + 1) * CHUNK + A_WINDOW <= SOFT_W else KEYS - SOFT_W
            sq = s[rows, c0:c0 + SOFT_W] + bm_ref[hp, rows, c0:c0 + SOFT_W]
            if start_penalty is not None:
                sq = sq + start_penalty[:, c0:c0 + SOFT_W]
            m = jnp.max(sq, axis=-1, keepdims=True)
            p = jnp.exp(sq - m)
            inv_l.append(1.0 / jnp.sum(p, axis=-1, keepdims=True))
            pad = jnp.zeros((2 * CHUNK, KEYS - SOFT_W), BF16)
            probs.append(jnp.concatenate([p.astype(BF16), pad] if c0 == 0 else [pad, p.astype(BF16)], axis=1))
        pm = jnp.concatenate(probs, axis=0)
        out = sum(jnp.dot(pm[:, i * TQ:(i + 1) * TQ], v[:, cols], preferred_element_type=F32)
                  for i, v in enumerate(v_refs))
        for qc in range(n_qc):
            r0 = qc * 2 * CHUNK
            o0 = out[r0:r0 + CHUNK] * inv_l[qc][0:CHUNK]
            o1 = out[r0 + CHUNK:r0 + 2 * CHUNK] * inv_l[qc][CHUNK:2 * CHUNK]
            o_ref[qc * CHUNK:(qc + 1) * CHUNK, cols] = jnp.where(low_half, o0, o1).astype(o_ref.dtype)


def _attn_prompt_kernel(q_ref, k2_ref, k1_ref, k0_ref, v2_ref, v1_ref, v0_ref, base_ref, o_ref, bm):
    t = pl.program_id(1)

    @pl.when((pl.program_id(0) == 0) & (t == 0))
    def _():
        _build_band_bias(base_ref, bm)

    k_refs = (k2_ref, k1_ref, k0_ref)
    v_refs = (v2_ref, v1_ref, v0_ref)

    @pl.when(t >= 2)
    def _():
        _band_tile(q_ref, k_refs, v_refs, bm, o_ref, None)

    @pl.when(t < 2)
    def _():
        col = lax.broadcasted_iota(jnp.int32, (1, KEYS), 1)
        penalty = jnp.where(col < (2 - t) * TQ, NEG_BIG, 0.0)
        _band_tile(q_ref, k_refs, v_refs, bm, o_ref, penalty)


def _attn_prompt(q, k, v, base, n_batch, seq):
    nt = seq // TQ
    blk = lambda back: pl.BlockSpec((TQ, WIDTH_A), lambda b, t: (b * nt + jnp.maximum(t - back, 0), 0))
    return pl.pallas_call(
        _attn_prompt_kernel,
        grid=(n_batch, nt),
        in_specs=[blk(0), blk(2), blk(1), blk(0), blk(2), blk(1), blk(0), _const_spec((N_HEADS_A, KEYS))],
        out_specs=blk(0),
        out_shape=jax.ShapeDtypeStruct((n_batch * seq, WIDTH_A), BF16),
        scratch_shapes=[pltpu.VMEM((N_HEADS_A // 2, 2 * TQ, KEYS), F32)],
        compiler_params=_params(2, VMEM_LIMIT),
        name="band_attn_prompt",
    )(q, k, k, k, v, v, v, base)


def _attn_sample_kernel(n_new, q_ref, kn_ref, vn_ref, kc_ref, vc_ref, base_ref, o_ref, kcat, vcat, bias):
    @pl.when(pl.program_id(0) == 0)
    def _():
        _build_bias(base_ref, bias)

    n_cache = kc_ref.shape[0]
    kcat[0:n_cache] = kc_ref[...]
    kcat[n_cache:n_cache + n_new] = kn_ref[...]
    vcat[0:n_cache] = vc_ref[...]
    vcat[n_cache:n_cache + n_new] = vn_ref[...]
    _attend(q_ref, 0, n_new, kcat, vcat, 0, n_cache + n_new, bias, None, o_ref)


def _attn_sample(q, k_new, v_new, k_cache, v_cache, base, n_batch, n_new):
    n_cache = k_cache.shape[0] // n_batch
    assert n_cache == A_WINDOW and n_new <= CHUNK
    new = pl.BlockSpec((n_new, WIDTH_A), lambda b: (b, 0))
    old = pl.BlockSpec((n_cache, WIDTH_A), lambda b: (b, 0))
    return pl.pallas_call(
        functools.partial(_attn_sample_kernel, n_new),
        grid=(n_batch,),
        in_specs=[new, new, new, old, old, _const_spec((N_HEADS_A, BIAS_W))],
        out_specs=new,
        out_shape=jax.ShapeDtypeStruct((n_batch * n_new, WIDTH_A), BF16),
        scratch_shapes=[pltpu.VMEM((n_cache + n_new, WIDTH_A), BF16), pltpu.VMEM((n_cache + n_new, WIDTH_A), BF16),
                        pltpu.VMEM((N_HEADS_A, CHUNK, BIAS_W), F32)],
        compiler_params=_params(1, VMEM_LIMIT),
        name="band_attn_sample",
    )(q, k_new, v_new, k_cache, v_cache, base)


def _gla_levels(chunk):
    return [chunk >> i for i in range(int(math.log2(chunk)) + 1)]


def _segment_matrix(chunk):
    i = np.arange(chunk)[:, None]
    t = np.arange(chunk)[None, :]
    blocks = []
    for li, s in enumerate(_gla_levels(chunk)):
        start = (i // s) * s
        f_rows = (t >= start) & (t <= i)
        r_rows = (t > i) & (t <= start + s - 1)
        if li == 0:
            blocks += [f_rows, r_rows]
        else:
            blocks.append(np.where((i // s) % 2 == 1, f_rows, r_rows))
    seg = np.concatenate(blocks, axis=0).astype(np.float32)
    return np.concatenate([seg, seg], axis=1)


def _level_masks(group_tokens):
    tg = group_tokens
    i = (np.arange(2 * tg) % tg)[:, None]
    j = np.arange(tg)[None, :]
    masks = [i == j]
    s = tg // 2
    while s >= 1:
        masks.append(((i // s) % 2 == 1) & (j // s == i // s - 1))
        s //= 2
    return np.stack(masks).astype(np.float32)


def _stack_heads(x_pair):
    low_half = lax.broadcasted_iota(jnp.int32, x_pair.shape, 1) < KEY_DIM_B
    zero = jnp.zeros_like(x_pair)
    return jnp.concatenate([jnp.where(low_half, x_pair, zero), jnp.where(low_half, zero, x_pair)], axis=0)


def _gla_tile(q, k, v, la, seg_ref, mask_ref, state_ref, chunk, n_chunks):
    n, tg = n_chunks, chunk * n_chunks
    levels = _gla_levels(chunk)
    la_hi = la.astype(BF16)
    la_lo = (la - la_hi.astype(F32)).astype(BF16)
    split = jnp.concatenate(
        [jnp.concatenate([la_hi[c * chunk:(c + 1) * chunk], la_lo[c * chunk:(c + 1) * chunk]], axis=0)
         for c in range(n)], axis=1)
    sums = jnp.dot(seg_ref[...], split, preferred_element_type=F32)

    def seg_sum(block):
        rows = slice(block * chunk, (block + 1) * chunk)
        return jnp.minimum(jnp.concatenate(
            [sums[rows, c * WIDTH_BK:(c + 1) * WIDTH_BK] for c in range(n)], axis=0), 0.0)

    fwd0, rev0 = seg_sum(0), seg_sum(1)
    from_start, to_end = jnp.exp(fwd0), jnp.exp(rev0)
    total = [fwd0[(c + 1) * chunk - 1:(c + 1) * chunk] for c in range(n)]

    def chunk_sum(cs):
        cs = list(cs)
        return (sum(total[c] for c in cs[1:]) + total[cs[0]]) if cs else None

    def extend(base, offsets):
        pieces = []
        for c in range(n):
            x = base[c * chunk:(c + 1) * chunk]
            pieces.append(x if offsets[c] is None else x * jnp.exp(offsets[c]))
        return jnp.concatenate(pieces, axis=0) if n > 1 else pieces[0]

    q_state = (q * extend(from_start, [chunk_sum(range(0, c)) for c in range(n)])).astype(BF16)
    k_state = (k * extend(to_end, [chunk_sum(range(c + 1, n)) for c in range(n)])).astype(BF16)
    decay_tile = jnp.exp(chunk_sum(range(n)))
    ops = {}
    s = tg // 2
    while s >= chunk and n > 1:
        per = s // chunk
        offsets, bases = [], []
        for c in range(n):
            sg = c // per
            offsets.append(chunk_sum(range(sg * per, c)) if sg % 2 else chunk_sum(range(c + 1, (sg + 1) * per)))
            bases.append((from_start if sg % 2 else to_end)[c * chunk:(c + 1) * chunk])
        w = extend(jnp.concatenate(bases, axis=0), offsets)
        ops[s] = ((q * w).astype(BF16), (k * w).astype(BF16))
        s //= 2
    for li in range(1, len(levels)):
        w = jnp.exp(seg_sum(li + 1))
        ops[levels[li]] = ((q * w).astype(BF16), (k * w).astype(BF16))
    q16, k16 = q.astype(BF16), k.astype(BF16)

    gt = mask_ref.shape[2]
    own = (lax.broadcasted_iota(jnp.int32, (2 * VAL_DIM_B, LANES), 0) < VAL_DIM_B) == \
          (lax.broadcasted_iota(jnp.int32, (2 * VAL_DIM_B, LANES), 1) < KEY_DIM_B)
    nt_dims = (((1,), (1,)), ((), ()))
    tn_dims = (((0,), (0,)), ((), ()))
    outs = []
    for p in range(N_HEADS_B // 2):
        cols = slice(p * LANES, (p + 1) * LANES)
        vp = v[:, p * 2 * VAL_DIM_B:(p + 1) * 2 * VAL_DIM_B]
        st = state_ref[p]
        o_state = lax.dot_general(q_state[:, cols], st.astype(BF16), nt_dims, preferred_element_type=F32)
        upd = lax.dot_general(vp, k_state[:, cols], tn_dims, preferred_element_type=F32)
        state_ref[p] = st * decay_tile[:, cols] + jnp.where(own, upd, 0.0)
        piece = [[None] * (tg // gt) for _ in range(2)]
        for g in range(tg // gt):
            rows = slice(g * gt, (g + 1) * gt)
            a = mask_ref[0] * lax.dot_general(_stack_heads(q16[rows, cols]), k16[rows, cols], nt_dims,
                                              preferred_element_type=F32)
            m, s = 1, gt // 2
            while s >= 1:
                qd, kd = ops[s]
                a = a + mask_ref[m] * lax.dot_general(_stack_heads(qd[rows, cols]), kd[rows, cols], nt_dims,
                                                      preferred_element_type=F32)
                m, s = m + 1, s // 2
            a = a.astype(BF16)
            for e in range(2):
                hv = slice(e * VAL_DIM_B, (e + 1) * VAL_DIM_B)
                piece[e][g] = o_state[rows, hv] + jnp.dot(a[e * gt:(e + 1) * gt], vp[rows, hv],
                                                          preferred_element_type=F32)
        s = tg // 2
        while s >= gt:
            qd, kd = ops[s]
            for blk in range(tg // (2 * s)):
                kr = slice(blk * 2 * s, blk * 2 * s + s)
                qr = slice(blk * 2 * s + s, (blk + 1) * 2 * s)
                ab = lax.dot_general(_stack_heads(qd[qr, cols]), kd[kr, cols], nt_dims,
                                     preferred_element_type=F32).astype(BF16)
                for e in range(2):
                    hv = slice(e * VAL_DIM_B, (e + 1) * VAL_DIM_B)
                    contrib = jnp.dot(ab[e * s:(e + 1) * s], vp[kr, hv], preferred_element_type=F32)
                    for gi in range(s // gt):
                        g = (blk * 2 * s + s) // gt + gi
                        piece[e][g] = piece[e][g] + contrib[gi * gt:(gi + 1) * gt]
            s //= 2
        for e in range(2):
            outs.append(jnp.concatenate(piece[e], axis=0) if len(piece[e]) > 1 else piece[e][0])
    return jnp.concatenate(outs, axis=1)


def _gla_kernel(chunk, chunks_per_tile, has_init, *refs):
    if has_init:
        q_ref, k_ref, v_ref, la_ref, seg_ref, mask_ref, s0_ref, o_ref, sout_ref, state = refs
    else:
        q_ref, k_ref, v_ref, la_ref, seg_ref, mask_ref, o_ref, sout_ref, state = refs
    t = pl.program_id(1)

    @pl.when(t == 0)
    def _():
        state[...] = s0_ref[...] if has_init else jnp.zeros_like(state)

    o_ref[...] = _gla_tile(q_ref[...], k_ref[...], v_ref[...], la_ref[...], seg_ref, mask_ref, state,
                           chunk, chunks_per_tile)

    @pl.when(t == pl.num_programs(1) - 1)
    def _():
        sout_ref[...] = state[...]


def _gla(q, k, v, la, s0, n_batch, seq, chunk, chunks_per_tile):
    tg = chunk * chunks_per_tile
    nt = seq // tg
    seg = jnp.asarray(_segment_matrix(chunk), BF16)
    masks = jnp.asarray(_level_masks(min(tg, LANES)), F32)
    row = lambda w: pl.BlockSpec((tg, w), lambda b, t: (b * nt + t, 0))
    st = pl.BlockSpec((None, 2, 2 * VAL_DIM_B, LANES), lambda b, t: (b, 0, 0, 0))
    in_specs = [row(WIDTH_BK), row(WIDTH_BK), row(WIDTH_BV), row(WIDTH_BK), _const_spec(seg.shape),
                _const_spec(masks.shape)]
    args = [q, k, v, la, seg, masks]
    if s0 is not None:
        in_specs.append(st)
        args.append(s0)
    return pl.pallas_call(
        functools.partial(_gla_kernel, chunk, chunks_per_tile, s0 is not None),
        grid=(n_batch, nt),
        in_specs=in_specs,
        out_specs=(row(WIDTH_BV), st),
        out_shape=(jax.ShapeDtypeStruct((n_batch * seq, WIDTH_BV), F32),
                   jax.ShapeDtypeStruct((n_batch, 2, 2 * VAL_DIM_B, LANES), F32)),
        scratch_shapes=[pltpu.VMEM((2, 2 * VAL_DIM_B, LANES), F32)],
        compiler_params=_params(2, VMEM_LIMIT),
        name="gla_chunk%d" % chunk,
    )(*args)


def _state_to_pairs(s):
    n = s.shape[0]
    st = jnp.swapaxes(s, -1, -2).reshape(n, 2, 2, VAL_DIM_B, KEY_DIM_B)
    z = jnp.zeros_like(st[:, :, 0])
    top = jnp.concatenate([st[:, :, 0], z], axis=-1)
    bot = jnp.concatenate([z, st[:, :, 1]], axis=-1)
    return jnp.concatenate([top, bot], axis=-2)


def _pairs_to_state(sp):
    n = sp.shape[0]
    h0 = sp[:, :, :VAL_DIM_B, :KEY_DIM_B]
    h1 = sp[:, :, VAL_DIM_B:, KEY_DIM_B:]
    st = jnp.stack([h0, h1], axis=2).reshape(n, N_HEADS_B, VAL_DIM_B, KEY_DIM_B)
    return jnp.swapaxes(st, -1, -2)


def _mem_kv_kernel(m_ref, g_ref, wk_ref, wv_ref, k_ref, v_ref):
    m = _rms(m_ref[...], g_ref[...]).astype(BF16)
    k_ref[...] = jnp.dot(m, wk_ref[...], preferred_element_type=F32)
    v_ref[...] = jnp.dot(m, wv_ref[...], preferred_element_type=F32)


def _mem_kv(mem, g, wk, wv):
    n = mem.shape[0]
    tm = N_MEM
    row = pl.BlockSpec((tm, D_MODEL), lambda i: (i, 0))
    return pl.pallas_call(
        _mem_kv_kernel,
        grid=(n // tm,),
        in_specs=[row, _const_spec((1, D_MODEL)), _const_spec((D_MODEL, D_MODEL)), _const_spec((D_MODEL, D_MODEL))],
        out_specs=(row, row),
        out_shape=(jax.ShapeDtypeStruct((n, D_MODEL), F32), jax.ShapeDtypeStruct((n, D_MODEL), F32)),
        compiler_params=_params(1, VMEM_LIMIT),
        name="mem_kv",
    )(mem, g, wk, wv)


def _stack_kernel(n_seg, x_ref, oa_ref, ob_ref, r_ref, mk_ref, mv_ref,
                  g_gla_ref, wo_ref, g_post_mix_ref, g_pre_mem_ref, wq_ref, wmo_ref, g_post_mem_ref,
                  g_pre_ffn_ref, wg_ref, wu_ref, wd_ref, g_post_ffn_ref, y_ref):
    tm = x_ref.shape[0]
    ob = ob_ref[...]
    normed = []
    for h in range(N_HEADS_B):
        seg = ob[:, h * VAL_DIM_B:(h + 1) * VAL_DIM_B]
        normed.append(seg * lax.rsqrt(jnp.mean(seg * seg, axis=-1, keepdims=True) + EPS))
    yb = jnp.concatenate(normed, axis=1) * g_gla_ref[...] * _silu(r_ref[...])
    mix = (jnp.dot(oa_ref[...], wo_ref[0:WIDTH_A, :], preferred_element_type=F32)
           + jnp.dot(yb.astype(BF16), wo_ref[WIDTH_A:, :], preferred_element_type=F32))
    x1 = x_ref[...] + _rms(mix, g_post_mix_ref[...])

    hq = _rms(x1, g_pre_mem_ref[...]).astype(BF16)
    q = (jnp.dot(hq, wq_ref[...], preferred_element_type=F32) * (HEAD_DIM_MEM ** -0.5)).astype(BF16)
    rows_per_seg = tm // n_seg
    seg_outs = []
    for sg in range(n_seg):
        head_outs = []
        for h in range(N_HEADS_MEM):
            cols = slice(h * HEAD_DIM_MEM, (h + 1) * HEAD_DIM_MEM)
            qh = q[sg * rows_per_seg:(sg + 1) * rows_per_seg, cols]
            kh = mk_ref[sg * N_MEM:(sg + 1) * N_MEM, cols]
            vh = mv_ref[sg * N_MEM:(sg + 1) * N_MEM, cols]
            s = lax.dot_general(qh, kh, (((1,), (1,)), ((), ())), preferred_element_type=F32)
            p = jnp.exp(s - jnp.max(s, axis=-1, keepdims=True))
            l = jnp.sum(p, axis=-1, keepdims=True)
            head_outs.append(jnp.dot(p.astype(BF16), vh, preferred_element_type=F32) / l)
        seg_outs.append(jnp.concatenate(head_outs, axis=1))
    o = jnp.concatenate(seg_outs, axis=0) if n_seg > 1 else seg_outs[0]
    att = jnp.dot(o.astype(BF16), wmo_ref[...], preferred_element_type=F32)
    x2 = x1 + _rms(att, g_post_mem_ref[...])

    hf = _rms(x2, g_pre_ffn_ref[...]).astype(BF16)
    f = jnp.zeros((tm, D_MODEL), F32)
    for c in range(D_FF // FF_BLOCK):
        cols = slice(c * FF_BLOCK, (c + 1) * FF_BLOCK)
        gate = jnp.dot(hf, wg_ref[:, cols], preferred_element_type=F32)
        up = jnp.dot(hf, wu_ref[:, cols], preferred_element_type=F32)
        f = f + jnp.dot((_silu(gate) * up).astype(BF16), wd_ref[cols, :], preferred_element_type=F32)
    y_ref[...] = x2 + _rms(f, g_post_ffn_ref[...])


def _stack(x, oa, ob, r, mk, mv, weights, tm, n_seg, tiles_per_mem_block):
    ntok = x.shape[0]
    row = lambda w: pl.BlockSpec((tm, w), lambda i: (i, 0))
    mem = pl.BlockSpec((n_seg * N_MEM, D_MODEL), lambda i: (i // tiles_per_mem_block, 0))
    w_specs = [_const_spec(w.shape) for w in weights]
    return pl.pallas_call(
        functools.partial(_stack_kernel, n_seg),
        grid=(ntok // tm,),
        in_specs=[row(D_MODEL), row(WIDTH_A), row(WIDTH_BV), row(WIDTH_BV), mem, mem] + w_specs,
        out_specs=row(D_MODEL),
        out_shape=jax.ShapeDtypeStruct((ntok, D_MODEL), F32),
        compiler_params=_params(1, VMEM_LIMIT),
        name="token_stack",
    )(x, oa, ob, r, mk, mv, *weights)


def _bias_base(table, width):
    u = np.arange(width)
    idx = np.where(u < (BAND_CHUNKS + 1) * CHUNK, np.clip(A_WINDOW - u, -REL_CLIP, REL_CLIP) + REL_CLIP, 2 * REL_CLIP)
    return table[:, idx]


def kernel(x_prompt, x_sample, mem_prompt, cache_a_k, cache_a_v, state_gla, cache_mem_k, cache_mem_v,
           g_pre_mix, w_in, rel_bias, w_alpha2, b_alpha, g_gla_out, w_o, g_post_mix,
           g_pre_mem, g_mem, w_mq, w_mk, w_mv, w_mo, g_post_mem,
           g_pre_ffn, w_ffn_gate, w_ffn_up, w_ffn_down, g_post_ffn):
    depth = w_in.shape[0]
    assert depth == 1
    l = 0
    n_p, seq, _ = x_prompt.shape
    n_s, seq_s, _ = x_sample.shape
    vec = lambda g: g[l].reshape(1, -1)

    wi = w_in[l]
    w_cat = jnp.concatenate(
        [wi[:, :2560], wi[:, 2576:3088], wi[:, 2560:2576], jnp.zeros((D_MODEL, W_CAT - 3088), F32)], axis=1).astype(BF16)
    wa2 = jnp.concatenate([w_alpha2[l], jnp.zeros((LANES - GATE_RANK, WIDTH_BK), F32)], axis=0).astype(BF16)
    base = _bias_base(rel_bias[l], BIAS_W)
    stack_w = [vec(g_gla_out), w_o[l].astype(BF16), vec(g_post_mix), vec(g_pre_mem), w_mq[l].astype(BF16),
               w_mo[l].astype(BF16), vec(g_post_mem), vec(g_pre_ffn), w_ffn_gate[l].astype(BF16),
               w_ffn_up[l].astype(BF16), w_ffn_down[l].astype(BF16), vec(g_post_ffn)]

    xp = x_prompt.reshape(n_p * seq, D_MODEL)
    tm_p = 512
    qa, ka, va, qb, kb, vb, r, la, k_tail, v_tail = _proj(xp, vec(g_pre_mix), w_cat, wa2, vec(b_alpha), tm_p, seq // tm_p)
    oa = _attn_prompt(qa, ka, va, _bias_base(rel_bias[l], KEYS), n_p, seq)
    ob, sp = _gla(qb, kb, vb, la, None, n_p, seq, CHUNK, 4)
    mk, mv = _mem_kv(mem_prompt.reshape(n_p * N_MEM, D_MODEL), vec(g_mem), w_mk[l].astype(BF16), w_mv[l].astype(BF16))
    tm = 512
    yp = _stack(xp, oa, ob, r, mk.astype(BF16), mv.astype(BF16), stack_w, tm, 1, seq // tm)

    xs = x_sample.reshape(n_s * seq_s, D_MODEL)
    ntok_s = n_s * seq_s
    qa_s, ka_s, va_s, qb_s, kb_s, vb_s, r_s, la_s, k_new, v_new = _proj(
        xs, vec(g_pre_mix), w_cat, wa2, vec(b_alpha), ntok_s, 1)
    n_cache = cache_a_k.shape[2]
    oa_s = _attn_sample(qa_s, ka_s, va_s,
                        cache_a_k[l].reshape(n_s * n_cache, WIDTH_A).astype(BF16),
                        cache_a_v[l].reshape(n_s * n_cache, WIDTH_A).astype(BF16), base, n_s, seq_s)
    ob_s, ss = _gla(qb_s, kb_s, vb_s, la_s, _state_to_pairs(state_gla[l]), n_s, seq_s, seq_s, 1)
    seg_s = 8
    tm_s = seg_s * seq_s
    ys = _stack(xs, oa_s, ob_s, r_s,
                cache_mem_k[l].reshape(n_s * N_MEM, D_MODEL).astype(BF16),
                cache_mem_v[l].reshape(n_s * N_MEM, D_MODEL).astype(BF16), stack_w, tm_s, seg_s, 1)

    keep = min(A_WINDOW, seq)
    heads = lambda z, n, t: z.reshape(1, n, t, N_HEADS_A, HEAD_DIM_A)
    return (yp.reshape(n_p, seq, D_MODEL), ys.reshape(n_s, seq_s, D_MODEL),
            heads(k_tail, n_p, keep), heads(v_tail, n_p, keep), _pairs_to_state(sp)[None],
            mk.reshape(1, n_p, N_MEM, N_HEADS_MEM, HEAD_DIM_MEM), mv.reshape(1, n_p, N_MEM, N_HEADS_MEM, HEAD_DIM_MEM),
            heads(k_new, n_s, seq_s), heads(v_new, n_s, seq_s), _pairs_to_state(ss)[None])
```

```python
import functools
import math

import jax
import jax.numpy as jnp
import numpy as np
from jax import lax
from jax.experimental import pallas as pl
from jax.experimental.pallas import tpu as pltpu

F32 = jnp.float32
BF16 = jnp.bfloat16

D_MODEL = 1024
CHUNK = 64
BAND_CHUNKS = 8
A_WINDOW = BAND_CHUNKS * CHUNK
N_HEADS_A = 8
HEAD_DIM_A = 64
WIDTH_A = 512
REL_CLIP = 256
N_HEADS_B = 4
KEY_DIM_B = 64
VAL_DIM_B = 128
WIDTH_BK = 256
WIDTH_BV = 512
GATE_RANK = 16
GATE_TAU = 16.0
N_MEM = 256
N_HEADS_MEM = 4
HEAD_DIM_MEM = 256
D_FF = 2816
EPS = 1e-6
NEG_BIG = -1e30

LANES = 128
BIAS_W = 640
FF_BLOCK = 256
VMEM_LIMIT = 56 * 1024 * 1024

OFF_QA, OFF_KA, OFF_VA, OFF_QB, OFF_KB, OFF_VB, OFF_R, OFF_G, W_CAT = 0, 512, 1024, 1536, 1792, 2048, 2560, 3072, 3200


def _rms(x, g):
    return x * lax.rsqrt(jnp.mean(x * x, axis=-1, keepdims=True) + EPS) * g


def _silu(x):
    return x / (1.0 + jnp.exp(-x))


def _log_sigmoid(z):
    return jnp.minimum(z, 0.0) - jnp.log(1.0 + jnp.exp(-jnp.abs(z)))


def _const_spec(shape):
    nd = len(shape)
    return pl.BlockSpec(shape, lambda *_: (0,) * nd, pipeline_mode=pl.Buffered(1))


def _params(n_axes, vmem=None):
    return pltpu.CompilerParams(dimension_semantics=("arbitrary",) * n_axes, vmem_limit_bytes=vmem)


def _proj_kernel(tiles_per_seq, x_ref, g_ref, w_ref, wa2_ref, ba_ref,
                 qa_ref, ka_ref, va_ref, qb_ref, kb_ref, vb_ref, r_ref, la_ref, kt_ref, vt_ref):
    h = _rms(x_ref[...], g_ref[...]).astype(BF16)

    def proj(lo, hi):
        return jnp.dot(h, w_ref[:, lo:hi], preferred_element_type=F32)

    qa_ref[...] = (proj(OFF_QA, OFF_KA) * (HEAD_DIM_A ** -0.5)).astype(BF16)
    ka = proj(OFF_KA, OFF_VA)
    va = proj(OFF_VA, OFF_QB)
    ka_ref[...] = ka.astype(BF16)
    va_ref[...] = va.astype(BF16)
    qb_ref[...] = proj(OFF_QB, OFF_KB) * (KEY_DIM_B ** -0.5)
    kb_ref[...] = proj(OFF_KB, OFF_VB)
    vb_ref[...] = proj(OFF_VB, OFF_R).astype(BF16)
    r_ref[...] = proj(OFF_R, OFF_G)
    g_low = proj(OFF_G, W_CAT).astype(BF16)
    z = jnp.dot(g_low, wa2_ref[...], preferred_element_type=F32) + ba_ref[...]
    la_ref[...] = _log_sigmoid(z) * (1.0 / GATE_TAU)

    @pl.when(pl.program_id(0) % tiles_per_seq == tiles_per_seq - 1)
    def _():
        kt_ref[...] = ka
        vt_ref[...] = va


def _proj(x, g, w_cat, wa2, ba, tm, tiles_per_seq):
    ntok = x.shape[0]
    nt = ntok // tm
    nseq = nt // tiles_per_seq
    row = lambda w: pl.BlockSpec((tm, w), lambda i: (i, 0))
    tail = pl.BlockSpec((tm, WIDTH_A), lambda i: (i // tiles_per_seq, 0))
    out_shape = (
        jax.ShapeDtypeStruct((ntok, WIDTH_A), BF16),
        jax.ShapeDtypeStruct((ntok, WIDTH_A), BF16),
        jax.ShapeDtypeStruct((ntok, WIDTH_A), BF16),
        jax.ShapeDtypeStruct((ntok, WIDTH_BK), F32),
        jax.ShapeDtypeStruct((ntok, WIDTH_BK), F32),
        jax.ShapeDtypeStruct((ntok, WIDTH_BV), BF16),
        jax.ShapeDtypeStruct((ntok, WIDTH_BV), F32),
        jax.ShapeDtypeStruct((ntok, WIDTH_BK), F32),
        jax.ShapeDtypeStruct((nseq * tm, WIDTH_A), F32),
        jax.ShapeDtypeStruct((nseq * tm, WIDTH_A), F32),
    )
    return pl.pallas_call(
        functools.partial(_proj_kernel, tiles_per_seq),
        grid=(nt,),
        in_specs=[row(D_MODEL), _const_spec((1, D_MODEL)), _const_spec((D_MODEL, W_CAT)),
                  _const_spec((LANES, WIDTH_BK)), _const_spec((1, WIDTH_BK))],
        out_specs=(row(WIDTH_A), row(WIDTH_A), row(WIDTH_A), row(WIDTH_BK), row(WIDTH_BK),
                   row(WIDTH_BV), row(WIDTH_BV), row(WIDTH_BK), tail, tail),
        out_shape=out_shape,
        compiler_params=_params(1, VMEM_LIMIT),
        name="proj",
    )(x, g, w_cat, wa2, ba)


def _stack_heads(x_pair):
    low_half = lax.broadcasted_iota(jnp.int32, x_pair.shape, 1) < HEAD_DIM_A
    zero = jnp.zeros_like(x_pair)
    return jnp.concatenate([jnp.where(low_half, x_pair, zero), jnp.where(low_half, zero, x_pair)], axis=0)


TQ = 4 * CHUNK
KEYS = 3 * TQ
SOFT_W = 5 * LANES
assert 2 * TQ == A_WINDOW and SOFT_W >= (BAND_CHUNKS + 1) * CHUNK + CHUNK


def _build_band_bias(base_ref, bm_ref):
    n_band = (BAND_CHUNKS + 1) * CHUNK
    col = lax.broadcasted_iota(jnp.int32, (CHUNK, KEYS), 1)
    for h in range(N_HEADS_A):
        rows = jnp.broadcast_to(base_ref[h:h + 1, :], (CHUNK, KEYS))
        for qc in range(TQ // CHUNK):
            toeplitz = pltpu.roll(rows, qc * CHUNK, 1, stride=1, stride_axis=0)
            in_band = (col >= qc * CHUNK) & (col < qc * CHUNK + n_band)
            r0 = (qc * 2 + h % 2) * CHUNK
            bm_ref[h // 2, r0:r0 + CHUNK, :] = jnp.where(in_band, toeplitz, NEG_BIG)


def _band_tile(q_ref, k_refs, v_refs, bm_ref, o_ref, start_penalty):
    n_qc = TQ // CHUNK
    low_half = lax.broadcasted_iota(jnp.int32, (CHUNK, LANES), 1) < HEAD_DIM_A
    nt_dims = (((1,), (1,)), ((), ()))
    pairs = [slice(hp * LANES, (hp + 1) * LANES) for hp in range(N_HEADS_A // 2)]
    scores = []
    for cols in pairs:
        qs = jnp.concatenate([_stack_heads(q_ref[qc * CHUNK:(qc + 1) * CHUNK, cols]) for qc in range(n_qc)],
                             axis=0)
        scores.append(jnp.concatenate([lax.dot_general(qs, k[:, cols], nt_dims, preferred_element_type=F32)
                                       for k in k_refs], axis=1))
    probs, inv_l = [], []
    for hp, s in enumerate(scores):
        pm = []
        for qc in range(n_qc):
            rows = slice(qc * 2 * CHUNK, (qc + 1) * 2 * CHUNK)
            c0 = 0 if (qc + 1) * CHUNK + A_WINDOW <= SOFT_W else KEYS - SOFT_W
            sq = s[rows, c0:c0 + SOFT_W] + bm_ref[hp, rows, c0:c0 + SOFT_W]
            if start_penalty is not None:
                sq = sq + start_penalty[:, c0:c0 + SOFT_W]
            p = jnp.exp(sq - jnp.max(sq, axis=-1, keepdims=True))
            inv_l.append(1.0 / jnp.sum(p, axis=-1, keepdims=True))
            pad = jnp.zeros((2 * CHUNK, KEYS - SOFT_W), BF16)
            pm.append(jnp.concatenate([p.astype(BF16), pad] if c0 == 0 else [pad, p.astype(BF16)], axis=1))
        probs.append(jnp.concatenate(pm, axis=0))
    for hp, cols in enumerate(pairs):
        out = sum(jnp.dot(probs[hp][:, i * TQ:(i + 1) * TQ], v[:, cols], preferred_element_type=F32)
                  for i, v in enumerate(v_refs))
        for qc in range(n_qc):
            r0, inv = qc * 2 * CHUNK, inv_l[hp * n_qc + qc]
            o0 = out[r0:r0 + CHUNK] * inv[0:CHUNK]
            o1 = out[r0 + CHUNK:r0 + 2 * CHUNK] * inv[CHUNK:2 * CHUNK]
            o_ref[qc * CHUNK:(qc + 1) * CHUNK, cols] = jnp.where(low_half, o0, o1).astype(o_ref.dtype)


def _attn_prompt_kernel(q_ref, k2_ref, k1_ref, k0_ref, v2_ref, v1_ref, v0_ref, base_ref, o_ref, bm):
    t = pl.program_id(1)

    @pl.when((pl.program_id(0) == 0) & (t == 0))
    def _():
        _build_band_bias(base_ref, bm)

    k_refs = (k2_ref, k1_ref, k0_ref)
    v_refs = (v2_ref, v1_ref, v0_ref)

    @pl.when(t >= 2)
    def _():
        _band_tile(q_ref, k_refs, v_refs, bm, o_ref, None)

    @pl.when(t < 2)
    def _():
        col = lax.broadcasted_iota(jnp.int32, (1, KEYS), 1)
        penalty = jnp.where(col < (2 - t) * TQ, NEG_BIG, 0.0)
        _band_tile(q_ref, k_refs, v_refs, bm, o_ref, penalty)


def _attn_prompt(q, k, v, base, n_batch, seq):
    nt = seq // TQ
    blk = lambda back: pl.BlockSpec((TQ, WIDTH_A), lambda b, t: (b * nt + jnp.maximum(t - back, 0), 0))
    return pl.pallas_call(
        _attn_prompt_kernel,
        grid=(n_batch, nt),
        in_specs=[blk(0), blk(2), blk(1), blk(0), blk(2), blk(1), blk(0), _const_spec((N_HEADS_A, KEYS))],
        out_specs=blk(0),
        out_shape=jax.ShapeDtypeStruct((n_batch * seq, WIDTH_A), BF16),
        scratch_shapes=[pltpu.VMEM((N_HEADS_A // 2, 2 * TQ, KEYS), F32)],
        compiler_params=_params(2, VMEM_LIMIT),
        name="band_attn_prompt",
    )(q, k, k, k, v, v, v, base)


def _attn_sample_kernel(n_new, q_ref, kn_ref, vn_ref, kc_ref, vc_ref, base_ref, o_ref, bias):
    @pl.when(pl.program_id(0) == 0)
    def _():
        for h in range(N_HEADS_A):
            rows = jnp.broadcast_to(base_ref[h:h + 1, :], (n_new, BIAS_W))
            bias[h // 2, (h % 2) * n_new:(h % 2 + 1) * n_new, :] = pltpu.roll(rows, 0, 1, stride=1, stride_axis=0)

    n_cache = kc_ref.shape[0]
    nt_dims = (((1,), (1,)), ((), ()))
    low_half = lax.broadcasted_iota(jnp.int32, (n_new, LANES), 1) < HEAD_DIM_A
    pairs = [slice(hp * LANES, (hp + 1) * LANES) for hp in range(N_HEADS_A // 2)]
    scores = []
    for hp, cols in enumerate(pairs):
        qs = _stack_heads(q_ref[:, cols])
        s = jnp.concatenate([lax.dot_general(qs, kc_ref[:, cols], nt_dims, preferred_element_type=F32),
                             lax.dot_general(qs, kn_ref[:, cols], nt_dims, preferred_element_type=F32)], axis=1)
        scores.append(s + bias[hp, :, 0:n_cache + n_new])
    probs, inv_l = [], []
    for s in scores:
        p = jnp.exp(s - jnp.max(s, axis=-1, keepdims=True))
        inv_l.append(1.0 / jnp.sum(p, axis=-1, keepdims=True))
        probs.append(p.astype(BF16))
    for hp, cols in enumerate(pairs):
        p = probs[hp]
        out = (jnp.dot(p[:, 0:n_cache], vc_ref[:, cols], preferred_element_type=F32)
               + jnp.dot(p[:, n_cache:], vn_ref[:, cols], preferred_element_type=F32)) * inv_l[hp]
        o_ref[:, cols] = jnp.where(low_half, out[0:n_new], out[n_new:]).astype(o_ref.dtype)


def _attn_sample(q, k_new, v_new, k_cache, v_cache, base, n_batch, n_new):
    n_cache = k_cache.shape[0] // n_batch
    assert n_cache == A_WINDOW and n_new <= CHUNK
    new = pl.BlockSpec((n_new, WIDTH_A), lambda b: (b, 0))
    old = pl.BlockSpec((n_cache, WIDTH_A), lambda b: (b, 0))
    return pl.pallas_call(
        functools.partial(_attn_sample_kernel, n_new),
        grid=(n_batch,),
        in_specs=[new, new, new, old, old, _const_spec((N_HEADS_A, BIAS_W))],
        out_specs=new,
        out_shape=jax.ShapeDtypeStruct((n_batch * n_new, WIDTH_A), BF16),
        scratch_shapes=[pltpu.VMEM((N_HEADS_A // 2, 2 * n_new, BIAS_W), F32)],
        compiler_params=_params(1, VMEM_LIMIT),
        name="band_attn_sample",
    )(q, k_new, v_new, k_cache, v_cache, base)


def _gla_levels(chunk):
    return [chunk >> i for i in range(int(math.log2(chunk)) + 1)]


def _segment_matrix(chunk):
    i = np.arange(chunk)[:, None]
    t = np.arange(chunk)[None, :]
    blocks = []
    for li, s in enumerate(_gla_levels(chunk)):
        start = (i // s) * s
        f_rows = (t >= start) & (t <= i)
        r_rows = (t > i) & (t <= start + s - 1)
        if li == 0:
            blocks += [f_rows, r_rows]
        else:
            blocks.append(np.where((i // s) % 2 == 1, f_rows, r_rows))
    seg = np.concatenate(blocks, axis=0).astype(np.float32)
    return np.concatenate([seg, seg], axis=1)


def _level_masks(group_tokens):
    tg = group_tokens
    i = (np.arange(2 * tg) % tg)[:, None]
    j = np.arange(tg)[None, :]
    masks = [i == j]
    s = tg // 2
    while s >= 1:
        masks.append(((i // s) % 2 == 1) & (j // s == i // s - 1))
        s //= 2
    return np.stack(masks).astype(np.float32)


def _gla_tile(q, k, v, la, seg_ref, mask_ref, state_ref, chunk, n_chunks):
    n, tg = n_chunks, chunk * n_chunks
    levels = _gla_levels(chunk)
    la_hi = la.astype(BF16)
    la_lo = (la - la_hi.astype(F32)).astype(BF16)
    split = jnp.concatenate(
        [jnp.concatenate([la_hi[c * chunk:(c + 1) * chunk], la_lo[c * chunk:(c + 1) * chunk]], axis=0)
         for c in range(n)], axis=1)
    sums = jnp.dot(seg_ref[...], split, preferred_element_type=F32)

    def seg_sum(block):
        rows = slice(block * chunk, (block + 1) * chunk)
        return jnp.minimum(jnp.concatenate(
            [sums[rows, c * WIDTH_BK:(c + 1) * WIDTH_BK] for c in range(n)], axis=0), 0.0)

    fwd0, rev0 = seg_sum(0), seg_sum(1)
    from_start, to_end = jnp.exp(fwd0), jnp.exp(rev0)
    total = [fwd0[(c + 1) * chunk - 1:(c + 1) * chunk] for c in range(n)]

    def chunk_sum(cs):
        cs = list(cs)
        return (sum(total[c] for c in cs[1:]) + total[cs[0]]) if cs else None

    def extend(base, offsets):
        pieces = []
        for c in range(n):
            x = base[c * chunk:(c + 1) * chunk]
            pieces.append(x if offsets[c] is None else x * jnp.exp(offsets[c]))
        return jnp.concatenate(pieces, axis=0) if n > 1 else pieces[0]

    q_state = (q * extend(from_start, [chunk_sum(range(0, c)) for c in range(n)])).astype(BF16)
    k_state = (k * extend(to_end, [chunk_sum(range(c + 1, n)) for c in range(n)])).astype(BF16)
    decay_tile = jnp.exp(chunk_sum(range(n)))
    ops = {}
    s = tg // 2
    while s >= chunk and n > 1:
        per = s // chunk
        offsets, bases = [], []
        for c in range(n):
            sg = c // per
            offsets.append(chunk_sum(range(sg * per, c)) if sg % 2 else chunk_sum(range(c + 1, (sg + 1) * per)))
            bases.append((from_start if sg % 2 else to_end)[c * chunk:(c + 1) * chunk])
        w = extend(jnp.concatenate(bases, axis=0), offsets)
        ops[s] = ((q * w).astype(BF16), (k * w).astype(BF16))
        s //= 2
    for li in range(1, len(levels)):
        w = jnp.exp(seg_sum(li + 1))
        ops[levels[li]] = ((q * w).astype(BF16), (k * w).astype(BF16))
    q16, k16 = q.astype(BF16), k.astype(BF16)

    gt = mask_ref.shape[2]
    own = (lax.broadcasted_iota(jnp.int32, (2 * VAL_DIM_B, LANES), 0) < VAL_DIM_B) == \
          (lax.broadcasted_iota(jnp.int32, (2 * VAL_DIM_B, LANES), 1) < KEY_DIM_B)
    nt_dims = (((1,), (1,)), ((), ()))
    tn_dims = (((0,), (0,)), ((), ()))
    n_pairs, n_groups = N_HEADS_B // 2, tg // gt
    pair_cols = [slice(p * LANES, (p + 1) * LANES) for p in range(n_pairs)]
    pair_v = [v[:, p * 2 * VAL_DIM_B:(p + 1) * 2 * VAL_DIM_B] for p in range(n_pairs)]
    head_v = lambda e: slice(e * VAL_DIM_B, (e + 1) * VAL_DIM_B)
    o_state = []
    for p, cols in enumerate(pair_cols):
        st = state_ref[p]
        o_state.append(lax.dot_general(q_state[:, cols], st.astype(BF16), nt_dims, preferred_element_type=F32))
        upd = lax.dot_general(pair_v[p], k_state[:, cols], tn_dims, preferred_element_type=F32)
        state_ref[p] = st * decay_tile[:, cols] + jnp.where(own, upd, 0.0)
    blocks = []
    for p, cols in enumerate(pair_cols):
        for g in range(n_groups):
            rows = slice(g * gt, (g + 1) * gt)
            a = mask_ref[0] * lax.dot_general(_stack_heads(q16[rows, cols]), k16[rows, cols], nt_dims,
                                              preferred_element_type=F32)
            m, s = 1, gt // 2
            while s >= 1:
                qd, kd = ops[s]
                a = a + mask_ref[m] * lax.dot_general(_stack_heads(qd[rows, cols]), kd[rows, cols], nt_dims,
                                                      preferred_element_type=F32)
                m, s = m + 1, s // 2
            blocks.append((p, g * gt, g * gt, a.astype(BF16)))
        s = tg // 2
        while s >= gt:
            qd, kd = ops[s]
            for blk in range(tg // (2 * s)):
                k0, q0 = blk * 2 * s, blk * 2 * s + s
                ab = lax.dot_general(_stack_heads(qd[q0:q0 + s, cols]), kd[k0:k0 + s, cols], nt_dims,
                                     preferred_element_type=F32)
                blocks.append((p, q0, k0, ab.astype(BF16)))
            s //= 2
    piece = [[[o_state[p][g * gt:(g + 1) * gt, head_v(e)] for g in range(n_groups)] for e in range(2)]
             for p in range(n_pairs)]
    for p, q0, k0, a in blocks:
        nq, nk = a.shape[0] // 2, a.shape[1]
        for e in range(2):
            contrib = jnp.dot(a[e * nq:(e + 1) * nq], pair_v[p][k0:k0 + nk, head_v(e)], preferred_element_type=F32)
            for gi in range(nq // gt):
                g = q0 // gt + gi
                piece[p][e][g] = piece[p][e][g] + contrib[gi * gt:(gi + 1) * gt]
    outs = [jnp.concatenate(piece[p][e], axis=0) if n_groups > 1 else piece[p][e][0]
            for p in range(n_pairs) for e in range(2)]
    return jnp.concatenate(outs, axis=1)


def _gla_kernel(chunk, chunks_per_tile, has_init, *refs):
    if has_init:
        q_ref, k_ref, v_ref, la_ref, seg_ref, mask_ref, s0_ref, o_ref, sout_ref, state = refs
    else:
        q_ref, k_ref, v_ref, la_ref, seg_ref, mask_ref, o_ref, sout_ref, state = refs
    t = pl.program_id(1)

    @pl.when(t == 0)
    def _():
        state[...] = s0_ref[...] if has_init else jnp.zeros_like(state)

    o_ref[...] = _gla_tile(q_ref[...], k_ref[...], v_ref[...], la_ref[...], seg_ref, mask_ref, state,
                           chunk, chunks_per_tile)

    @pl.when(t == pl.num_programs(1) - 1)
    def _():
        sout_ref[...] = state[...]


def _gla(q, k, v, la, s0, n_batch, seq, chunk, chunks_per_tile):
    tg = chunk * chunks_per_tile
    nt = seq // tg
    seg = jnp.asarray(_segment_matrix(chunk), BF16)
    masks = jnp.asarray(_level_masks(min(tg, LANES)), F32)
    row = lambda w: pl.BlockSpec((tg, w), lambda b, t: (b * nt + t, 0))
    st = pl.BlockSpec((None, 2, 2 * VAL_DIM_B, LANES), lambda b, t: (b, 0, 0, 0))
    in_specs = [row(WIDTH_BK), row(WIDTH_BK), row(WIDTH_BV), row(WIDTH_BK), _const_spec(seg.shape),
                _const_spec(masks.shape)]
    args = [q, k, v, la, seg, masks]
    if s0 is not None:
        in_specs.append(st)
        args.append(s0)
    return pl.pallas_call(
        functools.partial(_gla_kernel, chunk, chunks_per_tile, s0 is not None),
        grid=(n_batch, nt),
        in_specs=in_specs,
        out_specs=(row(WIDTH_BV), st),
        out_shape=(jax.ShapeDtypeStruct((n_batch * seq, WIDTH_BV), F32),
                   jax.ShapeDtypeStruct((n_batch, 2, 2 * VAL_DIM_B, LANES), F32)),
        scratch_shapes=[pltpu.VMEM((2, 2 * VAL_DIM_B, LANES), F32)],
        compiler_params=_params(2, VMEM_LIMIT),
        name="gla_chunk%d" % chunk,
    )(*args)


def _state_to_pairs(s):
    n = s.shape[0]
    st = jnp.swapaxes(s, -1, -2).reshape(n, 2, 2, VAL_DIM_B, KEY_DIM_B)
    z = jnp.zeros_like(st[:, :, 0])
    top = jnp.concatenate([st[:, :, 0], z], axis=-1)
    bot = jnp.concatenate([z, st[:, :, 1]], axis=-1)
    return jnp.concatenate([top, bot], axis=-2)


def _pairs_to_state(sp):
    n = sp.shape[0]
    h0 = sp[:, :, :VAL_DIM_B, :KEY_DIM_B]
    h1 = sp[:, :, VAL_DIM_B:, KEY_DIM_B:]
    st = jnp.stack([h0, h1], axis=2).reshape(n, N_HEADS_B, VAL_DIM_B, KEY_DIM_B)
    return jnp.swapaxes(st, -1, -2)


def _mem_kv_kernel(m_ref, g_ref, wk_ref, wv_ref, k_ref, v_ref):
    m = _rms(m_ref[...], g_ref[...]).astype(BF16)
    k_ref[...] = jnp.dot(m, wk_ref[...], preferred_element_type=F32)
    v_ref[...] = jnp.dot(m, wv_ref[...], preferred_element_type=F32)


def _mem_kv(mem, g, wk, wv):
    n = mem.shape[0]
    tm = N_MEM
    row = pl.BlockSpec((tm, D_MODEL), lambda i: (i, 0))
    return pl.pallas_call(
        _mem_kv_kernel,
        grid=(n // tm,),
        in_specs=[row, _const_spec((1, D_MODEL)), _const_spec((D_MODEL, D_MODEL)), _const_spec((D_MODEL, D_MODEL))],
        out_specs=(row, row),
        out_shape=(jax.ShapeDtypeStruct((n, D_MODEL), F32), jax.ShapeDtypeStruct((n, D_MODEL), F32)),
        compiler_params=_params(1, VMEM_LIMIT),
        name="mem_kv",
    )(mem, g, wk, wv)


def _stack_kernel(n_seg, x_ref, oa_ref, ob_ref, r_ref, mk_ref, mv_ref,
                  g_gla_ref, wo_ref, g_post_mix_ref, g_pre_mem_ref, wq_ref, wmo_ref, g_post_mem_ref,
                  g_pre_ffn_ref, wg_ref, wu_ref, wd_ref, g_post_ffn_ref, y_ref):
    tm = x_ref.shape[0]
    ob = ob_ref[...]
    normed = []
    for h in range(N_HEADS_B):
        seg = ob[:, h * VAL_DIM_B:(h + 1) * VAL_DIM_B]
        normed.append(seg * lax.rsqrt(jnp.mean(seg * seg, axis=-1, keepdims=True) + EPS))
    yb = jnp.concatenate(normed, axis=1) * g_gla_ref[...] * _silu(r_ref[...])
    mix = (jnp.dot(oa_ref[...], wo_ref[0:WIDTH_A, :], preferred_element_type=F32)
           + jnp.dot(yb.astype(BF16), wo_ref[WIDTH_A:, :], preferred_element_type=F32))
    x1 = x_ref[...] + _rms(mix, g_post_mix_ref[...])

    hq = _rms(x1, g_pre_mem_ref[...]).astype(BF16)
    q = (jnp.dot(hq, wq_ref[...], preferred_element_type=F32) * (HEAD_DIM_MEM ** -0.5)).astype(BF16)
    rows_per_seg = tm // n_seg
    units = [(sg, h) for sg in range(n_seg) for h in range(N_HEADS_MEM)]
    mem_rows = lambda sg: slice(sg * N_MEM, (sg + 1) * N_MEM)
    head_cols = lambda h: slice(h * HEAD_DIM_MEM, (h + 1) * HEAD_DIM_MEM)
    scores = [lax.dot_general(q[sg * rows_per_seg:(sg + 1) * rows_per_seg, head_cols(h)],
                              mk_ref[mem_rows(sg), head_cols(h)], (((1,), (1,)), ((), ())),
                              preferred_element_type=F32) for sg, h in units]
    probs, inv_l = [], []
    for s in scores:
        p = jnp.exp(s - jnp.max(s, axis=-1, keepdims=True))
        inv_l.append(1.0 / jnp.sum(p, axis=-1, keepdims=True))
        probs.append(p.astype(BF16))
    outs = [jnp.dot(probs[u], mv_ref[mem_rows(sg), head_cols(h)], preferred_element_type=F32) * inv_l[u]
            for u, (sg, h) in enumerate(units)]
    seg_outs = [jnp.concatenate(outs[sg * N_HEADS_MEM:(sg + 1) * N_HEADS_MEM], axis=1) for sg in range(n_seg)]
    o = jnp.concatenate(seg_outs, axis=0) if n_seg > 1 else seg_outs[0]
    att = jnp.dot(o.astype(BF16), wmo_ref[...], preferred_element_type=F32)
    x2 = x1 + _rms(att, g_post_mem_ref[...])

    hf = _rms(x2, g_pre_ffn_ref[...]).astype(BF16)
    f = jnp.zeros((tm, D_MODEL), F32)
    act, act_cols = None, None
    for c in range(D_FF // FF_BLOCK):
        cols = slice(c * FF_BLOCK, (c + 1) * FF_BLOCK)
        gate = jnp.dot(hf, wg_ref[:, cols], preferred_element_type=F32)
        up = jnp.dot(hf, wu_ref[:, cols], preferred_element_type=F32)
        if act is not None:
            f = f + jnp.dot(act, wd_ref[act_cols, :], preferred_element_type=F32)
        act, act_cols = (_silu(gate) * up).astype(BF16), cols
    f = f + jnp.dot(act, wd_ref[act_cols, :], preferred_element_type=F32)
    y_ref[...] = x2 + _rms(f, g_post_ffn_ref[...])


def _stack(x, oa, ob, r, mk, mv, weights, tm, n_seg, tiles_per_mem_block):
    ntok = x.shape[0]
    row = lambda w: pl.BlockSpec((tm, w), lambda i: (i, 0))
    mem = pl.BlockSpec((n_seg * N_MEM, D_MODEL), lambda i: (i // tiles_per_mem_block, 0))
    w_specs = [_const_spec(w.shape) for w in weights]
    return pl.pallas_call(
        functools.partial(_stack_kernel, n_seg),
        grid=(ntok // tm,),
        in_specs=[row(D_MODEL), row(WIDTH_A), row(WIDTH_BV), row(WIDTH_BV), mem, mem] + w_specs,
        out_specs=row(D_MODEL),
        out_shape=jax.ShapeDtypeStruct((ntok, D_MODEL), F32),
        compiler_params=_params(1, VMEM_LIMIT),
        name="token_stack",
    )(x, oa, ob, r, mk, mv, *weights)


def _bias_base(table, width):
    u = np.arange(width)
    idx = np.where(u < (BAND_CHUNKS + 1) * CHUNK, np.clip(A_WINDOW - u, -REL_CLIP, REL_CLIP) + REL_CLIP, 2 * REL_CLIP)
    return table[:, idx]


def kernel(x_prompt, x_sample, mem_prompt, cache_a_k, cache_a_v, state_gla, cache_mem_k, cache_mem_v,
           g_pre_mix, w_in, rel_bias, w_alpha2, b_alpha, g_gla_out, w_o, g_post_mix,
           g_pre_mem, g_mem, w_mq, w_mk, w_mv, w_mo, g_post_mem,
           g_pre_ffn, w_ffn_gate, w_ffn_up, w_ffn_down, g_post_ffn):
    depth = w_in.shape[0]
    assert depth == 1
    l = 0
    n_p, seq, _ = x_prompt.shape
    n_s, seq_s, _ = x_sample.shape
    vec = lambda g: g[l].reshape(1, -1)

    wi = w_in[l]
    w_cat = jnp.concatenate(
        [wi[:, :2560], wi[:, 2576:3088], wi[:, 2560:2576], jnp.zeros((D_MODEL, W_CAT - 3088), F32)], axis=1).astype(BF16)
    wa2 = jnp.concatenate([w_alpha2[l], jnp.zeros((LANES - GATE_RANK, WIDTH_BK), F32)], axis=0).astype(BF16)
    base = _bias_base(rel_bias[l], BIAS_W)
    stack_w = [vec(g_gla_out), w_o[l].astype(BF16), vec(g_post_mix), vec(g_pre_mem), w_mq[l].astype(BF16),
               w_mo[l].astype(BF16), vec(g_post_mem), vec(g_pre_ffn), w_ffn_gate[l].astype(BF16),
               w_ffn_up[l].astype(BF16), w_ffn_down[l].astype(BF16), vec(g_post_ffn)]

    xp = x_prompt.reshape(n_p * seq, D_MODEL)
    tm_p = 512
    qa, ka, va, qb, kb, vb, r, la, k_tail, v_tail = _proj(xp, vec(g_pre_mix), w_cat, wa2, vec(b_alpha), tm_p, seq // tm_p)
    oa = _attn_prompt(qa, ka, va, _bias_base(rel_bias[l], KEYS), n_p, seq)
    ob, sp = _gla(qb, kb, vb, la, None, n_p, seq, CHUNK, 4)
    mk, mv = _mem_kv(mem_prompt.reshape(n_p * N_MEM, D_MODEL), vec(g_mem), w_mk[l].astype(BF16), w_mv[l].astype(BF16))
    tm = 512
    yp = _stack(xp, oa, ob, r, mk.astype(BF16), mv.astype(BF16), stack_w, tm, 1, seq // tm)

    xs = x_sample.reshape(n_s * seq_s, D_MODEL)
    ntok_s = n_s * seq_s
    qa_s, ka_s, va_s, qb_s, kb_s, vb_s, r_s, la_s, k_new, v_new = _proj(
        xs, vec(g_pre_mix), w_cat, wa2, vec(b_alpha), ntok_s, 1)
    n_cache = cache_a_k.shape[2]
    oa_s = _attn_sample(qa_s, ka_s, va_s,
                        cache_a_k[l].reshape(n_s * n_cache, WIDTH_A).astype(BF16),
                        cache_a_v[l].reshape(n_s * n_cache, WIDTH_A).astype(BF16), base, n_s, seq_s)
    ob_s, ss = _gla(qb_s, kb_s, vb_s, la_s, _state_to_pairs(state_gla[l]), n_s, seq_s, seq_s, 1)
    seg_s = 8
    tm_s = seg_s * seq_s
    ys = _stack(xs, oa_s, ob_s, r_s,
                cache_mem_k[l].reshape(n_s * N_MEM, D_MODEL).astype(BF16),
                cache_mem_v[l].reshape(n_s * N_MEM, D_MODEL).astype(BF16), stack_w, tm_s, seg_s, 1)

    keep = min(A_WINDOW, seq)
    heads = lambda z, n, t: z.reshape(1, n, t, N_HEADS_A, HEAD_DIM_A)
    return (yp.reshape(n_p, seq, D_MODEL), ys.reshape(n_s, seq_s, D_MODEL),
            heads(k_tail, n_p, keep), heads(v_tail, n_p, keep), _pairs_to_state(sp)[None],
            mk.reshape(1, n_p, N_MEM, N_HEADS_MEM, HEAD_DIM_MEM), mv.reshape(1, n_p, N_MEM, N_HEADS_MEM, HEAD_DIM_MEM),
            heads(k_new, n_s, seq_s), heads(v_new, n_s, seq_s), _pairs_to_state(ss)[None])
```

```python
import functools
import math

import jax
import jax.numpy as jnp
import numpy as np
from jax import lax
from jax.experimental import pallas as pl
from jax.experimental.pallas import tpu as pltpu

F32 = jnp.float32
BF16 = jnp.bfloat16

D_MODEL = 1024
CHUNK = 64
BAND_CHUNKS = 8
A_WINDOW = BAND_CHUNKS * CHUNK
N_HEADS_A = 8
HEAD_DIM_A = 64
WIDTH_A = 512
REL_CLIP = 256
N_HEADS_B = 4
KEY_DIM_B = 64
VAL_DIM_B = 128
WIDTH_BK = 256
WIDTH_BV = 512
GATE_RANK = 16
GATE_TAU = 16.0
N_MEM = 256
N_HEADS_MEM = 4
HEAD_DIM_MEM = 256
D_FF = 2816
EPS = 1e-6
NEG_BIG = -1e30

LANES = 128
BIAS_W = 640
FF_BLOCK = 256
VMEM_LIMIT = 56 * 1024 * 1024

OFF_QA, OFF_KA, OFF_VA, OFF_QB, OFF_KB, OFF_VB, OFF_R = 0, 512, 1024, 1536, 1792, 2048, 2560


def _rms(x, g):
    return x * lax.rsqrt(jnp.mean(x * x, axis=-1, keepdims=True) + EPS) * g


def _silu(x):
    return x / (1.0 + jnp.exp(-x))


def _log_sigmoid(z):
    return jnp.minimum(z, 0.0) - jnp.log(1.0 + jnp.exp(-jnp.abs(z)))


def _const_spec(shape):
    nd = len(shape)
    return pl.BlockSpec(shape, lambda *_: (0,) * nd, pipeline_mode=pl.Buffered(1))


def _params(n_axes, vmem=None):
    return pltpu.CompilerParams(dimension_semantics=("arbitrary",) * n_axes, vmem_limit_bytes=vmem)


def _proj_kernel(tiles_per_seq, x_ref, g_ref, w_ref, wr_ref, wg_ref, wa2_ref, ba_ref,
                 qa_ref, ka_ref, va_ref, qb_ref, kb_ref, vb_ref, r_ref, la_ref, kt_ref, vt_ref):
    h = _rms(x_ref[...], g_ref[...]).astype(BF16)

    def proj(lo, hi):
        return jnp.dot(h, w_ref[:, lo:hi], preferred_element_type=F32)

    g_low = jnp.dot(h, wg_ref[...], preferred_element_type=F32).astype(BF16)
    z = jnp.dot(g_low, wa2_ref[...], preferred_element_type=F32) + ba_ref[...]
    la_ref[...] = _log_sigmoid(z) * (1.0 / GATE_TAU)
    qa_ref[...] = (proj(OFF_QA, OFF_KA) * (HEAD_DIM_A ** -0.5)).astype(BF16)
    ka = proj(OFF_KA, OFF_VA)
    va = proj(OFF_VA, OFF_QB)
    ka_ref[...] = ka.astype(BF16)
    va_ref[...] = va.astype(BF16)
    qb_ref[...] = proj(OFF_QB, OFF_KB) * (KEY_DIM_B ** -0.5)
    kb_ref[...] = proj(OFF_KB, OFF_VB)
    vb_ref[...] = proj(OFF_VB, OFF_R).astype(BF16)
    r_ref[...] = jnp.dot(h, wr_ref[...], preferred_element_type=F32)

    @pl.when(pl.program_id(0) % tiles_per_seq == tiles_per_seq - 1)
    def _():
        kt_ref[...] = ka
        vt_ref[...] = va


def _proj(x, g, w_main, w_r, w_g, wa2, ba, tm, tiles_per_seq):
    ntok = x.shape[0]
    nt = ntok // tm
    nseq = nt // tiles_per_seq
    row = lambda w: pl.BlockSpec((tm, w), lambda i: (i, 0))
    tail = pl.BlockSpec((tm, WIDTH_A), lambda i: (i // tiles_per_seq, 0))
    out_shape = (
        jax.ShapeDtypeStruct((ntok, WIDTH_A), BF16),
        jax.ShapeDtypeStruct((ntok, WIDTH_A), BF16),
        jax.ShapeDtypeStruct((ntok, WIDTH_A), BF16),
        jax.ShapeDtypeStruct((ntok, WIDTH_BK), F32),
        jax.ShapeDtypeStruct((ntok, WIDTH_BK), F32),
        jax.ShapeDtypeStruct((ntok, WIDTH_BV), BF16),
        jax.ShapeDtypeStruct((ntok, WIDTH_BV), F32),
        jax.ShapeDtypeStruct((ntok, WIDTH_BK), F32),
        jax.ShapeDtypeStruct((nseq * tm, WIDTH_A), F32),
        jax.ShapeDtypeStruct((nseq * tm, WIDTH_A), F32),
    )
    return pl.pallas_call(
        functools.partial(_proj_kernel, tiles_per_seq),
        grid=(nt,),
        in_specs=[row(D_MODEL), _const_spec((1, D_MODEL)), _const_spec((D_MODEL, OFF_R)),
                  _const_spec((D_MODEL, WIDTH_BV)), _const_spec((D_MODEL, LANES)),
                  _const_spec((LANES, WIDTH_BK)), _const_spec((1, WIDTH_BK))],
        out_specs=(row(WIDTH_A), row(WIDTH_A), row(WIDTH_A), row(WIDTH_BK), row(WIDTH_BK),
                   row(WIDTH_BV), row(WIDTH_BV), row(WIDTH_BK), tail, tail),
        out_shape=out_shape,
        compiler_params=_params(1, VMEM_LIMIT),
        name="proj",
    )(x, g, w_main, w_r, w_g, wa2, ba)


def _stack_heads(x_pair):
    low_half = lax.broadcasted_iota(jnp.int32, x_pair.shape, 1) < HEAD_DIM_A
    zero = jnp.zeros_like(x_pair)
    return jnp.concatenate([jnp.where(low_half, x_pair, zero), jnp.where(low_half, zero, x_pair)], axis=0)


TQ = 4 * CHUNK
KEYS = 3 * TQ
SOFT_W = 5 * LANES
assert 2 * TQ == A_WINDOW and SOFT_W >= (BAND_CHUNKS + 1) * CHUNK + CHUNK


def _build_band_bias(base_ref, bm_ref):
    n_band = (BAND_CHUNKS + 1) * CHUNK
    col = lax.broadcasted_iota(jnp.int32, (CHUNK, KEYS), 1)
    for h in range(N_HEADS_A):
        rows = jnp.broadcast_to(base_ref[h:h + 1, :], (CHUNK, KEYS))
        for qc in range(TQ // CHUNK):
            toeplitz = pltpu.roll(rows, qc * CHUNK, 1, stride=1, stride_axis=0)
            in_band = (col >= qc * CHUNK) & (col < qc * CHUNK + n_band)
            r0 = (qc * 2 + h % 2) * CHUNK
            bm_ref[h // 2, r0:r0 + CHUNK, :] = jnp.where(in_band, toeplitz, NEG_BIG)


def _band_tile(q_ref, k_refs, v_refs, bm_ref, o_ref, start_penalty):
    n_qc = TQ // CHUNK
    low_half = lax.broadcasted_iota(jnp.int32, (CHUNK, LANES), 1) < HEAD_DIM_A
    nt_dims = (((1,), (1,)), ((), ()))
    for hp in range(N_HEADS_A // 2):
        cols = slice(hp * LANES, (hp + 1) * LANES)
        qs = jnp.concatenate([_stack_heads(q_ref[qc * CHUNK:(qc + 1) * CHUNK, cols]) for qc in range(n_qc)],
                             axis=0)
        s = jnp.concatenate([lax.dot_general(qs, k[:, cols], nt_dims, preferred_element_type=F32)
                             for k in k_refs], axis=1)
        probs, inv_l = [], []
        for qc in range(n_qc):
            rows = slice(qc * 2 * CHUNK, (qc + 1) * 2 * CHUNK)
            c0 = 0 if (qc + 1) * CHUNK + A_WINDOW <= SOFT_W else KEYS - SOFT_W
            sq = s[rows, c0:c0 + SOFT_W] + bm_ref[hp, rows, c0:c0 + SOFT_W]
            if start_penalty is not None:
                sq = sq + start_penalty[:, c0:c0 + SOFT_W]
            p = jnp.exp(sq - jnp.max(sq, axis=-1, keepdims=True))
            inv_l.append(1.0 / jnp.sum(p, axis=-1, keepdims=True))
            pad = jnp.zeros((2 * CHUNK, KEYS - SOFT_W), BF16)
            probs.append(jnp.concatenate([p.astype(BF16), pad] if c0 == 0 else [pad, p.astype(BF16)], axis=1))
        pm = jnp.concatenate(probs, axis=0)
        out = sum(jnp.dot(pm[:, i * TQ:(i + 1) * TQ], v[:, cols], preferred_element_type=F32)
                  for i, v in enumerate(v_refs))
        for qc in range(n_qc):
            r0 = qc * 2 * CHUNK
            o0 = out[r0:r0 + CHUNK] * inv_l[qc][0:CHUNK]
            o1 = out[r0 + CHUNK:r0 + 2 * CHUNK] * inv_l[qc][CHUNK:2 * CHUNK]
            o_ref[qc * CHUNK:(qc + 1) * CHUNK, cols] = jnp.where(low_half, o0, o1).astype(o_ref.dtype)


def _attn_prompt_kernel(q_ref, k2_ref, k1_ref, k0_ref, v2_ref, v1_ref, v0_ref, base_ref, o_ref, bm):
    t = pl.program_id(1)

    @pl.when((pl.program_id(0) == 0) & (t == 0))
    def _():
        _build_band_bias(base_ref, bm)

    k_refs = (k2_ref, k1_ref, k0_ref)
    v_refs = (v2_ref, v1_ref, v0_ref)

    @pl.when(t >= 2)
    def _():
        _band_tile(q_ref, k_refs, v_refs, bm, o_ref, None)

    @pl.when(t < 2)
    def _():
        col = lax.broadcasted_iota(jnp.int32, (1, KEYS), 1)
        penalty = jnp.where(col < (2 - t) * TQ, NEG_BIG, 0.0)
        _band_tile(q_ref, k_refs, v_refs, bm, o_ref, penalty)


def _attn_prompt(q, k, v, base, n_batch, seq):
    nt = seq // TQ
    blk = lambda back: pl.BlockSpec((TQ, WIDTH_A), lambda b, t: (b * nt + jnp.maximum(t - back, 0), 0))
    return pl.pallas_call(
        _attn_prompt_kernel,
        grid=(n_batch, nt),
        in_specs=[blk(0), blk(2), blk(1), blk(0), blk(2), blk(1), blk(0), _const_spec((N_HEADS_A, KEYS))],
        out_specs=blk(0),
        out_shape=jax.ShapeDtypeStruct((n_batch * seq, WIDTH_A), BF16),
        scratch_shapes=[pltpu.VMEM((N_HEADS_A // 2, 2 * TQ, KEYS), F32)],
        compiler_params=_params(2, VMEM_LIMIT),
        name="band_attn_prompt",
    )(q, k, k, k, v, v, v, base)


def _attn_sample_kernel(n_new, q_ref, kn_ref, vn_ref, kc_ref, vc_ref, base_ref, basei_ref, sel_ref, dup_ref,
                        o_ref, bias_c, bias_n):
    n_h = N_HEADS_A
    nc = kc_ref.shape[0]
    n_cache = nc // n_h

    @pl.when(pl.program_id(0) == 0)
    def _():
        for h in range(n_h):
            rows = jnp.broadcast_to(basei_ref[h:h + 1, :], (n_new, basei_ref.shape[1]))
            bias_c[h * n_new:(h + 1) * n_new, :] = pltpu.roll(rows, 0, 1, stride=n_h, stride_axis=0)[:, 0:nc]
            rows = jnp.broadcast_to(base_ref[h:h + 1, :], (n_new, BIAS_W))
            bias_n[h * n_new:(h + 1) * n_new, :] = pltpu.roll(rows, 0, 1, stride=1,
                                                              stride_axis=0)[:, n_cache:n_cache + n_new]

    nt_dims = (((1,), (1,)), ((), ()))
    low_half = lax.broadcasted_iota(jnp.int32, (n_new, LANES), 1) < HEAD_DIM_A
    pairs = [slice(hp * LANES, (hp + 1) * LANES) for hp in range(n_h // 2)]
    q2 = jnp.concatenate([jnp.dot(q_ref[:, cols], sel_ref[e], preferred_element_type=F32)
                          for cols in pairs for e in range(2)], axis=0).astype(BF16)
    s_c = lax.dot_general(q2, kc_ref[...].astype(BF16), nt_dims, preferred_element_type=F32) + bias_c[...]
    s_n = jnp.concatenate([lax.dot_general(_stack_heads(q_ref[:, cols]), kn_ref[:, cols], nt_dims,
                                           preferred_element_type=F32) for cols in pairs], axis=0) + bias_n[...]
    m = jnp.maximum(jnp.max(s_c, axis=-1, keepdims=True), jnp.max(s_n, axis=-1, keepdims=True))
    p_c, p_n = jnp.exp(s_c - m), jnp.exp(s_n - m)
    inv_l = 1.0 / (jnp.sum(p_c, axis=-1, keepdims=True) + jnp.sum(p_n, axis=-1, keepdims=True))
    out_c = jnp.dot(p_c.astype(BF16), vc_ref[...].astype(BF16), preferred_element_type=F32) * inv_l
    hi = out_c.astype(BF16)
    lo = (out_c - hi.astype(F32)).astype(BF16)
    out_c = (jnp.dot(hi, dup_ref[...], preferred_element_type=F32)
             + jnp.dot(lo, dup_ref[...], preferred_element_type=F32))
    p_n = p_n.astype(BF16)
    for hp, cols in enumerate(pairs):
        rows = slice(hp * 2 * n_new, (hp + 1) * 2 * n_new)
        out = out_c[rows] + jnp.dot(p_n[rows], vn_ref[:, cols], preferred_element_type=F32) * inv_l[rows]
        o_ref[:, cols] = jnp.where(low_half, out[0:n_new], out[n_new:]).astype(o_ref.dtype)


def _attn_sample(q, k_new, v_new, k_cache, v_cache, table, n_batch, n_new):
    n_h = N_HEADS_A
    nc = k_cache.shape[0] // n_batch
    n_cache = nc // n_h
    assert n_cache == A_WINDOW and n_new <= CHUNK
    base = _bias_base(table, BIAS_W)
    same_head = jnp.asarray(np.eye(n_h, dtype=bool)[:, None, :])
    base_i = jnp.where(same_head, base[:, :, None], NEG_BIG).reshape(n_h, BIAS_W * n_h)
    lane = np.arange(LANES)
    sel = np.stack([lane[:, None] == np.arange(HEAD_DIM_A)[None, :] + e * HEAD_DIM_A for e in range(2)])
    dup = np.arange(HEAD_DIM_A)[:, None] == lane[None, :] % HEAD_DIM_A
    new = pl.BlockSpec((n_new, WIDTH_A), lambda b: (b, 0))
    old = pl.BlockSpec((nc, HEAD_DIM_A), lambda b: (b, 0))
    return pl.pallas_call(
        functools.partial(_attn_sample_kernel, n_new),
        grid=(n_batch,),
        in_specs=[new, new, new, old, old, _const_spec((n_h, BIAS_W)), _const_spec((n_h, BIAS_W * n_h)),
                  _const_spec((2, LANES, HEAD_DIM_A)), _const_spec((HEAD_DIM_A, LANES))],
        out_specs=new,
        out_shape=jax.ShapeDtypeStruct((n_batch * n_new, WIDTH_A), BF16),
        scratch_shapes=[pltpu.VMEM((n_h * n_new, nc), F32), pltpu.VMEM((n_h * n_new, n_new), F32)],
        compiler_params=_params(1, VMEM_LIMIT),
        name="band_attn_sample",
    )(q, k_new, v_new, k_cache, v_cache, base, base_i, jnp.asarray(sel, BF16), jnp.asarray(dup, BF16))


def _gla_levels(chunk):
    return [chunk >> i for i in range(int(math.log2(chunk)) + 1)]


def _segment_matrix(chunk):
    i = np.arange(chunk)[:, None]
    t = np.arange(chunk)[None, :]
    blocks = []
    for li, s in enumerate(_gla_levels(chunk)):
        start = (i // s) * s
        f_rows = (t >= start) & (t <= i)
        r_rows = (t > i) & (t <= start + s - 1)
        if li == 0:
            blocks += [f_rows, r_rows]
        else:
            blocks.append(np.where((i // s) % 2 == 1, f_rows, r_rows))
    seg = np.concatenate(blocks, axis=0).astype(np.float32)
    return np.concatenate([seg, seg], axis=1)


def _level_masks(group_tokens):
    tg = group_tokens
    i = (np.arange(2 * tg) % tg)[:, None]
    j = np.arange(tg)[None, :]
    masks = [i == j]
    s = tg // 2
    while s >= 1:
        masks.append(((i // s) % 2 == 1) & (j // s == i // s - 1))
        s //= 2
    return np.stack(masks).astype(np.float32)


def _gla_tile(q, k, v, la, seg_ref, mask_ref, state_ref, chunk, n_chunks):
    n, tg = n_chunks, chunk * n_chunks
    levels = _gla_levels(chunk)
    la_hi = la.astype(BF16)
    la_lo = (la - la_hi.astype(F32)).astype(BF16)
    split = jnp.concatenate(
        [jnp.concatenate([la_hi[c * chunk:(c + 1) * chunk], la_lo[c * chunk:(c + 1) * chunk]], axis=0)
         for c in range(n)], axis=1)
    sums = jnp.dot(seg_ref[...], split, preferred_element_type=F32)

    def seg_sum(block):
        rows = slice(block * chunk, (block + 1) * chunk)
        return jnp.minimum(jnp.concatenate(
            [sums[rows, c * WIDTH_BK:(c + 1) * WIDTH_BK] for c in range(n)], axis=0), 0.0)

    fwd0, rev0 = seg_sum(0), seg_sum(1)
    from_start, to_end = jnp.exp(fwd0), jnp.exp(rev0)
    total = [fwd0[(c + 1) * chunk - 1:(c + 1) * chunk] for c in range(n)]

    def chunk_sum(cs):
        cs = list(cs)
        return (sum(total[c] for c in cs[1:]) + total[cs[0]]) if cs else None

    def extend(base, offsets):
        pieces = []
        for c in range(n):
            x = base[c * chunk:(c + 1) * chunk]
            pieces.append(x if offsets[c] is None else x * jnp.exp(offsets[c]))
        return jnp.concatenate(pieces, axis=0) if n > 1 else pieces[0]

    q_state = (q * extend(from_start, [chunk_sum(range(0, c)) for c in range(n)])).astype(BF16)
    k_state = (k * extend(to_end, [chunk_sum(range(c + 1, n)) for c in range(n)])).astype(BF16)
    decay_tile = jnp.exp(chunk_sum(range(n)))
    ops = {}
    s = tg // 2
    while s >= chunk and n > 1:
        per = s // chunk
        offsets, bases = [], []
        for c in range(n):
            sg = c // per
            offsets.append(chunk_sum(range(sg * per, c)) if sg % 2 else chunk_sum(range(c + 1, (sg + 1) * per)))
            bases.append((from_start if sg % 2 else to_end)[c * chunk:(c + 1) * chunk])
        w = extend(jnp.concatenate(bases, axis=0), offsets)
        ops[s] = ((q * w).astype(BF16), (k * w).astype(BF16))
        s //= 2
    for li in range(1, len(levels)):
        w = jnp.exp(seg_sum(li + 1))
        ops[levels[li]] = ((q * w).astype(BF16), (k * w).astype(BF16))
    q16, k16 = q.astype(BF16), k.astype(BF16)

    gt = mask_ref.shape[2]
    own = (lax.broadcasted_iota(jnp.int32, (2 * VAL_DIM_B, LANES), 0) < VAL_DIM_B) == \
          (lax.broadcasted_iota(jnp.int32, (2 * VAL_DIM_B, LANES), 1) < KEY_DIM_B)
    nt_dims = (((1,), (1,)), ((), ()))
    tn_dims = (((0,), (0,)), ((), ()))
    n_pairs, n_groups = N_HEADS_B // 2, tg // gt
    pair_cols = [slice(p * LANES, (p + 1) * LANES) for p in range(n_pairs)]
    pair_v = [v[:, p * 2 * VAL_DIM_B:(p + 1) * 2 * VAL_DIM_B] for p in range(n_pairs)]
    head_v = lambda e: slice(e * VAL_DIM_B, (e + 1) * VAL_DIM_B)
    o_state = []
    for p, cols in enumerate(pair_cols):
        st = state_ref[p]
        o_state.append(lax.dot_general(q_state[:, cols], st.astype(BF16), nt_dims, preferred_element_type=F32))
        upd = lax.dot_general(pair_v[p], k_state[:, cols], tn_dims, preferred_element_type=F32)
        state_ref[p] = st * decay_tile[:, cols] + jnp.where(own, upd, 0.0)
    blocks = []
    for p, cols in enumerate(pair_cols):
        for g in range(n_groups):
            rows = slice(g * gt, (g + 1) * gt)
            a = mask_ref[0] * lax.dot_general(_stack_heads(q16[rows, cols]), k16[rows, cols], nt_dims,
                                              preferred_element_type=F32)
            m, s = 1, gt // 2
            while s >= 1:
                qd, kd = ops[s]
                a = a + mask_ref[m] * lax.dot_general(_stack_heads(qd[rows, cols]), kd[rows, cols], nt_dims,
                                                      preferred_element_type=F32)
                m, s = m + 1, s // 2
            blocks.append((p, g * gt, g * gt, a.astype(BF16)))
        s = tg // 2
        while s >= gt:
            qd, kd = ops[s]
            for blk in range(tg // (2 * s)):
                k0, q0 = blk * 2 * s, blk * 2 * s + s
                ab = lax.dot_general(_stack_heads(qd[q0:q0 + s, cols]), kd[k0:k0 + s, cols], nt_dims,
                                     preferred_element_type=F32)
                blocks.append((p, q0, k0, ab.astype(BF16)))
            s //= 2
    piece = [[[o_state[p][g * gt:(g + 1) * gt, head_v(e)] for g in range(n_groups)] for e in range(2)]
             for p in range(n_pairs)]
    for p, q0, k0, a in blocks:
        nq, nk = a.shape[0] // 2, a.shape[1]
        for e in range(2):
            contrib = jnp.dot(a[e * nq:(e + 1) * nq], pair_v[p][k0:k0 + nk, head_v(e)], preferred_element_type=F32)
            for gi in range(nq // gt):
                g = q0 // gt + gi
                piece[p][e][g] = piece[p][e][g] + contrib[gi * gt:(gi + 1) * gt]
    outs = [jnp.concatenate(piece[p][e], axis=0) if n_groups > 1 else piece[p][e][0]
            for p in range(n_pairs) for e in range(2)]
    return jnp.concatenate(outs, axis=1)


def _gla_kernel(chunk, chunks_per_tile, has_init, *refs):
    if has_init:
        q_ref, k_ref, v_ref, la_ref, seg_ref, mask_ref, s0_ref, o_ref, sout_ref, state = refs
    else:
        q_ref, k_ref, v_ref, la_ref, seg_ref, mask_ref, o_ref, sout_ref, state = refs
    t = pl.program_id(1)

    @pl.when(t == 0)
    def _():
        state[...] = s0_ref[...] if has_init else jnp.zeros_like(state)

    o_ref[...] = _gla_tile(q_ref[...], k_ref[...], v_ref[...], la_ref[...], seg_ref, mask_ref, state,
                           chunk, chunks_per_tile)

    @pl.when(t == pl.num_programs(1) - 1)
    def _():
        sout_ref[...] = state[...]


def _gla(q, k, v, la, s0, n_batch, seq, chunk, chunks_per_tile):
    tg = chunk * chunks_per_tile
    nt = seq // tg
    seg = jnp.asarray(_segment_matrix(chunk), BF16)
    masks = jnp.asarray(_level_masks(min(tg, LANES)), F32)
    row = lambda w: pl.BlockSpec((tg, w), lambda b, t: (b * nt + t, 0))
    st = pl.BlockSpec((None, 2, 2 * VAL_DIM_B, LANES), lambda b, t: (b, 0, 0, 0))
    in_specs = [row(WIDTH_BK), row(WIDTH_BK), row(WIDTH_BV), row(WIDTH_BK), _const_spec(seg.shape),
                _const_spec(masks.shape)]
    args = [q, k, v, la, seg, masks]
    if s0 is not None:
        in_specs.append(st)
        args.append(s0)
    return pl.pallas_call(
        functools.partial(_gla_kernel, chunk, chunks_per_tile, s0 is not None),
        grid=(n_batch, nt),
        in_specs=in_specs,
        out_specs=(row(WIDTH_BV), st),
        out_shape=(jax.ShapeDtypeStruct((n_batch * seq, WIDTH_BV), F32),
                   jax.ShapeDtypeStruct((n_batch, 2, 2 * VAL_DIM_B, LANES), F32)),
        scratch_shapes=[pltpu.VMEM((2, 2 * VAL_DIM_B, LANES), F32)],
        compiler_params=_params(2, VMEM_LIMIT),
        name="gla_chunk%d" % chunk,
    )(*args)


def _state_to_pairs(s):
    n = s.shape[0]
    st = jnp.swapaxes(s, -1, -2).reshape(n, 2, 2, VAL_DIM_B, KEY_DIM_B)
    z = jnp.zeros_like(st[:, :, 0])
    top = jnp.concatenate([st[:, :, 0], z], axis=-1)
    bot = jnp.concatenate([z, st[:, :, 1]], axis=-1)
    return jnp.concatenate([top, bot], axis=-2)


def _pairs_to_state(sp):
    n = sp.shape[0]
    h0 = sp[:, :, :VAL_DIM_B, :KEY_DIM_B]
    h1 = sp[:, :, VAL_DIM_B:, KEY_DIM_B:]
    st = jnp.stack([h0, h1], axis=2).reshape(n, N_HEADS_B, VAL_DIM_B, KEY_DIM_B)
    return jnp.swapaxes(st, -1, -2)


def _mem_kv_kernel(m_ref, g_ref, wk_ref, wv_ref, k_ref, v_ref):
    m = _rms(m_ref[...], g_ref[...]).astype(BF16)
    k_ref[...] = jnp.dot(m, wk_ref[...], preferred_element_type=F32)
    v_ref[...] = jnp.dot(m, wv_ref[...], preferred_element_type=F32)


def _mem_kv(mem, g, wk, wv):
    n = mem.shape[0]
    tm = N_MEM
    row = pl.BlockSpec((tm, D_MODEL), lambda i: (i, 0))
    return pl.pallas_call(
        _mem_kv_kernel,
        grid=(n // tm,),
        in_specs=[row, _const_spec((1, D_MODEL)), _const_spec((D_MODEL, D_MODEL)), _const_spec((D_MODEL, D_MODEL))],
        out_specs=(row, row),
        out_shape=(jax.ShapeDtypeStruct((n, D_MODEL), F32), jax.ShapeDtypeStruct((n, D_MODEL), F32)),
        compiler_params=_params(1, VMEM_LIMIT),
        name="mem_kv",
    )(mem, g, wk, wv)


def _stack_kernel(n_seg, x_ref, oa_ref, ob_ref, r_ref, mk_ref, mv_ref,
                  g_gla_ref, wo_ref, g_post_mix_ref, g_pre_mem_ref, wq_ref, wmo_ref, g_post_mem_ref,
                  g_pre_ffn_ref, wg_ref, wu_ref, wd_ref, g_post_ffn_ref, y_ref):
    tm = x_ref.shape[0]
    ob = ob_ref[...]
    normed = []
    for h in range(N_HEADS_B):
        seg = ob[:, h * VAL_DIM_B:(h + 1) * VAL_DIM_B]
        normed.append(seg * lax.rsqrt(jnp.mean(seg * seg, axis=-1, keepdims=True) + EPS))
    yb = jnp.concatenate(normed, axis=1) * g_gla_ref[...] * _silu(r_ref[...])
    mix = (jnp.dot(oa_ref[...], wo_ref[0:WIDTH_A, :], preferred_element_type=F32)
           + jnp.dot(yb.astype(BF16), wo_ref[WIDTH_A:, :], preferred_element_type=F32))
    x1 = x_ref[...] + _rms(mix, g_post_mix_ref[...])

    hq = _rms(x1, g_pre_mem_ref[...]).astype(BF16)
    q = (jnp.dot(hq, wq_ref[...], preferred_element_type=F32) * (HEAD_DIM_MEM ** -0.5)).astype(BF16)
    rows_per_seg = tm // n_seg
    units = [(sg, h) for sg in range(n_seg) for h in range(N_HEADS_MEM)]
    mem_rows = lambda sg: slice(sg * N_MEM, (sg + 1) * N_MEM)
    head_cols = lambda h: slice(h * HEAD_DIM_MEM, (h + 1) * HEAD_DIM_MEM)
    scores = [lax.dot_general(q[sg * rows_per_seg:(sg + 1) * rows_per_seg, head_cols(h)],
                              mk_ref[mem_rows(sg), head_cols(h)].astype(BF16), (((1,), (1,)), ((), ())),
                              preferred_element_type=F32) for sg, h in units]
    probs, inv_l = [], []
    for s in scores:
        p = jnp.exp(s - jnp.max(s, axis=-1, keepdims=True))
        inv_l.append(1.0 / jnp.sum(p, axis=-1, keepdims=True))
        probs.append(p.astype(BF16))
    outs = [jnp.dot(probs[u], mv_ref[mem_rows(sg), head_cols(h)].astype(BF16), preferred_element_type=F32) * inv_l[u]
            for u, (sg, h) in enumerate(units)]
    seg_outs = [jnp.concatenate(outs[sg * N_HEADS_MEM:(sg + 1) * N_HEADS_MEM], axis=1) for sg in range(n_seg)]
    o = jnp.concatenate(seg_outs, axis=0) if n_seg > 1 else seg_outs[0]
    att = jnp.dot(o.astype(BF16), wmo_ref[...], preferred_element_type=F32)
    x2 = x1 + _rms(att, g_post_mem_ref[...])

    hf = _rms(x2, g_pre_ffn_ref[...]).astype(BF16)
    f = jnp.zeros((tm, D_MODEL), F32)
    act, act_cols = None, None
    for c in range(D_FF // FF_BLOCK):
        cols = slice(c * FF_BLOCK, (c + 1) * FF_BLOCK)
        gate = jnp.dot(hf, wg_ref[:, cols], preferred_element_type=F32)
        up = jnp.dot(hf, wu_ref[:, cols], preferred_element_type=F32)
        if act is not None:
            f = f + jnp.dot(act, wd_ref[act_cols, :], preferred_element_type=F32)
        act, act_cols = (_silu(gate) * up).astype(BF16), cols
    f = f + jnp.dot(act, wd_ref[act_cols, :], preferred_element_type=F32)
    y_ref[...] = x2 + _rms(f, g_post_ffn_ref[...])


def _stack(x, oa, ob, r, mk, mv, weights, tm, n_seg, tiles_per_mem_block):
    ntok = x.shape[0]
    row = lambda w: pl.BlockSpec((tm, w), lambda i: (i, 0))
    mem = pl.BlockSpec((n_seg * N_MEM, D_MODEL), lambda i: (i // tiles_per_mem_block, 0))
    w_specs = [_const_spec(w.shape) for w in weights]
    return pl.pallas_call(
        functools.partial(_stack_kernel, n_seg),
        grid=(ntok // tm,),
        in_specs=[row(D_MODEL), row(WIDTH_A), row(WIDTH_BV), row(WIDTH_BV), mem, mem] + w_specs,
        out_specs=row(D_MODEL),
        out_shape=jax.ShapeDtypeStruct((ntok, D_MODEL), F32),
        compiler_params=_params(1, VMEM_LIMIT),
        name="token_stack",
    )(x, oa, ob, r, mk, mv, *weights)


def _bias_base(table, width):
    u = np.arange(width)
    idx = np.where(u < (BAND_CHUNKS + 1) * CHUNK, np.clip(A_WINDOW - u, -REL_CLIP, REL_CLIP) + REL_CLIP, 2 * REL_CLIP)
    return table[:, idx]


def kernel(x_prompt, x_sample, mem_prompt, cache_a_k, cache_a_v, state_gla, cache_mem_k, cache_mem_v,
           g_pre_mix, w_in, rel_bias, w_alpha2, b_alpha, g_gla_out, w_o, g_post_mix,
           g_pre_mem, g_mem, w_mq, w_mk, w_mv, w_mo, g_post_mem,
           g_pre_ffn, w_ffn_gate, w_ffn_up, w_ffn_down, g_post_ffn):
    depth = w_in.shape[0]
    assert depth == 1
    l = 0
    n_p, seq, _ = x_prompt.shape
    n_s, seq_s, _ = x_sample.shape
    vec = lambda g: g[l].reshape(1, -1)

    wi = w_in[l]
    off_g, off_r = OFF_R, OFF_R + GATE_RANK
    proj_w = (wi[:, :off_g].astype(BF16), wi[:, off_r:].astype(BF16),
              jnp.pad(wi[:, off_g:off_r], ((0, 0), (0, LANES - GATE_RANK))).astype(BF16))
    wa2 = jnp.pad(w_alpha2[l], ((0, LANES - GATE_RANK), (0, 0))).astype(BF16)
    stack_w = [vec(g_gla_out), w_o[l].astype(BF16), vec(g_post_mix), vec(g_pre_mem), w_mq[l].astype(BF16),
               w_mo[l].astype(BF16), vec(g_post_mem), vec(g_pre_ffn), w_ffn_gate[l].astype(BF16),
               w_ffn_up[l].astype(BF16), w_ffn_down[l].astype(BF16), vec(g_post_ffn)]

    xp = x_prompt.reshape(n_p * seq, D_MODEL)
    tm_p = 512
    qa, ka, va, qb, kb, vb, r, la, k_tail, v_tail = _proj(xp, vec(g_pre_mix), *proj_w, wa2, vec(b_alpha), tm_p, seq // tm_p)
    oa = _attn_prompt(qa, ka, va, _bias_base(rel_bias[l], KEYS), n_p, seq)
    ob, sp = _gla(qb, kb, vb, la, None, n_p, seq, CHUNK, 4)
    mk, mv = _mem_kv(mem_prompt.reshape(n_p * N_MEM, D_MODEL), vec(g_mem), w_mk[l].astype(BF16), w_mv[l].astype(BF16))
    tm = 512
    yp = _stack(xp, oa, ob, r, mk.astype(BF16), mv.astype(BF16), stack_w, tm, 1, seq // tm)

    xs = x_sample.reshape(n_s * seq_s, D_MODEL)
    ntok_s = n_s * seq_s
    qa_s, ka_s, va_s, qb_s, kb_s, vb_s, r_s, la_s, k_new, v_new = _proj(
        xs, vec(g_pre_mix), *proj_w, wa2, vec(b_alpha), ntok_s, 1)
    n_cache = cache_a_k.shape[2]
    oa_s = _attn_sample(qa_s, ka_s, va_s,
                        cache_a_k[l].reshape(n_s * n_cache * N_HEADS_A, HEAD_DIM_A),
                        cache_a_v[l].reshape(n_s * n_cache * N_HEADS_A, HEAD_DIM_A), rel_bias[l], n_s, seq_s)
    ob_s, ss = _gla(qb_s, kb_s, vb_s, la_s, _state_to_pairs(state_gla[l]), n_s, seq_s, seq_s, 1)
    seg_s = 4
    tm_s = seg_s * seq_s
    ys = _stack(xs, oa_s, ob_s, r_s,
                cache_mem_k[l].reshape(n_s * N_MEM, D_MODEL),
                cache_mem_v[l].reshape(n_s * N_MEM, D_MODEL), stack_w, tm_s, seg_s, 1)

    keep = min(A_WINDOW, seq)
    heads = lambda z, n, t: z.reshape(1, n, t, N_HEADS_A, HEAD_DIM_A)
    return (yp.reshape(n_p, seq, D_MODEL), ys.reshape(n_s, seq_s, D_MODEL),
            heads(k_tail, n_p, keep), heads(v_tail, n_p, keep), _pairs_to_state(sp)[None],
            mk.reshape(1, n_p, N_MEM, N_HEADS_MEM, HEAD_DIM_MEM), mv.reshape(1, n_p, N_MEM, N_HEADS_MEM, HEAD_DIM_MEM),
            heads(k_new, n_s, seq_s), heads(v_new, n_s, seq_s), _pairs_to_state(ss)[None])
```

```python
import functools
import math

import jax
import jax.numpy as jnp
import numpy as np
from jax import lax
from jax.experimental import pallas as pl
from jax.experimental.pallas import tpu as pltpu

F32 = jnp.float32
BF16 = jnp.bfloat16

D_MODEL = 1024
CHUNK = 64
BAND_CHUNKS = 8
A_WINDOW = BAND_CHUNKS * CHUNK
N_HEADS_A = 8
HEAD_DIM_A = 64
WIDTH_A = 512
REL_CLIP = 256
N_HEADS_B = 4
KEY_DIM_B = 64
VAL_DIM_B = 128
WIDTH_BK = 256
WIDTH_BV = 512
GATE_RANK = 16
GATE_TAU = 16.0
N_MEM = 256
N_HEADS_MEM = 4
HEAD_DIM_MEM = 256
D_FF = 2816
EPS = 1e-6
NEG_BIG = -1e30

LANES = 128
BIAS_W = 640
FF_BLOCK = 256
VMEM_LIMIT = 56 * 1024 * 1024

OFF_QA, OFF_KA, OFF_VA, OFF_QB, OFF_KB, OFF_VB, OFF_R = 0, 512, 1024, 1536, 1792, 2048, 2560


def _rms(x, g):
    return x * lax.rsqrt(jnp.mean(x * x, axis=-1, keepdims=True) + EPS) * g


def _silu(x):
    return x / (1.0 + jnp.exp(-x))


def _log_sigmoid(z):
    return jnp.minimum(z, 0.0) - jnp.log(1.0 + jnp.exp(-jnp.abs(z)))


def _const_spec(shape):
    nd = len(shape)
    return pl.BlockSpec(shape, lambda *_: (0,) * nd, pipeline_mode=pl.Buffered(1))


def _params(n_axes, vmem=None):
    return pltpu.CompilerParams(dimension_semantics=("arbitrary",) * n_axes, vmem_limit_bytes=vmem)


def _proj_kernel(tiles_per_seq, x_ref, g_ref, w_ref, wr_ref, wg_ref, wa2_ref, ba_ref,
                 qa_ref, ka_ref, va_ref, qb_ref, kb_ref, vb_ref, r_ref, la_ref, kt_ref, vt_ref):
    h = _rms(x_ref[...], g_ref[...]).astype(BF16)

    def proj(lo, hi):
        return jnp.dot(h, w_ref[:, lo:hi], preferred_element_type=F32)

    g_low = jnp.dot(h, wg_ref[...], preferred_element_type=F32).astype(BF16)
    z = jnp.dot(g_low, wa2_ref[...], preferred_element_type=F32) + ba_ref[...]
    la_ref[...] = _log_sigmoid(z) * (1.0 / GATE_TAU)
    qa_ref[...] = (proj(OFF_QA, OFF_KA) * (HEAD_DIM_A ** -0.5)).astype(BF16)
    ka = proj(OFF_KA, OFF_VA)
    va = proj(OFF_VA, OFF_QB)
    ka_ref[...] = ka.astype(BF16)
    va_ref[...] = va.astype(BF16)
    qb_ref[...] = proj(OFF_QB, OFF_KB) * (KEY_DIM_B ** -0.5)
    kb_ref[...] = proj(OFF_KB, OFF_VB)
    vb_ref[...] = proj(OFF_VB, OFF_R).astype(BF16)
    r_ref[...] = jnp.dot(h, wr_ref[...], preferred_element_type=F32)

    @pl.when(pl.program_id(0) % tiles_per_seq == tiles_per_seq - 1)
    def _():
        kt_ref[...] = ka
        vt_ref[...] = va


def _proj(x, g, w_main, w_r, w_g, wa2, ba, tm, tiles_per_seq):
    ntok = x.shape[0]
    nt = ntok // tm
    nseq = nt // tiles_per_seq
    row = lambda w: pl.BlockSpec((tm, w), lambda i: (i, 0))
    tail = pl.BlockSpec((tm, WIDTH_A), lambda i: (i // tiles_per_seq, 0))
    out_shape = (
        jax.ShapeDtypeStruct((ntok, WIDTH_A), BF16),
        jax.ShapeDtypeStruct((ntok, WIDTH_A), BF16),
        jax.ShapeDtypeStruct((ntok, WIDTH_A), BF16),
        jax.ShapeDtypeStruct((ntok, WIDTH_BK), F32),
        jax.ShapeDtypeStruct((ntok, WIDTH_BK), F32),
        jax.ShapeDtypeStruct((ntok, WIDTH_BV), BF16),
        jax.ShapeDtypeStruct((ntok, WIDTH_BV), F32),
        jax.ShapeDtypeStruct((ntok, WIDTH_BK), F32),
        jax.ShapeDtypeStruct((nseq * tm, WIDTH_A), F32),
        jax.ShapeDtypeStruct((nseq * tm, WIDTH_A), F32),
    )
    return pl.pallas_call(
        functools.partial(_proj_kernel, tiles_per_seq),
        grid=(nt,),
        in_specs=[row(D_MODEL), _const_spec((1, D_MODEL)), _const_spec((D_MODEL, OFF_R)),
                  _const_spec((D_MODEL, WIDTH_BV)), _const_spec((D_MODEL, LANES)),
                  _const_spec((LANES, WIDTH_BK)), _const_spec((1, WIDTH_BK))],
        out_specs=(row(WIDTH_A), row(WIDTH_A), row(WIDTH_A), row(WIDTH_BK), row(WIDTH_BK),
                   row(WIDTH_BV), row(WIDTH_BV), row(WIDTH_BK), tail, tail),
        out_shape=out_shape,
        compiler_params=_params(1, VMEM_LIMIT),
        name="proj",
    )(x, g, w_main, w_r, w_g, wa2, ba)


def _stack_heads(x_pair):
    low_half = lax.broadcasted_iota(jnp.int32, x_pair.shape, 1) < HEAD_DIM_A
    zero = jnp.zeros_like(x_pair)
    return jnp.concatenate([jnp.where(low_half, x_pair, zero), jnp.where(low_half, zero, x_pair)], axis=0)


TQ = 4 * CHUNK
KEYS = 3 * TQ
SOFT_W = 5 * LANES
assert 2 * TQ == A_WINDOW and SOFT_W >= (BAND_CHUNKS + 1) * CHUNK + CHUNK


def _build_band_bias(base_ref, bm_ref):
    n_band = (BAND_CHUNKS + 1) * CHUNK
    col = lax.broadcasted_iota(jnp.int32, (CHUNK, KEYS), 1)
    for h in range(N_HEADS_A):
        rows = jnp.broadcast_to(base_ref[h:h + 1, :], (CHUNK, KEYS))
        for qc in range(TQ // CHUNK):
            toeplitz = pltpu.roll(rows, qc * CHUNK, 1, stride=1, stride_axis=0)
            in_band = (col >= qc * CHUNK) & (col < qc * CHUNK + n_band)
            r0 = (qc * 2 + h % 2) * CHUNK
            bm_ref[h // 2, r0:r0 + CHUNK, :] = jnp.where(in_band, toeplitz, NEG_BIG)


def _band_tile(q_ref, k_refs, v_refs, bm_ref, o_ref, start_penalty):
    n_qc = TQ // CHUNK
    low_half = lax.broadcasted_iota(jnp.int32, (CHUNK, LANES), 1) < HEAD_DIM_A
    nt_dims = (((1,), (1,)), ((), ()))
    for hp in range(N_HEADS_A // 2):
        cols = slice(hp * LANES, (hp + 1) * LANES)
        qs = jnp.concatenate([_stack_heads(q_ref[qc * CHUNK:(qc + 1) * CHUNK, cols]) for qc in range(n_qc)],
                             axis=0)
        s = jnp.concatenate([lax.dot_general(qs, k[:, cols], nt_dims, preferred_element_type=F32)
                             for k in k_refs], axis=1)
        probs, inv_l = [], []
        for qc in range(n_qc):
            rows = slice(qc * 2 * CHUNK, (qc + 1) * 2 * CHUNK)
            c0 = 0 if (qc + 1) * CHUNK + A_WINDOW <= SOFT_W else KEYS - SOFT_W
            sq = s[rows, c0:c0 + SOFT_W] + bm_ref[hp, rows, c0:c0 + SOFT_W]
            if start_penalty is not None:
                sq = sq + start_penalty[:, c0:c0 + SOFT_W]
            p = jnp.exp(sq - jnp.max(sq, axis=-1, keepdims=True))
            inv_l.append(1.0 / jnp.sum(p, axis=-1, keepdims=True))
            pad = jnp.zeros((2 * CHUNK, KEYS - SOFT_W), BF16)
            probs.append(jnp.concatenate([p.astype(BF16), pad] if c0 == 0 else [pad, p.astype(BF16)], axis=1))
        pm = jnp.concatenate(probs, axis=0)
        out = sum(jnp.dot(pm[:, i * TQ:(i + 1) * TQ], v[:, cols], preferred_element_type=F32)
                  for i, v in enumerate(v_refs))
        for qc in range(n_qc):
            r0 = qc * 2 * CHUNK
            o0 = out[r0:r0 + CHUNK] * inv_l[qc][0:CHUNK]
            o1 = out[r0 + CHUNK:r0 + 2 * CHUNK] * inv_l[qc][CHUNK:2 * CHUNK]
            o_ref[qc * CHUNK:(qc + 1) * CHUNK, cols] = jnp.where(low_half, o0, o1).astype(o_ref.dtype)


def _attn_prompt_kernel(q_ref, k2_ref, k1_ref, k0_ref, v2_ref, v1_ref, v0_ref, base_ref, o_ref, bm):
    t = pl.program_id(1)

    @pl.when((pl.program_id(0) == 0) & (t == 0))
    def _():
        _build_band_bias(base_ref, bm)

    k_refs = (k2_ref, k1_ref, k0_ref)
    v_refs = (v2_ref, v1_ref, v0_ref)

    @pl.when(t >= 2)
    def _():
        _band_tile(q_ref, k_refs, v_refs, bm, o_ref, None)

    @pl.when(t < 2)
    def _():
        col = lax.broadcasted_iota(jnp.int32, (1, KEYS), 1)
        penalty = jnp.where(col < (2 - t) * TQ, NEG_BIG, 0.0)
        _band_tile(q_ref, k_refs, v_refs, bm, o_ref, penalty)


def _attn_prompt(q, k, v, base, n_batch, seq):
    nt = seq // TQ
    blk = lambda back: pl.BlockSpec((TQ, WIDTH_A), lambda b, t: (b * nt + jnp.maximum(t - back, 0), 0))
    return pl.pallas_call(
        _attn_prompt_kernel,
        grid=(n_batch, nt),
        in_specs=[blk(0), blk(2), blk(1), blk(0), blk(2), blk(1), blk(0), _const_spec((N_HEADS_A, KEYS))],
        out_specs=blk(0),
        out_shape=jax.ShapeDtypeStruct((n_batch * seq, WIDTH_A), BF16),
        scratch_shapes=[pltpu.VMEM((N_HEADS_A // 2, 2 * TQ, KEYS), F32)],
        compiler_params=_params(2, VMEM_LIMIT),
        name="band_attn_prompt",
    )(q, k, k, k, v, v, v, base)


def _attn_sample_kernel(n_new, q_ref, kn_ref, vn_ref, kc_ref, vc_ref, base_ref, sel_ref, dup_ref,
                        o_ref, bias_c, bias_n):
    n_h = N_HEADS_A
    n_cache = kc_ref.shape[2]

    @pl.when(pl.program_id(0) == 0)
    def _():
        for h in range(n_h):
            rows = jnp.broadcast_to(base_ref[h:h + 1, :], (n_new, BIAS_W))
            toeplitz = pltpu.roll(rows, 0, 1, stride=1, stride_axis=0)
            bias_c[h * n_new:(h + 1) * n_new, :] = toeplitz[:, 0:n_cache]
            bias_n[h * n_new:(h + 1) * n_new, :] = toeplitz[:, n_cache:n_cache + n_new]

    nt_dims = (((1,), (1,)), ((), ()))
    low_half = lax.broadcasted_iota(jnp.int32, (n_new, LANES), 1) < HEAD_DIM_A
    pairs = [slice(hp * LANES, (hp + 1) * LANES) for hp in range(n_h // 2)]
    q_heads = [jnp.dot(q_ref[:, cols], sel_ref[e], preferred_element_type=F32).astype(BF16)
               for cols in pairs for e in range(2)]
    s_c = jnp.concatenate([jnp.dot(q_heads[h], kc_ref[h].astype(BF16), preferred_element_type=F32)
                           for h in range(n_h)], axis=0) + bias_c[...]
    s_n = jnp.concatenate([lax.dot_general(_stack_heads(q_ref[:, cols]), kn_ref[:, cols], nt_dims,
                                           preferred_element_type=F32) for cols in pairs], axis=0) + bias_n[...]
    m = jnp.maximum(jnp.max(s_c, axis=-1, keepdims=True), jnp.max(s_n, axis=-1, keepdims=True))
    p_c, p_n = jnp.exp(s_c - m), jnp.exp(s_n - m)
    inv_l = 1.0 / (jnp.sum(p_c, axis=-1, keepdims=True) + jnp.sum(p_n, axis=-1, keepdims=True))
    p_c, p_n = p_c.astype(BF16), p_n.astype(BF16)
    out_c = jnp.concatenate([lax.dot_general(p_c[h * n_new:(h + 1) * n_new], vc_ref[h].astype(BF16), nt_dims,
                                             preferred_element_type=F32) for h in range(n_h)], axis=0) * inv_l
    hi = out_c.astype(BF16)
    lo = (out_c - hi.astype(F32)).astype(BF16)
    out_c = (jnp.dot(hi, dup_ref[...], preferred_element_type=F32)
             + jnp.dot(lo, dup_ref[...], preferred_element_type=F32))
    for hp, cols in enumerate(pairs):
        rows = slice(hp * 2 * n_new, (hp + 1) * 2 * n_new)
        out = out_c[rows] + jnp.dot(p_n[rows], vn_ref[:, cols], preferred_element_type=F32) * inv_l[rows]
        o_ref[:, cols] = jnp.where(low_half, out[0:n_new], out[n_new:]).astype(o_ref.dtype)


def _attn_sample(q, k_new, v_new, k_cache_t, v_cache_t, table, n_new):
    n_batch, n_h, _, n_cache = k_cache_t.shape
    assert n_cache == A_WINDOW and n_new <= CHUNK and n_h == N_HEADS_A
    lane = np.arange(LANES)
    sel = np.stack([lane[:, None] == np.arange(HEAD_DIM_A)[None, :] + e * HEAD_DIM_A for e in range(2)])
    dup = np.arange(HEAD_DIM_A)[:, None] == lane[None, :] % HEAD_DIM_A
    new = pl.BlockSpec((n_new, WIDTH_A), lambda b: (b, 0))
    old = pl.BlockSpec((None, n_h, HEAD_DIM_A, n_cache), lambda b: (b, 0, 0, 0))
    return pl.pallas_call(
        functools.partial(_attn_sample_kernel, n_new),
        grid=(n_batch,),
        in_specs=[new, new, new, old, old, _const_spec((n_h, BIAS_W)),
                  _const_spec((2, LANES, HEAD_DIM_A)), _const_spec((HEAD_DIM_A, LANES))],
        out_specs=new,
        out_shape=jax.ShapeDtypeStruct((n_batch * n_new, WIDTH_A), BF16),
        scratch_shapes=[pltpu.VMEM((n_h * n_new, n_cache), F32), pltpu.VMEM((n_h * n_new, n_new), F32)],
        compiler_params=_params(1, VMEM_LIMIT),
        name="band_attn_sample",
    )(q, k_new, v_new, k_cache_t, v_cache_t, _bias_base(table, BIAS_W), jnp.asarray(sel, BF16),
      jnp.asarray(dup, BF16))


def _gla_levels(chunk):
    return [chunk >> i for i in range(int(math.log2(chunk)) + 1)]


def _segment_matrix(chunk):
    i = np.arange(chunk)[:, None]
    t = np.arange(chunk)[None, :]
    blocks = []
    for li, s in enumerate(_gla_levels(chunk)):
        start = (i // s) * s
        f_rows = (t >= start) & (t <= i)
        r_rows = (t > i) & (t <= start + s - 1)
        if li == 0:
            blocks += [f_rows, r_rows]
        else:
            blocks.append(np.where((i // s) % 2 == 1, f_rows, r_rows))
    seg = np.concatenate(blocks, axis=0).astype(np.float32)
    return np.concatenate([seg, seg], axis=1)


def _level_masks(group_tokens):
    tg = group_tokens
    i = (np.arange(2 * tg) % tg)[:, None]
    j = np.arange(tg)[None, :]
    masks = [i == j]
    s = tg // 2
    while s >= 1:
        masks.append(((i // s) % 2 == 1) & (j // s == i // s - 1))
        s //= 2
    return np.stack(masks).astype(np.float32)


def _gla_tile(q, k, v, la, seg_ref, mask_ref, state_ref, chunk, n_chunks):
    n, tg = n_chunks, chunk * n_chunks
    levels = _gla_levels(chunk)
    la_hi = la.astype(BF16)
    la_lo = (la - la_hi.astype(F32)).astype(BF16)
    split = jnp.concatenate(
        [jnp.concatenate([la_hi[c * chunk:(c + 1) * chunk], la_lo[c * chunk:(c + 1) * chunk]], axis=0)
         for c in range(n)], axis=1)
    sums = jnp.dot(seg_ref[...], split, preferred_element_type=F32)

    def seg_sum(block):
        rows = slice(block * chunk, (block + 1) * chunk)
        return jnp.minimum(jnp.concatenate(
            [sums[rows, c * WIDTH_BK:(c + 1) * WIDTH_BK] for c in range(n)], axis=0), 0.0)

    fwd0, rev0 = seg_sum(0), seg_sum(1)
    from_start, to_end = jnp.exp(fwd0), jnp.exp(rev0)
    total = [fwd0[(c + 1) * chunk - 1:(c + 1) * chunk] for c in range(n)]

    def chunk_sum(cs):
        cs = list(cs)
        return (sum(total[c] for c in cs[1:]) + total[cs[0]]) if cs else None

    def extend(base, offsets):
        pieces = []
        for c in range(n):
            x = base[c * chunk:(c + 1) * chunk]
            pieces.append(x if offsets[c] is None else x * jnp.exp(offsets[c]))
        return jnp.concatenate(pieces, axis=0) if n > 1 else pieces[0]

    q_state = (q * extend(from_start, [chunk_sum(range(0, c)) for c in range(n)])).astype(BF16)
    k_state = (k * extend(to_end, [chunk_sum(range(c + 1, n)) for c in range(n)])).astype(BF16)
    decay_tile = jnp.exp(chunk_sum(range(n)))
    ops = {}
    s = tg // 2
    while s >= chunk and n > 1:
        per = s // chunk
        offsets, bases = [], []
        for c in range(n):
            sg = c // per
            offsets.append(chunk_sum(range(sg * per, c)) if sg % 2 else chunk_sum(range(c + 1, (sg + 1) * per)))
            bases.append((from_start if sg % 2 else to_end)[c * chunk:(c + 1) * chunk])
        w = extend(jnp.concatenate(bases, axis=0), offsets)
        ops[s] = ((q * w).astype(BF16), (k * w).astype(BF16))
        s //= 2
    for li in range(1, len(levels)):
        w = jnp.exp(seg_sum(li + 1))
        ops[levels[li]] = ((q * w).astype(BF16), (k * w).astype(BF16))
    q16, k16 = q.astype(BF16), k.astype(BF16)

    gt = mask_ref.shape[2]
    own = (lax.broadcasted_iota(jnp.int32, (2 * VAL_DIM_B, LANES), 0) < VAL_DIM_B) == \
          (lax.broadcasted_iota(jnp.int32, (2 * VAL_DIM_B, LANES), 1) < KEY_DIM_B)
    nt_dims = (((1,), (1,)), ((), ()))
    tn_dims = (((0,), (0,)), ((), ()))
    n_pairs, n_groups = N_HEADS_B // 2, tg // gt
    pair_cols = [slice(p * LANES, (p + 1) * LANES) for p in range(n_pairs)]
    pair_v = [v[:, p * 2 * VAL_DIM_B:(p + 1) * 2 * VAL_DIM_B] for p in range(n_pairs)]
    head_v = lambda e: slice(e * VAL_DIM_B, (e + 1) * VAL_DIM_B)
    o_state = []
    for p, cols in enumerate(pair_cols):
        st = state_ref[p]
        o_state.append(lax.dot_general(q_state[:, cols], st.astype(BF16), nt_dims, preferred_element_type=F32))
        upd = lax.dot_general(pair_v[p], k_state[:, cols], tn_dims, preferred_element_type=F32)
        state_ref[p] = st * decay_tile[:, cols] + jnp.where(own, upd, 0.0)
    blocks = []
    for p, cols in enumerate(pair_cols):
        for g in range(n_groups):
            rows = slice(g * gt, (g + 1) * gt)
            a = mask_ref[0] * lax.dot_general(_stack_heads(q16[rows, cols]), k16[rows, cols], nt_dims,
                                              preferred_element_type=F32)
            m, s = 1, gt // 2
            while s >= 1:
                qd, kd = ops[s]
                a = a + mask_ref[m] * lax.dot_general(_stack_heads(qd[rows, cols]), kd[rows, cols], nt_dims,
                                                      preferred_element_type=F32)
                m, s = m + 1, s // 2
            blocks.append((p, g * gt, g * gt, a.astype(BF16)))
        s = tg // 2
        while s >= gt:
            qd, kd = ops[s]
            for blk in range(tg // (2 * s)):
                k0, q0 = blk * 2 * s, blk * 2 * s + s
                ab = lax.dot_general(_stack_heads(qd[q0:q0 + s, cols]), kd[k0:k0 + s, cols], nt_dims,
                                     preferred_element_type=F32)
                blocks.append((p, q0, k0, ab.astype(BF16)))
            s //= 2
    piece = [[[o_state[p][g * gt:(g + 1) * gt, head_v(e)] for g in range(n_groups)] for e in range(2)]
             for p in range(n_pairs)]
    for p, q0, k0, a in blocks:
        nq, nk = a.shape[0] // 2, a.shape[1]
        for e in range(2):
            contrib = jnp.dot(a[e * nq:(e + 1) * nq], pair_v[p][k0:k0 + nk, head_v(e)], preferred_element_type=F32)
            for gi in range(nq // gt):
                g = q0 // gt + gi
                piece[p][e][g] = piece[p][e][g] + contrib[gi * gt:(gi + 1) * gt]
    outs = [jnp.concatenate(piece[p][e], axis=0) if n_groups > 1 else piece[p][e][0]
            for p in range(n_pairs) for e in range(2)]
    return jnp.concatenate(outs, axis=1)


def _gla_kernel(chunk, chunks_per_tile, has_init, *refs):
    if has_init:
        q_ref, k_ref, v_ref, la_ref, seg_ref, mask_ref, s0_ref, o_ref, sout_ref, state = refs
    else:
        q_ref, k_ref, v_ref, la_ref, seg_ref, mask_ref, o_ref, sout_ref, state = refs
    t = pl.program_id(1)

    @pl.when(t == 0)
    def _():
        state[...] = s0_ref[...] if has_init else jnp.zeros_like(state)

    o_ref[...] = _gla_tile(q_ref[...], k_ref[...], v_ref[...], la_ref[...], seg_ref, mask_ref, state,
                           chunk, chunks_per_tile)

    @pl.when(t == pl.num_programs(1) - 1)
    def _():
        sout_ref[...] = state[...]


def _gla(q, k, v, la, s0, n_batch, seq, chunk, chunks_per_tile):
    tg = chunk * chunks_per_tile
    nt = seq // tg
    seg = jnp.asarray(_segment_matrix(chunk), BF16)
    masks = jnp.asarray(_level_masks(min(tg, LANES)), F32)
    row = lambda w: pl.BlockSpec((tg, w), lambda b, t: (b * nt + t, 0))
    st = pl.BlockSpec((None, 2, 2 * VAL_DIM_B, LANES), lambda b, t: (b, 0, 0, 0))
    in_specs = [row(WIDTH_BK), row(WIDTH_BK), row(WIDTH_BV), row(WIDTH_BK), _const_spec(seg.shape),
                _const_spec(masks.shape)]
    args = [q, k, v, la, seg, masks]
    if s0 is not None:
        in_specs.append(st)
        args.append(s0)
    return pl.pallas_call(
        functools.partial(_gla_kernel, chunk, chunks_per_tile, s0 is not None),
        grid=(n_batch, nt),
        in_specs=in_specs,
        out_specs=(row(WIDTH_BV), st),
        out_shape=(jax.ShapeDtypeStruct((n_batch * seq, WIDTH_BV), F32),
                   jax.ShapeDtypeStruct((n_batch, 2, 2 * VAL_DIM_B, LANES), F32)),
        scratch_shapes=[pltpu.VMEM((2, 2 * VAL_DIM_B, LANES), F32)],
        compiler_params=_params(2, VMEM_LIMIT),
        name="gla_chunk%d" % chunk,
    )(*args)


def _state_to_pairs(s):
    n = s.shape[0]
    st = jnp.swapaxes(s, -1, -2).reshape(n, 2, 2, VAL_DIM_B, KEY_DIM_B)
    z = jnp.zeros_like(st[:, :, 0])
    top = jnp.concatenate([st[:, :, 0], z], axis=-1)
    bot = jnp.concatenate([z, st[:, :, 1]], axis=-1)
    return jnp.concatenate([top, bot], axis=-2)


def _pairs_to_state(sp):
    n = sp.shape[0]
    h0 = sp[:, :, :VAL_DIM_B, :KEY_DIM_B]
    h1 = sp[:, :, VAL_DIM_B:, KEY_DIM_B:]
    st = jnp.stack([h0, h1], axis=2).reshape(n, N_HEADS_B, VAL_DIM_B, KEY_DIM_B)
    return jnp.swapaxes(st, -1, -2)


def _mem_kv_kernel(m_ref, g_ref, wk_ref, wv_ref, k_ref, v_ref):
    m = _rms(m_ref[...], g_ref[...]).astype(BF16)
    k_ref[...] = jnp.dot(m, wk_ref[...], preferred_element_type=F32)
    v_ref[...] = jnp.dot(m, wv_ref[...], preferred_element_type=F32)


def _mem_kv(mem, g, wk, wv):
    n = mem.shape[0]
    tm = N_MEM
    row = pl.BlockSpec((tm, D_MODEL), lambda i: (i, 0))
    return pl.pallas_call(
        _mem_kv_kernel,
        grid=(n // tm,),
        in_specs=[row, _const_spec((1, D_MODEL)), _const_spec((D_MODEL, D_MODEL)), _const_spec((D_MODEL, D_MODEL))],
        out_specs=(row, row),
        out_shape=(jax.ShapeDtypeStruct((n, D_MODEL), F32), jax.ShapeDtypeStruct((n, D_MODEL), F32)),
        compiler_params=_params(1, VMEM_LIMIT),
        name="mem_kv",
    )(mem, g, wk, wv)


def _stack_kernel(n_seg, x_ref, oa_ref, ob_ref, r_ref, mk_ref, mv_ref,
                  g_gla_ref, wo_ref, g_post_mix_ref, g_pre_mem_ref, wq_ref, wmo_ref, g_post_mem_ref,
                  g_pre_ffn_ref, wg_ref, wu_ref, wd_ref, g_post_ffn_ref, y_ref):
    tm = x_ref.shape[0]
    ob = ob_ref[...]
    normed = []
    for h in range(N_HEADS_B):
        seg = ob[:, h * VAL_DIM_B:(h + 1) * VAL_DIM_B]
        normed.append(seg * lax.rsqrt(jnp.mean(seg * seg, axis=-1, keepdims=True) + EPS))
    yb = jnp.concatenate(normed, axis=1) * g_gla_ref[...] * _silu(r_ref[...])
    mix = (jnp.dot(oa_ref[...], wo_ref[0:WIDTH_A, :], preferred_element_type=F32)
           + jnp.dot(yb.astype(BF16), wo_ref[WIDTH_A:, :], preferred_element_type=F32))
    x1 = x_ref[...] + _rms(mix, g_post_mix_ref[...])

    hq = _rms(x1, g_pre_mem_ref[...]).astype(BF16)
    q = (jnp.dot(hq, wq_ref[...], preferred_element_type=F32) * (HEAD_DIM_MEM ** -0.5)).astype(BF16)
    rows_per_seg = tm // n_seg
    units = [(sg, h) for sg in range(n_seg) for h in range(N_HEADS_MEM)]
    mem_rows = lambda sg: slice(sg * N_MEM, (sg + 1) * N_MEM)
    head_cols = lambda h: slice(h * HEAD_DIM_MEM, (h + 1) * HEAD_DIM_MEM)
    scores = [lax.dot_general(q[sg * rows_per_seg:(sg + 1) * rows_per_seg, head_cols(h)],
                              mk_ref[mem_rows(sg), head_cols(h)].astype(BF16), (((1,), (1,)), ((), ())),
                              preferred_element_type=F32) for sg, h in units]
    probs, inv_l = [], []
    for s in scores:
        p = jnp.exp(s - jnp.max(s, axis=-1, keepdims=True))
        inv_l.append(1.0 / jnp.sum(p, axis=-1, keepdims=True))
        probs.append(p.astype(BF16))
    outs = [jnp.dot(probs[u], mv_ref[mem_rows(sg), head_cols(h)].astype(BF16), preferred_element_type=F32) * inv_l[u]
            for u, (sg, h) in enumerate(units)]
    seg_outs = [jnp.concatenate(outs[sg * N_HEADS_MEM:(sg + 1) * N_HEADS_MEM], axis=1) for sg in range(n_seg)]
    o = jnp.concatenate(seg_outs, axis=0) if n_seg > 1 else seg_outs[0]
    att = jnp.dot(o.astype(BF16), wmo_ref[...], preferred_element_type=F32)
    x2 = x1 + _rms(att, g_post_mem_ref[...])

    hf = _rms(x2, g_pre_ffn_ref[...]).astype(BF16)
    f = jnp.zeros((tm, D_MODEL), F32)
    act, act_cols = None, None
    for c in range(D_FF // FF_BLOCK):
        cols = slice(c * FF_BLOCK, (c + 1) * FF_BLOCK)
        gate = jnp.dot(hf, wg_ref[:, cols], preferred_element_type=F32)
        up = jnp.dot(hf, wu_ref[:, cols], preferred_element_type=F32)
        if act is not None:
            f = f + jnp.dot(act, wd_ref[act_cols, :], preferred_element_type=F32)
        act, act_cols = (_silu(gate) * up).astype(BF16), cols
    f = f + jnp.dot(act, wd_ref[act_cols, :], preferred_element_type=F32)
    y_ref[...] = x2 + _rms(f, g_post_ffn_ref[...])


def _stack(x, oa, ob, r, mk, mv, weights, tm, n_seg, tiles_per_mem_block):
    ntok = x.shape[0]
    row = lambda w: pl.BlockSpec((tm, w), lambda i: (i, 0))
    mem = pl.BlockSpec((n_seg * N_MEM, D_MODEL), lambda i: (i // tiles_per_mem_block, 0))
    w_specs = [_const_spec(w.shape) for w in weights]
    return pl.pallas_call(
        functools.partial(_stack_kernel, n_seg),
        grid=(ntok // tm,),
        in_specs=[row(D_MODEL), row(WIDTH_A), row(WIDTH_BV), row(WIDTH_BV), mem, mem] + w_specs,
        out_specs=row(D_MODEL),
        out_shape=jax.ShapeDtypeStruct((ntok, D_MODEL), F32),
        compiler_params=_params(1, VMEM_LIMIT),
        name="token_stack",
    )(x, oa, ob, r, mk, mv, *weights)


def _bias_base(table, width):
    u = np.arange(width)
    idx = np.where(u < (BAND_CHUNKS + 1) * CHUNK, np.clip(A_WINDOW - u, -REL_CLIP, REL_CLIP) + REL_CLIP, 2 * REL_CLIP)
    return table[:, idx]


def kernel(x_prompt, x_sample, mem_prompt, cache_a_k, cache_a_v, state_gla, cache_mem_k, cache_mem_v,
           g_pre_mix, w_in, rel_bias, w_alpha2, b_alpha, g_gla_out, w_o, g_post_mix,
           g_pre_mem, g_mem, w_mq, w_mk, w_mv, w_mo, g_post_mem,
           g_pre_ffn, w_ffn_gate, w_ffn_up, w_ffn_down, g_post_ffn):
    depth = w_in.shape[0]
    assert depth == 1
    l = 0
    n_p, seq, _ = x_prompt.shape
    n_s, seq_s, _ = x_sample.shape
    vec = lambda g: g[l].reshape(1, -1)

    wi = w_in[l]
    off_g, off_r = OFF_R, OFF_R + GATE_RANK
    proj_w = (wi[:, :off_g].astype(BF16), wi[:, off_r:].astype(BF16),
              jnp.pad(wi[:, off_g:off_r], ((0, 0), (0, LANES - GATE_RANK))).astype(BF16))
    wa2 = jnp.pad(w_alpha2[l], ((0, LANES - GATE_RANK), (0, 0))).astype(BF16)
    stack_w = [vec(g_gla_out), w_o[l].astype(BF16), vec(g_post_mix), vec(g_pre_mem), w_mq[l].astype(BF16),
               w_mo[l].astype(BF16), vec(g_post_mem), vec(g_pre_ffn), w_ffn_gate[l].astype(BF16),
               w_ffn_up[l].astype(BF16), w_ffn_down[l].astype(BF16), vec(g_post_ffn)]

    xp = x_prompt.reshape(n_p * seq, D_MODEL)
    tm_p = 512
    qa, ka, va, qb, kb, vb, r, la, k_tail, v_tail = _proj(xp, vec(g_pre_mix), *proj_w, wa2, vec(b_alpha), tm_p, seq // tm_p)
    oa = _attn_prompt(qa, ka, va, _bias_base(rel_bias[l], KEYS), n_p, seq)
    ob, sp = _gla(qb, kb, vb, la, None, n_p, seq, CHUNK, 4)
    mk, mv = _mem_kv(mem_prompt.reshape(n_p * N_MEM, D_MODEL), vec(g_mem), w_mk[l].astype(BF16), w_mv[l].astype(BF16))
    tm = 512
    yp = _stack(xp, oa, ob, r, mk.astype(BF16), mv.astype(BF16), stack_w, tm, 1, seq // tm)

    xs = x_sample.reshape(n_s * seq_s, D_MODEL)
    ntok_s = n_s * seq_s
    qa_s, ka_s, va_s, qb_s, kb_s, vb_s, r_s, la_s, k_new, v_new = _proj(
        xs, vec(g_pre_mix), *proj_w, wa2, vec(b_alpha), ntok_s, 1)
    oa_s = _attn_sample(qa_s, ka_s, va_s, jnp.transpose(cache_a_k[l], (0, 2, 3, 1)),
                        jnp.transpose(cache_a_v[l], (0, 2, 3, 1)), rel_bias[l], seq_s)
    ob_s, ss = _gla(qb_s, kb_s, vb_s, la_s, _state_to_pairs(state_gla[l]), n_s, seq_s, seq_s, 1)
    seg_s = 4
    tm_s = seg_s * seq_s
    ys = _stack(xs, oa_s, ob_s, r_s,
                cache_mem_k[l].reshape(n_s * N_MEM, D_MODEL),
                cache_mem_v[l].reshape(n_s * N_MEM, D_MODEL), stack_w, tm_s, seg_s, 1)

    keep = min(A_WINDOW, seq)
    heads = lambda z, n, t: z.reshape(1, n, t, N_HEADS_A, HEAD_DIM_A)
    return (yp.reshape(n_p, seq, D_MODEL), ys.reshape(n_s, seq_s, D_MODEL),
            heads(k_tail, n_p, keep), heads(v_tail, n_p, keep), _pairs_to_state(sp)[None],
            mk.reshape(1, n_p, N_MEM, N_HEADS_MEM, HEAD_DIM_MEM), mv.reshape(1, n_p, N_MEM, N_HEADS_MEM, HEAD_DIM_MEM),
            heads(k_new, n_s, seq_s), heads(v_new, n_s, seq_s), _pairs_to_state(ss)[None])
```

```python
import functools
import math

import jax
import jax.numpy as jnp
import numpy as np
from jax import lax
from jax.experimental import pallas as pl
from jax.experimental.pallas import tpu as pltpu

F32 = jnp.float32
BF16 = jnp.bfloat16

D_MODEL = 1024
CHUNK = 64
BAND_CHUNKS = 8
A_WINDOW = BAND_CHUNKS * CHUNK
N_HEADS_A = 8
HEAD_DIM_A = 64
WIDTH_A = 512
REL_CLIP = 256
N_HEADS_B = 4
KEY_DIM_B = 64
VAL_DIM_B = 128
WIDTH_BK = 256
WIDTH_BV = 512
GATE_RANK = 16
GATE_TAU = 16.0
N_MEM = 256
N_HEADS_MEM = 4
HEAD_DIM_MEM = 256
D_FF = 2816
EPS = 1e-6
NEG_BIG = -1e30
NT_DIMS = (((1,), (1,)), ((), ()))

LANES = 128
BIAS_W = 640
FF_BLOCK = 256
VMEM_LIMIT = 56 * 1024 * 1024

OFF_QA, OFF_KA, OFF_VA, OFF_QB, OFF_KB, OFF_VB, OFF_R = 0, 512, 1024, 1536, 1792, 2048, 2560
IN_WIDTH = OFF_R + GATE_RANK + WIDTH_BV


def _rms(x, g):
    return x * lax.rsqrt(jnp.mean(x * x, axis=-1, keepdims=True) + EPS) * g


def _silu(x):
    return x / (1.0 + jnp.exp(-x))


def _log_sigmoid(z):
    return jnp.minimum(z, 0.0) - jnp.log(1.0 + jnp.exp(-jnp.abs(z)))


def _const_spec(shape):
    nd = len(shape)
    return pl.BlockSpec(shape, lambda *_: (0,) * nd, pipeline_mode=pl.Buffered(1))


def _params(n_axes, vmem=None):
    return pltpu.CompilerParams(dimension_semantics=("arbitrary",) * n_axes, vmem_limit_bytes=vmem)


def _proj_kernel(tiles_per_seq, tail_t, x_ref, g_ref, w_ref, wa2_ref, ba_ref,
                 qa_ref, ka_ref, va_ref, qb_ref, kb_ref, vb_ref, r_ref, la_ref, kt_ref, vt_ref):
    h = _rms(x_ref[...], g_ref[...]).astype(BF16)

    def proj(lo, hi):
        return lax.dot_general(h, w_ref[lo:hi, :], NT_DIMS, preferred_element_type=F32)

    g_low = proj(OFF_R, OFF_R + LANES).astype(BF16)
    z = jnp.dot(g_low, wa2_ref[...], preferred_element_type=F32) + ba_ref[...]
    la_ref[...] = _log_sigmoid(z) * (1.0 / GATE_TAU)
    qa_ref[...] = (proj(OFF_QA, OFF_KA) * (HEAD_DIM_A ** -0.5)).astype(BF16)
    ka = proj(OFF_KA, OFF_VA)
    va = proj(OFF_VA, OFF_QB)
    ka_ref[...] = ka.astype(BF16)
    va_ref[...] = va.astype(BF16)
    qb_ref[...] = proj(OFF_QB, OFF_KB) * (KEY_DIM_B ** -0.5)
    kb_ref[...] = proj(OFF_KB, OFF_VB)
    vb_ref[...] = proj(OFF_VB, OFF_R).astype(BF16)
    r_ref[...] = proj(OFF_R + GATE_RANK, IN_WIDTH)

    @pl.when(pl.program_id(0) % tiles_per_seq == tiles_per_seq - 1)
    def _():
        kt_ref[...] = ka.T if tail_t else ka
        vt_ref[...] = va.T if tail_t else va


def _proj(x, g, w_t, wa2, ba, tm, tiles_per_seq, tail_t):
    ntok = x.shape[0]
    nt = ntok // tm
    nseq = nt // tiles_per_seq
    row = lambda w: pl.BlockSpec((tm, w), lambda i: (i, 0))
    tail_shape = (WIDTH_A, tm) if tail_t else (tm, WIDTH_A)
    tail = pl.BlockSpec(tail_shape, lambda i: (i // tiles_per_seq, 0))
    out_shape = (
        jax.ShapeDtypeStruct((ntok, WIDTH_A), BF16),
        jax.ShapeDtypeStruct((ntok, WIDTH_A), BF16),
        jax.ShapeDtypeStruct((ntok, WIDTH_A), BF16),
        jax.ShapeDtypeStruct((ntok, WIDTH_BK), F32),
        jax.ShapeDtypeStruct((ntok, WIDTH_BK), F32),
        jax.ShapeDtypeStruct((ntok, WIDTH_BV), BF16),
        jax.ShapeDtypeStruct((ntok, WIDTH_BV), F32),
        jax.ShapeDtypeStruct((ntok, WIDTH_BK), F32),
        jax.ShapeDtypeStruct((nseq * tail_shape[0], tail_shape[1]), F32),
        jax.ShapeDtypeStruct((nseq * tail_shape[0], tail_shape[1]), F32),
    )
    return pl.pallas_call(
        functools.partial(_proj_kernel, tiles_per_seq, tail_t),
        grid=(nt,),
        in_specs=[row(D_MODEL), _const_spec((1, D_MODEL)), _const_spec((IN_WIDTH, D_MODEL)),
                  _const_spec((LANES, WIDTH_BK)), _const_spec((1, WIDTH_BK))],
        out_specs=(row(WIDTH_A), row(WIDTH_A), row(WIDTH_A), row(WIDTH_BK), row(WIDTH_BK),
                   row(WIDTH_BV), row(WIDTH_BV), row(WIDTH_BK), tail, tail),
        out_shape=out_shape,
        compiler_params=_params(1, VMEM_LIMIT),
        name="proj",
    )(x, g, w_t, wa2, ba)


def _stack_heads(x_pair):
    low_half = lax.broadcasted_iota(jnp.int32, x_pair.shape, 1) < HEAD_DIM_A
    zero = jnp.zeros_like(x_pair)
    return jnp.concatenate([jnp.where(low_half, x_pair, zero), jnp.where(low_half, zero, x_pair)], axis=0)


TQ = 4 * CHUNK
KEYS = 3 * TQ
SOFT_W = 5 * LANES
assert 2 * TQ == A_WINDOW and SOFT_W >= (BAND_CHUNKS + 1) * CHUNK + CHUNK


def _build_band_bias(base_ref, bm_ref):
    n_band = (BAND_CHUNKS + 1) * CHUNK
    col = lax.broadcasted_iota(jnp.int32, (CHUNK, KEYS), 1)
    for h in range(N_HEADS_A):
        rows = jnp.broadcast_to(base_ref[h:h + 1, :], (CHUNK, KEYS))
        for qc in range(TQ // CHUNK):
            toeplitz = pltpu.roll(rows, qc * CHUNK, 1, stride=1, stride_axis=0)
            in_band = (col >= qc * CHUNK) & (col < qc * CHUNK + n_band)
            r0 = (qc * 2 + h % 2) * CHUNK
            bm_ref[h // 2, r0:r0 + CHUNK, :] = jnp.where(in_band, toeplitz, NEG_BIG)


def _band_tile(q_ref, k_refs, v_refs, bm_ref, o_ref, start_penalty):
    n_qc = TQ // CHUNK
    low_half = lax.broadcasted_iota(jnp.int32, (CHUNK, LANES), 1) < HEAD_DIM_A
    nt_dims = (((1,), (1,)), ((), ()))
    for hp in range(N_HEADS_A // 2):
        cols = slice(hp * LANES, (hp + 1) * LANES)
        qs = jnp.concatenate([_stack_heads(q_ref[qc * CHUNK:(qc + 1) * CHUNK, cols]) for qc in range(n_qc)],
                             axis=0)
        s = jnp.concatenate([lax.dot_general(qs, k[:, cols], nt_dims, preferred_element_type=F32)
                             for k in k_refs], axis=1)
        probs, inv_l = [], []
        for qc in range(n_qc):
            rows = slice(qc * 2 * CHUNK, (qc + 1) * 2 * CHUNK)
            c0 = 0 if (qc + 1) * CHUNK + A_WINDOW <= SOFT_W else KEYS - SOFT_W
            sq = s[rows, c0:c0 + SOFT_W] + bm_ref[hp, rows, c0:c0 + SOFT_W]
            if start_penalty is not None:
                sq = sq + start_penalty[:, c0:c0 + SOFT_W]
            p = jnp.exp(sq - jnp.max(sq, axis=-1, keepdims=True))
            inv_l.append(1.0 / jnp.sum(p, axis=-1, keepdims=True))
            pad = jnp.zeros((2 * CHUNK, KEYS - SOFT_W), BF16)
            probs.append(jnp.concatenate([p.astype(BF16), pad] if c0 == 0 else [pad, p.astype(BF16)], axis=1))
        pm = jnp.concatenate(probs, axis=0)
        out = sum(jnp.dot(pm[:, i * TQ:(i + 1) * TQ], v[:, cols], preferred_element_type=F32)
                  for i, v in enumerate(v_refs))
        for qc in range(n_qc):
            r0 = qc * 2 * CHUNK
            o0 = out[r0:r0 + CHUNK] * inv_l[qc][0:CHUNK]
            o1 = out[r0 + CHUNK:r0 + 2 * CHUNK] * inv_l[qc][CHUNK:2 * CHUNK]
            o_ref[qc * CHUNK:(qc + 1) * CHUNK, cols] = jnp.where(low_half, o0, o1).astype(o_ref.dtype)


def _attn_prompt_kernel(q_ref, k2_ref, k1_ref, k0_ref, v2_ref, v1_ref, v0_ref, base_ref, o_ref, bm):
    t = pl.program_id(1)

    @pl.when((pl.program_id(0) == 0) & (t == 0))
    def _():
        _build_band_bias(base_ref, bm)

    k_refs = (k2_ref, k1_ref, k0_ref)
    v_refs = (v2_ref, v1_ref, v0_ref)

    @pl.when(t >= 2)
    def _():
        _band_tile(q_ref, k_refs, v_refs, bm, o_ref, None)

    @pl.when(t < 2)
    def _():
        col = lax.broadcasted_iota(jnp.int32, (1, KEYS), 1)
        penalty = jnp.where(col < (2 - t) * TQ, NEG_BIG, 0.0)
        _band_tile(q_ref, k_refs, v_refs, bm, o_ref, penalty)


def _attn_prompt(q, k, v, base, n_batch, seq):
    nt = seq // TQ
    blk = lambda back: pl.BlockSpec((TQ, WIDTH_A), lambda b, t: (b * nt + jnp.maximum(t - back, 0), 0))
    return pl.pallas_call(
        _attn_prompt_kernel,
        grid=(n_batch, nt),
        in_specs=[blk(0), blk(2), blk(1), blk(0), blk(2), blk(1), blk(0), _const_spec((N_HEADS_A, KEYS))],
        out_specs=blk(0),
        out_shape=jax.ShapeDtypeStruct((n_batch * seq, WIDTH_A), BF16),
        scratch_shapes=[pltpu.VMEM((N_HEADS_A // 2, 2 * TQ, KEYS), F32)],
        compiler_params=_params(2, VMEM_LIMIT),
        name="band_attn_prompt",
    )(q, k, k, k, v, v, v, base)


def _attn_sample_kernel(n_new, q_ref, kn_ref, vn_ref, kc_ref, vc_ref, base_ref, sel_ref, dup_ref,
                        o_ref, bias_c, bias_n):
    n_h = N_HEADS_A
    n_cache = kc_ref.shape[2]

    @pl.when(pl.program_id(0) == 0)
    def _():
        for h in range(n_h):
            rows = jnp.broadcast_to(base_ref[h:h + 1, :], (n_new, BIAS_W))
            toeplitz = pltpu.roll(rows, 0, 1, stride=1, stride_axis=0)
            bias_c[h * n_new:(h + 1) * n_new, :] = toeplitz[:, 0:n_cache]
            bias_n[h * n_new:(h + 1) * n_new, :] = toeplitz[:, n_cache:n_cache + n_new]

    nt_dims = (((1,), (1,)), ((), ()))
    low_half = lax.broadcasted_iota(jnp.int32, (n_new, LANES), 1) < HEAD_DIM_A
    pairs = [slice(hp * LANES, (hp + 1) * LANES) for hp in range(n_h // 2)]
    q_heads = [jnp.dot(q_ref[:, cols], sel_ref[e], preferred_element_type=F32).astype(BF16)
               for cols in pairs for e in range(2)]
    s_c = jnp.concatenate([jnp.dot(q_heads[h], kc_ref[h].astype(BF16), preferred_element_type=F32)
                           for h in range(n_h)], axis=0) + bias_c[...]
    s_n = jnp.concatenate([lax.dot_general(_stack_heads(q_ref[:, cols]), kn_ref[:, cols], nt_dims,
                                           preferred_element_type=F32) for cols in pairs], axis=0) + bias_n[...]
    m = jnp.maximum(jnp.max(s_c, axis=-1, keepdims=True), jnp.max(s_n, axis=-1, keepdims=True))
    p_c, p_n = jnp.exp(s_c - m), jnp.exp(s_n - m)
    inv_l = 1.0 / (jnp.sum(p_c, axis=-1, keepdims=True) + jnp.sum(p_n, axis=-1, keepdims=True))
    p_c, p_n = p_c.astype(BF16), p_n.astype(BF16)
    out_c = jnp.concatenate([lax.dot_general(p_c[h * n_new:(h + 1) * n_new], vc_ref[h].astype(BF16), nt_dims,
                                             preferred_element_type=F32) for h in range(n_h)], axis=0) * inv_l
    hi = out_c.astype(BF16)
    lo = (out_c - hi.astype(F32)).astype(BF16)
    out_c = (jnp.dot(hi, dup_ref[...], preferred_element_type=F32)
             + jnp.dot(lo, dup_ref[...], preferred_element_type=F32))
    for hp, cols in enumerate(pairs):
        rows = slice(hp * 2 * n_new, (hp + 1) * 2 * n_new)
        out = out_c[rows] + jnp.dot(p_n[rows], vn_ref[:, cols], preferred_element_type=F32) * inv_l[rows]
        o_ref[:, cols] = jnp.where(low_half, out[0:n_new], out[n_new:]).astype(o_ref.dtype)


def _attn_sample(q, k_new, v_new, k_cache_t, v_cache_t, table, n_new):
    n_batch, n_h, _, n_cache = k_cache_t.shape
    assert n_cache == A_WINDOW and n_new <= CHUNK and n_h == N_HEADS_A
    lane = np.arange(LANES)
    sel = np.stack([lane[:, None] == np.arange(HEAD_DIM_A)[None, :] + e * HEAD_DIM_A for e in range(2)])
    dup = np.arange(HEAD_DIM_A)[:, None] == lane[None, :] % HEAD_DIM_A
    new = pl.BlockSpec((n_new, WIDTH_A), lambda b: (b, 0))
    old = pl.BlockSpec((None, n_h, HEAD_DIM_A, n_cache), lambda b: (b, 0, 0, 0))
    return pl.pallas_call(
        functools.partial(_attn_sample_kernel, n_new),
        grid=(n_batch,),
        in_specs=[new, new, new, old, old, _const_spec((n_h, BIAS_W)),
                  _const_spec((2, LANES, HEAD_DIM_A)), _const_spec((HEAD_DIM_A, LANES))],
        out_specs=new,
        out_shape=jax.ShapeDtypeStruct((n_batch * n_new, WIDTH_A), BF16),
        scratch_shapes=[pltpu.VMEM((n_h * n_new, n_cache), F32), pltpu.VMEM((n_h * n_new, n_new), F32)],
        compiler_params=_params(1, VMEM_LIMIT),
        name="band_attn_sample",
    )(q, k_new, v_new, k_cache_t, v_cache_t, _bias_base(table, BIAS_W), jnp.asarray(sel, BF16),
      jnp.asarray(dup, BF16))


def _gla_levels(chunk):
    return [chunk >> i for i in range(int(math.log2(chunk)) + 1)]


def _segment_matrix(chunk):
    i = np.arange(chunk)[:, None]
    t = np.arange(chunk)[None, :]
    blocks = []
    for li, s in enumerate(_gla_levels(chunk)):
        start = (i // s) * s
        f_rows = (t >= start) & (t <= i)
        r_rows = (t > i) & (t <= start + s - 1)
        if li == 0:
            blocks += [f_rows, r_rows]
        else:
            blocks.append(np.where((i // s) % 2 == 1, f_rows, r_rows))
    seg = np.concatenate(blocks, axis=0).astype(np.float32)
    return np.concatenate([seg, seg], axis=1)


def _level_masks(group_tokens):
    tg = group_tokens
    i = (np.arange(2 * tg) % tg)[:, None]
    j = np.arange(tg)[None, :]
    masks = [i == j]
    s = tg // 2
    while s >= 1:
        masks.append(((i // s) % 2 == 1) & (j // s == i // s - 1))
        s //= 2
    return np.stack(masks).astype(np.float32)


def _gla_tile(q, k, v, la, seg_ref, mask_ref, state_ref, chunk, n_chunks):
    n, tg = n_chunks, chunk * n_chunks
    levels = _gla_levels(chunk)
    la_hi = la.astype(BF16)
    la_lo = (la - la_hi.astype(F32)).astype(BF16)
    split = jnp.concatenate(
        [jnp.concatenate([la_hi[c * chunk:(c + 1) * chunk], la_lo[c * chunk:(c + 1) * chunk]], axis=0)
         for c in range(n)], axis=1)
    sums = jnp.dot(seg_ref[...], split, preferred_element_type=F32)

    def seg_sum(block):
        rows = slice(block * chunk, (block + 1) * chunk)
        return jnp.minimum(jnp.concatenate(
            [sums[rows, c * WIDTH_BK:(c + 1) * WIDTH_BK] for c in range(n)], axis=0), 0.0)

    fwd0, rev0 = seg_sum(0), seg_sum(1)
    from_start, to_end = jnp.exp(fwd0), jnp.exp(rev0)
    total = [fwd0[(c + 1) * chunk - 1:(c + 1) * chunk] for c in range(n)]

    def chunk_sum(cs):
        cs = list(cs)
        return (sum(total[c] for c in cs[1:]) + total[cs[0]]) if cs else None

    def extend(base, offsets):
        pieces = []
        for c in range(n):
            x = base[c * chunk:(c + 1) * chunk]
            pieces.append(x if offsets[c] is None else x * jnp.exp(offsets[c]))
        return jnp.concatenate(pieces, axis=0) if n > 1 else pieces[0]

    q_state = (q * extend(from_start, [chunk_sum(range(0, c)) for c in range(n)])).astype(BF16)
    k_state = (k * extend(to_end, [chunk_sum(range(c + 1, n)) for c in range(n)])).astype(BF16)
    decay_tile = jnp.exp(chunk_sum(range(n)))
    ops = {}
    s = tg // 2
    while s >= chunk and n > 1:
        per = s // chunk
        offsets, bases = [], []
        for c in range(n):
            sg = c // per
            offsets.append(chunk_sum(range(sg * per, c)) if sg % 2 else chunk_sum(range(c + 1, (sg + 1) * per)))
            bases.append((from_start if sg % 2 else to_end)[c * chunk:(c + 1) * chunk])
        w = extend(jnp.concatenate(bases, axis=0), offsets)
        ops[s] = ((q * w).astype(BF16), (k * w).astype(BF16))
        s //= 2
    for li in range(1, len(levels)):
        w = jnp.exp(seg_sum(li + 1))
        ops[levels[li]] = ((q * w).astype(BF16), (k * w).astype(BF16))
    q16, k16 = q.astype(BF16), k.astype(BF16)

    gt = mask_ref.shape[2]
    own = (lax.broadcasted_iota(jnp.int32, (2 * VAL_DIM_B, LANES), 0) < VAL_DIM_B) == \
          (lax.broadcasted_iota(jnp.int32, (2 * VAL_DIM_B, LANES), 1) < KEY_DIM_B)
    nt_dims = (((1,), (1,)), ((), ()))
    tn_dims = (((0,), (0,)), ((), ()))
    n_pairs, n_groups = N_HEADS_B // 2, tg // gt
    pair_cols = [slice(p * LANES, (p + 1) * LANES) for p in range(n_pairs)]
    pair_v = [v[:, p * 2 * VAL_DIM_B:(p + 1) * 2 * VAL_DIM_B] for p in range(n_pairs)]
    head_v = lambda e: slice(e * VAL_DIM_B, (e + 1) * VAL_DIM_B)
    o_state = []
    for p, cols in enumerate(pair_cols):
        st = state_ref[p]
        o_state.append(lax.dot_general(q_state[:, cols], st.astype(BF16), nt_dims, preferred_element_type=F32))
        upd = lax.dot_general(pair_v[p], k_state[:, cols], tn_dims, preferred_element_type=F32)
        state_ref[p] = st * decay_tile[:, cols] + jnp.where(own, upd, 0.0)
    blocks = []
    for p, cols in enumerate(pair_cols):
        for g in range(n_groups):
            rows = slice(g * gt, (g + 1) * gt)
            a = mask_ref[0] * lax.dot_general(_stack_heads(q16[rows, cols]), k16[rows, cols], nt_dims,
                                              preferred_element_type=F32)
            m, s = 1, gt // 2
            while s >= 1:
                qd, kd = ops[s]
                a = a + mask_ref[m] * lax.dot_general(_stack_heads(qd[rows, cols]), kd[rows, cols], nt_dims,
                                                      preferred_element_type=F32)
                m, s = m + 1, s // 2
            blocks.append((p, g * gt, g * gt, a.astype(BF16)))
        s = tg // 2
        while s >= gt:
            qd, kd = ops[s]
            for blk in range(tg // (2 * s)):
                k0, q0 = blk * 2 * s, blk * 2 * s + s
                ab = lax.dot_general(_stack_heads(qd[q0:q0 + s, cols]), kd[k0:k0 + s, cols], nt_dims,
                                     preferred_element_type=F32)
                blocks.append((p, q0, k0, ab.astype(BF16)))
            s //= 2
    piece = [[[o_state[p][g * gt:(g + 1) * gt, head_v(e)] for g in range(n_groups)] for e in range(2)]
             for p in range(n_pairs)]
    for p, q0, k0, a in blocks:
        nq, nk = a.shape[0] // 2, a.shape[1]
        for e in range(2):
            contrib = jnp.dot(a[e * nq:(e + 1) * nq], pair_v[p][k0:k0 + nk, head_v(e)], preferred_element_type=F32)
            for gi in range(nq // gt):
                g = q0 // gt + gi
                piece[p][e][g] = piece[p][e][g] + contrib[gi * gt:(gi + 1) * gt]
    outs = [jnp.concatenate(piece[p][e], axis=0) if n_groups > 1 else piece[p][e][0]
            for p in range(n_pairs) for e in range(2)]
    return jnp.concatenate(outs, axis=1)


def _gla_kernel(chunk, chunks_per_tile, has_init, *refs):
    if has_init:
        q_ref, k_ref, v_ref, la_ref, seg_ref, mask_ref, s0_ref, o_ref, sout_ref, state = refs
    else:
        q_ref, k_ref, v_ref, la_ref, seg_ref, mask_ref, o_ref, sout_ref, state = refs
    t = pl.program_id(1)

    @pl.when(t == 0)
    def _():
        state[...] = s0_ref[...] if has_init else jnp.zeros_like(state)

    o_ref[...] = _gla_tile(q_ref[...], k_ref[...], v_ref[...], la_ref[...], seg_ref, mask_ref, state,
                           chunk, chunks_per_tile)

    @pl.when(t == pl.num_programs(1) - 1)
    def _():
        sout_ref[...] = state[...]


def _gla(q, k, v, la, s0, n_batch, seq, chunk, chunks_per_tile):
    tg = chunk * chunks_per_tile
    nt = seq // tg
    seg = jnp.asarray(_segment_matrix(chunk), BF16)
    masks = jnp.asarray(_level_masks(min(tg, LANES)), F32)
    row = lambda w: pl.BlockSpec((tg, w), lambda b, t: (b * nt + t, 0))
    st = pl.BlockSpec((None, 2, 2 * VAL_DIM_B, LANES), lambda b, t: (b, 0, 0, 0))
    in_specs = [row(WIDTH_BK), row(WIDTH_BK), row(WIDTH_BV), row(WIDTH_BK), _const_spec(seg.shape),
                _const_spec(masks.shape)]
    args = [q, k, v, la, seg, masks]
    if s0 is not None:
        in_specs.append(st)
        args.append(s0)
    return pl.pallas_call(
        functools.partial(_gla_kernel, chunk, chunks_per_tile, s0 is not None),
        grid=(n_batch, nt),
        in_specs=in_specs,
        out_specs=(row(WIDTH_BV), st),
        out_shape=(jax.ShapeDtypeStruct((n_batch * seq, WIDTH_BV), F32),
                   jax.ShapeDtypeStruct((n_batch, 2, 2 * VAL_DIM_B, LANES), F32)),
        scratch_shapes=[pltpu.VMEM((2, 2 * VAL_DIM_B, LANES), F32)],
        compiler_params=_params(2, VMEM_LIMIT),
        name="gla_chunk%d" % chunk,
    )(*args)


def _state_to_pairs(s):
    n = s.shape[0]
    st = jnp.swapaxes(s, -1, -2).reshape(n, 2, 2, VAL_DIM_B, KEY_DIM_B)
    z = jnp.zeros_like(st[:, :, 0])
    top = jnp.concatenate([st[:, :, 0], z], axis=-1)
    bot = jnp.concatenate([z, st[:, :, 1]], axis=-1)
    return jnp.concatenate([top, bot], axis=-2)


def _pairs_to_state(sp):
    n = sp.shape[0]
    h0 = sp[:, :, :VAL_DIM_B, :KEY_DIM_B]
    h1 = sp[:, :, VAL_DIM_B:, KEY_DIM_B:]
    st = jnp.stack([h0, h1], axis=2).reshape(n, N_HEADS_B, VAL_DIM_B, KEY_DIM_B)
    return jnp.swapaxes(st, -1, -2)


def _mem_kv_kernel(m_ref, g_ref, wk_ref, wv_ref, k_ref, v_ref):
    m = _rms(m_ref[...], g_ref[...]).astype(BF16)
    k_ref[...] = jnp.dot(m, wk_ref[...], preferred_element_type=F32)
    v_ref[...] = jnp.dot(m, wv_ref[...], preferred_element_type=F32)


def _mem_kv(mem, g, wk, wv):
    n = mem.shape[0]
    tm = N_MEM
    row = pl.BlockSpec((tm, D_MODEL), lambda i: (i, 0))
    return pl.pallas_call(
        _mem_kv_kernel,
        grid=(n // tm,),
        in_specs=[row, _const_spec((1, D_MODEL)), _const_spec((D_MODEL, D_MODEL)), _const_spec((D_MODEL, D_MODEL))],
        out_specs=(row, row),
        out_shape=(jax.ShapeDtypeStruct((n, D_MODEL), F32), jax.ShapeDtypeStruct((n, D_MODEL), F32)),
        compiler_params=_params(1, VMEM_LIMIT),
        name="mem_kv",
    )(mem, g, wk, wv)


def _stack_kernel(n_seg, x_ref, oa_ref, ob_ref, r_ref, mk_ref, mv_ref,
                  g_gla_ref, wo_ref, g_post_mix_ref, g_pre_mem_ref, wq_ref, wmo_ref, g_post_mem_ref,
                  g_pre_ffn_ref, wg_ref, wu_ref, wd_ref, g_post_ffn_ref, y_ref):
    tm = x_ref.shape[0]
    ob = ob_ref[...]
    normed = []
    for h in range(N_HEADS_B):
        seg = ob[:, h * VAL_DIM_B:(h + 1) * VAL_DIM_B]
        normed.append(seg * lax.rsqrt(jnp.mean(seg * seg, axis=-1, keepdims=True) + EPS))
    yb = jnp.concatenate(normed, axis=1) * g_gla_ref[...] * _silu(r_ref[...])
    mix = (jnp.dot(oa_ref[...], wo_ref[0:WIDTH_A, :], preferred_element_type=F32)
           + jnp.dot(yb.astype(BF16), wo_ref[WIDTH_A:, :], preferred_element_type=F32))
    x1 = x_ref[...] + _rms(mix, g_post_mix_ref[...])

    hq = _rms(x1, g_pre_mem_ref[...]).astype(BF16)
    q = (jnp.dot(hq, wq_ref[...], preferred_element_type=F32) * (HEAD_DIM_MEM ** -0.5)).astype(BF16)
    rows_per_seg = tm // n_seg
    units = [(sg, h) for sg in range(n_seg) for h in range(N_HEADS_MEM)]
    mem_rows = lambda sg: slice(sg * N_MEM, (sg + 1) * N_MEM)
    head_cols = lambda h: slice(h * HEAD_DIM_MEM, (h + 1) * HEAD_DIM_MEM)
    scores = [lax.dot_general(q[sg * rows_per_seg:(sg + 1) * rows_per_seg, head_cols(h)],
                              mk_ref[mem_rows(sg), head_cols(h)].astype(BF16), (((1,), (1,)), ((), ())),
                              preferred_element_type=F32) for sg, h in units]
    probs, inv_l = [], []
    for s in scores:
        p = jnp.exp(s - jnp.max(s, axis=-1, keepdims=True))
        inv_l.append(1.0 / jnp.sum(p, axis=-1, keepdims=True))
        probs.append(p.astype(BF16))
    outs = [jnp.dot(probs[u], mv_ref[mem_rows(sg), head_cols(h)].astype(BF16), preferred_element_type=F32) * inv_l[u]
            for u, (sg, h) in enumerate(units)]
    seg_outs = [jnp.concatenate(outs[sg * N_HEADS_MEM:(sg + 1) * N_HEADS_MEM], axis=1) for sg in range(n_seg)]
    o = jnp.concatenate(seg_outs, axis=0) if n_seg > 1 else seg_outs[0]
    att = jnp.dot(o.astype(BF16), wmo_ref[...], preferred_element_type=F32)
    x2 = x1 + _rms(att, g_post_mem_ref[...])

    hf = _rms(x2, g_pre_ffn_ref[...]).astype(BF16)
    f = jnp.zeros((tm, D_MODEL), F32)
    act, act_cols = None, None
    for c in range(D_FF // FF_BLOCK):
        cols = slice(c * FF_BLOCK, (c + 1) * FF_BLOCK)
        gate = jnp.dot(hf, wg_ref[:, cols], preferred_element_type=F32)
        up = jnp.dot(hf, wu_ref[:, cols], preferred_element_type=F32)
        if act is not None:
            f = f + jnp.dot(act, wd_ref[act_cols, :], preferred_element_type=F32)
        act, act_cols = (_silu(gate) * up).astype(BF16), cols
    f = f + jnp.dot(act, wd_ref[act_cols, :], preferred_element_type=F32)
    y_ref[...] = x2 + _rms(f, g_post_ffn_ref[...])


def _stack(x, oa, ob, r, mk, mv, weights, tm, n_seg, tiles_per_mem_block):
    ntok = x.shape[0]
    row = lambda w: pl.BlockSpec((tm, w), lambda i: (i, 0))
    mem = pl.BlockSpec((n_seg * N_MEM, D_MODEL), lambda i: (i // tiles_per_mem_block, 0))
    w_specs = [_const_spec(w.shape) for w in weights]
    return pl.pallas_call(
        functools.partial(_stack_kernel, n_seg),
        grid=(ntok // tm,),
        in_specs=[row(D_MODEL), row(WIDTH_A), row(WIDTH_BV), row(WIDTH_BV), mem, mem] + w_specs,
        out_specs=row(D_MODEL),
        out_shape=jax.ShapeDtypeStruct((ntok, D_MODEL), F32),
        compiler_params=_params(1, VMEM_LIMIT),
        name="token_stack",
    )(x, oa, ob, r, mk, mv, *weights)


def _bias_base(table, width):
    u = np.arange(width)
    idx = np.where(u < (BAND_CHUNKS + 1) * CHUNK, np.clip(A_WINDOW - u, -REL_CLIP, REL_CLIP) + REL_CLIP, 2 * REL_CLIP)
    return table[:, idx]


def kernel(x_prompt, x_sample, mem_prompt, cache_a_k, cache_a_v, state_gla, cache_mem_k, cache_mem_v,
           g_pre_mix, w_in, rel_bias, w_alpha2, b_alpha, g_gla_out, w_o, g_post_mix,
           g_pre_mem, g_mem, w_mq, w_mk, w_mv, w_mo, g_post_mem,
           g_pre_ffn, w_ffn_gate, w_ffn_up, w_ffn_down, g_post_ffn):
    depth = w_in.shape[0]
    assert depth == 1
    l = 0
    n_p, seq, _ = x_prompt.shape
    n_s, seq_s, _ = x_sample.shape
    vec = lambda g: g[l].reshape(1, -1)

    wi = w_in[l]
    w_t = jnp.swapaxes(wi, 0, 1).astype(BF16)
    wa2 = jnp.pad(w_alpha2[l], ((0, LANES - GATE_RANK), (0, 0))).astype(BF16)
    stack_w = [vec(g_gla_out), w_o[l].astype(BF16), vec(g_post_mix), vec(g_pre_mem), w_mq[l].astype(BF16),
               w_mo[l].astype(BF16), vec(g_post_mem), vec(g_pre_ffn), w_ffn_gate[l].astype(BF16),
               w_ffn_up[l].astype(BF16), w_ffn_down[l].astype(BF16), vec(g_post_ffn)]

    xp = x_prompt.reshape(n_p * seq, D_MODEL)
    tm_p = 512
    qa, ka, va, qb, kb, vb, r, la, k_tail, v_tail = _proj(xp, vec(g_pre_mix), w_t, wa2, vec(b_alpha), tm_p, seq // tm_p, True)
    oa = _attn_prompt(qa, ka, va, _bias_base(rel_bias[l], KEYS), n_p, seq)
    ob, sp = _gla(qb, kb, vb, la, None, n_p, seq, CHUNK, 4)
    mk, mv = _mem_kv(mem_prompt.reshape(n_p * N_MEM, D_MODEL), vec(g_mem), w_mk[l].astype(BF16), w_mv[l].astype(BF16))
    tm = 512
    yp = _stack(xp, oa, ob, r, mk.astype(BF16), mv.astype(BF16), stack_w, tm, 1, seq // tm)

    xs = x_sample.reshape(n_s * seq_s, D_MODEL)
    ntok_s = n_s * seq_s
    qa_s, ka_s, va_s, qb_s, kb_s, vb_s, r_s, la_s, k_new, v_new = _proj(
        xs, vec(g_pre_mix), w_t, wa2, vec(b_alpha), ntok_s, 1, False)
    oa_s = _attn_sample(qa_s, ka_s, va_s, jnp.transpose(cache_a_k[l], (0, 2, 3, 1)),
                        jnp.transpose(cache_a_v[l], (0, 2, 3, 1)), rel_bias[l], seq_s)
    ob_s, ss = _gla(qb_s, kb_s, vb_s, la_s, _state_to_pairs(state_gla[l]), n_s, seq_s, seq_s, 1)
    seg_s = 4
    tm_s = seg_s * seq_s
    ys = _stack(xs, oa_s, ob_s, r_s,
                cache_mem_k[l].reshape(n_s * N_MEM, D_MODEL),
                cache_mem_v[l].reshape(n_s * N_MEM, D_MODEL), stack_w, tm_s, seg_s, 1)

    keep = min(A_WINDOW, seq)
    heads = lambda z, n, t: z.reshape(1, n, t, N_HEADS_A, HEAD_DIM_A)
    tails = lambda z: jnp.transpose(z.reshape(n_p, N_HEADS_A, HEAD_DIM_A, keep), (0, 3, 1, 2))[None]
    return (yp.reshape(n_p, seq, D_MODEL), ys.reshape(n_s, seq_s, D_MODEL),
            tails(k_tail), tails(v_tail), _pairs_to_state(sp)[None],
            mk.reshape(1, n_p, N_MEM, N_HEADS_MEM, HEAD_DIM_MEM), mv.reshape(1, n_p, N_MEM, N_HEADS_MEM, HEAD_DIM_MEM),
            heads(k_new, n_s, seq_s), heads(v_new, n_s, seq_s), _pairs_to_state(ss)[None])
```

```python
import functools
import itertools
import math

import jax
import jax.numpy as jnp
import numpy as np
from jax import lax
from jax.experimental import pallas as pl
from jax.experimental.pallas import tpu as pltpu

F32 = jnp.float32
BF16 = jnp.bfloat16

D_MODEL = 1024
CHUNK = 64
BAND_CHUNKS = 8
A_WINDOW = BAND_CHUNKS * CHUNK
N_HEADS_A = 8
HEAD_DIM_A = 64
WIDTH_A = 512
REL_CLIP = 256
N_HEADS_B = 4
KEY_DIM_B = 64
VAL_DIM_B = 128
WIDTH_BK = 256
WIDTH_BV = 512
GATE_RANK = 16
GATE_TAU = 16.0
N_MEM = 256
N_HEADS_MEM = 4
HEAD_DIM_MEM = 256
D_FF = 2816
EPS = 1e-6
NEG_BIG = -1e30
NT_DIMS = (((1,), (1,)), ((), ()))

LANES = 128
BIAS_W = 640
FF_BLOCK = 256
VMEM_LIMIT = 56 * 1024 * 1024

OFF_QA, OFF_KA, OFF_VA, OFF_QB, OFF_KB, OFF_VB, OFF_R = 0, 512, 1024, 1536, 1792, 2048, 2560
IN_WIDTH = OFF_R + GATE_RANK + WIDTH_BV


def _rms(x, g):
    return x * lax.rsqrt(jnp.mean(x * x, axis=-1, keepdims=True) + EPS) * g


def _silu(x):
    return x / (1.0 + jnp.exp(-x))


def _log_sigmoid(z):
    return jnp.minimum(z, 0.0) - jnp.log(1.0 + jnp.exp(-jnp.abs(z)))


def _const_spec(shape):
    nd = len(shape)
    return pl.BlockSpec(shape, lambda *_: (0,) * nd, pipeline_mode=pl.Buffered(1))


def _params(n_axes, vmem=None):
    return pltpu.CompilerParams(dimension_semantics=("arbitrary",) * n_axes, vmem_limit_bytes=vmem)


def _proj_kernel(tiles_per_seq, tail_t, x_ref, g_ref, w_ref, wa2_ref, ba_ref,
                 qa_ref, ka_ref, va_ref, qb_ref, kb_ref, vb_ref, r_ref, la_ref, kt_ref, vt_ref):
    h = _rms(x_ref[...], g_ref[...]).astype(BF16)

    def proj(lo, hi):
        return lax.dot_general(h, w_ref[lo:hi, :], NT_DIMS, preferred_element_type=F32)

    g_low = proj(OFF_R, OFF_R + LANES).astype(BF16)
    z = jnp.dot(g_low, wa2_ref[...], preferred_element_type=F32) + ba_ref[...]
    la_ref[...] = _log_sigmoid(z) * (1.0 / GATE_TAU)
    qa_ref[...] = (proj(OFF_QA, OFF_KA) * (HEAD_DIM_A ** -0.5)).astype(BF16)
    ka = proj(OFF_KA, OFF_VA)
    va = proj(OFF_VA, OFF_QB)
    ka_ref[...] = ka.astype(BF16)
    va_ref[...] = va.astype(BF16)
    qb_ref[...] = proj(OFF_QB, OFF_KB) * (KEY_DIM_B ** -0.5)
    kb_ref[...] = proj(OFF_KB, OFF_VB)
    vb_ref[...] = proj(OFF_VB, OFF_R).astype(BF16)
    r_ref[...] = proj(OFF_R + GATE_RANK, IN_WIDTH)

    @pl.when(pl.program_id(0) % tiles_per_seq == tiles_per_seq - 1)
    def _():
        kt_ref[...] = ka.T if tail_t else ka
        vt_ref[...] = va.T if tail_t else va


def _proj(x, g, w_t, wa2, ba, tm, tiles_per_seq, tail_t):
    ntok = x.shape[0]
    nt = ntok // tm
    nseq = nt // tiles_per_seq
    row = lambda w: pl.BlockSpec((tm, w), lambda i: (i, 0))
    tail_shape = (WIDTH_A, tm) if tail_t else (tm, WIDTH_A)
    tail = pl.BlockSpec(tail_shape, lambda i: (i // tiles_per_seq, 0))
    out_shape = (
        jax.ShapeDtypeStruct((ntok, WIDTH_A), BF16),
        jax.ShapeDtypeStruct((ntok, WIDTH_A), BF16),
        jax.ShapeDtypeStruct((ntok, WIDTH_A), BF16),
        jax.ShapeDtypeStruct((ntok, WIDTH_BK), F32),
        jax.ShapeDtypeStruct((ntok, WIDTH_BK), F32),
        jax.ShapeDtypeStruct((ntok, WIDTH_BV), BF16),
        jax.ShapeDtypeStruct((ntok, WIDTH_BV), F32),
        jax.ShapeDtypeStruct((ntok, WIDTH_BK), F32),
        jax.ShapeDtypeStruct((nseq * tail_shape[0], tail_shape[1]), F32),
        jax.ShapeDtypeStruct((nseq * tail_shape[0], tail_shape[1]), F32),
    )
    return pl.pallas_call(
        functools.partial(_proj_kernel, tiles_per_seq, tail_t),
        grid=(nt,),
        in_specs=[row(D_MODEL), _const_spec((1, D_MODEL)), _const_spec((IN_WIDTH, D_MODEL)),
                  _const_spec((LANES, WIDTH_BK)), _const_spec((1, WIDTH_BK))],
        out_specs=(row(WIDTH_A), row(WIDTH_A), row(WIDTH_A), row(WIDTH_BK), row(WIDTH_BK),
                   row(WIDTH_BV), row(WIDTH_BV), row(WIDTH_BK), tail, tail),
        out_shape=out_shape,
        compiler_params=_params(1, VMEM_LIMIT),
        name="proj",
    )(x, g, w_t, wa2, ba)


def _stack_heads(x_pair):
    low_half = lax.broadcasted_iota(jnp.int32, x_pair.shape, 1) < HEAD_DIM_A
    zero = jnp.zeros_like(x_pair)
    return jnp.concatenate([jnp.where(low_half, x_pair, zero), jnp.where(low_half, zero, x_pair)], axis=0)


TQ = 4 * CHUNK
KEYS = 3 * TQ
SOFT_W = 5 * LANES
assert 2 * TQ == A_WINDOW and SOFT_W >= (BAND_CHUNKS + 1) * CHUNK + CHUNK


def _build_band_bias(base_ref, bm_ref):
    n_band = (BAND_CHUNKS + 1) * CHUNK
    col = lax.broadcasted_iota(jnp.int32, (CHUNK, KEYS), 1)
    for h in range(N_HEADS_A):
        rows = jnp.broadcast_to(base_ref[h:h + 1, :], (CHUNK, KEYS))
        for qc in range(TQ // CHUNK):
            toeplitz = pltpu.roll(rows, qc * CHUNK, 1, stride=1, stride_axis=0)
            in_band = (col >= qc * CHUNK) & (col < qc * CHUNK + n_band)
            r0 = (qc * 2 + h % 2) * CHUNK
            bm_ref[h // 2, r0:r0 + CHUNK, :] = jnp.where(in_band, toeplitz, NEG_BIG)


def _band_tile(q_ref, k_refs, v_refs, bm_ref, o_ref, start_penalty):
    n_qc, n_pairs = TQ // CHUNK, N_HEADS_A // 2
    low_half = lax.broadcasted_iota(jnp.int32, (CHUNK, LANES), 1) < HEAD_DIM_A
    pair_cols = lambda hp: slice(hp * LANES, (hp + 1) * LANES)

    def scores(hp):
        cols = pair_cols(hp)
        qs = jnp.concatenate([_stack_heads(q_ref[qc * CHUNK:(qc + 1) * CHUNK, cols]) for qc in range(n_qc)],
                             axis=0)
        return jnp.concatenate([lax.dot_general(qs, k[:, cols], NT_DIMS, preferred_element_type=F32)
                                for k in k_refs], axis=1)

    def softmax(hp, s):
        probs, inv_l = [], []
        for qc in range(n_qc):
            rows = slice(qc * 2 * CHUNK, (qc + 1) * 2 * CHUNK)
            c0 = 0 if (qc + 1) * CHUNK + A_WINDOW <= SOFT_W else KEYS - SOFT_W
            sq = s[rows, c0:c0 + SOFT_W] + bm_ref[hp, rows, c0:c0 + SOFT_W]
            if start_penalty is not None:
                sq = sq + start_penalty[:, c0:c0 + SOFT_W]
            p = jnp.exp(sq - jnp.max(sq, axis=-1, keepdims=True))
            inv_l.append(1.0 / jnp.sum(p, axis=-1, keepdims=True))
            pad = jnp.zeros((2 * CHUNK, KEYS - SOFT_W), BF16)
            probs.append(jnp.concatenate([p.astype(BF16), pad] if c0 == 0 else [pad, p.astype(BF16)], axis=1))
        return jnp.concatenate(probs, axis=0), inv_l

    def values(hp, pm, inv_l):
        cols = pair_cols(hp)
        out = sum(jnp.dot(pm[:, i * TQ:(i + 1) * TQ], v[:, cols], preferred_element_type=F32)
                  for i, v in enumerate(v_refs))
        for qc in range(n_qc):
            r0 = qc * 2 * CHUNK
            o0 = out[r0:r0 + CHUNK] * inv_l[qc][0:CHUNK]
            o1 = out[r0 + CHUNK:r0 + 2 * CHUNK] * inv_l[qc][CHUNK:2 * CHUNK]
            o_ref[qc * CHUNK:(qc + 1) * CHUNK, cols] = jnp.where(low_half, o0, o1).astype(o_ref.dtype)

    for hp in range(n_pairs):
        values(hp, *softmax(hp, scores(hp)))


def _attn_prompt_kernel(q_ref, k2_ref, k1_ref, k0_ref, v2_ref, v1_ref, v0_ref, base_ref, o_ref, bm):
    t = pl.program_id(1)

    @pl.when((pl.program_id(0) == 0) & (t == 0))
    def _():
        _build_band_bias(base_ref, bm)

    k_refs = (k2_ref, k1_ref, k0_ref)
    v_refs = (v2_ref, v1_ref, v0_ref)

    @pl.when(t >= 2)
    def _():
        _band_tile(q_ref, k_refs, v_refs, bm, o_ref, None)

    @pl.when(t < 2)
    def _():
        col = lax.broadcasted_iota(jnp.int32, (1, KEYS), 1)
        penalty = jnp.where(col < (2 - t) * TQ, NEG_BIG, 0.0)
        _band_tile(q_ref, k_refs, v_refs, bm, o_ref, penalty)


def _attn_prompt(q, k, v, base, n_batch, seq):
    nt = seq // TQ
    blk = lambda back: pl.BlockSpec((TQ, WIDTH_A), lambda b, t: (b * nt + jnp.maximum(t - back, 0), 0))
    return pl.pallas_call(
        _attn_prompt_kernel,
        grid=(n_batch, nt),
        in_specs=[blk(0), blk(2), blk(1), blk(0), blk(2), blk(1), blk(0), _const_spec((N_HEADS_A, KEYS))],
        out_specs=blk(0),
        out_shape=jax.ShapeDtypeStruct((n_batch * seq, WIDTH_A), BF16),
        scratch_shapes=[pltpu.VMEM((N_HEADS_A // 2, 2 * TQ, KEYS), F32)],
        compiler_params=_params(2, VMEM_LIMIT),
        name="band_attn_prompt",
    )(q, k, k, k, v, v, v, base)


def _attn_sample_kernel(n_new, q_ref, kn_ref, vn_ref, kc_ref, vc_ref, base_ref, sel_ref, dup_ref,
                        o_ref, bias_c, bias_n):
    n_h = N_HEADS_A
    n_cache = kc_ref.shape[2]

    @pl.when(pl.program_id(0) == 0)
    def _():
        for h in range(n_h):
            rows = jnp.broadcast_to(base_ref[h:h + 1, :], (n_new, BIAS_W))
            toeplitz = pltpu.roll(rows, 0, 1, stride=1, stride_axis=0)
            bias_c[h * n_new:(h + 1) * n_new, :] = toeplitz[:, 0:n_cache]
            bias_n[h * n_new:(h + 1) * n_new, :] = toeplitz[:, n_cache:n_cache + n_new]

    nt_dims = (((1,), (1,)), ((), ()))
    low_half = lax.broadcasted_iota(jnp.int32, (n_new, LANES), 1) < HEAD_DIM_A
    pairs = [slice(hp * LANES, (hp + 1) * LANES) for hp in range(n_h // 2)]
    q_heads = [jnp.dot(q_ref[:, cols], sel_ref[e], preferred_element_type=F32).astype(BF16)
               for cols in pairs for e in range(2)]
    s_c = jnp.concatenate([jnp.dot(q_heads[h], kc_ref[h].astype(BF16), preferred_element_type=F32)
                           for h in range(n_h)], axis=0) + bias_c[...]
    s_n = jnp.concatenate([lax.dot_general(_stack_heads(q_ref[:, cols]), kn_ref[:, cols], nt_dims,
                                           preferred_element_type=F32) for cols in pairs], axis=0) + bias_n[...]
    m = jnp.maximum(jnp.max(s_c, axis=-1, keepdims=True), jnp.max(s_n, axis=-1, keepdims=True))
    p_c, p_n = jnp.exp(s_c - m), jnp.exp(s_n - m)
    inv_l = 1.0 / (jnp.sum(p_c, axis=-1, keepdims=True) + jnp.sum(p_n, axis=-1, keepdims=True))
    p_c, p_n = p_c.astype(BF16), p_n.astype(BF16)
    out_c = jnp.concatenate([lax.dot_general(p_c[h * n_new:(h + 1) * n_new], vc_ref[h].astype(BF16), nt_dims,
                                             preferred_element_type=F32) for h in range(n_h)], axis=0) * inv_l
    hi = out_c.astype(BF16)
    lo = (out_c - hi.astype(F32)).astype(BF16)
    out_c = (jnp.dot(hi, dup_ref[...], preferred_element_type=F32)
             + jnp.dot(lo, dup_ref[...], preferred_element_type=F32))
    for hp, cols in enumerate(pairs):
        rows = slice(hp * 2 * n_new, (hp + 1) * 2 * n_new)
        out = out_c[rows] + jnp.dot(p_n[rows], vn_ref[:, cols], preferred_element_type=F32) * inv_l[rows]
        o_ref[:, cols] = jnp.where(low_half, out[0:n_new], out[n_new:]).astype(o_ref.dtype)


def _attn_sample(q, k_new, v_new, k_cache_t, v_cache_t, table, n_new):
    n_batch, n_h, _, n_cache = k_cache_t.shape
    assert n_cache == A_WINDOW and n_new <= CHUNK and n_h == N_HEADS_A
    lane = np.arange(LANES)
    sel = np.stack([lane[:, None] == np.arange(HEAD_DIM_A)[None, :] + e * HEAD_DIM_A for e in range(2)])
    dup = np.arange(HEAD_DIM_A)[:, None] == lane[None, :] % HEAD_DIM_A
    new = pl.BlockSpec((n_new, WIDTH_A), lambda b: (b, 0))
    old = pl.BlockSpec((None, n_h, HEAD_DIM_A, n_cache), lambda b: (b, 0, 0, 0))
    return pl.pallas_call(
        functools.partial(_attn_sample_kernel, n_new),
        grid=(n_batch,),
        in_specs=[new, new, new, old, old, _const_spec((n_h, BIAS_W)),
                  _const_spec((2, LANES, HEAD_DIM_A)), _const_spec((HEAD_DIM_A, LANES))],
        out_specs=new,
        out_shape=jax.ShapeDtypeStruct((n_batch * n_new, WIDTH_A), BF16),
        scratch_shapes=[pltpu.VMEM((n_h * n_new, n_cache), F32), pltpu.VMEM((n_h * n_new, n_new), F32)],
        compiler_params=_params(1, VMEM_LIMIT),
        name="band_attn_sample",
    )(q, k_new, v_new, k_cache_t, v_cache_t, _bias_base(table, BIAS_W), jnp.asarray(sel, BF16),
      jnp.asarray(dup, BF16))


def _gla_levels(chunk):
    return [chunk >> i for i in range(int(math.log2(chunk)) + 1)]


def _segment_matrix(chunk):
    i = np.arange(chunk)[:, None]
    t = np.arange(chunk)[None, :]
    blocks = []
    for li, s in enumerate(_gla_levels(chunk)):
        start = (i // s) * s
        f_rows = (t >= start) & (t <= i)
        r_rows = (t > i) & (t <= start + s - 1)
        if li == 0:
            blocks += [f_rows, r_rows]
        else:
            blocks.append(np.where((i // s) % 2 == 1, f_rows, r_rows))
    seg = np.concatenate(blocks, axis=0).astype(np.float32)
    return np.concatenate([seg, seg], axis=1)


def _level_masks(group_tokens):
    tg = group_tokens
    i = (np.arange(2 * tg) % tg)[:, None]
    j = np.arange(tg)[None, :]
    masks = [i == j]
    s = tg // 2
    while s >= 1:
        masks.append(((i // s) % 2 == 1) & (j // s == i // s - 1))
        s //= 2
    return np.stack(masks).astype(np.float32)


def _gla_tile(q, k, v, la, seg_ref, mask_ref, state_ref, o_ref, chunk, n_chunks):
    n, tg = n_chunks, chunk * n_chunks
    levels = _gla_levels(chunk)
    la_hi = la.astype(BF16)
    la_lo = (la - la_hi.astype(F32)).astype(BF16)
    split = jnp.concatenate(
        [jnp.concatenate([la_hi[c * chunk:(c + 1) * chunk], la_lo[c * chunk:(c + 1) * chunk]], axis=0)
         for c in range(n)], axis=1)
    sums = jnp.dot(seg_ref[...], split, preferred_element_type=F32)

    def seg_sum(block):
        rows = slice(block * chunk, (block + 1) * chunk)
        return jnp.minimum(jnp.concatenate(
            [sums[rows, c * WIDTH_BK:(c + 1) * WIDTH_BK] for c in range(n)], axis=0), 0.0)

    fwd0, rev0 = seg_sum(0), seg_sum(1)
    from_start, to_end = jnp.exp(fwd0), jnp.exp(rev0)
    total = [fwd0[(c + 1) * chunk - 1:(c + 1) * chunk] for c in range(n)]

    def chunk_sum(cs):
        cs = list(cs)
        return (sum(total[c] for c in cs[1:]) + total[cs[0]]) if cs else None

    def extend(base, offsets):
        pieces = []
        for c in range(n):
            x = base[c * chunk:(c + 1) * chunk]
            pieces.append(x if offsets[c] is None else x * jnp.exp(offsets[c]))
        return jnp.concatenate(pieces, axis=0) if n > 1 else pieces[0]

    q_state = (q * extend(from_start, [chunk_sum(range(0, c)) for c in range(n)])).astype(BF16)
    k_state = (k * extend(to_end, [chunk_sum(range(c + 1, n)) for c in range(n)])).astype(BF16)
    decay_tile = jnp.exp(chunk_sum(range(n)))
    ops = {}
    s = tg // 2
    while s >= chunk and n > 1:
        per = s // chunk
        offsets, bases = [], []
        for c in range(n):
            sg = c // per
            offsets.append(chunk_sum(range(sg * per, c)) if sg % 2 else chunk_sum(range(c + 1, (sg + 1) * per)))
            bases.append((from_start if sg % 2 else to_end)[c * chunk:(c + 1) * chunk])
        w = extend(jnp.concatenate(bases, axis=0), offsets)
        ops[s] = ((q * w).astype(BF16), (k * w).astype(BF16))
        s //= 2
    for li in range(1, len(levels)):
        w = jnp.exp(seg_sum(li + 1))
        ops[levels[li]] = ((q * w).astype(BF16), (k * w).astype(BF16))
    q16, k16 = q.astype(BF16), k.astype(BF16)
    yield

    gt = mask_ref.shape[2]
    own = (lax.broadcasted_iota(jnp.int32, (2 * VAL_DIM_B, LANES), 0) < VAL_DIM_B) == \
          (lax.broadcasted_iota(jnp.int32, (2 * VAL_DIM_B, LANES), 1) < KEY_DIM_B)
    nt_dims = (((1,), (1,)), ((), ()))
    tn_dims = (((0,), (0,)), ((), ()))
    n_pairs, n_groups = N_HEADS_B // 2, tg // gt
    pair_cols = [slice(p * LANES, (p + 1) * LANES) for p in range(n_pairs)]
    pair_v = [v[:, p * 2 * VAL_DIM_B:(p + 1) * 2 * VAL_DIM_B] for p in range(n_pairs)]
    head_v = lambda e: slice(e * VAL_DIM_B, (e + 1) * VAL_DIM_B)
    o_state = []
    for p, cols in enumerate(pair_cols):
        st = state_ref[p]
        o_state.append(lax.dot_general(q_state[:, cols], st.astype(BF16), nt_dims, preferred_element_type=F32))
        upd = lax.dot_general(pair_v[p], k_state[:, cols], tn_dims, preferred_element_type=F32)
        state_ref[p] = st * decay_tile[:, cols] + jnp.where(own, upd, 0.0)
    yield
    blocks = []
    for p, cols in enumerate(pair_cols):
        for g in range(n_groups):
            rows = slice(g * gt, (g + 1) * gt)
            a = mask_ref[0] * lax.dot_general(_stack_heads(q16[rows, cols]), k16[rows, cols], nt_dims,
                                              preferred_element_type=F32)
            m, s = 1, gt // 2
            while s >= 1:
                qd, kd = ops[s]
                a = a + mask_ref[m] * lax.dot_general(_stack_heads(qd[rows, cols]), kd[rows, cols], nt_dims,
                                                      preferred_element_type=F32)
                m, s = m + 1, s // 2
            blocks.append((p, g * gt, g * gt, a.astype(BF16)))
            yield
        s = tg // 2
        while s >= gt:
            qd, kd = ops[s]
            for blk in range(tg // (2 * s)):
                k0, q0 = blk * 2 * s, blk * 2 * s + s
                ab = lax.dot_general(_stack_heads(qd[q0:q0 + s, cols]), kd[k0:k0 + s, cols], nt_dims,
                                     preferred_element_type=F32)
                blocks.append((p, q0, k0, ab.astype(BF16)))
                yield
            s //= 2
    piece = [[[o_state[p][g * gt:(g + 1) * gt, head_v(e)] for g in range(n_groups)] for e in range(2)]
             for p in range(n_pairs)]
    for p, q0, k0, a in blocks:
        nq, nk = a.shape[0] // 2, a.shape[1]
        for e in range(2):
            contrib = jnp.dot(a[e * nq:(e + 1) * nq], pair_v[p][k0:k0 + nk, head_v(e)], preferred_element_type=F32)
            for gi in range(nq // gt):
                g = q0 // gt + gi
                piece[p][e][g] = piece[p][e][g] + contrib[gi * gt:(gi + 1) * gt]
        yield
    outs = [jnp.concatenate(piece[p][e], axis=0) if n_groups > 1 else piece[p][e][0]
            for p in range(n_pairs) for e in range(2)]
    o_ref[...] = jnp.concatenate(outs, axis=1)


def _gla_kernel(chunk, chunks_per_tile, has_init, *refs):
    if has_init:
        q_ref, k_ref, v_ref, la_ref, seg_ref, mask_ref, s0_ref, o_ref, sout_ref, state = refs
    else:
        q_ref, k_ref, v_ref, la_ref, seg_ref, mask_ref, o_ref, sout_ref, state = refs
    t = pl.program_id(1)

    @pl.when(t == 0)
    def _():
        state[...] = s0_ref[...] if has_init else jnp.zeros_like(state)

    for _ in _gla_tile(q_ref[...], k_ref[...], v_ref[...], la_ref[...], seg_ref, mask_ref, state, o_ref,
                       chunk, chunks_per_tile):
        pass

    @pl.when(t == pl.num_programs(1) - 1)
    def _():
        sout_ref[...] = state[...]


def _gla(q, k, v, la, s0, n_batch, seq, chunk, chunks_per_tile):
    tg = chunk * chunks_per_tile
    nt = seq // tg
    seg = jnp.asarray(_segment_matrix(chunk), BF16)
    masks = jnp.asarray(_level_masks(min(tg, LANES)), F32)
    row = lambda w: pl.BlockSpec((tg, w), lambda b, t: (b * nt + t, 0))
    st = pl.BlockSpec((None, 2, 2 * VAL_DIM_B, LANES), lambda b, t: (b, 0, 0, 0))
    in_specs = [row(WIDTH_BK), row(WIDTH_BK), row(WIDTH_BV), row(WIDTH_BK), _const_spec(seg.shape),
                _const_spec(masks.shape)]
    args = [q, k, v, la, seg, masks]
    if s0 is not None:
        in_specs.append(st)
        args.append(s0)
    return pl.pallas_call(
        functools.partial(_gla_kernel, chunk, chunks_per_tile, s0 is not None),
        grid=(n_batch, nt),
        in_specs=in_specs,
        out_specs=(row(WIDTH_BV), st),
        out_shape=(jax.ShapeDtypeStruct((n_batch * seq, WIDTH_BV), F32),
                   jax.ShapeDtypeStruct((n_batch, 2, 2 * VAL_DIM_B, LANES), F32)),
        scratch_shapes=[pltpu.VMEM((2, 2 * VAL_DIM_B, LANES), F32)],
        compiler_params=_params(2, VMEM_LIMIT),
        name="gla_chunk%d" % chunk,
    )(*args)


def _state_to_pairs(s):
    n = s.shape[0]
    st = jnp.swapaxes(s, -1, -2).reshape(n, 2, 2, VAL_DIM_B, KEY_DIM_B)
    z = jnp.zeros_like(st[:, :, 0])
    top = jnp.concatenate([st[:, :, 0], z], axis=-1)
    bot = jnp.concatenate([z, st[:, :, 1]], axis=-1)
    return jnp.concatenate([top, bot], axis=-2)


def _pairs_to_state(sp):
    n = sp.shape[0]
    h0 = sp[:, :, :VAL_DIM_B, :KEY_DIM_B]
    h1 = sp[:, :, VAL_DIM_B:, KEY_DIM_B:]
    st = jnp.stack([h0, h1], axis=2).reshape(n, N_HEADS_B, VAL_DIM_B, KEY_DIM_B)
    return jnp.swapaxes(st, -1, -2)


def _mem_kv_kernel(m_ref, g_ref, wk_ref, wv_ref, k_ref, v_ref):
    m = _rms(m_ref[...], g_ref[...]).astype(BF16)
    k_ref[...] = jnp.dot(m, wk_ref[...], preferred_element_type=F32)
    v_ref[...] = jnp.dot(m, wv_ref[...], preferred_element_type=F32)


def _mem_kv(mem, g, wk, wv):
    n = mem.shape[0]
    tm = N_MEM
    row = pl.BlockSpec((tm, D_MODEL), lambda i: (i, 0))
    return pl.pallas_call(
        _mem_kv_kernel,
        grid=(n // tm,),
        in_specs=[row, _const_spec((1, D_MODEL)), _const_spec((D_MODEL, D_MODEL)), _const_spec((D_MODEL, D_MODEL))],
        out_specs=(row, row),
        out_shape=(jax.ShapeDtypeStruct((n, D_MODEL), F32), jax.ShapeDtypeStruct((n, D_MODEL), F32)),
        compiler_params=_params(1, VMEM_LIMIT),
        name="mem_kv",
    )(mem, g, wk, wv)


def _stack_front(rows, segs, x_ref, oa_ref, ob_ref, r_ref, mk_ref, mv_ref, g_gla_ref, wo_ref, g_post_mix_ref,
                 g_pre_mem_ref, wq_ref, wmo_ref, g_post_mem_ref, result):
    n_rows = rows.stop - rows.start
    ob = ob_ref[rows, :]
    normed = []
    for h in range(N_HEADS_B):
        seg = ob[:, h * VAL_DIM_B:(h + 1) * VAL_DIM_B]
        normed.append(seg * lax.rsqrt(jnp.mean(seg * seg, axis=-1, keepdims=True) + EPS))
    yb = (jnp.concatenate(normed, axis=1) * g_gla_ref[...] * _silu(r_ref[rows, :])).astype(BF16)
    yield
    mix = (jnp.dot(oa_ref[rows, :], wo_ref[0:WIDTH_A, :], preferred_element_type=F32)
           + jnp.dot(yb, wo_ref[WIDTH_A:, :], preferred_element_type=F32))
    yield
    x1 = x_ref[rows, :] + _rms(mix, g_post_mix_ref[...])
    hq = _rms(x1, g_pre_mem_ref[...]).astype(BF16)
    yield
    q = (jnp.dot(hq, wq_ref[...], preferred_element_type=F32) * (HEAD_DIM_MEM ** -0.5)).astype(BF16)
    yield
    rows_per_seg = n_rows // len(segs)
    units = [(u, sg, h) for u, sg in enumerate(segs) for h in range(N_HEADS_MEM)]
    mem_rows = lambda sg: slice(sg * N_MEM, (sg + 1) * N_MEM)
    head_cols = lambda h: slice(h * HEAD_DIM_MEM, (h + 1) * HEAD_DIM_MEM)
    scores = [lax.dot_general(q[u * rows_per_seg:(u + 1) * rows_per_seg, head_cols(h)],
                              mk_ref[mem_rows(sg), head_cols(h)].astype(BF16), NT_DIMS,
                              preferred_element_type=F32) for u, sg, h in units]
    yield
    probs, inv_l = [], []
    for s in scores:
        p = jnp.exp(s - jnp.max(s, axis=-1, keepdims=True))
        inv_l.append(1.0 / jnp.sum(p, axis=-1, keepdims=True))
        probs.append(p.astype(BF16))
    yield
    outs = [jnp.dot(probs[i], mv_ref[mem_rows(sg), head_cols(h)].astype(BF16), preferred_element_type=F32) * inv_l[i]
            for i, (u, sg, h) in enumerate(units)]
    seg_outs = [jnp.concatenate(outs[u * N_HEADS_MEM:(u + 1) * N_HEADS_MEM], axis=1) for u in range(len(segs))]
    o = (jnp.concatenate(seg_outs, axis=0) if len(segs) > 1 else seg_outs[0]).astype(BF16)
    yield
    att = jnp.dot(o, wmo_ref[...], preferred_element_type=F32)
    yield
    result.append(x1 + _rms(att, g_post_mem_ref[...]))


def _stack_kernel(n_seg, x_ref, oa_ref, ob_ref, r_ref, mk_ref, mv_ref,
                  g_gla_ref, wo_ref, g_post_mix_ref, g_pre_mem_ref, wq_ref, wmo_ref, g_post_mem_ref,
                  g_pre_ffn_ref, wg_ref, wu_ref, wd_ref, g_post_ffn_ref, y_ref):
    tm = x_ref.shape[0]
    half = tm // 2
    halves = []
    for i in range(2):
        segs = list(range(i * n_seg // 2, (i + 1) * n_seg // 2)) if n_seg > 1 else [0]
        result = []
        halves.append((result, _stack_front(slice(i * half, (i + 1) * half), segs, x_ref, oa_ref, ob_ref, r_ref,
                                            mk_ref, mv_ref, g_gla_ref, wo_ref, g_post_mix_ref, g_pre_mem_ref,
                                            wq_ref, wmo_ref, g_post_mem_ref, result)))
    for _ in itertools.zip_longest(*[gen for _, gen in halves]):
        pass
    x2 = jnp.concatenate([result[0] for result, _ in halves], axis=0)

    hf = _rms(x2, g_pre_ffn_ref[...]).astype(BF16)
    f = jnp.zeros((tm, D_MODEL), F32)
    act, act_cols = None, None
    for c in range(D_FF // FF_BLOCK):
        cols = slice(c * FF_BLOCK, (c + 1) * FF_BLOCK)
        gate = jnp.dot(hf, wg_ref[:, cols], preferred_element_type=F32)
        up = jnp.dot(hf, wu_ref[:, cols], preferred_element_type=F32)
        if act is not None:
            f = f + jnp.dot(act, wd_ref[act_cols, :], preferred_element_type=F32)
        act, act_cols = (_silu(gate) * up).astype(BF16), cols
    f = f + jnp.dot(act, wd_ref[act_cols, :], preferred_element_type=F32)
    y_ref[...] = x2 + _rms(f, g_post_ffn_ref[...])


def _stack(x, oa, ob, r, mk, mv, weights, tm, n_seg, tiles_per_mem_block):
    ntok = x.shape[0]
    row = lambda w: pl.BlockSpec((tm, w), lambda i: (i, 0))
    mem = pl.BlockSpec((n_seg * N_MEM, D_MODEL), lambda i: (i // tiles_per_mem_block, 0))
    w_specs = [_const_spec(w.shape) for w in weights]
    return pl.pallas_call(
        functools.partial(_stack_kernel, n_seg),
        grid=(ntok // tm,),
        in_specs=[row(D_MODEL), row(WIDTH_A), row(WIDTH_BV), row(WIDTH_BV), mem, mem] + w_specs,
        out_specs=row(D_MODEL),
        out_shape=jax.ShapeDtypeStruct((ntok, D_MODEL), F32),
        compiler_params=_params(1, VMEM_LIMIT),
        name="token_stack",
    )(x, oa, ob, r, mk, mv, *weights)


def _bias_base(table, width):
    u = np.arange(width)
    idx = np.where(u < (BAND_CHUNKS + 1) * CHUNK, np.clip(A_WINDOW - u, -REL_CLIP, REL_CLIP) + REL_CLIP, 2 * REL_CLIP)
    return table[:, idx]


def kernel(x_prompt, x_sample, mem_prompt, cache_a_k, cache_a_v, state_gla, cache_mem_k, cache_mem_v,
           g_pre_mix, w_in, rel_bias, w_alpha2, b_alpha, g_gla_out, w_o, g_post_mix,
           g_pre_mem, g_mem, w_mq, w_mk, w_mv, w_mo, g_post_mem,
           g_pre_ffn, w_ffn_gate, w_ffn_up, w_ffn_down, g_post_ffn):
    depth = w_in.shape[0]
    assert depth == 1
    l = 0
    n_p, seq, _ = x_prompt.shape
    n_s, seq_s, _ = x_sample.shape
    vec = lambda g: g[l].reshape(1, -1)

    wi = w_in[l]
    w_t = jnp.swapaxes(wi, 0, 1).astype(BF16)
    wa2 = jnp.pad(w_alpha2[l], ((0, LANES - GATE_RANK), (0, 0))).astype(BF16)
    stack_w = [vec(g_gla_out), w_o[l].astype(BF16), vec(g_post_mix), vec(g_pre_mem), w_mq[l].astype(BF16),
               w_mo[l].astype(BF16), vec(g_post_mem), vec(g_pre_ffn), w_ffn_gate[l].astype(BF16),
               w_ffn_up[l].astype(BF16), w_ffn_down[l].astype(BF16), vec(g_post_ffn)]

    xp = x_prompt.reshape(n_p * seq, D_MODEL)
    tm_p = 512
    qa, ka, va, qb, kb, vb, r, la, k_tail, v_tail = _proj(xp, vec(g_pre_mix), w_t, wa2, vec(b_alpha), tm_p, seq // tm_p, True)
    oa = _attn_prompt(qa, ka, va, _bias_base(rel_bias[l], KEYS), n_p, seq)
    ob, sp = _gla(qb, kb, vb, la, None, n_p, seq, CHUNK, 4)
    mk, mv = _mem_kv(mem_prompt.reshape(n_p * N_MEM, D_MODEL), vec(g_mem), w_mk[l].astype(BF16), w_mv[l].astype(BF16))
    tm = 512
    yp = _stack(xp, oa, ob, r, mk.astype(BF16), mv.astype(BF16), stack_w, tm, 1, seq // tm)

    xs = x_sample.reshape(n_s * seq_s, D_MODEL)
    ntok_s = n_s * seq_s
    qa_s, ka_s, va_s, qb_s, kb_s, vb_s, r_s, la_s, k_new, v_new = _proj(
        xs, vec(g_pre_mix), w_t, wa2, vec(b_alpha), ntok_s, 1, False)
    oa_s = _attn_sample(qa_s, ka_s, va_s, jnp.transpose(cache_a_k[l], (0, 2, 3, 1)),
                        jnp.transpose(cache_a_v[l], (0, 2, 3, 1)), rel_bias[l], seq_s)
    ob_s, ss = _gla(qb_s, kb_s, vb_s, la_s, _state_to_pairs(state_gla[l]), n_s, seq_s, seq_s, 1)
    seg_s = 4
    tm_s = seg_s * seq_s
    ys = _stack(xs, oa_s, ob_s, r_s,
                cache_mem_k[l].reshape(n_s * N_MEM, D_MODEL),
                cache_mem_v[l].reshape(n_s * N_MEM, D_MODEL), stack_w, tm_s, seg_s, 1)

    keep = min(A_WINDOW, seq)
    heads = lambda z, n, t: z.reshape(1, n, t, N_HEADS_A, HEAD_DIM_A)
    tails = lambda z: jnp.transpose(z.reshape(n_p, N_HEADS_A, HEAD_DIM_A, keep), (0, 3, 1, 2))[None]
    return (yp.reshape(n_p, seq, D_MODEL), ys.reshape(n_s, seq_s, D_MODEL),
            tails(k_tail), tails(v_tail), _pairs_to_state(sp)[None],
            mk.reshape(1, n_p, N_MEM, N_HEADS_MEM, HEAD_DIM_MEM), mv.reshape(1, n_p, N_MEM, N_HEADS_MEM, HEAD_DIM_MEM),
            heads(k_new, n_s, seq_s), heads(v_new, n_s, seq_s), _pairs_to_state(ss)[None])
```

```python
import functools
import itertools
import math

import jax
import jax.numpy as jnp
import numpy as np
from jax import lax
from jax.experimental import pallas as pl
from jax.experimental.pallas import tpu as pltpu

F32 = jnp.float32
BF16 = jnp.bfloat16

D_MODEL = 1024
CHUNK = 64
BAND_CHUNKS = 8
A_WINDOW = BAND_CHUNKS * CHUNK
N_HEADS_A = 8
HEAD_DIM_A = 64
WIDTH_A = 512
REL_CLIP = 256
N_HEADS_B = 4
KEY_DIM_B = 64
VAL_DIM_B = 128
WIDTH_BK = 256
WIDTH_BV = 512
GATE_RANK = 16
GATE_TAU = 16.0
N_MEM = 256
N_HEADS_MEM = 4
HEAD_DIM_MEM = 256
D_FF = 2816
EPS = 1e-6
NEG_BIG = -1e30
NT_DIMS = (((1,), (1,)), ((), ()))

LANES = 128
BIAS_W = 640
FF_BLOCK = 256
VMEM_LIMIT = 56 * 1024 * 1024

OFF_QA, OFF_KA, OFF_VA, OFF_QB, OFF_KB, OFF_VB, OFF_R = 0, 512, 1024, 1536, 1792, 2048, 2560
IN_WIDTH = OFF_R + GATE_RANK + WIDTH_BV


def _rms(x, g):
    return x * lax.rsqrt(jnp.mean(x * x, axis=-1, keepdims=True) + EPS) * g


def _silu(x):
    return x / (1.0 + jnp.exp(-x))


def _log_sigmoid(z):
    return jnp.minimum(z, 0.0) - jnp.log(1.0 + jnp.exp(-jnp.abs(z)))


def _const_spec(shape):
    nd = len(shape)
    return pl.BlockSpec(shape, lambda *_: (0,) * nd, pipeline_mode=pl.Buffered(1))


def _params(n_axes, vmem=None):
    return pltpu.CompilerParams(dimension_semantics=("arbitrary",) * n_axes, vmem_limit_bytes=vmem)


def _proj_kernel(tiles_per_seq, tail_t, x_ref, g_ref, w_ref, wa2_ref, ba_ref,
                 qa_ref, ka_ref, va_ref, qb_ref, kb_ref, vb_ref, r_ref, la_ref, kt_ref, vt_ref):
    h = _rms(x_ref[...], g_ref[...]).astype(BF16)

    def proj(lo, hi):
        return lax.dot_general(h, w_ref[lo:hi, :], NT_DIMS, preferred_element_type=F32)

    g_low = proj(OFF_R, OFF_R + LANES).astype(BF16)
    z = jnp.dot(g_low, wa2_ref[...], preferred_element_type=F32) + ba_ref[...]
    la_ref[...] = _log_sigmoid(z) * (1.0 / GATE_TAU)
    qa_ref[...] = (proj(OFF_QA, OFF_KA) * (HEAD_DIM_A ** -0.5)).astype(BF16)
    ka = proj(OFF_KA, OFF_VA)
    va = proj(OFF_VA, OFF_QB)
    ka_ref[...] = ka.astype(BF16)
    va_ref[...] = va.astype(BF16)
    qb_ref[...] = proj(OFF_QB, OFF_KB) * (KEY_DIM_B ** -0.5)
    kb_ref[...] = proj(OFF_KB, OFF_VB)
    vb_ref[...] = proj(OFF_VB, OFF_R).astype(BF16)
    r_ref[...] = proj(OFF_R + GATE_RANK, IN_WIDTH)

    @pl.when(pl.program_id(0) % tiles_per_seq == tiles_per_seq - 1)
    def _():
        kt_ref[...] = ka.T if tail_t else ka
        vt_ref[...] = va.T if tail_t else va


def _proj(x, g, w_t, wa2, ba, tm, tiles_per_seq, tail_t):
    ntok = x.shape[0]
    nt = ntok // tm
    nseq = nt // tiles_per_seq
    row = lambda w: pl.BlockSpec((tm, w), lambda i: (i, 0))
    tail_shape = (WIDTH_A, tm) if tail_t else (tm, WIDTH_A)
    tail = pl.BlockSpec(tail_shape, lambda i: (i // tiles_per_seq, 0))
    out_shape = (
        jax.ShapeDtypeStruct((ntok, WIDTH_A), BF16),
        jax.ShapeDtypeStruct((ntok, WIDTH_A), BF16),
        jax.ShapeDtypeStruct((ntok, WIDTH_A), BF16),
        jax.ShapeDtypeStruct((ntok, WIDTH_BK), F32),
        jax.ShapeDtypeStruct((ntok, WIDTH_BK), F32),
        jax.ShapeDtypeStruct((ntok, WIDTH_BV), BF16),
        jax.ShapeDtypeStruct((ntok, WIDTH_BV), F32),
        jax.ShapeDtypeStruct((ntok, WIDTH_BK), F32),
        jax.ShapeDtypeStruct((nseq * tail_shape[0], tail_shape[1]), F32),
        jax.ShapeDtypeStruct((nseq * tail_shape[0], tail_shape[1]), F32),
    )
    return pl.pallas_call(
        functools.partial(_proj_kernel, tiles_per_seq, tail_t),
        grid=(nt,),
        in_specs=[row(D_MODEL), _const_spec((1, D_MODEL)), _const_spec((IN_WIDTH, D_MODEL)),
                  _const_spec((LANES, WIDTH_BK)), _const_spec((1, WIDTH_BK))],
        out_specs=(row(WIDTH_A), row(WIDTH_A), row(WIDTH_A), row(WIDTH_BK), row(WIDTH_BK),
                   row(WIDTH_BV), row(WIDTH_BV), row(WIDTH_BK), tail, tail),
        out_shape=out_shape,
        compiler_params=_params(1, VMEM_LIMIT),
        name="proj",
    )(x, g, w_t, wa2, ba)


def _stack_heads(x_pair):
    low_half = lax.broadcasted_iota(jnp.int32, x_pair.shape, 1) < HEAD_DIM_A
    zero = jnp.zeros_like(x_pair)
    return jnp.concatenate([jnp.where(low_half, x_pair, zero), jnp.where(low_half, zero, x_pair)], axis=0)


TQ = 4 * CHUNK
KEYS = 3 * TQ
SOFT_W = 5 * LANES
assert 2 * TQ == A_WINDOW and SOFT_W >= (BAND_CHUNKS + 1) * CHUNK + CHUNK


def _build_band_bias(base_ref, bm_ref):
    n_band = (BAND_CHUNKS + 1) * CHUNK
    col = lax.broadcasted_iota(jnp.int32, (CHUNK, KEYS), 1)
    for h in range(N_HEADS_A):
        rows = jnp.broadcast_to(base_ref[h:h + 1, :], (CHUNK, KEYS))
        for qc in range(TQ // CHUNK):
            toeplitz = pltpu.roll(rows, qc * CHUNK, 1, stride=1, stride_axis=0)
            in_band = (col >= qc * CHUNK) & (col < qc * CHUNK + n_band)
            r0 = (qc * 2 + h % 2) * CHUNK
            bm_ref[h // 2, r0:r0 + CHUNK, :] = jnp.where(in_band, toeplitz, NEG_BIG)


def _band_tile(q_ref, k_refs, v_refs, bm_ref, o_ref, start_penalty):
    n_qc, n_pairs = TQ // CHUNK, N_HEADS_A // 2
    low_half = lax.broadcasted_iota(jnp.int32, (CHUNK, LANES), 1) < HEAD_DIM_A
    pair_cols = lambda hp: slice(hp * LANES, (hp + 1) * LANES)

    def scores(hp):
        cols = pair_cols(hp)
        qs = jnp.concatenate([_stack_heads(q_ref[qc * CHUNK:(qc + 1) * CHUNK, cols]) for qc in range(n_qc)],
                             axis=0)
        return jnp.concatenate([lax.dot_general(qs, k[:, cols], NT_DIMS, preferred_element_type=F32)
                                for k in k_refs], axis=1)

    def softmax(hp, s):
        probs, inv_l = [], []
        for qc in range(n_qc):
            rows = slice(qc * 2 * CHUNK, (qc + 1) * 2 * CHUNK)
            c0 = 0 if (qc + 1) * CHUNK + A_WINDOW <= SOFT_W else KEYS - SOFT_W
            sq = s[rows, c0:c0 + SOFT_W] + bm_ref[hp, rows, c0:c0 + SOFT_W]
            if start_penalty is not None:
                sq = sq + start_penalty[:, c0:c0 + SOFT_W]
            p = jnp.exp(sq - jnp.max(sq, axis=-1, keepdims=True))
            inv_l.append(1.0 / jnp.sum(p, axis=-1, keepdims=True))
            pad = jnp.zeros((2 * CHUNK, KEYS - SOFT_W), BF16)
            probs.append(jnp.concatenate([p.astype(BF16), pad] if c0 == 0 else [pad, p.astype(BF16)], axis=1))
        return jnp.concatenate(probs, axis=0), inv_l

    def values(hp, pm, inv_l):
        cols = pair_cols(hp)
        out = sum(jnp.dot(pm[:, i * TQ:(i + 1) * TQ], v[:, cols], preferred_element_type=F32)
                  for i, v in enumerate(v_refs))
        for qc in range(n_qc):
            r0 = qc * 2 * CHUNK
            o0 = out[r0:r0 + CHUNK] * inv_l[qc][0:CHUNK]
            o1 = out[r0 + CHUNK:r0 + 2 * CHUNK] * inv_l[qc][CHUNK:2 * CHUNK]
            o_ref[qc * CHUNK:(qc + 1) * CHUNK, cols] = jnp.where(low_half, o0, o1).astype(o_ref.dtype)

    for hp in range(n_pairs):
        values(hp, *softmax(hp, scores(hp)))


def _attn_prompt_kernel(q_ref, k2_ref, k1_ref, k0_ref, v2_ref, v1_ref, v0_ref, base_ref, o_ref, bm):
    t = pl.program_id(1)

    @pl.when((pl.program_id(0) == 0) & (t == 0))
    def _():
        _build_band_bias(base_ref, bm)

    k_refs = (k2_ref, k1_ref, k0_ref)
    v_refs = (v2_ref, v1_ref, v0_ref)

    @pl.when(t >= 2)
    def _():
        _band_tile(q_ref, k_refs, v_refs, bm, o_ref, None)

    @pl.when(t < 2)
    def _():
        col = lax.broadcasted_iota(jnp.int32, (1, KEYS), 1)
        penalty = jnp.where(col < (2 - t) * TQ, NEG_BIG, 0.0)
        _band_tile(q_ref, k_refs, v_refs, bm, o_ref, penalty)


def _attn_prompt(q, k, v, base, n_batch, seq):
    nt = seq // TQ
    blk = lambda back: pl.BlockSpec((TQ, WIDTH_A), lambda b, t: (b * nt + jnp.maximum(t - back, 0), 0))
    return pl.pallas_call(
        _attn_prompt_kernel,
        grid=(n_batch, nt),
        in_specs=[blk(0), blk(2), blk(1), blk(0), blk(2), blk(1), blk(0), _const_spec((N_HEADS_A, KEYS))],
        out_specs=blk(0),
        out_shape=jax.ShapeDtypeStruct((n_batch * seq, WIDTH_A), BF16),
        scratch_shapes=[pltpu.VMEM((N_HEADS_A // 2, 2 * TQ, KEYS), F32)],
        compiler_params=_params(2, VMEM_LIMIT),
        name="band_attn_prompt",
    )(q, k, k, k, v, v, v, base)


def _attn_sample_one(rows, q_ref, kn_ref, vn_ref, kc_ref, vc_ref, sel_ref, dup_ref, o_ref, bias_c, bias_n):
    n_h = N_HEADS_A
    n_new = rows.stop - rows.start
    low_half = lax.broadcasted_iota(jnp.int32, (n_new, LANES), 1) < HEAD_DIM_A
    pairs = [slice(hp * LANES, (hp + 1) * LANES) for hp in range(n_h // 2)]
    q_heads = [jnp.dot(q_ref[rows, cols], sel_ref[e], preferred_element_type=F32).astype(BF16)
               for cols in pairs for e in range(2)]
    yield
    s_c = jnp.concatenate([jnp.dot(q_heads[h], kc_ref[h].astype(BF16), preferred_element_type=F32)
                           for h in range(n_h)], axis=0) + bias_c[...]
    s_n = jnp.concatenate([lax.dot_general(_stack_heads(q_ref[rows, cols]), kn_ref[rows, cols], NT_DIMS,
                                           preferred_element_type=F32) for cols in pairs], axis=0) + bias_n[...]
    yield
    m = jnp.maximum(jnp.max(s_c, axis=-1, keepdims=True), jnp.max(s_n, axis=-1, keepdims=True))
    p_c, p_n = jnp.exp(s_c - m), jnp.exp(s_n - m)
    inv_l = 1.0 / (jnp.sum(p_c, axis=-1, keepdims=True) + jnp.sum(p_n, axis=-1, keepdims=True))
    p_c, p_n = p_c.astype(BF16), p_n.astype(BF16)
    yield
    out_c = jnp.concatenate([lax.dot_general(p_c[h * n_new:(h + 1) * n_new], vc_ref[h].astype(BF16), NT_DIMS,
                                             preferred_element_type=F32) for h in range(n_h)], axis=0) * inv_l
    yield
    hi = out_c.astype(BF16)
    lo = (out_c - hi.astype(F32)).astype(BF16)
    out_c = (jnp.dot(hi, dup_ref[...], preferred_element_type=F32)
             + jnp.dot(lo, dup_ref[...], preferred_element_type=F32))
    for hp, cols in enumerate(pairs):
        pr = slice(hp * 2 * n_new, (hp + 1) * 2 * n_new)
        out = out_c[pr] + jnp.dot(p_n[pr], vn_ref[rows, cols], preferred_element_type=F32) * inv_l[pr]
        o_ref[rows, cols] = jnp.where(low_half, out[0:n_new], out[n_new:]).astype(o_ref.dtype)


def _attn_sample_kernel(n_new, q_ref, kn_ref, vn_ref, kc_ref, vc_ref, base_ref, sel_ref, dup_ref,
                        o_ref, bias_c, bias_n):
    n_cache = kc_ref.shape[3]

    @pl.when(pl.program_id(0) == 0)
    def _():
        for h in range(N_HEADS_A):
            rows = jnp.broadcast_to(base_ref[h:h + 1, :], (n_new, BIAS_W))
            toeplitz = pltpu.roll(rows, 0, 1, stride=1, stride_axis=0)
            bias_c[h * n_new:(h + 1) * n_new, :] = toeplitz[:, 0:n_cache]
            bias_n[h * n_new:(h + 1) * n_new, :] = toeplitz[:, n_cache:n_cache + n_new]

    requests = [_attn_sample_one(slice(i * n_new, (i + 1) * n_new), q_ref, kn_ref, vn_ref, kc_ref.at[i],
                                 vc_ref.at[i], sel_ref, dup_ref, o_ref, bias_c, bias_n)
                for i in range(kc_ref.shape[0])]
    for _ in itertools.zip_longest(*requests):
        pass


def _attn_sample(q, k_new, v_new, k_cache_t, v_cache_t, table, n_new, n_par):
    n_batch, n_h, _, n_cache = k_cache_t.shape
    assert n_cache == A_WINDOW and n_new <= CHUNK and n_h == N_HEADS_A
    lane = np.arange(LANES)
    sel = np.stack([lane[:, None] == np.arange(HEAD_DIM_A)[None, :] + e * HEAD_DIM_A for e in range(2)])
    dup = np.arange(HEAD_DIM_A)[:, None] == lane[None, :] % HEAD_DIM_A
    new = pl.BlockSpec((n_par * n_new, WIDTH_A), lambda g: (g, 0))
    old = pl.BlockSpec((n_par, n_h, HEAD_DIM_A, n_cache), lambda g: (g, 0, 0, 0))
    return pl.pallas_call(
        functools.partial(_attn_sample_kernel, n_new),
        grid=(n_batch // n_par,),
        in_specs=[new, new, new, old, old, _const_spec((n_h, BIAS_W)),
                  _const_spec((2, LANES, HEAD_DIM_A)), _const_spec((HEAD_DIM_A, LANES))],
        out_specs=new,
        out_shape=jax.ShapeDtypeStruct((n_batch * n_new, WIDTH_A), BF16),
        scratch_shapes=[pltpu.VMEM((n_h * n_new, n_cache), F32), pltpu.VMEM((n_h * n_new, n_new), F32)],
        compiler_params=_params(1, VMEM_LIMIT),
        name="band_attn_sample",
    )(q, k_new, v_new, k_cache_t, v_cache_t, _bias_base(table, BIAS_W), jnp.asarray(sel, BF16),
      jnp.asarray(dup, BF16))


def _gla_levels(chunk):
    return [chunk >> i for i in range(int(math.log2(chunk)) + 1)]


def _segment_matrix(chunk):
    i = np.arange(chunk)[:, None]
    t = np.arange(chunk)[None, :]
    blocks = []
    for li, s in enumerate(_gla_levels(chunk)):
        start = (i // s) * s
        f_rows = (t >= start) & (t <= i)
        r_rows = (t > i) & (t <= start + s - 1)
        if li == 0:
            blocks += [f_rows, r_rows]
        else:
            blocks.append(np.where((i // s) % 2 == 1, f_rows, r_rows))
    seg = np.concatenate(blocks, axis=0).astype(np.float32)
    return np.concatenate([seg, seg], axis=1)


def _level_masks(group_tokens):
    tg = group_tokens
    i = (np.arange(2 * tg) % tg)[:, None]
    j = np.arange(tg)[None, :]
    masks = [i == j]
    s = tg // 2
    while s >= 1:
        masks.append(((i // s) % 2 == 1) & (j // s == i // s - 1))
        s //= 2
    return np.stack(masks).astype(np.float32)


def _gla_tile(q, k, v, la, seg_ref, mask_ref, state_ref, o_ref, chunk, n_chunks):
    n, tg = n_chunks, chunk * n_chunks
    levels = _gla_levels(chunk)
    la_hi = la.astype(BF16)
    la_lo = (la - la_hi.astype(F32)).astype(BF16)
    split = jnp.concatenate(
        [jnp.concatenate([la_hi[c * chunk:(c + 1) * chunk], la_lo[c * chunk:(c + 1) * chunk]], axis=0)
         for c in range(n)], axis=1)
    sums = jnp.dot(seg_ref[...], split, preferred_element_type=F32)

    def seg_sum(block):
        rows = slice(block * chunk, (block + 1) * chunk)
        return jnp.minimum(jnp.concatenate(
            [sums[rows, c * WIDTH_BK:(c + 1) * WIDTH_BK] for c in range(n)], axis=0), 0.0)

    fwd0, rev0 = seg_sum(0), seg_sum(1)
    from_start, to_end = jnp.exp(fwd0), jnp.exp(rev0)
    total = [fwd0[(c + 1) * chunk - 1:(c + 1) * chunk] for c in range(n)]

    def chunk_sum(cs):
        cs = list(cs)
        return (sum(total[c] for c in cs[1:]) + total[cs[0]]) if cs else None

    def extend(base, offsets):
        pieces = []
        for c in range(n):
            x = base[c * chunk:(c + 1) * chunk]
            pieces.append(x if offsets[c] is None else x * jnp.exp(offsets[c]))
        return jnp.concatenate(pieces, axis=0) if n > 1 else pieces[0]

    q_state = (q * extend(from_start, [chunk_sum(range(0, c)) for c in range(n)])).astype(BF16)
    k_state = (k * extend(to_end, [chunk_sum(range(c + 1, n)) for c in range(n)])).astype(BF16)
    decay_tile = jnp.exp(chunk_sum(range(n)))
    ops = {}
    s = tg // 2
    while s >= chunk and n > 1:
        per = s // chunk
        offsets, bases = [], []
        for c in range(n):
            sg = c // per
            offsets.append(chunk_sum(range(sg * per, c)) if sg % 2 else chunk_sum(range(c + 1, (sg + 1) * per)))
            bases.append((from_start if sg % 2 else to_end)[c * chunk:(c + 1) * chunk])
        w = extend(jnp.concatenate(bases, axis=0), offsets)
        ops[s] = ((q * w).astype(BF16), (k * w).astype(BF16))
        s //= 2
    for li in range(1, len(levels)):
        w = jnp.exp(seg_sum(li + 1))
        ops[levels[li]] = ((q * w).astype(BF16), (k * w).astype(BF16))
    q16, k16 = q.astype(BF16), k.astype(BF16)
    yield

    gt = mask_ref.shape[2]
    own = (lax.broadcasted_iota(jnp.int32, (2 * VAL_DIM_B, LANES), 0) < VAL_DIM_B) == \
          (lax.broadcasted_iota(jnp.int32, (2 * VAL_DIM_B, LANES), 1) < KEY_DIM_B)
    nt_dims = (((1,), (1,)), ((), ()))
    tn_dims = (((0,), (0,)), ((), ()))
    n_pairs, n_groups = N_HEADS_B // 2, tg // gt
    pair_cols = [slice(p * LANES, (p + 1) * LANES) for p in range(n_pairs)]
    pair_v = [v[:, p * 2 * VAL_DIM_B:(p + 1) * 2 * VAL_DIM_B] for p in range(n_pairs)]
    head_v = lambda e: slice(e * VAL_DIM_B, (e + 1) * VAL_DIM_B)
    o_state = []
    for p, cols in enumerate(pair_cols):
        st = state_ref[p]
        o_state.append(lax.dot_general(q_state[:, cols], st.astype(BF16), nt_dims, preferred_element_type=F32))
        upd = lax.dot_general(pair_v[p], k_state[:, cols], tn_dims, preferred_element_type=F32)
        state_ref[p] = st * decay_tile[:, cols] + jnp.where(own, upd, 0.0)
    yield
    blocks = []
    for p, cols in enumerate(pair_cols):
        for g in range(n_groups):
            rows = slice(g * gt, (g + 1) * gt)
            a = mask_ref[0] * lax.dot_general(_stack_heads(q16[rows, cols]), k16[rows, cols], nt_dims,
                                              preferred_element_type=F32)
            m, s = 1, gt // 2
            while s >= 1:
                qd, kd = ops[s]
                a = a + mask_ref[m] * lax.dot_general(_stack_heads(qd[rows, cols]), kd[rows, cols], nt_dims,
                                                      preferred_element_type=F32)
                m, s = m + 1, s // 2
            blocks.append((p, g * gt, g * gt, a.astype(BF16)))
            yield
        s = tg // 2
        while s >= gt:
            qd, kd = ops[s]
            for blk in range(tg // (2 * s)):
                k0, q0 = blk * 2 * s, blk * 2 * s + s
                ab = lax.dot_general(_stack_heads(qd[q0:q0 + s, cols]), kd[k0:k0 + s, cols], nt_dims,
                                     preferred_element_type=F32)
                blocks.append((p, q0, k0, ab.astype(BF16)))
                yield
            s //= 2
    piece = [[[o_state[p][g * gt:(g + 1) * gt, head_v(e)] for g in range(n_groups)] for e in range(2)]
             for p in range(n_pairs)]
    for p, q0, k0, a in blocks:
        nq, nk = a.shape[0] // 2, a.shape[1]
        for e in range(2):
            contrib = jnp.dot(a[e * nq:(e + 1) * nq], pair_v[p][k0:k0 + nk, head_v(e)], preferred_element_type=F32)
            for gi in range(nq // gt):
                g = q0 // gt + gi
                piece[p][e][g] = piece[p][e][g] + contrib[gi * gt:(gi + 1) * gt]
        yield
    outs = [jnp.concatenate(piece[p][e], axis=0) if n_groups > 1 else piece[p][e][0]
            for p in range(n_pairs) for e in range(2)]
    o_ref[...] = jnp.concatenate(outs, axis=1)


def _gla_kernel(chunk, chunks_per_tile, has_init, *refs):
    if has_init:
        q_ref, k_ref, v_ref, la_ref, seg_ref, mask_ref, s0_ref, o_ref, sout_ref, state = refs
    else:
        q_ref, k_ref, v_ref, la_ref, seg_ref, mask_ref, o_ref, sout_ref, state = refs
    t = pl.program_id(1)

    @pl.when(t == 0)
    def _():
        state[...] = s0_ref[...] if has_init else jnp.zeros_like(state)

    tiles = [_gla_tile(q_ref[i], k_ref[i], v_ref[i], la_ref[i], seg_ref, mask_ref, state.at[i], o_ref.at[i],
                       chunk, chunks_per_tile) for i in range(q_ref.shape[0])]
    for _ in itertools.zip_longest(*tiles):
        pass

    @pl.when(t == pl.num_programs(1) - 1)
    def _():
        sout_ref[...] = state[...]


def _gla(q, k, v, la, s0, n_batch, seq, chunk, chunks_per_tile, n_par):
    tg = chunk * chunks_per_tile
    nt = seq // tg
    seg = jnp.asarray(_segment_matrix(chunk), BF16)
    masks = jnp.asarray(_level_masks(min(tg, LANES)), F32)
    row = lambda w: pl.BlockSpec((n_par, tg, w), lambda g, t: (g, t, 0))
    st = pl.BlockSpec((n_par, 2, 2 * VAL_DIM_B, LANES), lambda g, t: (g, 0, 0, 0))
    per_seq = lambda z: z.reshape(n_batch, seq, z.shape[-1])
    in_specs = [row(WIDTH_BK), row(WIDTH_BK), row(WIDTH_BV), row(WIDTH_BK), _const_spec(seg.shape),
                _const_spec(masks.shape)]
    args = [per_seq(q), per_seq(k), per_seq(v), per_seq(la), seg, masks]
    if s0 is not None:
        in_specs.append(st)
        args.append(s0)
    o, s_out = pl.pallas_call(
        functools.partial(_gla_kernel, chunk, chunks_per_tile, s0 is not None),
        grid=(n_batch // n_par, nt),
        in_specs=in_specs,
        out_specs=(row(WIDTH_BV), st),
        out_shape=(jax.ShapeDtypeStruct((n_batch, seq, WIDTH_BV), F32),
                   jax.ShapeDtypeStruct((n_batch, 2, 2 * VAL_DIM_B, LANES), F32)),
        scratch_shapes=[pltpu.VMEM((n_par, 2, 2 * VAL_DIM_B, LANES), F32)],
        compiler_params=_params(2, VMEM_LIMIT),
        name="gla_chunk%d" % chunk,
    )(*args)
    return o.reshape(n_batch * seq, WIDTH_BV), s_out


def _state_to_pairs(s):
    n = s.shape[0]
    st = jnp.swapaxes(s, -1, -2).reshape(n, 2, 2, VAL_DIM_B, KEY_DIM_B)
    z = jnp.zeros_like(st[:, :, 0])
    top = jnp.concatenate([st[:, :, 0], z], axis=-1)
    bot = jnp.concatenate([z, st[:, :, 1]], axis=-1)
    return jnp.concatenate([top, bot], axis=-2)


def _pairs_to_state(sp):
    n = sp.shape[0]
    h0 = sp[:, :, :VAL_DIM_B, :KEY_DIM_B]
    h1 = sp[:, :, VAL_DIM_B:, KEY_DIM_B:]
    st = jnp.stack([h0, h1], axis=2).reshape(n, N_HEADS_B, VAL_DIM_B, KEY_DIM_B)
    return jnp.swapaxes(st, -1, -2)


def _mem_kv_kernel(m_ref, g_ref, wk_ref, wv_ref, k_ref, v_ref):
    m = _rms(m_ref[...], g_ref[...]).astype(BF16)
    k_ref[...] = jnp.dot(m, wk_ref[...], preferred_element_type=F32)
    v_ref[...] = jnp.dot(m, wv_ref[...], preferred_element_type=F32)


def _mem_kv(mem, g, wk, wv):
    n = mem.shape[0]
    tm = N_MEM
    row = pl.BlockSpec((tm, D_MODEL), lambda i: (i, 0))
    return pl.pallas_call(
        _mem_kv_kernel,
        grid=(n // tm,),
        in_specs=[row, _const_spec((1, D_MODEL)), _const_spec((D_MODEL, D_MODEL)), _const_spec((D_MODEL, D_MODEL))],
        out_specs=(row, row),
        out_shape=(jax.ShapeDtypeStruct((n, D_MODEL), F32), jax.ShapeDtypeStruct((n, D_MODEL), F32)),
        compiler_params=_params(1, VMEM_LIMIT),
        name="mem_kv",
    )(mem, g, wk, wv)


def _stack_front(rows, segs, x_ref, oa_ref, ob_ref, r_ref, mk_ref, mv_ref, g_gla_ref, wo_ref, g_post_mix_ref,
                 g_pre_mem_ref, wq_ref, wmo_ref, g_post_mem_ref, result):
    n_rows = rows.stop - rows.start
    ob = ob_ref[rows, :]
    normed = []
    for h in range(N_HEADS_B):
        seg = ob[:, h * VAL_DIM_B:(h + 1) * VAL_DIM_B]
        normed.append(seg * lax.rsqrt(jnp.mean(seg * seg, axis=-1, keepdims=True) + EPS))
    yb = (jnp.concatenate(normed, axis=1) * g_gla_ref[...] * _silu(r_ref[rows, :])).astype(BF16)
    yield
    mix = (jnp.dot(oa_ref[rows, :], wo_ref[0:WIDTH_A, :], preferred_element_type=F32)
           + jnp.dot(yb, wo_ref[WIDTH_A:, :], preferred_element_type=F32))
    yield
    x1 = x_ref[rows, :] + _rms(mix, g_post_mix_ref[...])
    hq = _rms(x1, g_pre_mem_ref[...]).astype(BF16)
    yield
    q = (jnp.dot(hq, wq_ref[...], preferred_element_type=F32) * (HEAD_DIM_MEM ** -0.5)).astype(BF16)
    yield
    rows_per_seg = n_rows // len(segs)
    units = [(u, sg, h) for u, sg in enumerate(segs) for h in range(N_HEADS_MEM)]
    mem_rows = lambda sg: slice(sg * N_MEM, (sg + 1) * N_MEM)
    head_cols = lambda h: slice(h * HEAD_DIM_MEM, (h + 1) * HEAD_DIM_MEM)
    scores = [lax.dot_general(q[u * rows_per_seg:(u + 1) * rows_per_seg, head_cols(h)],
                              mk_ref[mem_rows(sg), head_cols(h)].astype(BF16), NT_DIMS,
                              preferred_element_type=F32) for u, sg, h in units]
    yield
    probs, inv_l = [], []
    for s in scores:
        p = jnp.exp(s - jnp.max(s, axis=-1, keepdims=True))
        inv_l.append(1.0 / jnp.sum(p, axis=-1, keepdims=True))
        probs.append(p.astype(BF16))
    yield
    outs = [jnp.dot(probs[i], mv_ref[mem_rows(sg), head_cols(h)].astype(BF16), preferred_element_type=F32) * inv_l[i]
            for i, (u, sg, h) in enumerate(units)]
    seg_outs = [jnp.concatenate(outs[u * N_HEADS_MEM:(u + 1) * N_HEADS_MEM], axis=1) for u in range(len(segs))]
    o = (jnp.concatenate(seg_outs, axis=0) if len(segs) > 1 else seg_outs[0]).astype(BF16)
    yield
    att = jnp.dot(o, wmo_ref[...], preferred_element_type=F32)
    yield
    result.append(x1 + _rms(att, g_post_mem_ref[...]))


def _stack_kernel(n_seg, x_ref, oa_ref, ob_ref, r_ref, mk_ref, mv_ref,
                  g_gla_ref, wo_ref, g_post_mix_ref, g_pre_mem_ref, wq_ref, wmo_ref, g_post_mem_ref,
                  g_pre_ffn_ref, wg_ref, wu_ref, wd_ref, g_post_ffn_ref, y_ref):
    tm = x_ref.shape[0]
    half = tm // 2
    halves = []
    for i in range(2):
        segs = list(range(i * n_seg // 2, (i + 1) * n_seg // 2)) if n_seg > 1 else [0]
        result = []
        halves.append((result, _stack_front(slice(i * half, (i + 1) * half), segs, x_ref, oa_ref, ob_ref, r_ref,
                                            mk_ref, mv_ref, g_gla_ref, wo_ref, g_post_mix_ref, g_pre_mem_ref,
                                            wq_ref, wmo_ref, g_post_mem_ref, result)))
    for _ in itertools.zip_longest(*[gen for _, gen in halves]):
        pass
    x2 = jnp.concatenate([result[0] for result, _ in halves], axis=0)

    hf = _rms(x2, g_pre_ffn_ref[...]).astype(BF16)
    f = jnp.zeros((tm, D_MODEL), F32)
    act, act_cols = None, None
    for c in range(D_FF // FF_BLOCK):
        cols = slice(c * FF_BLOCK, (c + 1) * FF_BLOCK)
        gate = jnp.dot(hf, wg_ref[:, cols], preferred_element_type=F32)
        up = jnp.dot(hf, wu_ref[:, cols], preferred_element_type=F32)
        if act is not None:
            f = f + jnp.dot(act, wd_ref[act_cols, :], preferred_element_type=F32)
        act, act_cols = (_silu(gate) * up).astype(BF16), cols
    f = f + jnp.dot(act, wd_ref[act_cols, :], preferred_element_type=F32)
    y_ref[...] = x2 + _rms(f, g_post_ffn_ref[...])


def _stack(x, oa, ob, r, mk, mv, weights, tm, n_seg, tiles_per_mem_block, mem_buffers):
    ntok = x.shape[0]
    row = lambda w: pl.BlockSpec((tm, w), lambda i: (i, 0))
    mem = pl.BlockSpec((n_seg * N_MEM, D_MODEL), lambda i: (i // tiles_per_mem_block, 0),
                       pipeline_mode=pl.Buffered(mem_buffers))
    w_specs = [_const_spec(w.shape) for w in weights]
    return pl.pallas_call(
        functools.partial(_stack_kernel, n_seg),
        grid=(ntok // tm,),
        in_specs=[row(D_MODEL), row(WIDTH_A), row(WIDTH_BV), row(WIDTH_BV), mem, mem] + w_specs,
        out_specs=row(D_MODEL),
        out_shape=jax.ShapeDtypeStruct((ntok, D_MODEL), F32),
        compiler_params=_params(1, VMEM_LIMIT),
        name="token_stack",
    )(x, oa, ob, r, mk, mv, *weights)


def _bias_base(table, width):
    u = np.arange(width)
    idx = np.where(u < (BAND_CHUNKS + 1) * CHUNK, np.clip(A_WINDOW - u, -REL_CLIP, REL_CLIP) + REL_CLIP, 2 * REL_CLIP)
    return table[:, idx]


def kernel(x_prompt, x_sample, mem_prompt, cache_a_k, cache_a_v, state_gla, cache_mem_k, cache_mem_v,
           g_pre_mix, w_in, rel_bias, w_alpha2, b_alpha, g_gla_out, w_o, g_post_mix,
           g_pre_mem, g_mem, w_mq, w_mk, w_mv, w_mo, g_post_mem,
           g_pre_ffn, w_ffn_gate, w_ffn_up, w_ffn_down, g_post_ffn):
    depth = w_in.shape[0]
    assert depth == 1
    l = 0
    n_p, seq, _ = x_prompt.shape
    n_s, seq_s, _ = x_sample.shape
    vec = lambda g: g[l].reshape(1, -1)

    wi = w_in[l]
    w_t = jnp.swapaxes(wi, 0, 1).astype(BF16)
    wa2 = jnp.pad(w_alpha2[l], ((0, LANES - GATE_RANK), (0, 0))).astype(BF16)
    stack_w = [vec(g_gla_out), w_o[l].astype(BF16), vec(g_post_mix), vec(g_pre_mem), w_mq[l].astype(BF16),
               w_mo[l].astype(BF16), vec(g_post_mem), vec(g_pre_ffn), w_ffn_gate[l].astype(BF16),
               w_ffn_up[l].astype(BF16), w_ffn_down[l].astype(BF16), vec(g_post_ffn)]

    xp = x_prompt.reshape(n_p * seq, D_MODEL)
    tm_p = 512
    qa, ka, va, qb, kb, vb, r, la, k_tail, v_tail = _proj(xp, vec(g_pre_mix), w_t, wa2, vec(b_alpha), tm_p, seq // tm_p, True)
    oa = _attn_prompt(qa, ka, va, _bias_base(rel_bias[l], KEYS), n_p, seq)
    ob, sp = _gla(qb, kb, vb, la, None, n_p, seq, CHUNK, 4, n_p)
    mk, mv = _mem_kv(mem_prompt.reshape(n_p * N_MEM, D_MODEL), vec(g_mem), w_mk[l].astype(BF16), w_mv[l].astype(BF16))
    tm = 512
    yp = _stack(xp, oa, ob, r, mk.astype(BF16), mv.astype(BF16), stack_w, tm, 1, seq // tm, 2)

    xs = x_sample.reshape(n_s * seq_s, D_MODEL)
    ntok_s = n_s * seq_s
    qa_s, ka_s, va_s, qb_s, kb_s, vb_s, r_s, la_s, k_new, v_new = _proj(
        xs, vec(g_pre_mix), w_t, wa2, vec(b_alpha), ntok_s, 1, False)
    oa_s = _attn_sample(qa_s, ka_s, va_s, jnp.transpose(cache_a_k[l], (0, 2, 3, 1)),
                        jnp.transpose(cache_a_v[l], (0, 2, 3, 1)), rel_bias[l], seq_s, 4)
    ob_s, ss = _gla(qb_s, kb_s, vb_s, la_s, _state_to_pairs(state_gla[l]), n_s, seq_s, seq_s, 1, 8)
    seg_s = 8
    tm_s = seg_s * seq_s
    ys = _stack(xs, oa_s, ob_s, r_s,
                cache_mem_k[l].reshape(n_s * N_MEM, D_MODEL),
                cache_mem_v[l].reshape(n_s * N_MEM, D_MODEL), stack_w, tm_s, seg_s, 1, 1)

    keep = min(A_WINDOW, seq)
    heads = lambda z, n, t: z.reshape(1, n, t, N_HEADS_A, HEAD_DIM_A)
    tails = lambda z: jnp.transpose(z.reshape(n_p, N_HEADS_A, HEAD_DIM_A, keep), (0, 3, 1, 2))[None]
    return (yp.reshape(n_p, seq, D_MODEL), ys.reshape(n_s, seq_s, D_MODEL),
            tails(k_tail), tails(v_tail), _pairs_to_state(sp)[None],
            mk.reshape(1, n_p, N_MEM, N_HEADS_MEM, HEAD_DIM_MEM), mv.reshape(1, n_p, N_MEM, N_HEADS_MEM, HEAD_DIM_MEM),
            heads(k_new, n_s, seq_s), heads(v_new, n_s, seq_s), _pairs_to_state(ss)[None])
```

```python
import functools
import itertools
import math

import jax
import jax.numpy as jnp
import numpy as np
from jax import lax
from jax.experimental import pallas as pl
from jax.experimental.pallas import tpu as pltpu

F32 = jnp.float32
BF16 = jnp.bfloat16

D_MODEL = 1024
CHUNK = 64
BAND_CHUNKS = 8
A_WINDOW = BAND_CHUNKS * CHUNK
N_HEADS_A = 8
HEAD_DIM_A = 64
WIDTH_A = 512
REL_CLIP = 256
N_HEADS_B = 4
KEY_DIM_B = 64
VAL_DIM_B = 128
WIDTH_BK = 256
WIDTH_BV = 512
GATE_RANK = 16
GATE_TAU = 16.0
N_MEM = 256
N_HEADS_MEM = 4
HEAD_DIM_MEM = 256
D_FF = 2816
EPS = 1e-6
NEG_BIG = -1e30
NT_DIMS = (((1,), (1,)), ((), ()))

LANES = 128
BIAS_W = 640
FF_BLOCK = 256
VMEM_LIMIT = 56 * 1024 * 1024

OFF_QA, OFF_KA, OFF_VA, OFF_QB, OFF_KB, OFF_VB, OFF_R = 0, 512, 1024, 1536, 1792, 2048, 2560
IN_WIDTH = OFF_R + GATE_RANK + WIDTH_BV


def _rms(x, g):
    return x * lax.rsqrt(jnp.mean(x * x, axis=-1, keepdims=True) + EPS) * g


def _silu(x):
    return x / (1.0 + jnp.exp(-x))


def _log_sigmoid(z):
    return jnp.minimum(z, 0.0) - jnp.log(1.0 + jnp.exp(-jnp.abs(z)))


def _const_spec(shape):
    nd = len(shape)
    return pl.BlockSpec(shape, lambda *_: (0,) * nd, pipeline_mode=pl.Buffered(1))


def _params(n_axes, vmem=None):
    return pltpu.CompilerParams(dimension_semantics=("arbitrary",) * n_axes, vmem_limit_bytes=vmem)


def _proj_kernel(tiles_per_seq, tail_rows, tail_t, x_ref, g_ref, w_ref, wa2_ref, ba_ref,
                 qa_ref, ka_ref, va_ref, qb_ref, kb_ref, vb_ref, r_ref, la_ref, kt_ref, vt_ref):
    h = _rms(x_ref[...], g_ref[...]).astype(BF16)

    def proj(lo, hi):
        return lax.dot_general(h, w_ref[lo:hi, :], NT_DIMS, preferred_element_type=F32)

    g_low = proj(OFF_R, OFF_R + LANES).astype(BF16)
    z = jnp.dot(g_low, wa2_ref[...], preferred_element_type=F32) + ba_ref[...]
    la_ref[...] = _log_sigmoid(z) * (1.0 / GATE_TAU)
    qa_ref[...] = (proj(OFF_QA, OFF_KA) * (HEAD_DIM_A ** -0.5)).astype(BF16)
    ka = proj(OFF_KA, OFF_VA)
    va = proj(OFF_VA, OFF_QB)
    ka_ref[...] = ka.astype(BF16)
    va_ref[...] = va.astype(BF16)
    qb_ref[...] = proj(OFF_QB, OFF_KB) * (KEY_DIM_B ** -0.5)
    kb_ref[...] = proj(OFF_KB, OFF_VB)
    vb_ref[...] = proj(OFF_VB, OFF_R).astype(BF16)
    r_ref[...] = proj(OFF_R + GATE_RANK, IN_WIDTH)

    @pl.when(pl.program_id(0) % tiles_per_seq == tiles_per_seq - 1)
    def _():
        k_keep, v_keep = ka[ka.shape[0] - tail_rows:], va[va.shape[0] - tail_rows:]
        kt_ref[...] = k_keep.T if tail_t else k_keep
        vt_ref[...] = v_keep.T if tail_t else v_keep


def _proj(x, g, w_t, wa2, ba, tm, tiles_per_seq, tail_rows, tail_t):
    ntok = x.shape[0]
    nt = ntok // tm
    nseq = nt // tiles_per_seq
    row = lambda w: pl.BlockSpec((tm, w), lambda i: (i, 0))
    assert tail_rows <= tm
    tail_shape = (WIDTH_A, tail_rows) if tail_t else (tail_rows, WIDTH_A)
    tail = pl.BlockSpec(tail_shape, lambda i: (i // tiles_per_seq, 0))
    out_shape = (
        jax.ShapeDtypeStruct((ntok, WIDTH_A), BF16),
        jax.ShapeDtypeStruct((ntok, WIDTH_A), BF16),
        jax.ShapeDtypeStruct((ntok, WIDTH_A), BF16),
        jax.ShapeDtypeStruct((ntok, WIDTH_BK), F32),
        jax.ShapeDtypeStruct((ntok, WIDTH_BK), F32),
        jax.ShapeDtypeStruct((ntok, WIDTH_BV), BF16),
        jax.ShapeDtypeStruct((ntok, WIDTH_BV), F32),
        jax.ShapeDtypeStruct((ntok, WIDTH_BK), F32),
        jax.ShapeDtypeStruct((nseq * tail_shape[0], tail_shape[1]), F32),
        jax.ShapeDtypeStruct((nseq * tail_shape[0], tail_shape[1]), F32),
    )
    return pl.pallas_call(
        functools.partial(_proj_kernel, tiles_per_seq, tail_rows, tail_t),
        grid=(nt,),
        in_specs=[row(D_MODEL), _const_spec((1, D_MODEL)), _const_spec((IN_WIDTH, D_MODEL)),
                  _const_spec((LANES, WIDTH_BK)), _const_spec((1, WIDTH_BK))],
        out_specs=(row(WIDTH_A), row(WIDTH_A), row(WIDTH_A), row(WIDTH_BK), row(WIDTH_BK),
                   row(WIDTH_BV), row(WIDTH_BV), row(WIDTH_BK), tail, tail),
        out_shape=out_shape,
        compiler_params=_params(1, VMEM_LIMIT),
        name="proj",
    )(x, g, w_t, wa2, ba)


def _stack_heads(x_pair):
    low_half = lax.broadcasted_iota(jnp.int32, x_pair.shape, 1) < HEAD_DIM_A
    zero = jnp.zeros_like(x_pair)
    return jnp.concatenate([jnp.where(low_half, x_pair, zero), jnp.where(low_half, zero, x_pair)], axis=0)


TQ = 4 * CHUNK
KEYS = 3 * TQ
SOFT_W = 5 * LANES
assert 2 * TQ == A_WINDOW and SOFT_W >= (BAND_CHUNKS + 1) * CHUNK + CHUNK


def _build_band_bias(base_ref, bm_ref):
    n_band = (BAND_CHUNKS + 1) * CHUNK
    col = lax.broadcasted_iota(jnp.int32, (CHUNK, KEYS), 1)
    for h in range(N_HEADS_A):
        rows = jnp.broadcast_to(base_ref[h:h + 1, :], (CHUNK, KEYS))
        for qc in range(TQ // CHUNK):
            toeplitz = pltpu.roll(rows, qc * CHUNK, 1, stride=1, stride_axis=0)
            in_band = (col >= qc * CHUNK) & (col < qc * CHUNK + n_band)
            r0 = (qc * 2 + h % 2) * CHUNK
            bm_ref[h // 2, r0:r0 + CHUNK, :] = jnp.where(in_band, toeplitz, NEG_BIG)


def _band_tile(q_ref, k_refs, v_refs, bm_ref, o_ref, start_penalty):
    n_qc, n_pairs = TQ // CHUNK, N_HEADS_A // 2
    low_half = lax.broadcasted_iota(jnp.int32, (CHUNK, LANES), 1) < HEAD_DIM_A
    pair_cols = lambda hp: slice(hp * LANES, (hp + 1) * LANES)

    def scores(hp):
        cols = pair_cols(hp)
        qs = jnp.concatenate([_stack_heads(q_ref[qc * CHUNK:(qc + 1) * CHUNK, cols]) for qc in range(n_qc)],
                             axis=0)
        return jnp.concatenate([lax.dot_general(qs, k[:, cols], NT_DIMS, preferred_element_type=F32)
                                for k in k_refs], axis=1)

    def softmax(hp, s):
        probs, inv_l = [], []
        for qc in range(n_qc):
            rows = slice(qc * 2 * CHUNK, (qc + 1) * 2 * CHUNK)
            c0 = 0 if (qc + 1) * CHUNK + A_WINDOW <= SOFT_W else KEYS - SOFT_W
            sq = s[rows, c0:c0 + SOFT_W] + bm_ref[hp, rows, c0:c0 + SOFT_W]
            if start_penalty is not None:
                sq = sq + start_penalty[:, c0:c0 + SOFT_W]
            p = jnp.exp(sq - jnp.max(sq, axis=-1, keepdims=True))
            inv_l.append(1.0 / jnp.sum(p, axis=-1, keepdims=True))
            pad = jnp.zeros((2 * CHUNK, KEYS - SOFT_W), BF16)
            probs.append(jnp.concatenate([p.astype(BF16), pad] if c0 == 0 else [pad, p.astype(BF16)], axis=1))
        return jnp.concatenate(probs, axis=0), inv_l

    def values(hp, pm, inv_l):
        cols = pair_cols(hp)
        out = sum(jnp.dot(pm[:, i * TQ:(i + 1) * TQ], v[:, cols], preferred_element_type=F32)
                  for i, v in enumerate(v_refs))
        for qc in range(n_qc):
            r0 = qc * 2 * CHUNK
            o0 = out[r0:r0 + CHUNK] * inv_l[qc][0:CHUNK]
            o1 = out[r0 + CHUNK:r0 + 2 * CHUNK] * inv_l[qc][CHUNK:2 * CHUNK]
            o_ref[qc * CHUNK:(qc + 1) * CHUNK, cols] = jnp.where(low_half, o0, o1).astype(o_ref.dtype)

    for hp in range(n_pairs):
        values(hp, *softmax(hp, scores(hp)))


def _attn_prompt_kernel(q_ref, k2_ref, k1_ref, k0_ref, v2_ref, v1_ref, v0_ref, base_ref, o_ref, bm):
    t = pl.program_id(1)

    @pl.when((pl.program_id(0) == 0) & (t == 0))
    def _():
        _build_band_bias(base_ref, bm)

    k_refs = (k2_ref, k1_ref, k0_ref)
    v_refs = (v2_ref, v1_ref, v0_ref)

    @pl.when(t >= 2)
    def _():
        _band_tile(q_ref, k_refs, v_refs, bm, o_ref, None)

    @pl.when(t < 2)
    def _():
        col = lax.broadcasted_iota(jnp.int32, (1, KEYS), 1)
        penalty = jnp.where(col < (2 - t) * TQ, NEG_BIG, 0.0)
        _band_tile(q_ref, k_refs, v_refs, bm, o_ref, penalty)


def _attn_prompt(q, k, v, base, n_batch, seq):
    nt = seq // TQ
    blk = lambda back: pl.BlockSpec((TQ, WIDTH_A), lambda b, t: (b * nt + jnp.maximum(t - back, 0), 0))
    return pl.pallas_call(
        _attn_prompt_kernel,
        grid=(n_batch, nt),
        in_specs=[blk(0), blk(2), blk(1), blk(0), blk(2), blk(1), blk(0), _const_spec((N_HEADS_A, KEYS))],
        out_specs=blk(0),
        out_shape=jax.ShapeDtypeStruct((n_batch * seq, WIDTH_A), BF16),
        scratch_shapes=[pltpu.VMEM((N_HEADS_A // 2, 2 * TQ, KEYS), F32)],
        compiler_params=_params(2, VMEM_LIMIT),
        name="band_attn_prompt",
    )(q, k, k, k, v, v, v, base)


def _attn_sample_one(rows, q_ref, kn_ref, vn_ref, kc_ref, vc_ref, sel_ref, dup_ref, o_ref, bias_c, bias_n):
    n_h = N_HEADS_A
    n_new = rows.stop - rows.start
    low_half = lax.broadcasted_iota(jnp.int32, (n_new, LANES), 1) < HEAD_DIM_A
    pairs = [slice(hp * LANES, (hp + 1) * LANES) for hp in range(n_h // 2)]
    q_heads = [jnp.dot(q_ref[rows, cols], sel_ref[e], preferred_element_type=F32).astype(BF16)
               for cols in pairs for e in range(2)]
    yield
    s_c = jnp.concatenate([jnp.dot(q_heads[h], kc_ref[h].astype(BF16), preferred_element_type=F32)
                           for h in range(n_h)], axis=0) + bias_c[...]
    s_n = jnp.concatenate([lax.dot_general(_stack_heads(q_ref[rows, cols]), kn_ref[rows, cols], NT_DIMS,
                                           preferred_element_type=F32) for cols in pairs], axis=0) + bias_n[...]
    yield
    m = jnp.maximum(jnp.max(s_c, axis=-1, keepdims=True), jnp.max(s_n, axis=-1, keepdims=True))
    p_c, p_n = jnp.exp(s_c - m), jnp.exp(s_n - m)
    inv_l = 1.0 / (jnp.sum(p_c, axis=-1, keepdims=True) + jnp.sum(p_n, axis=-1, keepdims=True))
    p_c, p_n = p_c.astype(BF16), p_n.astype(BF16)
    yield
    out_c = jnp.concatenate([lax.dot_general(p_c[h * n_new:(h + 1) * n_new], vc_ref[h].astype(BF16), NT_DIMS,
                                             preferred_element_type=F32) for h in range(n_h)], axis=0) * inv_l
    yield
    hi = out_c.astype(BF16)
    lo = (out_c - hi.astype(F32)).astype(BF16)
    out_c = (jnp.dot(hi, dup_ref[...], preferred_element_type=F32)
             + jnp.dot(lo, dup_ref[...], preferred_element_type=F32))
    for hp, cols in enumerate(pairs):
        pr = slice(hp * 2 * n_new, (hp + 1) * 2 * n_new)
        out = out_c[pr] + jnp.dot(p_n[pr], vn_ref[rows, cols], preferred_element_type=F32) * inv_l[pr]
        o_ref[rows, cols] = jnp.where(low_half, out[0:n_new], out[n_new:]).astype(o_ref.dtype)


def _attn_sample_kernel(n_new, q_ref, kn_ref, vn_ref, kc_ref, vc_ref, base_ref, sel_ref, dup_ref,
                        o_ref, bias_c, bias_n):
    n_cache = kc_ref.shape[3]

    @pl.when(pl.program_id(0) == 0)
    def _():
        for h in range(N_HEADS_A):
            rows = jnp.broadcast_to(base_ref[h:h + 1, :], (n_new, BIAS_W))
            toeplitz = pltpu.roll(rows, 0, 1, stride=1, stride_axis=0)
            bias_c[h * n_new:(h + 1) * n_new, :] = toeplitz[:, 0:n_cache]
            bias_n[h * n_new:(h + 1) * n_new, :] = toeplitz[:, n_cache:n_cache + n_new]

    requests = [_attn_sample_one(slice(i * n_new, (i + 1) * n_new), q_ref, kn_ref, vn_ref, kc_ref.at[i],
                                 vc_ref.at[i], sel_ref, dup_ref, o_ref, bias_c, bias_n)
                for i in range(kc_ref.shape[0])]
    for _ in itertools.zip_longest(*requests):
        pass


def _attn_sample(q, k_new, v_new, k_cache_t, v_cache_t, table, n_new, n_par):
    n_batch, n_h, _, n_cache = k_cache_t.shape
    assert n_cache == A_WINDOW and n_new <= CHUNK and n_h == N_HEADS_A
    lane = np.arange(LANES)
    sel = np.stack([lane[:, None] == np.arange(HEAD_DIM_A)[None, :] + e * HEAD_DIM_A for e in range(2)])
    dup = np.arange(HEAD_DIM_A)[:, None] == lane[None, :] % HEAD_DIM_A
    new = pl.BlockSpec((n_par * n_new, WIDTH_A), lambda g: (g, 0))
    old = pl.BlockSpec((n_par, n_h, HEAD_DIM_A, n_cache), lambda g: (g, 0, 0, 0))
    return pl.pallas_call(
        functools.partial(_attn_sample_kernel, n_new),
        grid=(n_batch // n_par,),
        in_specs=[new, new, new, old, old, _const_spec((n_h, BIAS_W)),
                  _const_spec((2, LANES, HEAD_DIM_A)), _const_spec((HEAD_DIM_A, LANES))],
        out_specs=new,
        out_shape=jax.ShapeDtypeStruct((n_batch * n_new, WIDTH_A), BF16),
        scratch_shapes=[pltpu.VMEM((n_h * n_new, n_cache), F32), pltpu.VMEM((n_h * n_new, n_new), F32)],
        compiler_params=_params(1, VMEM_LIMIT),
        name="band_attn_sample",
    )(q, k_new, v_new, k_cache_t, v_cache_t, _bias_base(table, BIAS_W), jnp.asarray(sel, BF16),
      jnp.asarray(dup, BF16))


def _gla_levels(chunk):
    return [chunk >> i for i in range(int(math.log2(chunk)) + 1)]


def _segment_matrix(chunk):
    i = np.arange(chunk)[:, None]
    t = np.arange(chunk)[None, :]
    blocks = []
    for li, s in enumerate(_gla_levels(chunk)):
        start = (i // s) * s
        f_rows = (t >= start) & (t <= i)
        r_rows = (t > i) & (t <= start + s - 1)
        if li == 0:
            blocks += [f_rows, r_rows]
        else:
            blocks.append(np.where((i // s) % 2 == 1, f_rows, r_rows))
    seg = np.concatenate(blocks, axis=0).astype(np.float32)
    return np.concatenate([seg, seg], axis=1)


def _level_masks(group_tokens):
    tg = group_tokens
    i = (np.arange(2 * tg) % tg)[:, None]
    j = np.arange(tg)[None, :]
    masks = [i == j]
    s = tg // 2
    while s >= 1:
        masks.append(((i // s) % 2 == 1) & (j // s == i // s - 1))
        s //= 2
    return np.stack(masks).astype(np.float32)


def _gla_tile(q, k, v, la, seg_ref, mask_ref, state_ref, o_ref, chunk, n_chunks):
    n, tg = n_chunks, chunk * n_chunks
    levels = _gla_levels(chunk)
    la_hi = la.astype(BF16)
    la_lo = (la - la_hi.astype(F32)).astype(BF16)
    split = jnp.concatenate(
        [jnp.concatenate([la_hi[c * chunk:(c + 1) * chunk], la_lo[c * chunk:(c + 1) * chunk]], axis=0)
         for c in range(n)], axis=1)
    sums = jnp.dot(seg_ref[...], split, preferred_element_type=F32)

    def seg_sum(block):
        rows = slice(block * chunk, (block + 1) * chunk)
        return jnp.minimum(jnp.concatenate(
            [sums[rows, c * WIDTH_BK:(c + 1) * WIDTH_BK] for c in range(n)], axis=0), 0.0)

    fwd0, rev0 = seg_sum(0), seg_sum(1)
    from_start, to_end = jnp.exp(fwd0), jnp.exp(rev0)
    total = [fwd0[(c + 1) * chunk - 1:(c + 1) * chunk] for c in range(n)]

    def chunk_sum(cs):
        cs = list(cs)
        return (sum(total[c] for c in cs[1:]) + total[cs[0]]) if cs else None

    def extend(base, offsets):
        pieces = []
        for c in range(n):
            x = base[c * chunk:(c + 1) * chunk]
            pieces.append(x if offsets[c] is None else x * jnp.exp(offsets[c]))
        return jnp.concatenate(pieces, axis=0) if n > 1 else pieces[0]

    q_state = (q * extend(from_start, [chunk_sum(range(0, c)) for c in range(n)])).astype(BF16)
    k_state = (k * extend(to_end, [chunk_sum(range(c + 1, n)) for c in range(n)])).astype(BF16)
    decay_tile = jnp.exp(chunk_sum(range(n)))
    ops = {}
    s = tg // 2
    while s >= chunk and n > 1:
        per = s // chunk
        offsets, bases = [], []
        for c in range(n):
            sg = c // per
            offsets.append(chunk_sum(range(sg * per, c)) if sg % 2 else chunk_sum(range(c + 1, (sg + 1) * per)))
            bases.append((from_start if sg % 2 else to_end)[c * chunk:(c + 1) * chunk])
        w = extend(jnp.concatenate(bases, axis=0), offsets)
        ops[s] = ((q * w).astype(BF16), (k * w).astype(BF16))
        s //= 2
    for li in range(1, len(levels)):
        w = jnp.exp(seg_sum(li + 1))
        ops[levels[li]] = ((q * w).astype(BF16), (k * w).astype(BF16))
    q16, k16 = q.astype(BF16), k.astype(BF16)
    yield

    gt = mask_ref.shape[2]
    own = (lax.broadcasted_iota(jnp.int32, (2 * VAL_DIM_B, LANES), 0) < VAL_DIM_B) == \
          (lax.broadcasted_iota(jnp.int32, (2 * VAL_DIM_B, LANES), 1) < KEY_DIM_B)
    nt_dims = (((1,), (1,)), ((), ()))
    tn_dims = (((0,), (0,)), ((), ()))
    n_pairs, n_groups = N_HEADS_B // 2, tg // gt
    pair_cols = [slice(p * LANES, (p + 1) * LANES) for p in range(n_pairs)]
    pair_v = [v[:, p * 2 * VAL_DIM_B:(p + 1) * 2 * VAL_DIM_B] for p in range(n_pairs)]
    head_v = lambda e: slice(e * VAL_DIM_B, (e + 1) * VAL_DIM_B)
    o_state = []
    for p, cols in enumerate(pair_cols):
        st = state_ref[p]
        o_state.append(lax.dot_general(q_state[:, cols], st.astype(BF16), nt_dims, preferred_element_type=F32))
        upd = lax.dot_general(pair_v[p], k_state[:, cols], tn_dims, preferred_element_type=F32)
        state_ref[p] = st * decay_tile[:, cols] + jnp.where(own, upd, 0.0)
    yield
    blocks = []
    for p, cols in enumerate(pair_cols):
        for g in range(n_groups):
            rows = slice(g * gt, (g + 1) * gt)
            a = mask_ref[0] * lax.dot_general(_stack_heads(q16[rows, cols]), k16[rows, cols], nt_dims,
                                              preferred_element_type=F32)
            m, s = 1, gt // 2
            while s >= 1:
                qd, kd = ops[s]
                a = a + mask_ref[m] * lax.dot_general(_stack_heads(qd[rows, cols]), kd[rows, cols], nt_dims,
                                                      preferred_element_type=F32)
                m, s = m + 1, s // 2
            blocks.append((p, g * gt, g * gt, a.astype(BF16)))
            yield
        s = tg // 2
        while s >= gt:
            qd, kd = ops[s]
            for blk in range(tg // (2 * s)):
                k0, q0 = blk * 2 * s, blk * 2 * s + s
                ab = lax.dot_general(_stack_heads(qd[q0:q0 + s, cols]), kd[k0:k0 + s, cols], nt_dims,
                                     preferred_element_type=F32)
                blocks.append((p, q0, k0, ab.astype(BF16)))
                yield
            s //= 2
    piece = [[[o_state[p][g * gt:(g + 1) * gt, head_v(e)] for g in range(n_groups)] for e in range(2)]
             for p in range(n_pairs)]
    for p, q0, k0, a in blocks:
        nq, nk = a.shape[0] // 2, a.shape[1]
        for e in range(2):
            contrib = jnp.dot(a[e * nq:(e + 1) * nq], pair_v[p][k0:k0 + nk, head_v(e)], preferred_element_type=F32)
            for gi in range(nq // gt):
                g = q0 // gt + gi
                piece[p][e][g] = piece[p][e][g] + contrib[gi * gt:(gi + 1) * gt]
        yield
    outs = [jnp.concatenate(piece[p][e], axis=0) if n_groups > 1 else piece[p][e][0]
            for p in range(n_pairs) for e in range(2)]
    o_ref[...] = jnp.concatenate(outs, axis=1)


def _gla_kernel(chunk, chunks_per_tile, has_init, *refs):
    if has_init:
        q_ref, k_ref, v_ref, la_ref, seg_ref, mask_ref, s0_ref, o_ref, sout_ref, state = refs
    else:
        q_ref, k_ref, v_ref, la_ref, seg_ref, mask_ref, o_ref, sout_ref, state = refs
    t = pl.program_id(1)

    @pl.when(t == 0)
    def _():
        state[...] = s0_ref[...] if has_init else jnp.zeros_like(state)

    tiles = [_gla_tile(q_ref[i], k_ref[i], v_ref[i], la_ref[i], seg_ref, mask_ref, state.at[i], o_ref.at[i],
                       chunk, chunks_per_tile) for i in range(q_ref.shape[0])]
    for _ in itertools.zip_longest(*tiles):
        pass

    @pl.when(t == pl.num_programs(1) - 1)
    def _():
        sout_ref[...] = state[...]


def _gla(q, k, v, la, s0, n_batch, seq, chunk, chunks_per_tile, n_par):
    tg = chunk * chunks_per_tile
    nt = seq // tg
    seg = jnp.asarray(_segment_matrix(chunk), BF16)
    masks = jnp.asarray(_level_masks(min(tg, LANES)), F32)
    row = lambda w: pl.BlockSpec((n_par, tg, w), lambda g, t: (g, t, 0))
    st = pl.BlockSpec((n_par, 2, 2 * VAL_DIM_B, LANES), lambda g, t: (g, 0, 0, 0))
    per_seq = lambda z: z.reshape(n_batch, seq, z.shape[-1])
    in_specs = [row(WIDTH_BK), row(WIDTH_BK), row(WIDTH_BV), row(WIDTH_BK), _const_spec(seg.shape),
                _const_spec(masks.shape)]
    args = [per_seq(q), per_seq(k), per_seq(v), per_seq(la), seg, masks]
    if s0 is not None:
        in_specs.append(st)
        args.append(s0)
    o, s_out = pl.pallas_call(
        functools.partial(_gla_kernel, chunk, chunks_per_tile, s0 is not None),
        grid=(n_batch // n_par, nt),
        in_specs=in_specs,
        out_specs=(row(WIDTH_BV), st),
        out_shape=(jax.ShapeDtypeStruct((n_batch, seq, WIDTH_BV), F32),
                   jax.ShapeDtypeStruct((n_batch, 2, 2 * VAL_DIM_B, LANES), F32)),
        scratch_shapes=[pltpu.VMEM((n_par, 2, 2 * VAL_DIM_B, LANES), F32)],
        compiler_params=_params(2, VMEM_LIMIT),
        name="gla_chunk%d" % chunk,
    )(*args)
    return o.reshape(n_batch * seq, WIDTH_BV), s_out


def _state_to_pairs(s):
    n = s.shape[0]
    st = jnp.swapaxes(s, -1, -2).reshape(n, 2, 2, VAL_DIM_B, KEY_DIM_B)
    z = jnp.zeros_like(st[:, :, 0])
    top = jnp.concatenate([st[:, :, 0], z], axis=-1)
    bot = jnp.concatenate([z, st[:, :, 1]], axis=-1)
    return jnp.concatenate([top, bot], axis=-2)


def _pairs_to_state(sp):
    n = sp.shape[0]
    h0 = sp[:, :, :VAL_DIM_B, :KEY_DIM_B]
    h1 = sp[:, :, VAL_DIM_B:, KEY_DIM_B:]
    st = jnp.stack([h0, h1], axis=2).reshape(n, N_HEADS_B, VAL_DIM_B, KEY_DIM_B)
    return jnp.swapaxes(st, -1, -2)


def _mem_kv_kernel(m_ref, g_ref, wk_ref, wv_ref, k_ref, v_ref):
    m = _rms(m_ref[...], g_ref[...]).astype(BF16)
    k_ref[...] = jnp.dot(m, wk_ref[...], preferred_element_type=F32)
    v_ref[...] = jnp.dot(m, wv_ref[...], preferred_element_type=F32)


def _mem_kv(mem, g, wk, wv):
    n = mem.shape[0]
    tm = N_MEM
    row = pl.BlockSpec((tm, D_MODEL), lambda i: (i, 0))
    return pl.pallas_call(
        _mem_kv_kernel,
        grid=(n // tm,),
        in_specs=[row, _const_spec((1, D_MODEL)), _const_spec((D_MODEL, D_MODEL)), _const_spec((D_MODEL, D_MODEL))],
        out_specs=(row, row),
        out_shape=(jax.ShapeDtypeStruct((n, D_MODEL), F32), jax.ShapeDtypeStruct((n, D_MODEL), F32)),
        compiler_params=_params(1, VMEM_LIMIT),
        name="mem_kv",
    )(mem, g, wk, wv)


def _stack_front(rows, segs, x_ref, oa_ref, ob_ref, r_ref, mk_ref, mv_ref, g_gla_ref, wo_ref, g_post_mix_ref,
                 g_pre_mem_ref, wq_ref, wmo_ref, g_post_mem_ref, result):
    n_rows = rows.stop - rows.start
    ob = ob_ref[rows, :]
    normed = []
    for h in range(N_HEADS_B):
        seg = ob[:, h * VAL_DIM_B:(h + 1) * VAL_DIM_B]
        normed.append(seg * lax.rsqrt(jnp.mean(seg * seg, axis=-1, keepdims=True) + EPS))
    yb = (jnp.concatenate(normed, axis=1) * g_gla_ref[...] * _silu(r_ref[rows, :])).astype(BF16)
    yield
    mix = (jnp.dot(oa_ref[rows, :], wo_ref[0:WIDTH_A, :], preferred_element_type=F32)
           + jnp.dot(yb, wo_ref[WIDTH_A:, :], preferred_element_type=F32))
    yield
    x1 = x_ref[rows, :] + _rms(mix, g_post_mix_ref[...])
    hq = _rms(x1, g_pre_mem_ref[...]).astype(BF16)
    yield
    q = (jnp.dot(hq, wq_ref[...], preferred_element_type=F32) * (HEAD_DIM_MEM ** -0.5)).astype(BF16)
    yield
    rows_per_seg = n_rows // len(segs)
    units = [(u, sg, h) for u, sg in enumerate(segs) for h in range(N_HEADS_MEM)]
    mem_rows = lambda sg: slice(sg * N_MEM, (sg + 1) * N_MEM)
    head_cols = lambda h: slice(h * HEAD_DIM_MEM, (h + 1) * HEAD_DIM_MEM)
    scores = [lax.dot_general(q[u * rows_per_seg:(u + 1) * rows_per_seg, head_cols(h)],
                              mk_ref[mem_rows(sg), head_cols(h)].astype(BF16), NT_DIMS,
                              preferred_element_type=F32) for u, sg, h in units]
    yield
    probs, inv_l = [], []
    for s in scores:
        p = jnp.exp(s - jnp.max(s, axis=-1, keepdims=True))
        inv_l.append(1.0 / jnp.sum(p, axis=-1, keepdims=True))
        probs.append(p.astype(BF16))
    yield
    outs = [jnp.dot(probs[i], mv_ref[mem_rows(sg), head_cols(h)].astype(BF16), preferred_element_type=F32) * inv_l[i]
            for i, (u, sg, h) in enumerate(units)]
    seg_outs = [jnp.concatenate(outs[u * N_HEADS_MEM:(u + 1) * N_HEADS_MEM], axis=1) for u in range(len(segs))]
    o = (jnp.concatenate(seg_outs, axis=0) if len(segs) > 1 else seg_outs[0]).astype(BF16)
    yield
    att = jnp.dot(o, wmo_ref[...], preferred_element_type=F32)
    yield
    result.append(x1 + _rms(att, g_post_mem_ref[...]))


def _stack_kernel(n_seg, x_ref, oa_ref, ob_ref, r_ref, mk_ref, mv_ref,
                  g_gla_ref, wo_ref, g_post_mix_ref, g_pre_mem_ref, wq_ref, wmo_ref, g_post_mem_ref,
                  g_pre_ffn_ref, wg_ref, wu_ref, wd_ref, g_post_ffn_ref, y_ref):
    tm = x_ref.shape[0]
    half = tm // 2
    halves = []
    for i in range(2):
        segs = list(range(i * n_seg // 2, (i + 1) * n_seg // 2)) if n_seg > 1 else [0]
        result = []
        halves.append((result, _stack_front(slice(i * half, (i + 1) * half), segs, x_ref, oa_ref, ob_ref, r_ref,
                                            mk_ref, mv_ref, g_gla_ref, wo_ref, g_post_mix_ref, g_pre_mem_ref,
                                            wq_ref, wmo_ref, g_post_mem_ref, result)))
    for _ in itertools.zip_longest(*[gen for _, gen in halves]):
        pass
    x2 = jnp.concatenate([result[0] for result, _ in halves], axis=0)

    hf = _rms(x2, g_pre_ffn_ref[...]).astype(BF16)
    f = jnp.zeros((tm, D_MODEL), F32)
    act, act_cols = None, None
    for c in range(D_FF // FF_BLOCK):
        cols = slice(c * FF_BLOCK, (c + 1) * FF_BLOCK)
        gate = jnp.dot(hf, wg_ref[:, cols], preferred_element_type=F32)
        up = jnp.dot(hf, wu_ref[:, cols], preferred_element_type=F32)
        if act is not None:
            f = f + jnp.dot(act, wd_ref[act_cols, :], preferred_element_type=F32)
        act, act_cols = (_silu(gate) * up).astype(BF16), cols
    f = f + jnp.dot(act, wd_ref[act_cols, :], preferred_element_type=F32)
    y_ref[...] = x2 + _rms(f, g_post_ffn_ref[...])


def _stack(x, oa, ob, r, mk, mv, weights, tm, n_seg, tiles_per_mem_block, mem_buffers):
    ntok = x.shape[0]
    row = lambda w: pl.BlockSpec((tm, w), lambda i: (i, 0))
    mem = pl.BlockSpec((n_seg * N_MEM, D_MODEL), lambda i: (i // tiles_per_mem_block, 0),
                       pipeline_mode=pl.Buffered(mem_buffers))
    w_specs = [_const_spec(w.shape) for w in weights]
    return pl.pallas_call(
        functools.partial(_stack_kernel, n_seg),
        grid=(ntok // tm,),
        in_specs=[row(D_MODEL), row(WIDTH_A), row(WIDTH_BV), row(WIDTH_BV), mem, mem] + w_specs,
        out_specs=row(D_MODEL),
        out_shape=jax.ShapeDtypeStruct((ntok, D_MODEL), F32),
        compiler_params=_params(1, VMEM_LIMIT),
        name="token_stack",
    )(x, oa, ob, r, mk, mv, *weights)


def _bias_base(table, width):
    u = np.arange(width)
    idx = np.where(u < (BAND_CHUNKS + 1) * CHUNK, np.clip(A_WINDOW - u, -REL_CLIP, REL_CLIP) + REL_CLIP, 2 * REL_CLIP)
    return table[:, idx]


def kernel(x_prompt, x_sample, mem_prompt, cache_a_k, cache_a_v, state_gla, cache_mem_k, cache_mem_v,
           g_pre_mix, w_in, rel_bias, w_alpha2, b_alpha, g_gla_out, w_o, g_post_mix,
           g_pre_mem, g_mem, w_mq, w_mk, w_mv, w_mo, g_post_mem,
           g_pre_ffn, w_ffn_gate, w_ffn_up, w_ffn_down, g_post_ffn):
    depth = w_in.shape[0]
    assert depth == 1
    l = 0
    n_p, seq, _ = x_prompt.shape
    n_s, seq_s, _ = x_sample.shape
    vec = lambda g: g[l].reshape(1, -1)

    wi = w_in[l]
    w_t = jnp.swapaxes(wi, 0, 1).astype(BF16)
    wa2 = jnp.pad(w_alpha2[l], ((0, LANES - GATE_RANK), (0, 0))).astype(BF16)
    stack_w = [vec(g_gla_out), w_o[l].astype(BF16), vec(g_post_mix), vec(g_pre_mem), w_mq[l].astype(BF16),
               w_mo[l].astype(BF16), vec(g_post_mem), vec(g_pre_ffn), w_ffn_gate[l].astype(BF16),
               w_ffn_up[l].astype(BF16), w_ffn_down[l].astype(BF16), vec(g_post_ffn)]

    xp = x_prompt.reshape(n_p * seq, D_MODEL)
    tm_p = 1024
    keep = min(A_WINDOW, seq)
    qa, ka, va, qb, kb, vb, r, la, k_tail, v_tail = _proj(xp, vec(g_pre_mix), w_t, wa2, vec(b_alpha), tm_p, seq // tm_p,
                                                          keep, True)
    oa = _attn_prompt(qa, ka, va, _bias_base(rel_bias[l], KEYS), n_p, seq)
    ob, sp = _gla(qb, kb, vb, la, None, n_p, seq, CHUNK, 4, n_p)
    mk, mv = _mem_kv(mem_prompt.reshape(n_p * N_MEM, D_MODEL), vec(g_mem), w_mk[l].astype(BF16), w_mv[l].astype(BF16))
    tm = 512
    yp = _stack(xp, oa, ob, r, mk.astype(BF16), mv.astype(BF16), stack_w, tm, 1, seq // tm, 2)

    xs = x_sample.reshape(n_s * seq_s, D_MODEL)
    ntok_s = n_s * seq_s
    qa_s, ka_s, va_s, qb_s, kb_s, vb_s, r_s, la_s, k_new, v_new = _proj(
        xs, vec(g_pre_mix), w_t, wa2, vec(b_alpha), ntok_s, 1, ntok_s, False)
    oa_s = _attn_sample(qa_s, ka_s, va_s, jnp.transpose(cache_a_k[l], (0, 2, 3, 1)),
                        jnp.transpose(cache_a_v[l], (0, 2, 3, 1)), rel_bias[l], seq_s, 4)
    ob_s, ss = _gla(qb_s, kb_s, vb_s, la_s, _state_to_pairs(state_gla[l]), n_s, seq_s, seq_s, 1, 8)
    seg_s = 4
    tm_s = seg_s * seq_s
    ys = _stack(xs, oa_s, ob_s, r_s,
                cache_mem_k[l].reshape(n_s * N_MEM, D_MODEL),
                cache_mem_v[l].reshape(n_s * N_MEM, D_MODEL), stack_w, tm_s, seg_s, 1, 2)

    heads =lambda z, n, t: z.reshape(1, n, t, N_HEADS_A, HEAD_DIM_A)
    tails = lambda z: jnp.transpose(z.reshape(n_p, N_HEADS_A, HEAD_DIM_A, keep), (0, 3, 1, 2))[None]
    return (yp.reshape(n_p, seq, D_MODEL), ys.reshape(n_s, seq_s, D_MODEL),
            tails(k_tail), tails(v_tail), _pairs_to_state(sp)[None],
            mk.reshape(1, n_p, N_MEM, N_HEADS_MEM, HEAD_DIM_MEM), mv.reshape(1, n_p, N_MEM, N_HEADS_MEM, HEAD_DIM_MEM),
            heads(k_new, n_s, seq_s), heads(v_new, n_s, seq_s), _pairs_to_state(ss)[None])
```

```python
import functools
import itertools
import math
from typing import NamedTuple

import jax
import jax.numpy as jnp
import numpy as np
from jax import lax
from jax.experimental import pallas as pl
from jax.experimental.pallas import tpu as pltpu

F32 = jnp.float32
BF16 = jnp.bfloat16

D_MODEL = 1024
CHUNK = 64
BAND_CHUNKS = 8
A_WINDOW = BAND_CHUNKS * CHUNK
N_HEADS_A = 8
HEAD_DIM_A = 64
WIDTH_A = 512
REL_CLIP = 256
N_HEADS_B = 4
KEY_DIM_B = 64
VAL_DIM_B = 128
WIDTH_BK = 256
WIDTH_BV = 512
GATE_RANK = 16
GATE_TAU = 16.0
N_MEM = 256
N_HEADS_MEM = 4
HEAD_DIM_MEM = 256
D_FF = 2816
EPS = 1e-6
NEG_BIG = -1e30
NT_DIMS = (((1,), (1,)), ((), ()))

LANES = 128
BIAS_W = 640
FF_BLOCK = 256
VMEM_LIMIT = 56 * 1024 * 1024


class _TilePlan(NamedTuple):
    proj_rows: int = 1024
    stack_rows: int = 512
    sample_attn_requests: int = 4
    sample_gla_requests: int = 8
    sample_stack_requests: int = 4


TILES = _TilePlan()

OFF_QA, OFF_KA, OFF_VA, OFF_QB, OFF_KB, OFF_VB, OFF_R = 0, 512, 1024, 1536, 1792, 2048, 2560
IN_WIDTH = OFF_R + GATE_RANK + WIDTH_BV


def _rms(x, g):
    return x * lax.rsqrt(jnp.mean(x * x, axis=-1, keepdims=True) + EPS) * g


def _silu(x):
    return x / (1.0 + jnp.exp(-x))


def _log_sigmoid(z):
    return jnp.minimum(z, 0.0) - jnp.log(1.0 + jnp.exp(-jnp.abs(z)))


def _const_spec(shape):
    nd = len(shape)
    return pl.BlockSpec(shape, lambda *_: (0,) * nd, pipeline_mode=pl.Buffered(1))


def _params(n_axes, vmem=None):
    return pltpu.CompilerParams(dimension_semantics=("arbitrary",) * n_axes, vmem_limit_bytes=vmem)


def _proj_kernel(tiles_per_seq, tail_rows, tail_t, x_ref, g_ref, w_ref, wa2_ref, ba_ref,
                 qa_ref, ka_ref, va_ref, qb_ref, kb_ref, vb_ref, r_ref, la_ref, kt_ref, vt_ref):
    h = _rms(x_ref[...], g_ref[...]).astype(BF16)

    def proj(lo, hi):
        return lax.dot_general(h, w_ref[lo:hi, :], NT_DIMS, preferred_element_type=F32)

    g_low = proj(OFF_R, OFF_R + LANES).astype(BF16)
    z = jnp.dot(g_low, wa2_ref[...], preferred_element_type=F32) + ba_ref[...]
    la_ref[...] = _log_sigmoid(z) * (1.0 / GATE_TAU)
    qa_ref[...] = (proj(OFF_QA, OFF_KA) * (HEAD_DIM_A ** -0.5)).astype(BF16)
    ka = proj(OFF_KA, OFF_VA)
    va = proj(OFF_VA, OFF_QB)
    ka_ref[...] = ka.astype(BF16)
    va_ref[...] = va.astype(BF16)
    qb_ref[...] = proj(OFF_QB, OFF_KB) * (KEY_DIM_B ** -0.5)
    kb_ref[...] = proj(OFF_KB, OFF_VB)
    vb_ref[...] = proj(OFF_VB, OFF_R).astype(BF16)
    r_ref[...] = proj(OFF_R + GATE_RANK, IN_WIDTH)

    @pl.when(pl.program_id(0) % tiles_per_seq == tiles_per_seq - 1)
    def _():
        k_keep, v_keep = ka[ka.shape[0] - tail_rows:], va[va.shape[0] - tail_rows:]
        kt_ref[...] = k_keep.T if tail_t else k_keep
        vt_ref[...] = v_keep.T if tail_t else v_keep


def _proj(x, g, w_t, wa2, ba, tm, tiles_per_seq, tail_rows, tail_t):
    ntok = x.shape[0]
    nt = ntok // tm
    nseq = nt // tiles_per_seq
    row = lambda w: pl.BlockSpec((tm, w), lambda i: (i, 0))
    assert tail_rows <= tm
    tail_shape = (WIDTH_A, tail_rows) if tail_t else (tail_rows, WIDTH_A)
    tail = pl.BlockSpec(tail_shape, lambda i: (i // tiles_per_seq, 0))
    out_shape = (
        jax.ShapeDtypeStruct((ntok, WIDTH_A), BF16),
        jax.ShapeDtypeStruct((ntok, WIDTH_A), BF16),
        jax.ShapeDtypeStruct((ntok, WIDTH_A), BF16),
        jax.ShapeDtypeStruct((ntok, WIDTH_BK), F32),
        jax.ShapeDtypeStruct((ntok, WIDTH_BK), F32),
        jax.ShapeDtypeStruct((ntok, WIDTH_BV), BF16),
        jax.ShapeDtypeStruct((ntok, WIDTH_BV), F32),
        jax.ShapeDtypeStruct((ntok, WIDTH_BK), F32),
        jax.ShapeDtypeStruct((nseq * tail_shape[0], tail_shape[1]), F32),
        jax.ShapeDtypeStruct((nseq * tail_shape[0], tail_shape[1]), F32),
    )
    return pl.pallas_call(
        functools.partial(_proj_kernel, tiles_per_seq, tail_rows, tail_t),
        grid=(nt,),
        in_specs=[row(D_MODEL), _const_spec((1, D_MODEL)), _const_spec((IN_WIDTH, D_MODEL)),
                  _const_spec((LANES, WIDTH_BK)), _const_spec((1, WIDTH_BK))],
        out_specs=(row(WIDTH_A), row(WIDTH_A), row(WIDTH_A), row(WIDTH_BK), row(WIDTH_BK),
                   row(WIDTH_BV), row(WIDTH_BV), row(WIDTH_BK), tail, tail),
        out_shape=out_shape,
        compiler_params=_params(1, VMEM_LIMIT),
        name="proj",
    )(x, g, w_t, wa2, ba)


def _stack_heads(x_pair):
    low_half = lax.broadcasted_iota(jnp.int32, x_pair.shape, 1) < HEAD_DIM_A
    zero = jnp.zeros_like(x_pair)
    return jnp.concatenate([jnp.where(low_half, x_pair, zero), jnp.where(low_half, zero, x_pair)], axis=0)


TQ = 4 * CHUNK
KEYS = 3 * TQ
SOFT_W = 5 * LANES
assert 2 * TQ == A_WINDOW and SOFT_W >= (BAND_CHUNKS + 1) * CHUNK + CHUNK


def _build_band_bias(base_ref, bm_ref):
    n_band = (BAND_CHUNKS + 1) * CHUNK
    col = lax.broadcasted_iota(jnp.int32, (CHUNK, KEYS), 1)
    for h in range(N_HEADS_A):
        rows = jnp.broadcast_to(base_ref[h:h + 1, :], (CHUNK, KEYS))
        for qc in range(TQ // CHUNK):
            toeplitz = pltpu.roll(rows, qc * CHUNK, 1, stride=1, stride_axis=0)
            in_band = (col >= qc * CHUNK) & (col < qc * CHUNK + n_band)
            r0 = (qc * 2 + h % 2) * CHUNK
            bm_ref[h // 2, r0:r0 + CHUNK, :] = jnp.where(in_band, toeplitz, NEG_BIG)


def _band_tile(q_ref, k_refs, v_refs, bm_ref, o_ref, start_penalty):
    n_qc, n_pairs = TQ // CHUNK, N_HEADS_A // 2
    low_half = lax.broadcasted_iota(jnp.int32, (CHUNK, LANES), 1) < HEAD_DIM_A
    pair_cols = lambda hp: slice(hp * LANES, (hp + 1) * LANES)

    def scores(hp):
        cols = pair_cols(hp)
        qs = jnp.concatenate([_stack_heads(q_ref[qc * CHUNK:(qc + 1) * CHUNK, cols]) for qc in range(n_qc)],
                             axis=0)
        return jnp.concatenate([lax.dot_general(qs, k[:, cols], NT_DIMS, preferred_element_type=F32)
                                for k in k_refs], axis=1)

    def softmax(hp, s):
        probs, inv_l = [], []
        for qc in range(n_qc):
            rows = slice(qc * 2 * CHUNK, (qc + 1) * 2 * CHUNK)
            c0 = 0 if (qc + 1) * CHUNK + A_WINDOW <= SOFT_W else KEYS - SOFT_W
            sq = s[rows, c0:c0 + SOFT_W] + bm_ref[hp, rows, c0:c0 + SOFT_W]
            if start_penalty is not None:
                sq = sq + start_penalty[:, c0:c0 + SOFT_W]
            p = jnp.exp(sq - jnp.max(sq, axis=-1, keepdims=True))
            inv_l.append(1.0 / jnp.sum(p, axis=-1, keepdims=True))
            pad = jnp.zeros((2 * CHUNK, KEYS - SOFT_W), BF16)
            probs.append(jnp.concatenate([p.astype(BF16), pad] if c0 == 0 else [pad, p.astype(BF16)], axis=1))
        return jnp.concatenate(probs, axis=0), inv_l

    def values(hp, pm, inv_l):
        cols = pair_cols(hp)
        out = sum(jnp.dot(pm[:, i * TQ:(i + 1) * TQ], v[:, cols], preferred_element_type=F32)
                  for i, v in enumerate(v_refs))
        for qc in range(n_qc):
            r0 = qc * 2 * CHUNK
            o0 = out[r0:r0 + CHUNK] * inv_l[qc][0:CHUNK]
            o1 = out[r0 + CHUNK:r0 + 2 * CHUNK] * inv_l[qc][CHUNK:2 * CHUNK]
            o_ref[qc * CHUNK:(qc + 1) * CHUNK, cols] = jnp.where(low_half, o0, o1).astype(o_ref.dtype)

    for hp in range(n_pairs):
        values(hp, *softmax(hp, scores(hp)))


def _attn_prompt_kernel(q_ref, k2_ref, k1_ref, k0_ref, v2_ref, v1_ref, v0_ref, base_ref, o_ref, bm):
    t = pl.program_id(1)

    @pl.when((pl.program_id(0) == 0) & (t == 0))
    def _():
        _build_band_bias(base_ref, bm)

    k_refs = (k2_ref, k1_ref, k0_ref)
    v_refs = (v2_ref, v1_ref, v0_ref)

    @pl.when(t >= 2)
    def _():
        _band_tile(q_ref, k_refs, v_refs, bm, o_ref, None)

    @pl.when(t < 2)
    def _():
        col = lax.broadcasted_iota(jnp.int32, (1, KEYS), 1)
        penalty = jnp.where(col < (2 - t) * TQ, NEG_BIG, 0.0)
        _band_tile(q_ref, k_refs, v_refs, bm, o_ref, penalty)


def _attn_prompt(q, k, v, base, n_batch, seq):
    nt = seq // TQ
    blk = lambda back: pl.BlockSpec((TQ, WIDTH_A), lambda b, t: (b * nt + jnp.maximum(t - back, 0), 0))
    return pl.pallas_call(
        _attn_prompt_kernel,
        grid=(n_batch, nt),
        in_specs=[blk(0), blk(2), blk(1), blk(0), blk(2), blk(1), blk(0), _const_spec((N_HEADS_A, KEYS))],
        out_specs=blk(0),
        out_shape=jax.ShapeDtypeStruct((n_batch * seq, WIDTH_A), BF16),
        scratch_shapes=[pltpu.VMEM((N_HEADS_A // 2, 2 * TQ, KEYS), F32)],
        compiler_params=_params(2, VMEM_LIMIT),
        name="band_attn_prompt",
    )(q, k, k, k, v, v, v, base)


def _attn_sample_one(rows, q_ref, kn_ref, vn_ref, kc_ref, vc_ref, sel_ref, dup_ref, o_ref, bias_c, bias_n):
    n_h = N_HEADS_A
    n_new = rows.stop - rows.start
    low_half = lax.broadcasted_iota(jnp.int32, (n_new, LANES), 1) < HEAD_DIM_A
    pairs = [slice(hp * LANES, (hp + 1) * LANES) for hp in range(n_h // 2)]
    q_heads = [jnp.dot(q_ref[rows, cols], sel_ref[e], preferred_element_type=F32).astype(BF16)
               for cols in pairs for e in range(2)]
    yield
    s_c = jnp.concatenate([jnp.dot(q_heads[h], kc_ref[h].astype(BF16), preferred_element_type=F32)
                           for h in range(n_h)], axis=0) + bias_c[...]
    s_n = jnp.concatenate([lax.dot_general(_stack_heads(q_ref[rows, cols]), kn_ref[rows, cols], NT_DIMS,
                                           preferred_element_type=F32) for cols in pairs], axis=0) + bias_n[...]
    yield
    m = jnp.maximum(jnp.max(s_c, axis=-1, keepdims=True), jnp.max(s_n, axis=-1, keepdims=True))
    p_c, p_n = jnp.exp(s_c - m), jnp.exp(s_n - m)
    inv_l = 1.0 / (jnp.sum(p_c, axis=-1, keepdims=True) + jnp.sum(p_n, axis=-1, keepdims=True))
    p_c, p_n = p_c.astype(BF16), p_n.astype(BF16)
    yield
    out_c = jnp.concatenate([lax.dot_general(p_c[h * n_new:(h + 1) * n_new], vc_ref[h].astype(BF16), NT_DIMS,
                                             preferred_element_type=F32) for h in range(n_h)], axis=0) * inv_l
    yield
    hi = out_c.astype(BF16)
    lo = (out_c - hi.astype(F32)).astype(BF16)
    out_c = (jnp.dot(hi, dup_ref[...], preferred_element_type=F32)
             + jnp.dot(lo, dup_ref[...], preferred_element_type=F32))
    for hp, cols in enumerate(pairs):
        pr = slice(hp * 2 * n_new, (hp + 1) * 2 * n_new)
        out = out_c[pr] + jnp.dot(p_n[pr], vn_ref[rows, cols], preferred_element_type=F32) * inv_l[pr]
        o_ref[rows, cols] = jnp.where(low_half, out[0:n_new], out[n_new:]).astype(o_ref.dtype)


def _attn_sample_kernel(n_new, q_ref, kn_ref, vn_ref, kc_ref, vc_ref, base_ref, sel_ref, dup_ref,
                        o_ref, bias_c, bias_n):
    n_cache = kc_ref.shape[3]

    @pl.when(pl.program_id(0) == 0)
    def _():
        for h in range(N_HEADS_A):
            rows = jnp.broadcast_to(base_ref[h:h + 1, :], (n_new, BIAS_W))
            toeplitz = pltpu.roll(rows, 0, 1, stride=1, stride_axis=0)
            bias_c[h * n_new:(h + 1) * n_new, :] = toeplitz[:, 0:n_cache]
            bias_n[h * n_new:(h + 1) * n_new, :] = toeplitz[:, n_cache:n_cache + n_new]

    requests = [_attn_sample_one(slice(i * n_new, (i + 1) * n_new), q_ref, kn_ref, vn_ref, kc_ref.at[i],
                                 vc_ref.at[i], sel_ref, dup_ref, o_ref, bias_c, bias_n)
                for i in range(kc_ref.shape[0])]
    for _ in itertools.zip_longest(*requests):
        pass


def _attn_sample(q, k_new, v_new, k_cache_t, v_cache_t, table, n_new, n_par):
    n_batch, n_h, _, n_cache = k_cache_t.shape
    assert n_cache == A_WINDOW and n_new <= CHUNK and n_h == N_HEADS_A
    lane = np.arange(LANES)
    sel = np.stack([lane[:, None] == np.arange(HEAD_DIM_A)[None, :] + e * HEAD_DIM_A for e in range(2)])
    dup = np.arange(HEAD_DIM_A)[:, None] == lane[None, :] % HEAD_DIM_A
    new = pl.BlockSpec((n_par * n_new, WIDTH_A), lambda g: (g, 0))
    old = pl.BlockSpec((n_par, n_h, HEAD_DIM_A, n_cache), lambda g: (g, 0, 0, 0))
    return pl.pallas_call(
        functools.partial(_attn_sample_kernel, n_new),
        grid=(n_batch // n_par,),
        in_specs=[new, new, new, old, old, _const_spec((n_h, BIAS_W)),
                  _const_spec((2, LANES, HEAD_DIM_A)), _const_spec((HEAD_DIM_A, LANES))],
        out_specs=new,
        out_shape=jax.ShapeDtypeStruct((n_batch * n_new, WIDTH_A), BF16),
        scratch_shapes=[pltpu.VMEM((n_h * n_new, n_cache), F32), pltpu.VMEM((n_h * n_new, n_new), F32)],
        compiler_params=_params(1, VMEM_LIMIT),
        name="band_attn_sample",
    )(q, k_new, v_new, k_cache_t, v_cache_t, _bias_base(table, BIAS_W), jnp.asarray(sel, BF16),
      jnp.asarray(dup, BF16))


def _gla_levels(chunk):
    return [chunk >> i for i in range(int(math.log2(chunk)) + 1)]


def _segment_matrix(chunk):
    i = np.arange(chunk)[:, None]
    t = np.arange(chunk)[None, :]
    blocks = []
    for li, s in enumerate(_gla_levels(chunk)):
        start = (i // s) * s
        f_rows = (t >= start) & (t <= i)
        r_rows = (t > i) & (t <= start + s - 1)
        if li == 0:
            blocks += [f_rows, r_rows]
        else:
            blocks.append(np.where((i // s) % 2 == 1, f_rows, r_rows))
    seg = np.concatenate(blocks, axis=0).astype(np.float32)
    return np.concatenate([seg, seg], axis=1)


def _level_masks(group_tokens):
    tg = group_tokens
    i = (np.arange(2 * tg) % tg)[:, None]
    j = np.arange(tg)[None, :]
    masks = [i == j]
    s = tg // 2
    while s >= 1:
        masks.append(((i // s) % 2 == 1) & (j // s == i // s - 1))
        s //= 2
    return np.stack(masks).astype(np.float32)


def _gla_tile(q, k, v, la, seg_ref, mask_ref, state_ref, o_ref, chunk, n_chunks):
    n, tg = n_chunks, chunk * n_chunks
    levels = _gla_levels(chunk)
    la_hi = la.astype(BF16)
    la_lo = (la - la_hi.astype(F32)).astype(BF16)
    split = jnp.concatenate(
        [jnp.concatenate([la_hi[c * chunk:(c + 1) * chunk], la_lo[c * chunk:(c + 1) * chunk]], axis=0)
         for c in range(n)], axis=1)
    sums = jnp.dot(seg_ref[...], split, preferred_element_type=F32)

    def seg_sum(block):
        rows = slice(block * chunk, (block + 1) * chunk)
        return jnp.minimum(jnp.concatenate(
            [sums[rows, c * WIDTH_BK:(c + 1) * WIDTH_BK] for c in range(n)], axis=0), 0.0)

    fwd0, rev0 = seg_sum(0), seg_sum(1)
    from_start, to_end = jnp.exp(fwd0), jnp.exp(rev0)
    total = [fwd0[(c + 1) * chunk - 1:(c + 1) * chunk] for c in range(n)]

    def chunk_sum(cs):
        cs = list(cs)
        return (sum(total[c] for c in cs[1:]) + total[cs[0]]) if cs else None

    def extend(base, offsets):
        pieces = []
        for c in range(n):
            x = base[c * chunk:(c + 1) * chunk]
            pieces.append(x if offsets[c] is None else x * jnp.exp(offsets[c]))
        return jnp.concatenate(pieces, axis=0) if n > 1 else pieces[0]

    q_state = (q * extend(from_start, [chunk_sum(range(0, c)) for c in range(n)])).astype(BF16)
    k_state = (k * extend(to_end, [chunk_sum(range(c + 1, n)) for c in range(n)])).astype(BF16)
    decay_tile = jnp.exp(chunk_sum(range(n)))
    ops = {}
    s = tg // 2
    while s >= chunk and n > 1:
        per = s // chunk
        offsets, bases = [], []
        for c in range(n):
            sg = c // per
            offsets.append(chunk_sum(range(sg * per, c)) if sg % 2 else chunk_sum(range(c + 1, (sg + 1) * per)))
            bases.append((from_start if sg % 2 else to_end)[c * chunk:(c + 1) * chunk])
        w = extend(jnp.concatenate(bases, axis=0), offsets)
        ops[s] = ((q * w).astype(BF16), (k * w).astype(BF16))
        s //= 2
    for li in range(1, len(levels)):
        w = jnp.exp(seg_sum(li + 1))
        ops[levels[li]] = ((q * w).astype(BF16), (k * w).astype(BF16))
    q16, k16 = q.astype(BF16), k.astype(BF16)
    yield

    gt = mask_ref.shape[2]
    own = (lax.broadcasted_iota(jnp.int32, (2 * VAL_DIM_B, LANES), 0) < VAL_DIM_B) == \
          (lax.broadcasted_iota(jnp.int32, (2 * VAL_DIM_B, LANES), 1) < KEY_DIM_B)
    nt_dims = (((1,), (1,)), ((), ()))
    tn_dims = (((0,), (0,)), ((), ()))
    n_pairs, n_groups = N_HEADS_B // 2, tg // gt
    pair_cols = [slice(p * LANES, (p + 1) * LANES) for p in range(n_pairs)]
    pair_v = [v[:, p * 2 * VAL_DIM_B:(p + 1) * 2 * VAL_DIM_B] for p in range(n_pairs)]
    head_v = lambda e: slice(e * VAL_DIM_B, (e + 1) * VAL_DIM_B)
    o_state = []
    for p, cols in enumerate(pair_cols):
        st = state_ref[p]
        o_state.append(lax.dot_general(q_state[:, cols], st.astype(BF16), nt_dims, preferred_element_type=F32))
        upd = lax.dot_general(pair_v[p], k_state[:, cols], tn_dims, preferred_element_type=F32)
        state_ref[p] = st * decay_tile[:, cols] + jnp.where(own, upd, 0.0)
    yield
    blocks = []
    for p, cols in enumerate(pair_cols):
        for g in range(n_groups):
            rows = slice(g * gt, (g + 1) * gt)
            a = mask_ref[0] * lax.dot_general(_stack_heads(q16[rows, cols]), k16[rows, cols], nt_dims,
                                              preferred_element_type=F32)
            m, s = 1, gt // 2
            while s >= 1:
                qd, kd = ops[s]
                a = a + mask_ref[m] * lax.dot_general(_stack_heads(qd[rows, cols]), kd[rows, cols], nt_dims,
                                                      preferred_element_type=F32)
                m, s = m + 1, s // 2
            blocks.append((p, g * gt, g * gt, a.astype(BF16)))
            yield
        s = tg // 2
        while s >= gt:
            qd, kd = ops[s]
            for blk in range(tg // (2 * s)):
                k0, q0 = blk * 2 * s, blk * 2 * s + s
                ab = lax.dot_general(_stack_heads(qd[q0:q0 + s, cols]), kd[k0:k0 + s, cols], nt_dims,
                                     preferred_element_type=F32)
                blocks.append((p, q0, k0, ab.astype(BF16)))
                yield
            s //= 2
    piece = [[[o_state[p][g * gt:(g + 1) * gt, head_v(e)] for g in range(n_groups)] for e in range(2)]
             for p in range(n_pairs)]
    for p, q0, k0, a in blocks:
        nq, nk = a.shape[0] // 2, a.shape[1]
        for e in range(2):
            contrib = jnp.dot(a[e * nq:(e + 1) * nq], pair_v[p][k0:k0 + nk, head_v(e)], preferred_element_type=F32)
            for gi in range(nq // gt):
                g = q0 // gt + gi
                piece[p][e][g] = piece[p][e][g] + contrib[gi * gt:(gi + 1) * gt]
        yield
    outs = [jnp.concatenate(piece[p][e], axis=0) if n_groups > 1 else piece[p][e][0]
            for p in range(n_pairs) for e in range(2)]
    o_ref[...] = jnp.concatenate(outs, axis=1)


def _gla_kernel(chunk, chunks_per_tile, has_init, *refs):
    if has_init:
        q_ref, k_ref, v_ref, la_ref, seg_ref, mask_ref, s0_ref, o_ref, sout_ref, state = refs
    else:
        q_ref, k_ref, v_ref, la_ref, seg_ref, mask_ref, o_ref, sout_ref, state = refs
    t = pl.program_id(1)

    @pl.when(t == 0)
    def _():
        state[...] = s0_ref[...] if has_init else jnp.zeros_like(state)

    tiles = [_gla_tile(q_ref[i], k_ref[i], v_ref[i], la_ref[i], seg_ref, mask_ref, state.at[i], o_ref.at[i],
                       chunk, chunks_per_tile) for i in range(q_ref.shape[0])]
    for _ in itertools.zip_longest(*tiles):
        pass

    @pl.when(t == pl.num_programs(1) - 1)
    def _():
        sout_ref[...] = state[...]


def _gla(q, k, v, la, s0, n_batch, seq, chunk, chunks_per_tile, n_par):
    tg = chunk * chunks_per_tile
    nt = seq // tg
    seg = jnp.asarray(_segment_matrix(chunk), BF16)
    masks = jnp.asarray(_level_masks(min(tg, LANES)), F32)
    row = lambda w: pl.BlockSpec((n_par, tg, w), lambda g, t: (g, t, 0))
    st = pl.BlockSpec((n_par, 2, 2 * VAL_DIM_B, LANES), lambda g, t: (g, 0, 0, 0))
    per_seq = lambda z: z.reshape(n_batch, seq, z.shape[-1])
    in_specs = [row(WIDTH_BK), row(WIDTH_BK), row(WIDTH_BV), row(WIDTH_BK), _const_spec(seg.shape),
                _const_spec(masks.shape)]
    args = [per_seq(q), per_seq(k), per_seq(v), per_seq(la), seg, masks]
    if s0 is not None:
        in_specs.append(st)
        args.append(s0)
    o, s_out = pl.pallas_call(
        functools.partial(_gla_kernel, chunk, chunks_per_tile, s0 is not None),
        grid=(n_batch // n_par, nt),
        in_specs=in_specs,
        out_specs=(row(WIDTH_BV), st),
        out_shape=(jax.ShapeDtypeStruct((n_batch, seq, WIDTH_BV), F32),
                   jax.ShapeDtypeStruct((n_batch, 2, 2 * VAL_DIM_B, LANES), F32)),
        scratch_shapes=[pltpu.VMEM((n_par, 2, 2 * VAL_DIM_B, LANES), F32)],
        compiler_params=_params(2, VMEM_LIMIT),
        name="gla_chunk%d" % chunk,
    )(*args)
    return o.reshape(n_batch * seq, WIDTH_BV), s_out


def _state_to_pairs(s):
    n = s.shape[0]
    st = jnp.swapaxes(s, -1, -2).reshape(n, 2, 2, VAL_DIM_B, KEY_DIM_B)
    z = jnp.zeros_like(st[:, :, 0])
    top = jnp.concatenate([st[:, :, 0], z], axis=-1)
    bot = jnp.concatenate([z, st[:, :, 1]], axis=-1)
    return jnp.concatenate([top, bot], axis=-2)


def _pairs_to_state(sp):
    n = sp.shape[0]
    h0 = sp[:, :, :VAL_DIM_B, :KEY_DIM_B]
    h1 = sp[:, :, VAL_DIM_B:, KEY_DIM_B:]
    st = jnp.stack([h0, h1], axis=2).reshape(n, N_HEADS_B, VAL_DIM_B, KEY_DIM_B)
    return jnp.swapaxes(st, -1, -2)


def _mem_kv_kernel(m_ref, g_ref, wk_ref, wv_ref, k_ref, v_ref):
    m = _rms(m_ref[...], g_ref[...]).astype(BF16)
    k_ref[...] = jnp.dot(m, wk_ref[...], preferred_element_type=F32)
    v_ref[...] = jnp.dot(m, wv_ref[...], preferred_element_type=F32)


def _mem_kv(mem, g, wk, wv):
    n = mem.shape[0]
    tm = N_MEM
    row = pl.BlockSpec((tm, D_MODEL), lambda i: (i, 0))
    return pl.pallas_call(
        _mem_kv_kernel,
        grid=(n // tm,),
        in_specs=[row, _const_spec((1, D_MODEL)), _const_spec((D_MODEL, D_MODEL)), _const_spec((D_MODEL, D_MODEL))],
        out_specs=(row, row),
        out_shape=(jax.ShapeDtypeStruct((n, D_MODEL), F32), jax.ShapeDtypeStruct((n, D_MODEL), F32)),
        compiler_params=_params(1, VMEM_LIMIT),
        name="mem_kv",
    )(mem, g, wk, wv)


def _stack_front(rows, segs, x_ref, oa_ref, ob_ref, r_ref, mk_ref, mv_ref, g_gla_ref, wo_ref, g_post_mix_ref,
                 g_pre_mem_ref, wq_ref, wmo_ref, g_post_mem_ref, result):
    n_rows = rows.stop - rows.start
    ob = ob_ref[rows, :]
    normed = []
    for h in range(N_HEADS_B):
        seg = ob[:, h * VAL_DIM_B:(h + 1) * VAL_DIM_B]
        normed.append(seg * lax.rsqrt(jnp.mean(seg * seg, axis=-1, keepdims=True) + EPS))
    yb = (jnp.concatenate(normed, axis=1) * g_gla_ref[...] * _silu(r_ref[rows, :])).astype(BF16)
    yield
    mix = (jnp.dot(oa_ref[rows, :], wo_ref[0:WIDTH_A, :], preferred_element_type=F32)
           + jnp.dot(yb, wo_ref[WIDTH_A:, :], preferred_element_type=F32))
    yield
    x1 = x_ref[rows, :] + _rms(mix, g_post_mix_ref[...])
    hq = _rms(x1, g_pre_mem_ref[...]).astype(BF16)
    yield
    q = (jnp.dot(hq, wq_ref[...], preferred_element_type=F32) * (HEAD_DIM_MEM ** -0.5)).astype(BF16)
    yield
    rows_per_seg = n_rows // len(segs)
    units = [(u, sg, h) for u, sg in enumerate(segs) for h in range(N_HEADS_MEM)]
    mem_rows = lambda sg: slice(sg * N_MEM, (sg + 1) * N_MEM)
    head_cols = lambda h: slice(h * HEAD_DIM_MEM, (h + 1) * HEAD_DIM_MEM)
    scores = [lax.dot_general(q[u * rows_per_seg:(u + 1) * rows_per_seg, head_cols(h)],
                              mk_ref[mem_rows(sg), head_cols(h)].astype(BF16), NT_DIMS,
                              preferred_element_type=F32) for u, sg, h in units]
    yield
    probs, inv_l = [], []
    for s in scores:
        p = jnp.exp(s - jnp.max(s, axis=-1, keepdims=True))
        inv_l.append(1.0 / jnp.sum(p, axis=-1, keepdims=True))
        probs.append(p.astype(BF16))
    yield
    outs = [jnp.dot(probs[i], mv_ref[mem_rows(sg), head_cols(h)].astype(BF16), preferred_element_type=F32) * inv_l[i]
            for i, (u, sg, h) in enumerate(units)]
    seg_outs = [jnp.concatenate(outs[u * N_HEADS_MEM:(u + 1) * N_HEADS_MEM], axis=1) for u in range(len(segs))]
    o = (jnp.concatenate(seg_outs, axis=0) if len(segs) > 1 else seg_outs[0]).astype(BF16)
    yield
    att = jnp.dot(o, wmo_ref[...], preferred_element_type=F32)
    yield
    result.append(x1 + _rms(att, g_post_mem_ref[...]))


def _stack_ffn(x2, g_pre_ffn_ref, wg_ref, wu_ref, wd_ref, g_post_ffn_ref, y_ref):
    hf = _rms(x2, g_pre_ffn_ref[...]).astype(BF16)
    f = jnp.zeros(x2.shape, F32)
    act, act_cols = None, None
    for c in range(D_FF // FF_BLOCK):
        cols = slice(c * FF_BLOCK, (c + 1) * FF_BLOCK)
        gate = jnp.dot(hf, wg_ref[:, cols], preferred_element_type=F32)
        up = jnp.dot(hf, wu_ref[:, cols], preferred_element_type=F32)
        if act is not None:
            f = f + jnp.dot(act, wd_ref[act_cols, :], preferred_element_type=F32)
        act, act_cols = (_silu(gate) * up).astype(BF16), cols
    f = f + jnp.dot(act, wd_ref[act_cols, :], preferred_element_type=F32)
    y_ref[...] = x2 + _rms(f, g_post_ffn_ref[...])


def _stack_kernel(n_seg, x_ref, oa_ref, ob_ref, r_ref, mk_ref, mv_ref,
                  g_gla_ref, wo_ref, g_post_mix_ref, g_pre_mem_ref, wq_ref, wmo_ref, g_post_mem_ref,
                  g_pre_ffn_ref, wg_ref, wu_ref, wd_ref, g_post_ffn_ref, y_ref):
    tm = x_ref.shape[0]
    half = tm // 2
    halves = []
    for i in range(2):
        segs = list(range(i * n_seg // 2, (i + 1) * n_seg // 2)) if n_seg > 1 else [0]
        result = []
        halves.append((result, _stack_front(slice(i * half, (i + 1) * half), segs, x_ref, oa_ref, ob_ref, r_ref,
                                            mk_ref, mv_ref, g_gla_ref, wo_ref, g_post_mix_ref, g_pre_mem_ref,
                                            wq_ref, wmo_ref, g_post_mem_ref, result)))
    for _ in itertools.zip_longest(*[gen for _, gen in halves]):
        pass
    x2 = jnp.concatenate([result[0] for result, _ in halves], axis=0)
    _stack_ffn(x2, g_pre_ffn_ref, wg_ref, wu_ref, wd_ref, g_post_ffn_ref, y_ref)


def _stack(x, oa, ob, r, mk, mv, weights, tm, n_seg, tiles_per_mem_block):
    ntok = x.shape[0]
    row = lambda w: pl.BlockSpec((tm, w), lambda i: (i, 0))
    mem = pl.BlockSpec((n_seg * N_MEM, D_MODEL), lambda i: (i // tiles_per_mem_block, 0))
    w_specs = [_const_spec(w.shape) for w in weights]
    return pl.pallas_call(
        functools.partial(_stack_kernel, n_seg),
        grid=(ntok // tm,),
        in_specs=[row(D_MODEL), row(WIDTH_A), row(WIDTH_BV), row(WIDTH_BV), mem, mem] + w_specs,
        out_specs=row(D_MODEL),
        out_shape=jax.ShapeDtypeStruct((ntok, D_MODEL), F32),
        compiler_params=_params(1, VMEM_LIMIT),
        name="token_stack",
    )(x, oa, ob, r, mk, mv, *weights)


def _bias_base(table, width):
    u = np.arange(width)
    idx = np.where(u < (BAND_CHUNKS + 1) * CHUNK, np.clip(A_WINDOW - u, -REL_CLIP, REL_CLIP) + REL_CLIP, 2 * REL_CLIP)
    return table[:, idx]


def kernel(x_prompt, x_sample, mem_prompt, cache_a_k, cache_a_v, state_gla, cache_mem_k, cache_mem_v,
           g_pre_mix, w_in, rel_bias, w_alpha2, b_alpha, g_gla_out, w_o, g_post_mix,
           g_pre_mem, g_mem, w_mq, w_mk, w_mv, w_mo, g_post_mem,
           g_pre_ffn, w_ffn_gate, w_ffn_up, w_ffn_down, g_post_ffn):
    depth = w_in.shape[0]
    assert depth == 1
    l = 0
    n_p, seq, _ = x_prompt.shape
    n_s, seq_s, _ = x_sample.shape
    vec = lambda g: g[l].reshape(1, -1)

    wi = w_in[l]
    w_t = jnp.swapaxes(wi, 0, 1).astype(BF16)
    wa2 = jnp.pad(w_alpha2[l], ((0, LANES - GATE_RANK), (0, 0))).astype(BF16)
    stack_w = [vec(g_gla_out), w_o[l].astype(BF16), vec(g_post_mix), vec(g_pre_mem), w_mq[l].astype(BF16),
               w_mo[l].astype(BF16), vec(g_post_mem), vec(g_pre_ffn), w_ffn_gate[l].astype(BF16),
               w_ffn_up[l].astype(BF16), w_ffn_down[l].astype(BF16), vec(g_post_ffn)]

    xp = x_prompt.reshape(n_p * seq, D_MODEL)
    keep = min(A_WINDOW, seq)
    qa, ka, va, qb, kb, vb, r, la, k_tail, v_tail = _proj(xp, vec(g_pre_mix), w_t, wa2, vec(b_alpha), TILES.proj_rows,
                                                          seq // TILES.proj_rows, keep, True)
    oa = _attn_prompt(qa, ka, va, _bias_base(rel_bias[l], KEYS), n_p, seq)
    ob, sp = _gla(qb, kb, vb, la, None, n_p, seq, CHUNK, TQ // CHUNK, n_p)
    mk, mv = _mem_kv(mem_prompt.reshape(n_p * N_MEM, D_MODEL), vec(g_mem), w_mk[l].astype(BF16), w_mv[l].astype(BF16))
    yp = _stack(xp, oa, ob, r, mk.astype(BF16), mv.astype(BF16), stack_w, TILES.stack_rows, 1,
                seq // TILES.stack_rows)

    xs = x_sample.reshape(n_s * seq_s, D_MODEL)
    ntok_s = n_s * seq_s
    qa_s, ka_s, va_s, qb_s, kb_s, vb_s, r_s, la_s, k_new, v_new = _proj(
        xs, vec(g_pre_mix), w_t, wa2, vec(b_alpha), ntok_s, 1, ntok_s, False)
    oa_s = _attn_sample(qa_s, ka_s, va_s, jnp.transpose(cache_a_k[l], (0, 2, 3, 1)),
                        jnp.transpose(cache_a_v[l], (0, 2, 3, 1)), rel_bias[l], seq_s, TILES.sample_attn_requests)
    ob_s, ss = _gla(qb_s, kb_s, vb_s, la_s, _state_to_pairs(state_gla[l]), n_s, seq_s, seq_s, 1,
                    TILES.sample_gla_requests)
    seg_s = TILES.sample_stack_requests
    ys = _stack(xs, oa_s, ob_s, r_s,
                cache_mem_k[l].reshape(n_s * N_MEM, D_MODEL),
                cache_mem_v[l].reshape(n_s * N_MEM, D_MODEL), stack_w, seg_s * seq_s, seg_s, 1)

    heads = lambda z, n, t: z.reshape(1, n, t, N_HEADS_A, HEAD_DIM_A)
    tails = lambda z: jnp.transpose(z.reshape(n_p, N_HEADS_A, HEAD_DIM_A, keep), (0, 3, 1, 2))[None]
    return (yp.reshape(n_p, seq, D_MODEL), ys.reshape(n_s, seq_s, D_MODEL),
            tails(k_tail), tails(v_tail), _pairs_to_state(sp)[None],
            mk.reshape(1, n_p, N_MEM, N_HEADS_MEM, HEAD_DIM_MEM), mv.reshape(1, n_p, N_MEM, N_HEADS_MEM, HEAD_DIM_MEM),
            heads(k_new, n_s, seq_s), heads(v_new, n_s, seq_s), _pairs_to_state(ss)[None])
```

```python
import functools
import itertools
import math
from typing import NamedTuple

import jax
import jax.numpy as jnp
import numpy as np
from jax import lax
from jax.experimental import pallas as pl
from jax.experimental.pallas import tpu as pltpu

F32 = jnp.float32
BF16 = jnp.bfloat16

D_MODEL = 1024
CHUNK = 64
BAND_CHUNKS = 8
A_WINDOW = BAND_CHUNKS * CHUNK
N_HEADS_A = 8
HEAD_DIM_A = 64
WIDTH_A = 512
REL_CLIP = 256
N_HEADS_B = 4
KEY_DIM_B = 64
VAL_DIM_B = 128
WIDTH_BK = 256
WIDTH_BV = 512
GATE_RANK = 16
GATE_TAU = 16.0
N_MEM = 256
N_HEADS_MEM = 4
HEAD_DIM_MEM = 256
D_FF = 2816
EPS = 1e-6
NEG_BIG = -1e30
NT_DIMS = (((1,), (1,)), ((), ()))

LANES = 128
BIAS_W = 640
FF_BLOCK = 256
VMEM_LIMIT = 56 * 1024 * 1024


class _TilePlan(NamedTuple):
    proj_rows: int = 1024
    stack_rows: int = 512
    sample_attn_requests: int = 4
    sample_gla_requests: int = 8
    sample_stack_requests: int = 4


TILES = _TilePlan()

OFF_QA, OFF_KA, OFF_VA, OFF_QB, OFF_KB, OFF_VB, OFF_R = 0, 512, 1024, 1536, 1792, 2048, 2560
IN_WIDTH = OFF_R + GATE_RANK + WIDTH_BV


def _rms(x, g):
    return x * lax.rsqrt(jnp.mean(x * x, axis=-1, keepdims=True) + EPS) * g


def _silu(x):
    return x / (1.0 + jnp.exp(-x))


def _log_sigmoid(z):
    return jnp.minimum(z, 0.0) - jnp.log(1.0 + jnp.exp(-jnp.abs(z)))


def _const_spec(shape):
    nd = len(shape)
    return pl.BlockSpec(shape, lambda *_: (0,) * nd, pipeline_mode=pl.Buffered(1))


def _params(n_axes, vmem=None):
    return pltpu.CompilerParams(dimension_semantics=("arbitrary",) * n_axes, vmem_limit_bytes=vmem)


def _proj_kernel(tiles_per_seq, tail_rows, tail_t, x_ref, g_ref, w_ref, wa2_ref, ba_ref,
                 qa_ref, ka_ref, va_ref, qb_ref, kb_ref, vb_ref, r_ref, la_ref, kt_ref, vt_ref):
    h = _rms(x_ref[...], g_ref[...]).astype(BF16)

    def proj(lo, hi):
        return lax.dot_general(h, w_ref[lo:hi, :], NT_DIMS, preferred_element_type=F32)

    g_low = proj(OFF_R, OFF_R + LANES).astype(BF16)
    z = jnp.dot(g_low, wa2_ref[...], preferred_element_type=F32) + ba_ref[...]
    la_ref[...] = _log_sigmoid(z) * (1.0 / GATE_TAU)
    qa_ref[...] = (proj(OFF_QA, OFF_KA) * (HEAD_DIM_A ** -0.5)).astype(BF16)
    ka = proj(OFF_KA, OFF_VA)
    va = proj(OFF_VA, OFF_QB)
    ka_ref[...] = ka.astype(BF16)
    va_ref[...] = va.astype(BF16)
    qb_ref[...] = proj(OFF_QB, OFF_KB) * (KEY_DIM_B ** -0.5)
    kb_ref[...] = proj(OFF_KB, OFF_VB)
    vb_ref[...] = proj(OFF_VB, OFF_R).astype(BF16)
    r_ref[...] = proj(OFF_R + GATE_RANK, IN_WIDTH)

    @pl.when(pl.program_id(0) % tiles_per_seq == tiles_per_seq - 1)
    def _():
        k_keep, v_keep = ka[ka.shape[0] - tail_rows:], va[va.shape[0] - tail_rows:]
        kt_ref[...] = k_keep.T if tail_t else k_keep
        vt_ref[...] = v_keep.T if tail_t else v_keep


def _proj(x, g, w_t, wa2, ba, tm, tiles_per_seq, tail_rows, tail_t):
    ntok = x.shape[0]
    nt = ntok // tm
    nseq = nt // tiles_per_seq
    row = lambda w: pl.BlockSpec((tm, w), lambda i: (i, 0))
    assert tail_rows <= tm
    tail_shape = (WIDTH_A, tail_rows) if tail_t else (tail_rows, WIDTH_A)
    tail = pl.BlockSpec(tail_shape, lambda i: (i // tiles_per_seq, 0))
    out_shape = (
        jax.ShapeDtypeStruct((ntok, WIDTH_A), BF16),
        jax.ShapeDtypeStruct((ntok, WIDTH_A), BF16),
        jax.ShapeDtypeStruct((ntok, WIDTH_A), BF16),
        jax.ShapeDtypeStruct((ntok, WIDTH_BK), F32),
        jax.ShapeDtypeStruct((ntok, WIDTH_BK), F32),
        jax.ShapeDtypeStruct((ntok, WIDTH_BV), BF16),
        jax.ShapeDtypeStruct((ntok, WIDTH_BV), F32),
        jax.ShapeDtypeStruct((ntok, WIDTH_BK), F32),
        jax.ShapeDtypeStruct((nseq * tail_shape[0], tail_shape[1]), F32),
        jax.ShapeDtypeStruct((nseq * tail_shape[0], tail_shape[1]), F32),
    )
    return pl.pallas_call(
        functools.partial(_proj_kernel, tiles_per_seq, tail_rows, tail_t),
        grid=(nt,),
        in_specs=[row(D_MODEL), _const_spec((1, D_MODEL)), _const_spec((IN_WIDTH, D_MODEL)),
                  _const_spec((LANES, WIDTH_BK)), _const_spec((1, WIDTH_BK))],
        out_specs=(row(WIDTH_A), row(WIDTH_A), row(WIDTH_A), row(WIDTH_BK), row(WIDTH_BK),
                   row(WIDTH_BV), row(WIDTH_BV), row(WIDTH_BK), tail, tail),
        out_shape=out_shape,
        compiler_params=_params(1, VMEM_LIMIT),
        name="proj",
    )(x, g, w_t, wa2, ba)


def _stack_heads(x_pair):
    low_half = lax.broadcasted_iota(jnp.int32, x_pair.shape, 1) < HEAD_DIM_A
    zero = jnp.zeros_like(x_pair)
    return jnp.concatenate([jnp.where(low_half, x_pair, zero), jnp.where(low_half, zero, x_pair)], axis=0)


TQ = 4 * CHUNK
KEYS = 3 * TQ
SOFT_W = 5 * LANES
assert 2 * TQ == A_WINDOW and SOFT_W >= (BAND_CHUNKS + 1) * CHUNK + CHUNK


def _build_band_bias(base_ref, bm_ref):
    n_band = (BAND_CHUNKS + 1) * CHUNK
    col = lax.broadcasted_iota(jnp.int32, (CHUNK, KEYS), 1)
    for h in range(N_HEADS_A):
        rows = jnp.broadcast_to(base_ref[h:h + 1, :], (CHUNK, KEYS))
        for qc in range(TQ // CHUNK):
            toeplitz = pltpu.roll(rows, qc * CHUNK, 1, stride=1, stride_axis=0)
            in_band = (col >= qc * CHUNK) & (col < qc * CHUNK + n_band)
            r0 = (qc * 2 + h % 2) * CHUNK
            bm_ref[h // 2, r0:r0 + CHUNK, :] = jnp.where(in_band, toeplitz, NEG_BIG)


def _band_tile(q_ref, k_refs, v_refs, bm_ref, o_ref, start_penalty):
    n_qc, n_pairs = TQ // CHUNK, N_HEADS_A // 2
    low_half = lax.broadcasted_iota(jnp.int32, (CHUNK, LANES), 1) < HEAD_DIM_A
    pair_cols = lambda hp: slice(hp * LANES, (hp + 1) * LANES)

    def scores(hp):
        cols = pair_cols(hp)
        qs = jnp.concatenate([_stack_heads(q_ref[qc * CHUNK:(qc + 1) * CHUNK, cols]) for qc in range(n_qc)],
                             axis=0)
        return jnp.concatenate([lax.dot_general(qs, k[:, cols], NT_DIMS, preferred_element_type=F32)
                                for k in k_refs], axis=1)

    def softmax(hp, s):
        probs, inv_l = [], []
        for qc in range(n_qc):
            rows = slice(qc * 2 * CHUNK, (qc + 1) * 2 * CHUNK)
            c0 = 0 if (qc + 1) * CHUNK + A_WINDOW <= SOFT_W else KEYS - SOFT_W
            sq = s[rows, c0:c0 + SOFT_W] + bm_ref[hp, rows, c0:c0 + SOFT_W]
            if start_penalty is not None:
                sq = sq + start_penalty[:, c0:c0 + SOFT_W]
            p = jnp.exp(sq - jnp.max(sq, axis=-1, keepdims=True))
            inv_l.append(1.0 / jnp.sum(p, axis=-1, keepdims=True))
            pad = jnp.zeros((2 * CHUNK, KEYS - SOFT_W), BF16)
            probs.append(jnp.concatenate([p.astype(BF16), pad] if c0 == 0 else [pad, p.astype(BF16)], axis=1))
        return jnp.concatenate(probs, axis=0), inv_l

    def values(hp, pm, inv_l):
        cols = pair_cols(hp)
        out = sum(jnp.dot(pm[:, i * TQ:(i + 1) * TQ], v[:, cols], preferred_element_type=F32)
                  for i, v in enumerate(v_refs))
        for qc in range(n_qc):
            r0 = qc * 2 * CHUNK
            o0 = out[r0:r0 + CHUNK] * inv_l[qc][0:CHUNK]
            o1 = out[r0 + CHUNK:r0 + 2 * CHUNK] * inv_l[qc][CHUNK:2 * CHUNK]
            o_ref[qc * CHUNK:(qc + 1) * CHUNK, cols] = jnp.where(low_half, o0, o1).astype(o_ref.dtype)

    for hp in range(n_pairs):
        values(hp, *softmax(hp, scores(hp)))


def _attn_prompt_kernel(q_ref, k2_ref, k1_ref, k0_ref, v2_ref, v1_ref, v0_ref, base_ref, o_ref, bm):
    t = pl.program_id(1)

    @pl.when((pl.program_id(0) == 0) & (t == 0))
    def _():
        _build_band_bias(base_ref, bm)

    k_refs = (k2_ref, k1_ref, k0_ref)
    v_refs = (v2_ref, v1_ref, v0_ref)

    @pl.when(t >= 2)
    def _():
        _band_tile(q_ref, k_refs, v_refs, bm, o_ref, None)

    @pl.when(t < 2)
    def _():
        col = lax.broadcasted_iota(jnp.int32, (1, KEYS), 1)
        penalty = jnp.where(col < (2 - t) * TQ, NEG_BIG, 0.0)
        _band_tile(q_ref, k_refs, v_refs, bm, o_ref, penalty)


def _attn_prompt(q, k, v, base, n_batch, seq):
    nt = seq // TQ
    blk = lambda back: pl.BlockSpec((TQ, WIDTH_A), lambda b, t: (b * nt + jnp.maximum(t - back, 0), 0))
    return pl.pallas_call(
        _attn_prompt_kernel,
        grid=(n_batch, nt),
        in_specs=[blk(0), blk(2), blk(1), blk(0), blk(2), blk(1), blk(0), _const_spec((N_HEADS_A, KEYS))],
        out_specs=blk(0),
        out_shape=jax.ShapeDtypeStruct((n_batch * seq, WIDTH_A), BF16),
        scratch_shapes=[pltpu.VMEM((N_HEADS_A // 2, 2 * TQ, KEYS), F32)],
        compiler_params=_params(2, VMEM_LIMIT),
        name="band_attn_prompt",
    )(q, k, k, k, v, v, v, base)


def _attn_sample_one(rows, q_ref, kn_ref, vn_ref, kc_ref, vc_ref, sel_ref, dup_ref, o_ref, bias_c, bias_n):
    n_h = N_HEADS_A
    n_new = rows.stop - rows.start
    low_half = lax.broadcasted_iota(jnp.int32, (n_new, LANES), 1) < HEAD_DIM_A
    pairs = [slice(hp * LANES, (hp + 1) * LANES) for hp in range(n_h // 2)]
    q_heads = [jnp.dot(q_ref[rows, cols], sel_ref[e], preferred_element_type=F32).astype(BF16)
               for cols in pairs for e in range(2)]
    yield
    s_c = jnp.concatenate([jnp.dot(q_heads[h], kc_ref[h].astype(BF16), preferred_element_type=F32)
                           for h in range(n_h)], axis=0) + bias_c[...]
    s_n = jnp.concatenate([lax.dot_general(_stack_heads(q_ref[rows, cols]), kn_ref[rows, cols], NT_DIMS,
                                           preferred_element_type=F32) for cols in pairs], axis=0) + bias_n[...]
    yield
    m = jnp.maximum(jnp.max(s_c, axis=-1, keepdims=True), jnp.max(s_n, axis=-1, keepdims=True))
    p_c, p_n = jnp.exp(s_c - m), jnp.exp(s_n - m)
    inv_l = 1.0 / (jnp.sum(p_c, axis=-1, keepdims=True) + jnp.sum(p_n, axis=-1, keepdims=True))
    p_c, p_n = p_c.astype(BF16), p_n.astype(BF16)
    yield
    out_c = jnp.concatenate([lax.dot_general(p_c[h * n_new:(h + 1) * n_new], vc_ref[h].astype(BF16), NT_DIMS,
                                             preferred_element_type=F32) for h in range(n_h)], axis=0) * inv_l
    yield
    hi = out_c.astype(BF16)
    lo = (out_c - hi.astype(F32)).astype(BF16)
    out_c = (jnp.dot(hi, dup_ref[...], preferred_element_type=F32)
             + jnp.dot(lo, dup_ref[...], preferred_element_type=F32))
    for hp, cols in enumerate(pairs):
        pr = slice(hp * 2 * n_new, (hp + 1) * 2 * n_new)
        out = out_c[pr] + jnp.dot(p_n[pr], vn_ref[rows, cols], preferred_element_type=F32) * inv_l[pr]
        o_ref[rows, cols] = jnp.where(low_half, out[0:n_new], out[n_new:]).astype(o_ref.dtype)


def _attn_sample_kernel(n_new, q_ref, kn_ref, vn_ref, kc_ref, vc_ref, base_ref, sel_ref, dup_ref,
                        o_ref, bias_c, bias_n):
    n_cache = kc_ref.shape[3]

    @pl.when(pl.program_id(0) == 0)
    def _():
        for h in range(N_HEADS_A):
            rows = jnp.broadcast_to(base_ref[h:h + 1, :], (n_new, BIAS_W))
            toeplitz = pltpu.roll(rows, 0, 1, stride=1, stride_axis=0)
            bias_c[h * n_new:(h + 1) * n_new, :] = toeplitz[:, 0:n_cache]
            bias_n[h * n_new:(h + 1) * n_new, :] = toeplitz[:, n_cache:n_cache + n_new]

    requests = [_attn_sample_one(slice(i * n_new, (i + 1) * n_new), q_ref, kn_ref, vn_ref, kc_ref.at[i],
                                 vc_ref.at[i], sel_ref, dup_ref, o_ref, bias_c, bias_n)
                for i in range(kc_ref.shape[0])]
    for _ in itertools.zip_longest(*requests):
        pass


def _attn_sample(q, k_new, v_new, k_cache_t, v_cache_t, table, n_new, n_par):
    n_batch, n_h, _, n_cache = k_cache_t.shape
    assert n_cache == A_WINDOW and n_new <= CHUNK and n_h == N_HEADS_A
    lane = np.arange(LANES)
    sel = np.stack([lane[:, None] == np.arange(HEAD_DIM_A)[None, :] + e * HEAD_DIM_A for e in range(2)])
    dup = np.arange(HEAD_DIM_A)[:, None] == lane[None, :] % HEAD_DIM_A
    new = pl.BlockSpec((n_par * n_new, WIDTH_A), lambda g: (g, 0))
    old = pl.BlockSpec((n_par, n_h, HEAD_DIM_A, n_cache), lambda g: (g, 0, 0, 0))
    return pl.pallas_call(
        functools.partial(_attn_sample_kernel, n_new),
        grid=(n_batch // n_par,),
        in_specs=[new, new, new, old, old, _const_spec((n_h, BIAS_W)),
                  _const_spec((2, LANES, HEAD_DIM_A)), _const_spec((HEAD_DIM_A, LANES))],
        out_specs=new,
        out_shape=jax.ShapeDtypeStruct((n_batch * n_new, WIDTH_A), BF16),
        scratch_shapes=[pltpu.VMEM((n_h * n_new, n_cache), F32), pltpu.VMEM((n_h * n_new, n_new), F32)],
        compiler_params=_params(1, VMEM_LIMIT),
        name="band_attn_sample",
    )(q, k_new, v_new, k_cache_t, v_cache_t, _bias_base(table, BIAS_W), jnp.asarray(sel, BF16),
      jnp.asarray(dup, BF16))


def _gla_levels(chunk):
    return [chunk >> i for i in range(int(math.log2(chunk)) + 1)]


def _segment_matrix(chunk):
    i = np.arange(chunk)[:, None]
    t = np.arange(chunk)[None, :]
    blocks = []
    for li, s in enumerate(_gla_levels(chunk)):
        start = (i // s) * s
        f_rows = (t >= start) & (t <= i)
        r_rows = (t > i) & (t <= start + s - 1)
        if li == 0:
            blocks += [f_rows, r_rows]
        else:
            blocks.append(np.where((i // s) % 2 == 1, f_rows, r_rows))
    seg = np.concatenate(blocks, axis=0).astype(np.float32)
    return np.concatenate([seg, seg], axis=1)


def _level_masks(group_tokens):
    tg = group_tokens
    i = (np.arange(2 * tg) % tg)[:, None]
    j = np.arange(tg)[None, :]
    masks = [i == j]
    s = tg // 2
    while s >= 1:
        masks.append(((i // s) % 2 == 1) & (j // s == i // s - 1))
        s //= 2
    return np.stack(masks).astype(np.float32)


def _gla_tile(q, k, v, la, seg_ref, mask_ref, state_ref, o_ref, chunk, n_chunks):
    n, tg = n_chunks, chunk * n_chunks
    levels = _gla_levels(chunk)
    la_hi = la.astype(BF16)
    la_lo = (la - la_hi.astype(F32)).astype(BF16)
    split = jnp.concatenate(
        [jnp.concatenate([la_hi[c * chunk:(c + 1) * chunk], la_lo[c * chunk:(c + 1) * chunk]], axis=0)
         for c in range(n)], axis=1)
    sums = jnp.dot(seg_ref[...], split, preferred_element_type=F32)

    def seg_sum(block):
        rows = slice(block * chunk, (block + 1) * chunk)
        return jnp.minimum(jnp.concatenate(
            [sums[rows, c * WIDTH_BK:(c + 1) * WIDTH_BK] for c in range(n)], axis=0), 0.0)

    fwd0, rev0 = seg_sum(0), seg_sum(1)
    from_start, to_end = jnp.exp(fwd0), jnp.exp(rev0)
    total = [fwd0[(c + 1) * chunk - 1:(c + 1) * chunk] for c in range(n)]

    def chunk_sum(cs):
        cs = list(cs)
        return (sum(total[c] for c in cs[1:]) + total[cs[0]]) if cs else None

    def extend(base, offsets):
        pieces = []
        for c in range(n):
            x = base[c * chunk:(c + 1) * chunk]
            pieces.append(x if offsets[c] is None else x * jnp.exp(offsets[c]))
        return jnp.concatenate(pieces, axis=0) if n > 1 else pieces[0]

    q_state = (q * extend(from_start, [chunk_sum(range(0, c)) for c in range(n)])).astype(BF16)
    k_state = (k * extend(to_end, [chunk_sum(range(c + 1, n)) for c in range(n)])).astype(BF16)
    decay_tile = jnp.exp(chunk_sum(range(n)))
    ops = {}
    s = tg // 2
    while s >= chunk and n > 1:
        per = s // chunk
        offsets, bases = [], []
        for c in range(n):
            sg = c // per
            offsets.append(chunk_sum(range(sg * per, c)) if sg % 2 else chunk_sum(range(c + 1, (sg + 1) * per)))
            bases.append((from_start if sg % 2 else to_end)[c * chunk:(c + 1) * chunk])
        w = extend(jnp.concatenate(bases, axis=0), offsets)
        ops[s] = ((q * w).astype(BF16), (k * w).astype(BF16))
        s //= 2
    for li in range(1, len(levels)):
        w = jnp.exp(seg_sum(li + 1))
        ops[levels[li]] = ((q * w).astype(BF16), (k * w).astype(BF16))
    q16, k16 = q.astype(BF16), k.astype(BF16)
    yield

    gt = mask_ref.shape[2]
    nt_dims = (((1,), (1,)), ((), ()))
    tn_dims = (((0,), (0,)), ((), ()))
    n_pairs, n_groups = N_HEADS_B // 2, tg // gt
    pair_cols = [slice(p * LANES, (p + 1) * LANES) for p in range(n_pairs)]
    pair_v = [v[:, p * 2 * VAL_DIM_B:(p + 1) * 2 * VAL_DIM_B] for p in range(n_pairs)]
    head_v = lambda e: slice(e * VAL_DIM_B, (e + 1) * VAL_DIM_B)
    o_state = []
    for p, cols in enumerate(pair_cols):
        s0, s1 = state_ref[2 * p], state_ref[2 * p + 1]
        zero = jnp.zeros_like(s0)
        st = jnp.concatenate([jnp.concatenate([s0, zero], axis=1),
                              jnp.concatenate([zero, s1], axis=1)], axis=0)
        o_state.append(jnp.dot(q_state[:, cols], st.astype(BF16), preferred_element_type=F32))
        upd = lax.dot_general(k_state[:, cols], pair_v[p], tn_dims, preferred_element_type=F32)
        decay_rows = jnp.broadcast_to(decay_tile[:, cols], (LANES, LANES)).T
        state_ref[2 * p] = s0 * decay_rows[0:KEY_DIM_B] + upd[0:KEY_DIM_B, 0:VAL_DIM_B]
        state_ref[2 * p + 1] = s1 * decay_rows[KEY_DIM_B:] + upd[KEY_DIM_B:, VAL_DIM_B:]
    yield
    blocks = []
    for p, cols in enumerate(pair_cols):
        for g in range(n_groups):
            rows = slice(g * gt, (g + 1) * gt)
            a = mask_ref[0] * lax.dot_general(_stack_heads(q16[rows, cols]), k16[rows, cols], nt_dims,
                                              preferred_element_type=F32)
            m, s = 1, gt // 2
            while s >= 1:
                qd, kd = ops[s]
                a = a + mask_ref[m] * lax.dot_general(_stack_heads(qd[rows, cols]), kd[rows, cols], nt_dims,
                                                      preferred_element_type=F32)
                m, s = m + 1, s // 2
            blocks.append((p, g * gt, g * gt, a.astype(BF16)))
            yield
        s = tg // 2
        while s >= gt:
            qd, kd = ops[s]
            for blk in range(tg // (2 * s)):
                k0, q0 = blk * 2 * s, blk * 2 * s + s
                ab = lax.dot_general(_stack_heads(qd[q0:q0 + s, cols]), kd[k0:k0 + s, cols], nt_dims,
                                     preferred_element_type=F32)
                blocks.append((p, q0, k0, ab.astype(BF16)))
                yield
            s //= 2
    piece = [[[o_state[p][g * gt:(g + 1) * gt, head_v(e)] for g in range(n_groups)] for e in range(2)]
             for p in range(n_pairs)]
    for p, q0, k0, a in blocks:
        nq, nk = a.shape[0] // 2, a.shape[1]
        for e in range(2):
            contrib = jnp.dot(a[e * nq:(e + 1) * nq], pair_v[p][k0:k0 + nk, head_v(e)], preferred_element_type=F32)
            for gi in range(nq // gt):
                g = q0 // gt + gi
                piece[p][e][g] = piece[p][e][g] + contrib[gi * gt:(gi + 1) * gt]
        yield
    outs = [jnp.concatenate(piece[p][e], axis=0) if n_groups > 1 else piece[p][e][0]
            for p in range(n_pairs) for e in range(2)]
    o_ref[...] = jnp.concatenate(outs, axis=1)


def _gla_kernel(chunk, chunks_per_tile, has_init, *refs):
    if has_init:
        q_ref, k_ref, v_ref, la_ref, seg_ref, mask_ref, s0_ref, o_ref, sout_ref, state = refs
    else:
        q_ref, k_ref, v_ref, la_ref, seg_ref, mask_ref, o_ref, sout_ref, state = refs
    t = pl.program_id(1)

    @pl.when(t == 0)
    def _():
        state[...] = s0_ref[...] if has_init else jnp.zeros_like(state)

    tiles = [_gla_tile(q_ref[i], k_ref[i], v_ref[i], la_ref[i], seg_ref, mask_ref, state.at[i], o_ref.at[i],
                       chunk, chunks_per_tile) for i in range(q_ref.shape[0])]
    for _ in itertools.zip_longest(*tiles):
        pass

    @pl.when(t == pl.num_programs(1) - 1)
    def _():
        sout_ref[...] = state[...]


def _gla(q, k, v, la, s0, n_batch, seq, chunk, chunks_per_tile, n_par):
    tg = chunk * chunks_per_tile
    nt = seq // tg
    seg = jnp.asarray(_segment_matrix(chunk), BF16)
    masks = jnp.asarray(_level_masks(min(tg, LANES)), F32)
    row = lambda w: pl.BlockSpec((n_par, tg, w), lambda g, t: (g, t, 0))
    st = pl.BlockSpec((n_par, N_HEADS_B, KEY_DIM_B, VAL_DIM_B), lambda g, t: (g, 0, 0, 0))
    per_seq = lambda z: z.reshape(n_batch, seq, z.shape[-1])
    in_specs = [row(WIDTH_BK), row(WIDTH_BK), row(WIDTH_BV), row(WIDTH_BK), _const_spec(seg.shape),
                _const_spec(masks.shape)]
    args = [per_seq(q), per_seq(k), per_seq(v), per_seq(la), seg, masks]
    if s0 is not None:
        in_specs.append(st)
        args.append(s0)
    o, s_out = pl.pallas_call(
        functools.partial(_gla_kernel, chunk, chunks_per_tile, s0 is not None),
        grid=(n_batch // n_par, nt),
        in_specs=in_specs,
        out_specs=(row(WIDTH_BV), st),
        out_shape=(jax.ShapeDtypeStruct((n_batch, seq, WIDTH_BV), F32),
                   jax.ShapeDtypeStruct((n_batch, N_HEADS_B, KEY_DIM_B, VAL_DIM_B), F32)),
        scratch_shapes=[pltpu.VMEM((n_par, N_HEADS_B, KEY_DIM_B, VAL_DIM_B), F32)],
        compiler_params=_params(2, VMEM_LIMIT),
        name="gla_chunk%d" % chunk,
    )(*args)
    return o.reshape(n_batch * seq, WIDTH_BV), s_out


def _mem_kv_kernel(m_ref, g_ref, wk_ref, wv_ref, k_ref, v_ref, k16_ref, v16_ref):
    m = _rms(m_ref[...], g_ref[...]).astype(BF16)
    k = jnp.dot(m, wk_ref[...], preferred_element_type=F32)
    v = jnp.dot(m, wv_ref[...], preferred_element_type=F32)
    k_ref[...], v_ref[...] = k, v
    k16_ref[...], v16_ref[...] = k.astype(BF16), v.astype(BF16)


def _mem_kv(mem, g, wk, wv):
    n = mem.shape[0]
    tm = N_MEM
    row = pl.BlockSpec((tm, D_MODEL), lambda i: (i, 0))
    return pl.pallas_call(
        _mem_kv_kernel,
        grid=(n // tm,),
        in_specs=[row, _const_spec((1, D_MODEL)), _const_spec((D_MODEL, D_MODEL)), _const_spec((D_MODEL, D_MODEL))],
        out_specs=(row, row, row, row),
        out_shape=(jax.ShapeDtypeStruct((n, D_MODEL), F32), jax.ShapeDtypeStruct((n, D_MODEL), F32),
                   jax.ShapeDtypeStruct((n, D_MODEL), BF16), jax.ShapeDtypeStruct((n, D_MODEL), BF16)),
        compiler_params=_params(1, VMEM_LIMIT),
        name="mem_kv",
    )(mem, g, wk, wv)


def _stack_front(rows, segs, x_ref, oa_ref, ob_ref, r_ref, mk_ref, mv_ref, g_gla_ref, wo_ref, g_post_mix_ref,
                 g_pre_mem_ref, wq_ref, wmo_ref, g_post_mem_ref, result):
    n_rows = rows.stop - rows.start
    ob = ob_ref[rows, :]
    normed = []
    for h in range(N_HEADS_B):
        seg = ob[:, h * VAL_DIM_B:(h + 1) * VAL_DIM_B]
        normed.append(seg * lax.rsqrt(jnp.mean(seg * seg, axis=-1, keepdims=True) + EPS))
    yb = (jnp.concatenate(normed, axis=1) * g_gla_ref[...] * _silu(r_ref[rows, :])).astype(BF16)
    yield
    mix = (jnp.dot(oa_ref[rows, :], wo_ref[0:WIDTH_A, :], preferred_element_type=F32)
           + jnp.dot(yb, wo_ref[WIDTH_A:, :], preferred_element_type=F32))
    yield
    x1 = x_ref[rows, :] + _rms(mix, g_post_mix_ref[...])
    hq = _rms(x1, g_pre_mem_ref[...]).astype(BF16)
    yield
    q = (jnp.dot(hq, wq_ref[...], preferred_element_type=F32) * (HEAD_DIM_MEM ** -0.5)).astype(BF16)
    yield
    rows_per_seg = n_rows // len(segs)
    units = [(u, sg, h) for u, sg in enumerate(segs) for h in range(N_HEADS_MEM)]
    mem_rows = lambda sg: slice(sg * N_MEM, (sg + 1) * N_MEM)
    head_cols = lambda h: slice(h * HEAD_DIM_MEM, (h + 1) * HEAD_DIM_MEM)
    scores = [lax.dot_general(q[u * rows_per_seg:(u + 1) * rows_per_seg, head_cols(h)],
                              mk_ref[mem_rows(sg), head_cols(h)].astype(BF16), NT_DIMS,
                              preferred_element_type=F32) for u, sg, h in units]
    yield
    probs, inv_l = [], []
    for s in scores:
        p = jnp.exp(s - jnp.max(s, axis=-1, keepdims=True))
        inv_l.append(1.0 / jnp.sum(p, axis=-1, keepdims=True))
        probs.append(p.astype(BF16))
    yield
    outs = [jnp.dot(probs[i], mv_ref[mem_rows(sg), head_cols(h)].astype(BF16), preferred_element_type=F32) * inv_l[i]
            for i, (u, sg, h) in enumerate(units)]
    seg_outs = [jnp.concatenate(outs[u * N_HEADS_MEM:(u + 1) * N_HEADS_MEM], axis=1) for u in range(len(segs))]
    o = (jnp.concatenate(seg_outs, axis=0) if len(segs) > 1 else seg_outs[0]).astype(BF16)
    yield
    att = jnp.dot(o, wmo_ref[...], preferred_element_type=F32)
    yield
    result.append(x1 + _rms(att, g_post_mem_ref[...]))


def _stack_ffn(x2, g_pre_ffn_ref, wg_ref, wu_ref, wd_ref, g_post_ffn_ref, y_ref):
    hf = _rms(x2, g_pre_ffn_ref[...]).astype(BF16)
    f = jnp.zeros(x2.shape, F32)
    act, act_cols = None, None
    for c in range(D_FF // FF_BLOCK):
        cols = slice(c * FF_BLOCK, (c + 1) * FF_BLOCK)
        gate = jnp.dot(hf, wg_ref[:, cols], preferred_element_type=F32)
        up = jnp.dot(hf, wu_ref[:, cols], preferred_element_type=F32)
        if act is not None:
            f = f + jnp.dot(act, wd_ref[act_cols, :], preferred_element_type=F32)
        act, act_cols = (_silu(gate) * up).astype(BF16), cols
    f = f + jnp.dot(act, wd_ref[act_cols, :], preferred_element_type=F32)
    y_ref[...] = x2 + _rms(f, g_post_ffn_ref[...])


def _stack_kernel(n_seg, x_ref, oa_ref, ob_ref, r_ref, mk_ref, mv_ref,
                  g_gla_ref, wo_ref, g_post_mix_ref, g_pre_mem_ref, wq_ref, wmo_ref, g_post_mem_ref,
                  g_pre_ffn_ref, wg_ref, wu_ref, wd_ref, g_post_ffn_ref, y_ref):
    tm = x_ref.shape[0]
    half = tm // 2
    halves = []
    for i in range(2):
        segs = list(range(i * n_seg // 2, (i + 1) * n_seg // 2)) if n_seg > 1 else [0]
        result = []
        halves.append((result, _stack_front(slice(i * half, (i + 1) * half), segs, x_ref, oa_ref, ob_ref, r_ref,
                                            mk_ref, mv_ref, g_gla_ref, wo_ref, g_post_mix_ref, g_pre_mem_ref,
                                            wq_ref, wmo_ref, g_post_mem_ref, result)))
    for _ in itertools.zip_longest(*[gen for _, gen in halves]):
        pass
    x2 = jnp.concatenate([result[0] for result, _ in halves], axis=0)
    _stack_ffn(x2, g_pre_ffn_ref, wg_ref, wu_ref, wd_ref, g_post_ffn_ref, y_ref)


def _stack(x, oa, ob, r, mk, mv, weights, tm, n_seg, tiles_per_mem_block):
    ntok = x.shape[0]
    row = lambda w: pl.BlockSpec((tm, w), lambda i: (i, 0))
    mem = pl.BlockSpec((n_seg * N_MEM, D_MODEL), lambda i: (i // tiles_per_mem_block, 0))
    w_specs = [_const_spec(w.shape) for w in weights]
    return pl.pallas_call(
        functools.partial(_stack_kernel, n_seg),
        grid=(ntok // tm,),
        in_specs=[row(D_MODEL), row(WIDTH_A), row(WIDTH_BV), row(WIDTH_BV), mem, mem] + w_specs,
        out_specs=row(D_MODEL),
        out_shape=jax.ShapeDtypeStruct((ntok, D_MODEL), F32),
        compiler_params=_params(1, VMEM_LIMIT),
        name="token_stack",
    )(x, oa, ob, r, mk, mv, *weights)


def _bias_base(table, width):
    u = np.arange(width)
    idx = np.where(u < (BAND_CHUNKS + 1) * CHUNK, np.clip(A_WINDOW - u, -REL_CLIP, REL_CLIP) + REL_CLIP, 2 * REL_CLIP)
    return table[:, idx]


def kernel(x_prompt, x_sample, mem_prompt, cache_a_k, cache_a_v, state_gla, cache_mem_k, cache_mem_v,
           g_pre_mix, w_in, rel_bias, w_alpha2, b_alpha, g_gla_out, w_o, g_post_mix,
           g_pre_mem, g_mem, w_mq, w_mk, w_mv, w_mo, g_post_mem,
           g_pre_ffn, w_ffn_gate, w_ffn_up, w_ffn_down, g_post_ffn):
    depth = w_in.shape[0]
    assert depth == 1
    l = 0
    n_p, seq, _ = x_prompt.shape
    n_s, seq_s, _ = x_sample.shape
    vec = lambda g: g[l].reshape(1, -1)

    wi = w_in[l]
    w_t = jnp.swapaxes(wi, 0, 1).astype(BF16)
    wa2 = jnp.pad(w_alpha2[l], ((0, LANES - GATE_RANK), (0, 0))).astype(BF16)
    stack_w = [vec(g_gla_out), w_o[l].astype(BF16), vec(g_post_mix), vec(g_pre_mem), w_mq[l].astype(BF16),
               w_mo[l].astype(BF16), vec(g_post_mem), vec(g_pre_ffn), w_ffn_gate[l].astype(BF16),
               w_ffn_up[l].astype(BF16), w_ffn_down[l].astype(BF16), vec(g_post_ffn)]

    xp = x_prompt.reshape(n_p * seq, D_MODEL)
    keep = min(A_WINDOW, seq)
    qa, ka, va, qb, kb, vb, r, la, k_tail, v_tail = _proj(xp, vec(g_pre_mix), w_t, wa2, vec(b_alpha), TILES.proj_rows,
                                                          seq // TILES.proj_rows, keep, True)
    oa = _attn_prompt(qa, ka, va, _bias_base(rel_bias[l], KEYS), n_p, seq)
    ob, sp = _gla(qb, kb, vb, la, None, n_p, seq, CHUNK, TQ // CHUNK, n_p)
    mk, mv, mk16, mv16 = _mem_kv(mem_prompt.reshape(n_p * N_MEM, D_MODEL), vec(g_mem), w_mk[l].astype(BF16), w_mv[l].astype(BF16))
    yp = _stack(xp, oa, ob, r, mk16, mv16, stack_w, TILES.stack_rows, 1,
                seq // TILES.stack_rows)

    xs = x_sample.reshape(n_s * seq_s, D_MODEL)
    ntok_s = n_s * seq_s
    qa_s, ka_s, va_s, qb_s, kb_s, vb_s, r_s, la_s, k_new, v_new = _proj(
        xs, vec(g_pre_mix), w_t, wa2, vec(b_alpha), ntok_s, 1, ntok_s, False)
    oa_s = _attn_sample(qa_s, ka_s, va_s, jnp.transpose(cache_a_k[l], (0, 2, 3, 1)),
                        jnp.transpose(cache_a_v[l], (0, 2, 3, 1)), rel_bias[l], seq_s, TILES.sample_attn_requests)
    ob_s, ss = _gla(qb_s, kb_s, vb_s, la_s, state_gla[l], n_s, seq_s, seq_s, 1,
                    TILES.sample_gla_requests)
    seg_s = TILES.sample_stack_requests
    ys = _stack(xs, oa_s, ob_s, r_s,
                cache_mem_k[l].reshape(n_s * N_MEM, D_MODEL),
                cache_mem_v[l].reshape(n_s * N_MEM, D_MODEL), stack_w, seg_s * seq_s, seg_s, 1)

    heads = lambda z, n, t: z.reshape(1, n, t, N_HEADS_A, HEAD_DIM_A)
    tails = lambda z: jnp.transpose(z.reshape(n_p, N_HEADS_A, HEAD_DIM_A, keep), (0, 3, 1, 2))[None]
    return (yp.reshape(n_p, seq, D_MODEL), ys.reshape(n_s, seq_s, D_MODEL),
            tails(k_tail), tails(v_tail), sp[None],
            mk.reshape(1, n_p, N_MEM, N_HEADS_MEM, HEAD_DIM_MEM), mv.reshape(1, n_p, N_MEM, N_HEADS_MEM, HEAD_DIM_MEM),
            heads(k_new, n_s, seq_s), heads(v_new, n_s, seq_s), ss[None])
```

```python
import functools
import itertools
import math
from typing import NamedTuple

import jax
import jax.numpy as jnp
import numpy as np
from jax import lax
from jax.experimental import pallas as pl
from jax.experimental.pallas import tpu as pltpu

F32 = jnp.float32
BF16 = jnp.bfloat16

D_MODEL = 1024
CHUNK = 64
BAND_CHUNKS = 8
A_WINDOW = BAND_CHUNKS * CHUNK
N_HEADS_A = 8
HEAD_DIM_A = 64
WIDTH_A = 512
REL_CLIP = 256
N_HEADS_B = 4
KEY_DIM_B = 64
VAL_DIM_B = 128
WIDTH_BK = 256
WIDTH_BV = 512
GATE_RANK = 16
GATE_TAU = 16.0
N_MEM = 256
N_HEADS_MEM = 4
HEAD_DIM_MEM = 256
D_FF = 2816
EPS = 1e-6
NEG_BIG = -1e30
NT_DIMS = (((1,), (1,)), ((), ()))

LANES = 128
BIAS_W = 640
FF_BLOCK = 256
CAST_ROWS = 512
VMEM_LIMIT = 56 * 1024 * 1024


class _TilePlan(NamedTuple):
    proj_rows: int = 1024
    stack_rows: int = 512
    sample_attn_requests: int = 4
    sample_gla_requests: int = 8
    sample_stack_requests: int = 4


TILES = _TilePlan()

OFF_QA, OFF_KA, OFF_VA, OFF_QB, OFF_KB, OFF_VB, OFF_R = 0, 512, 1024, 1536, 1792, 2048, 2560
IN_WIDTH = OFF_R + GATE_RANK + WIDTH_BV


def _rms(x, g):
    return x * lax.rsqrt(jnp.mean(x * x, axis=-1, keepdims=True) + EPS) * g


def _silu(x):
    return x / (1.0 + jnp.exp(-x))


def _log_sigmoid(z):
    return jnp.minimum(z, 0.0) - jnp.log(1.0 + jnp.exp(-jnp.abs(z)))


def _const_spec(shape):
    nd = len(shape)
    return pl.BlockSpec(shape, lambda *_: (0,) * nd, pipeline_mode=pl.Buffered(1))


def _params(n_axes, vmem=None):
    return pltpu.CompilerParams(dimension_semantics=("arbitrary",) * n_axes, vmem_limit_bytes=vmem)


def _cast_rows(src_ref, dst_ref):
    n = src_ref.shape[0]
    for lo in range(0, n, CAST_ROWS):
        rows = slice(lo, min(lo + CAST_ROWS, n))
        dst_ref[rows, :] = src_ref[rows, :].astype(dst_ref.dtype)


def _proj_kernel(tiles_per_seq, tail_rows, tail_t, x_ref, g_ref, w32_ref, wa2_ref, ba_ref,
                 qa_ref, ka_ref, va_ref, qb_ref, kb_ref, vb_ref, r_ref, la_ref, kt_ref, vt_ref, w_ref):
    @pl.when(pl.program_id(0) == 0)
    def _():
        _cast_rows(w32_ref, w_ref)

    h = _rms(x_ref[...], g_ref[...]).astype(BF16)

    def proj(lo, hi):
        return lax.dot_general(h, w_ref[lo:hi, :], NT_DIMS, preferred_element_type=F32)

    g_low = proj(OFF_R, OFF_R + LANES).astype(BF16)
    z = jnp.dot(g_low, wa2_ref[...], preferred_element_type=F32) + ba_ref[...]
    la_ref[...] = _log_sigmoid(z) * (1.0 / GATE_TAU)
    qa_ref[...] = (proj(OFF_QA, OFF_KA) * (HEAD_DIM_A ** -0.5)).astype(BF16)
    ka = proj(OFF_KA, OFF_VA)
    va = proj(OFF_VA, OFF_QB)
    ka_ref[...] = ka.astype(BF16)
    va_ref[...] = va.astype(BF16)
    qb_ref[...] = proj(OFF_QB, OFF_KB) * (KEY_DIM_B ** -0.5)
    kb_ref[...] = proj(OFF_KB, OFF_VB)
    vb_ref[...] = proj(OFF_VB, OFF_R).astype(BF16)
    r_ref[...] = proj(OFF_R + GATE_RANK, IN_WIDTH)

    @pl.when(pl.program_id(0) % tiles_per_seq == tiles_per_seq - 1)
    def _():
        k_keep, v_keep = ka[ka.shape[0] - tail_rows:], va[va.shape[0] - tail_rows:]
        kt_ref[...] = k_keep.T if tail_t else k_keep
        vt_ref[...] = v_keep.T if tail_t else v_keep


def _proj(x, g, w_t, wa2, ba, tm, tiles_per_seq, tail_rows, tail_t):
    ntok = x.shape[0]
    nt = ntok // tm
    nseq = nt // tiles_per_seq
    row = lambda w: pl.BlockSpec((tm, w), lambda i: (i, 0))
    assert tail_rows <= tm
    tail_shape = (WIDTH_A, tail_rows) if tail_t else (tail_rows, WIDTH_A)
    tail = pl.BlockSpec(tail_shape, lambda i: (i // tiles_per_seq, 0))
    out_shape = (
        jax.ShapeDtypeStruct((ntok, WIDTH_A), BF16),
        jax.ShapeDtypeStruct((ntok, WIDTH_A), BF16),
        jax.ShapeDtypeStruct((ntok, WIDTH_A), BF16),
        jax.ShapeDtypeStruct((ntok, WIDTH_BK), F32),
        jax.ShapeDtypeStruct((ntok, WIDTH_BK), F32),
        jax.ShapeDtypeStruct((ntok, WIDTH_BV), BF16),
        jax.ShapeDtypeStruct((ntok, WIDTH_BV), F32),
        jax.ShapeDtypeStruct((ntok, WIDTH_BK), F32),
        jax.ShapeDtypeStruct((nseq * tail_shape[0], tail_shape[1]), F32),
        jax.ShapeDtypeStruct((nseq * tail_shape[0], tail_shape[1]), F32),
    )
    return pl.pallas_call(
        functools.partial(_proj_kernel, tiles_per_seq, tail_rows, tail_t),
        grid=(nt,),
        in_specs=[row(D_MODEL), _const_spec((1, D_MODEL)), _const_spec((IN_WIDTH, D_MODEL)),
                  _const_spec((LANES, WIDTH_BK)), _const_spec((1, WIDTH_BK))],
        out_specs=(row(WIDTH_A), row(WIDTH_A), row(WIDTH_A), row(WIDTH_BK), row(WIDTH_BK),
                   row(WIDTH_BV), row(WIDTH_BV), row(WIDTH_BK), tail, tail),
        out_shape=out_shape,
        scratch_shapes=[pltpu.VMEM((IN_WIDTH, D_MODEL), BF16)],
        compiler_params=_params(1, VMEM_LIMIT),
        name="proj",
    )(x, g, w_t, wa2, ba)


def _stack_heads(x_pair):
    low_half = lax.broadcasted_iota(jnp.int32, x_pair.shape, 1) < HEAD_DIM_A
    zero = jnp.zeros_like(x_pair)
    return jnp.concatenate([jnp.where(low_half, x_pair, zero), jnp.where(low_half, zero, x_pair)], axis=0)


TQ = 4 * CHUNK
KEYS = 3 * TQ
SOFT_W = 5 * LANES
assert 2 * TQ == A_WINDOW and SOFT_W >= (BAND_CHUNKS + 1) * CHUNK + CHUNK


def _build_band_bias(base_ref, bm_ref):
    n_band = (BAND_CHUNKS + 1) * CHUNK
    col = lax.broadcasted_iota(jnp.int32, (CHUNK, KEYS), 1)
    for h in range(N_HEADS_A):
        rows = jnp.broadcast_to(base_ref[h:h + 1, :], (CHUNK, KEYS))
        for qc in range(TQ // CHUNK):
            toeplitz = pltpu.roll(rows, qc * CHUNK, 1, stride=1, stride_axis=0)
            in_band = (col >= qc * CHUNK) & (col < qc * CHUNK + n_band)
            r0 = (qc * 2 + h % 2) * CHUNK
            bm_ref[h // 2, r0:r0 + CHUNK, :] = jnp.where(in_band, toeplitz, NEG_BIG)


def _band_tile(q_ref, k_refs, v_refs, bm_ref, o_ref, start_penalty):
    n_qc, n_pairs = TQ // CHUNK, N_HEADS_A // 2
    low_half = lax.broadcasted_iota(jnp.int32, (CHUNK, LANES), 1) < HEAD_DIM_A
    pair_cols = lambda hp: slice(hp * LANES, (hp + 1) * LANES)

    def scores(hp):
        cols = pair_cols(hp)
        qs = jnp.concatenate([_stack_heads(q_ref[qc * CHUNK:(qc + 1) * CHUNK, cols]) for qc in range(n_qc)],
                             axis=0)
        return jnp.concatenate([lax.dot_general(qs, k[:, cols], NT_DIMS, preferred_element_type=F32)
                                for k in k_refs], axis=1)

    def softmax(hp, s):
        probs, inv_l = [], []
        for qc in range(n_qc):
            rows = slice(qc * 2 * CHUNK, (qc + 1) * 2 * CHUNK)
            c0 = 0 if (qc + 1) * CHUNK + A_WINDOW <= SOFT_W else KEYS - SOFT_W
            sq = s[rows, c0:c0 + SOFT_W] + bm_ref[hp, rows, c0:c0 + SOFT_W]
            if start_penalty is not None:
                sq = sq + start_penalty[:, c0:c0 + SOFT_W]
            p = jnp.exp(sq - jnp.max(sq, axis=-1, keepdims=True))
            inv_l.append(1.0 / jnp.sum(p, axis=-1, keepdims=True))
            pad = jnp.zeros((2 * CHUNK, KEYS - SOFT_W), BF16)
            probs.append(jnp.concatenate([p.astype(BF16), pad] if c0 == 0 else [pad, p.astype(BF16)], axis=1))
        return jnp.concatenate(probs, axis=0), inv_l

    def values(hp, pm, inv_l):
        cols = pair_cols(hp)
        out = sum(jnp.dot(pm[:, i * TQ:(i + 1) * TQ], v[:, cols], preferred_element_type=F32)
                  for i, v in enumerate(v_refs))
        for qc in range(n_qc):
            r0 = qc * 2 * CHUNK
            o0 = out[r0:r0 + CHUNK] * inv_l[qc][0:CHUNK]
            o1 = out[r0 + CHUNK:r0 + 2 * CHUNK] * inv_l[qc][CHUNK:2 * CHUNK]
            o_ref[qc * CHUNK:(qc + 1) * CHUNK, cols] = jnp.where(low_half, o0, o1).astype(o_ref.dtype)

    for hp in range(n_pairs):
        values(hp, *softmax(hp, scores(hp)))


def _attn_prompt_kernel(q_ref, k2_ref, k1_ref, k0_ref, v2_ref, v1_ref, v0_ref, base_ref, o_ref, bm):
    t = pl.program_id(1)

    @pl.when((pl.program_id(0) == 0) & (t == 0))
    def _():
        _build_band_bias(base_ref, bm)

    k_refs = (k2_ref, k1_ref, k0_ref)
    v_refs = (v2_ref, v1_ref, v0_ref)

    @pl.when(t >= 2)
    def _():
        _band_tile(q_ref, k_refs, v_refs, bm, o_ref, None)

    @pl.when(t < 2)
    def _():
        col = lax.broadcasted_iota(jnp.int32, (1, KEYS), 1)
        penalty = jnp.where(col < (2 - t) * TQ, NEG_BIG, 0.0)
        _band_tile(q_ref, k_refs, v_refs, bm, o_ref, penalty)


def _attn_prompt(q, k, v, base, n_batch, seq):
    nt = seq // TQ
    blk = lambda back: pl.BlockSpec((TQ, WIDTH_A), lambda b, t: (b * nt + jnp.maximum(t - back, 0), 0))
    return pl.pallas_call(
        _attn_prompt_kernel,
        grid=(n_batch, nt),
        in_specs=[blk(0), blk(2), blk(1), blk(0), blk(2), blk(1), blk(0), _const_spec((N_HEADS_A, KEYS))],
        out_specs=blk(0),
        out_shape=jax.ShapeDtypeStruct((n_batch * seq, WIDTH_A), BF16),
        scratch_shapes=[pltpu.VMEM((N_HEADS_A // 2, 2 * TQ, KEYS), F32)],
        compiler_params=_params(2, VMEM_LIMIT),
        name="band_attn_prompt",
    )(q, k, k, k, v, v, v, base)


def _attn_sample_one(rows, q_ref, kn_ref, vn_ref, kc_ref, vc_ref, sel_ref, dup_ref, o_ref, bias_c, bias_n):
    n_h = N_HEADS_A
    n_new = rows.stop - rows.start
    low_half = lax.broadcasted_iota(jnp.int32, (n_new, LANES), 1) < HEAD_DIM_A
    pairs = [slice(hp * LANES, (hp + 1) * LANES) for hp in range(n_h // 2)]
    q_heads = [jnp.dot(q_ref[rows, cols], sel_ref[e], preferred_element_type=F32).astype(BF16)
               for cols in pairs for e in range(2)]
    yield
    s_c = jnp.concatenate([jnp.dot(q_heads[h], kc_ref[h].astype(BF16), preferred_element_type=F32)
                           for h in range(n_h)], axis=0) + bias_c[...]
    s_n = jnp.concatenate([lax.dot_general(_stack_heads(q_ref[rows, cols]), kn_ref[rows, cols], NT_DIMS,
                                           preferred_element_type=F32) for cols in pairs], axis=0) + bias_n[...]
    yield
    m = jnp.maximum(jnp.max(s_c, axis=-1, keepdims=True), jnp.max(s_n, axis=-1, keepdims=True))
    p_c, p_n = jnp.exp(s_c - m), jnp.exp(s_n - m)
    inv_l = 1.0 / (jnp.sum(p_c, axis=-1, keepdims=True) + jnp.sum(p_n, axis=-1, keepdims=True))
    p_c, p_n = p_c.astype(BF16), p_n.astype(BF16)
    yield
    out_c = jnp.concatenate([lax.dot_general(p_c[h * n_new:(h + 1) * n_new], vc_ref[h].astype(BF16), NT_DIMS,
                                             preferred_element_type=F32) for h in range(n_h)], axis=0) * inv_l
    yield
    hi = out_c.astype(BF16)
    lo = (out_c - hi.astype(F32)).astype(BF16)
    out_c = (jnp.dot(hi, dup_ref[...], preferred_element_type=F32)
             + jnp.dot(lo, dup_ref[...], preferred_element_type=F32))
    for hp, cols in enumerate(pairs):
        pr = slice(hp * 2 * n_new, (hp + 1) * 2 * n_new)
        out = out_c[pr] + jnp.dot(p_n[pr], vn_ref[rows, cols], preferred_element_type=F32) * inv_l[pr]
        o_ref[rows, cols] = jnp.where(low_half, out[0:n_new], out[n_new:]).astype(o_ref.dtype)


def _attn_sample_kernel(n_new, q_ref, kn_ref, vn_ref, kc_ref, vc_ref, base_ref, sel_ref, dup_ref,
                        o_ref, bias_c, bias_n):
    n_cache = kc_ref.shape[3]

    @pl.when(pl.program_id(0) == 0)
    def _():
        for h in range(N_HEADS_A):
            rows = jnp.broadcast_to(base_ref[h:h + 1, :], (n_new, BIAS_W))
            toeplitz = pltpu.roll(rows, 0, 1, stride=1, stride_axis=0)
            bias_c[h * n_new:(h + 1) * n_new, :] = toeplitz[:, 0:n_cache]
            bias_n[h * n_new:(h + 1) * n_new, :] = toeplitz[:, n_cache:n_cache + n_new]

    requests = [_attn_sample_one(slice(i * n_new, (i + 1) * n_new), q_ref, kn_ref, vn_ref, kc_ref.at[i],
                                 vc_ref.at[i], sel_ref, dup_ref, o_ref, bias_c, bias_n)
                for i in range(kc_ref.shape[0])]
    for _ in itertools.zip_longest(*requests):
        pass


def _attn_sample(q, k_new, v_new, k_cache_t, v_cache_t, table, n_new, n_par):
    n_batch, n_h, _, n_cache = k_cache_t.shape
    assert n_cache == A_WINDOW and n_new <= CHUNK and n_h == N_HEADS_A
    lane = np.arange(LANES)
    sel = np.stack([lane[:, None] == np.arange(HEAD_DIM_A)[None, :] + e * HEAD_DIM_A for e in range(2)])
    dup = np.arange(HEAD_DIM_A)[:, None] == lane[None, :] % HEAD_DIM_A
    new = pl.BlockSpec((n_par * n_new, WIDTH_A), lambda g: (g, 0))
    old = pl.BlockSpec((n_par, n_h, HEAD_DIM_A, n_cache), lambda g: (g, 0, 0, 0))
    return pl.pallas_call(
        functools.partial(_attn_sample_kernel, n_new),
        grid=(n_batch // n_par,),
        in_specs=[new, new, new, old, old, _const_spec((n_h, BIAS_W)),
                  _const_spec((2, LANES, HEAD_DIM_A)), _const_spec((HEAD_DIM_A, LANES))],
        out_specs=new,
        out_shape=jax.ShapeDtypeStruct((n_batch * n_new, WIDTH_A), BF16),
        scratch_shapes=[pltpu.VMEM((n_h * n_new, n_cache), F32), pltpu.VMEM((n_h * n_new, n_new), F32)],
        compiler_params=_params(1, VMEM_LIMIT),
        name="band_attn_sample",
    )(q, k_new, v_new, k_cache_t, v_cache_t, _bias_base(table, BIAS_W), jnp.asarray(sel, BF16),
      jnp.asarray(dup, BF16))


def _gla_levels(chunk):
    return [chunk >> i for i in range(int(math.log2(chunk)) + 1)]


def _segment_matrix(chunk):
    i = np.arange(chunk)[:, None]
    t = np.arange(chunk)[None, :]
    blocks = []
    for li, s in enumerate(_gla_levels(chunk)):
        start = (i // s) * s
        f_rows = (t >= start) & (t <= i)
        r_rows = (t > i) & (t <= start + s - 1)
        if li == 0:
            blocks += [f_rows, r_rows]
        else:
            blocks.append(np.where((i // s) % 2 == 1, f_rows, r_rows))
    seg = np.concatenate(blocks, axis=0).astype(np.float32)
    return np.concatenate([seg, seg], axis=1)


def _level_masks(group_tokens):
    tg = group_tokens
    i = (np.arange(2 * tg) % tg)[:, None]
    j = np.arange(tg)[None, :]
    masks = [i == j]
    s = tg // 2
    while s >= 1:
        masks.append(((i // s) % 2 == 1) & (j // s == i // s - 1))
        s //= 2
    return np.stack(masks).astype(np.float32)


def _gla_tile(q, k, v, la, seg_ref, mask_ref, state_ref, o_ref, chunk, n_chunks):
    n, tg = n_chunks, chunk * n_chunks
    levels = _gla_levels(chunk)
    la_hi = la.astype(BF16)
    la_lo = (la - la_hi.astype(F32)).astype(BF16)
    split = jnp.concatenate(
        [jnp.concatenate([la_hi[c * chunk:(c + 1) * chunk], la_lo[c * chunk:(c + 1) * chunk]], axis=0)
         for c in range(n)], axis=1)
    sums = jnp.dot(seg_ref[...], split, preferred_element_type=F32)

    def seg_sum(block):
        rows = slice(block * chunk, (block + 1) * chunk)
        return jnp.minimum(jnp.concatenate(
            [sums[rows, c * WIDTH_BK:(c + 1) * WIDTH_BK] for c in range(n)], axis=0), 0.0)

    fwd0, rev0 = seg_sum(0), seg_sum(1)
    from_start, to_end = jnp.exp(fwd0), jnp.exp(rev0)
    total = [fwd0[(c + 1) * chunk - 1:(c + 1) * chunk] for c in range(n)]

    def chunk_sum(cs):
        cs = list(cs)
        return (sum(total[c] for c in cs[1:]) + total[cs[0]]) if cs else None

    def extend(base, offsets):
        pieces = []
        for c in range(n):
            x = base[c * chunk:(c + 1) * chunk]
            pieces.append(x if offsets[c] is None else x * jnp.exp(offsets[c]))
        return jnp.concatenate(pieces, axis=0) if n > 1 else pieces[0]

    q_state = (q * extend(from_start, [chunk_sum(range(0, c)) for c in range(n)])).astype(BF16)
    k_state = (k * extend(to_end, [chunk_sum(range(c + 1, n)) for c in range(n)])).astype(BF16)
    decay_tile = jnp.exp(chunk_sum(range(n)))
    ops = {}
    s = tg // 2
    while s >= chunk and n > 1:
        per = s // chunk
        offsets, bases = [], []
        for c in range(n):
            sg = c // per
            offsets.append(chunk_sum(range(sg * per, c)) if sg % 2 else chunk_sum(range(c + 1, (sg + 1) * per)))
            bases.append((from_start if sg % 2 else to_end)[c * chunk:(c + 1) * chunk])
        w = extend(jnp.concatenate(bases, axis=0), offsets)
        ops[s] = ((q * w).astype(BF16), (k * w).astype(BF16))
        s //= 2
    for li in range(1, len(levels)):
        w = jnp.exp(seg_sum(li + 1))
        ops[levels[li]] = ((q * w).astype(BF16), (k * w).astype(BF16))
    q16, k16 = q.astype(BF16), k.astype(BF16)
    yield

    gt = mask_ref.shape[2]
    nt_dims = (((1,), (1,)), ((), ()))
    tn_dims = (((0,), (0,)), ((), ()))
    n_pairs, n_groups = N_HEADS_B // 2, tg // gt
    pair_cols = [slice(p * LANES, (p + 1) * LANES) for p in range(n_pairs)]
    pair_v = [v[:, p * 2 * VAL_DIM_B:(p + 1) * 2 * VAL_DIM_B] for p in range(n_pairs)]
    head_v = lambda e: slice(e * VAL_DIM_B, (e + 1) * VAL_DIM_B)
    o_state = []
    for p, cols in enumerate(pair_cols):
        s0, s1 = state_ref[2 * p], state_ref[2 * p + 1]
        zero = jnp.zeros_like(s0)
        st = jnp.concatenate([jnp.concatenate([s0, zero], axis=1),
                              jnp.concatenate([zero, s1], axis=1)], axis=0)
        o_state.append(jnp.dot(q_state[:, cols], st.astype(BF16), preferred_element_type=F32))
        upd = lax.dot_general(k_state[:, cols], pair_v[p], tn_dims, preferred_element_type=F32)
        decay_rows = jnp.broadcast_to(decay_tile[:, cols], (LANES, LANES)).T
        state_ref[2 * p] = s0 * decay_rows[0:KEY_DIM_B] + upd[0:KEY_DIM_B, 0:VAL_DIM_B]
        state_ref[2 * p + 1] = s1 * decay_rows[KEY_DIM_B:] + upd[KEY_DIM_B:, VAL_DIM_B:]
    yield
    blocks = []
    for p, cols in enumerate(pair_cols):
        for g in range(n_groups):
            rows = slice(g * gt, (g + 1) * gt)
            a = mask_ref[0] * lax.dot_general(_stack_heads(q16[rows, cols]), k16[rows, cols], nt_dims,
                                              preferred_element_type=F32)
            m, s = 1, gt // 2
            while s >= 1:
                qd, kd = ops[s]
                a = a + mask_ref[m] * lax.dot_general(_stack_heads(qd[rows, cols]), kd[rows, cols], nt_dims,
                                                      preferred_element_type=F32)
                m, s = m + 1, s // 2
            blocks.append((p, g * gt, g * gt, a.astype(BF16)))
            yield
        s = tg // 2
        while s >= gt:
            qd, kd = ops[s]
            for blk in range(tg // (2 * s)):
                k0, q0 = blk * 2 * s, blk * 2 * s + s
                ab = lax.dot_general(_stack_heads(qd[q0:q0 + s, cols]), kd[k0:k0 + s, cols], nt_dims,
                                     preferred_element_type=F32)
                blocks.append((p, q0, k0, ab.astype(BF16)))
                yield
            s //= 2
    piece = [[[o_state[p][g * gt:(g + 1) * gt, head_v(e)] for g in range(n_groups)] for e in range(2)]
             for p in range(n_pairs)]
    for p, q0, k0, a in blocks:
        nq, nk = a.shape[0] // 2, a.shape[1]
        for e in range(2):
            contrib = jnp.dot(a[e * nq:(e + 1) * nq], pair_v[p][k0:k0 + nk, head_v(e)], preferred_element_type=F32)
            for gi in range(nq // gt):
                g = q0 // gt + gi
                piece[p][e][g] = piece[p][e][g] + contrib[gi * gt:(gi + 1) * gt]
        yield
    outs = [jnp.concatenate(piece[p][e], axis=0) if n_groups > 1 else piece[p][e][0]
            for p in range(n_pairs) for e in range(2)]
    o_ref[...] = jnp.concatenate(outs, axis=1)


def _gla_kernel(chunk, chunks_per_tile, has_init, *refs):
    if has_init:
        q_ref, k_ref, v_ref, la_ref, seg_ref, mask_ref, s0_ref, o_ref, sout_ref, state = refs
    else:
        q_ref, k_ref, v_ref, la_ref, seg_ref, mask_ref, o_ref, sout_ref, state = refs
    t = pl.program_id(1)

    @pl.when(t == 0)
    def _():
        state[...] = s0_ref[...] if has_init else jnp.zeros_like(state)

    tiles = [_gla_tile(q_ref[i], k_ref[i], v_ref[i], la_ref[i], seg_ref, mask_ref, state.at[i], o_ref.at[i],
                       chunk, chunks_per_tile) for i in range(q_ref.shape[0])]
    for _ in itertools.zip_longest(*tiles):
        pass

    @pl.when(t == pl.num_programs(1) - 1)
    def _():
        sout_ref[...] = state[...]


def _gla(q, k, v, la, s0, n_batch, seq, chunk, chunks_per_tile, n_par):
    tg = chunk * chunks_per_tile
    nt = seq // tg
    seg = jnp.asarray(_segment_matrix(chunk), BF16)
    masks = jnp.asarray(_level_masks(min(tg, LANES)), F32)
    row = lambda w: pl.BlockSpec((n_par, tg, w), lambda g, t: (g, t, 0))
    st = pl.BlockSpec((n_par, N_HEADS_B, KEY_DIM_B, VAL_DIM_B), lambda g, t: (g, 0, 0, 0))
    per_seq = lambda z: z.reshape(n_batch, seq, z.shape[-1])
    in_specs = [row(WIDTH_BK), row(WIDTH_BK), row(WIDTH_BV), row(WIDTH_BK), _const_spec(seg.shape),
                _const_spec(masks.shape)]
    args = [per_seq(q), per_seq(k), per_seq(v), per_seq(la), seg, masks]
    if s0 is not None:
        in_specs.append(st)
        args.append(s0)
    o, s_out = pl.pallas_call(
        functools.partial(_gla_kernel, chunk, chunks_per_tile, s0 is not None),
        grid=(n_batch // n_par, nt),
        in_specs=in_specs,
        out_specs=(row(WIDTH_BV), st),
        out_shape=(jax.ShapeDtypeStruct((n_batch, seq, WIDTH_BV), F32),
                   jax.ShapeDtypeStruct((n_batch, N_HEADS_B, KEY_DIM_B, VAL_DIM_B), F32)),
        scratch_shapes=[pltpu.VMEM((n_par, N_HEADS_B, KEY_DIM_B, VAL_DIM_B), F32)],
        compiler_params=_params(2, VMEM_LIMIT),
        name="gla_chunk%d" % chunk,
    )(*args)
    return o.reshape(n_batch * seq, WIDTH_BV), s_out


def _mem_kv_kernel(m_ref, g_ref, wk32_ref, wv32_ref, k_ref, v_ref, k16_ref, v16_ref, wk_ref, wv_ref):
    @pl.when(pl.program_id(0) == 0)
    def _():
        _cast_rows(wk32_ref, wk_ref)
        _cast_rows(wv32_ref, wv_ref)

    m = _rms(m_ref[...], g_ref[...]).astype(BF16)
    k = jnp.dot(m, wk_ref[...], preferred_element_type=F32)
    v = jnp.dot(m, wv_ref[...], preferred_element_type=F32)
    k_ref[...], v_ref[...] = k, v
    k16_ref[...], v16_ref[...] = k.astype(BF16), v.astype(BF16)


def _mem_kv(mem, g, wk, wv):
    n = mem.shape[0]
    tm = N_MEM
    row = pl.BlockSpec((tm, D_MODEL), lambda i: (i, 0))
    return pl.pallas_call(
        _mem_kv_kernel,
        grid=(n // tm,),
        in_specs=[row, _const_spec((1, D_MODEL)), _const_spec((D_MODEL, D_MODEL)), _const_spec((D_MODEL, D_MODEL))],
        out_specs=(row, row, row, row),
        out_shape=(jax.ShapeDtypeStruct((n, D_MODEL), F32), jax.ShapeDtypeStruct((n, D_MODEL), F32),
                   jax.ShapeDtypeStruct((n, D_MODEL), BF16), jax.ShapeDtypeStruct((n, D_MODEL), BF16)),
        scratch_shapes=[pltpu.VMEM((D_MODEL, D_MODEL), BF16), pltpu.VMEM((D_MODEL, D_MODEL), BF16)],
        compiler_params=_params(1, VMEM_LIMIT),
        name="mem_kv",
    )(mem, g, wk, wv)


def _stack_front(rows, segs, x_ref, oa_ref, ob_ref, r_ref, mk_ref, mv_ref, g_gla_ref, wo_ref, g_post_mix_ref,
                 g_pre_mem_ref, wq_ref, wmo_ref, g_post_mem_ref, result):
    n_rows = rows.stop - rows.start
    ob = ob_ref[rows, :]
    normed = []
    for h in range(N_HEADS_B):
        seg = ob[:, h * VAL_DIM_B:(h + 1) * VAL_DIM_B]
        normed.append(seg * lax.rsqrt(jnp.mean(seg * seg, axis=-1, keepdims=True) + EPS))
    yb = (jnp.concatenate(normed, axis=1) * g_gla_ref[...] * _silu(r_ref[rows, :])).astype(BF16)
    yield
    mix = (jnp.dot(oa_ref[rows, :], wo_ref[0:WIDTH_A, :], preferred_element_type=F32)
           + jnp.dot(yb, wo_ref[WIDTH_A:, :], preferred_element_type=F32))
    yield
    x1 = x_ref[rows, :] + _rms(mix, g_post_mix_ref[...])
    hq = _rms(x1, g_pre_mem_ref[...]).astype(BF16)
    yield
    q = (jnp.dot(hq, wq_ref[...], preferred_element_type=F32) * (HEAD_DIM_MEM ** -0.5)).astype(BF16)
    yield
    rows_per_seg = n_rows // len(segs)
    units = [(u, sg, h) for u, sg in enumerate(segs) for h in range(N_HEADS_MEM)]
    mem_rows = lambda sg: slice(sg * N_MEM, (sg + 1) * N_MEM)
    head_cols = lambda h: slice(h * HEAD_DIM_MEM, (h + 1) * HEAD_DIM_MEM)
    scores = [lax.dot_general(q[u * rows_per_seg:(u + 1) * rows_per_seg, head_cols(h)],
                              mk_ref[mem_rows(sg), head_cols(h)].astype(BF16), NT_DIMS,
                              preferred_element_type=F32) for u, sg, h in units]
    yield
    probs, inv_l = [], []
    for s in scores:
        p = jnp.exp(s - jnp.max(s, axis=-1, keepdims=True))
        inv_l.append(1.0 / jnp.sum(p, axis=-1, keepdims=True))
        probs.append(p.astype(BF16))
    yield
    outs = [jnp.dot(probs[i], mv_ref[mem_rows(sg), head_cols(h)].astype(BF16), preferred_element_type=F32) * inv_l[i]
            for i, (u, sg, h) in enumerate(units)]
    seg_outs = [jnp.concatenate(outs[u * N_HEADS_MEM:(u + 1) * N_HEADS_MEM], axis=1) for u in range(len(segs))]
    o = (jnp.concatenate(seg_outs, axis=0) if len(segs) > 1 else seg_outs[0]).astype(BF16)
    yield
    att = jnp.dot(o, wmo_ref[...], preferred_element_type=F32)
    yield
    result.append(x1 + _rms(att, g_post_mem_ref[...]))


def _stack_ffn(x2, g_pre_ffn_ref, wg_ref, wu_ref, wd_ref, g_post_ffn_ref, y_ref):
    hf = _rms(x2, g_pre_ffn_ref[...]).astype(BF16)
    f = jnp.zeros(x2.shape, F32)
    act, act_cols = None, None
    for c in range(D_FF // FF_BLOCK):
        cols = slice(c * FF_BLOCK, (c + 1) * FF_BLOCK)
        gate = jnp.dot(hf, wg_ref[:, cols], preferred_element_type=F32)
        up = jnp.dot(hf, wu_ref[:, cols], preferred_element_type=F32)
        if act is not None:
            f = f + jnp.dot(act, wd_ref[act_cols, :], preferred_element_type=F32)
        act, act_cols = (_silu(gate) * up).astype(BF16), cols
    f = f + jnp.dot(act, wd_ref[act_cols, :], preferred_element_type=F32)
    y_ref[...] = x2 + _rms(f, g_post_ffn_ref[...])


def _stack_kernel(n_seg, x_ref, oa_ref, ob_ref, r_ref, mk_ref, mv_ref,
                  g_gla_ref, wo_ref, g_post_mix_ref, g_pre_mem_ref, wq_ref, wmo_ref, g_post_mem_ref,
                  g_pre_ffn_ref, wg_ref, wu_ref, wd_ref, g_post_ffn_ref, y_ref):
    tm = x_ref.shape[0]
    half = tm // 2
    halves = []
    for i in range(2):
        segs = list(range(i * n_seg // 2, (i + 1) * n_seg // 2)) if n_seg > 1 else [0]
        result = []
        halves.append((result, _stack_front(slice(i * half, (i + 1) * half), segs, x_ref, oa_ref, ob_ref, r_ref,
                                            mk_ref, mv_ref, g_gla_ref, wo_ref, g_post_mix_ref, g_pre_mem_ref,
                                            wq_ref, wmo_ref, g_post_mem_ref, result)))
    for _ in itertools.zip_longest(*[gen for _, gen in halves]):
        pass
    x2 = jnp.concatenate([result[0] for result, _ in halves], axis=0)
    _stack_ffn(x2, g_pre_ffn_ref, wg_ref, wu_ref, wd_ref, g_post_ffn_ref, y_ref)


def _stack(x, oa, ob, r, mk, mv, weights, tm, n_seg, tiles_per_mem_block):
    ntok = x.shape[0]
    row = lambda w: pl.BlockSpec((tm, w), lambda i: (i, 0))
    mem = pl.BlockSpec((n_seg * N_MEM, D_MODEL), lambda i: (i // tiles_per_mem_block, 0))
    w_specs = [_const_spec(w.shape) for w in weights]
    return pl.pallas_call(
        functools.partial(_stack_kernel, n_seg),
        grid=(ntok // tm,),
        in_specs=[row(D_MODEL), row(WIDTH_A), row(WIDTH_BV), row(WIDTH_BV), mem, mem] + w_specs,
        out_specs=row(D_MODEL),
        out_shape=jax.ShapeDtypeStruct((ntok, D_MODEL), F32),
        compiler_params=_params(1, VMEM_LIMIT),
        name="token_stack",
    )(x, oa, ob, r, mk, mv, *weights)


def _bias_base(table, width):
    u = np.arange(width)
    idx = np.where(u < (BAND_CHUNKS + 1) * CHUNK, np.clip(A_WINDOW - u, -REL_CLIP, REL_CLIP) + REL_CLIP, 2 * REL_CLIP)
    return table[:, idx]


def kernel(x_prompt, x_sample, mem_prompt, cache_a_k, cache_a_v, state_gla, cache_mem_k, cache_mem_v,
           g_pre_mix, w_in, rel_bias, w_alpha2, b_alpha, g_gla_out, w_o, g_post_mix,
           g_pre_mem, g_mem, w_mq, w_mk, w_mv, w_mo, g_post_mem,
           g_pre_ffn, w_ffn_gate, w_ffn_up, w_ffn_down, g_post_ffn):
    depth = w_in.shape[0]
    assert depth == 1
    l = 0
    n_p, seq, _ = x_prompt.shape
    n_s, seq_s, _ = x_sample.shape
    vec = lambda g: g[l].reshape(1, -1)

    wi = w_in[l]
    w_t = jnp.swapaxes(wi, 0, 1)
    wa2 = jnp.pad(w_alpha2[l], ((0, LANES - GATE_RANK), (0, 0))).astype(BF16)
    stack_w = [vec(g_gla_out), w_o[l].astype(BF16), vec(g_post_mix), vec(g_pre_mem), w_mq[l].astype(BF16),
               w_mo[l].astype(BF16), vec(g_post_mem), vec(g_pre_ffn), w_ffn_gate[l].astype(BF16),
               w_ffn_up[l].astype(BF16), w_ffn_down[l].astype(BF16), vec(g_post_ffn)]

    xp = x_prompt.reshape(n_p * seq, D_MODEL)
    keep = min(A_WINDOW, seq)
    qa, ka, va, qb, kb, vb, r, la, k_tail, v_tail = _proj(xp, vec(g_pre_mix), w_t, wa2, vec(b_alpha), TILES.proj_rows,
                                                          seq // TILES.proj_rows, keep, True)
    oa = _attn_prompt(qa, ka, va, _bias_base(rel_bias[l], KEYS), n_p, seq)
    ob, sp = _gla(qb, kb, vb, la, None, n_p, seq, CHUNK, TQ // CHUNK, n_p)
    mk, mv, mk16, mv16 = _mem_kv(mem_prompt.reshape(n_p * N_MEM, D_MODEL), vec(g_mem), w_mk[l], w_mv[l])
    yp = _stack(xp, oa, ob, r, mk16, mv16, stack_w, TILES.stack_rows, 1,
                seq // TILES.stack_rows)

    xs = x_sample.reshape(n_s * seq_s, D_MODEL)
    ntok_s = n_s * seq_s
    qa_s, ka_s, va_s, qb_s, kb_s, vb_s, r_s, la_s, k_new, v_new = _proj(
        xs, vec(g_pre_mix), w_t, wa2, vec(b_alpha), ntok_s, 1, ntok_s, False)
    oa_s = _attn_sample(qa_s, ka_s, va_s, jnp.transpose(cache_a_k[l], (0, 2, 3, 1)),
                        jnp.transpose(cache_a_v[l], (0, 2, 3, 1)), rel_bias[l], seq_s, TILES.sample_attn_requests)
    ob_s, ss = _gla(qb_s, kb_s, vb_s, la_s, state_gla[l], n_s, seq_s, seq_s, 1,
                    TILES.sample_gla_requests)
    seg_s = TILES.sample_stack_requests
    ys = _stack(xs, oa_s, ob_s, r_s,
                cache_mem_k[l].reshape(n_s * N_MEM, D_MODEL),
                cache_mem_v[l].reshape(n_s * N_MEM, D_MODEL), stack_w, seg_s * seq_s, seg_s, 1)

    heads = lambda z, n, t: z.reshape(1, n, t, N_HEADS_A, HEAD_DIM_A)
    tails = lambda z: jnp.transpose(z.reshape(n_p, N_HEADS_A, HEAD_DIM_A, keep), (0, 3, 1, 2))[None]
    return (yp.reshape(n_p, seq, D_MODEL), ys.reshape(n_s, seq_s, D_MODEL),
            tails(k_tail), tails(v_tail), sp[None],
            mk.reshape(1, n_p, N_MEM, N_HEADS_MEM, HEAD_DIM_MEM), mv.reshape(1, n_p, N_MEM, N_HEADS_MEM, HEAD_DIM_MEM),
            heads(k_new, n_s, seq_s), heads(v_new, n_s, seq_s), ss[None])
```

```python
import functools
import itertools
import math
from typing import NamedTuple

import jax
import jax.numpy as jnp
import numpy as np
from jax import lax
from jax.experimental import pallas as pl
from jax.experimental.pallas import tpu as pltpu

F32 = jnp.float32
BF16 = jnp.bfloat16

D_MODEL = 1024
CHUNK = 64
BAND_CHUNKS = 8
A_WINDOW = BAND_CHUNKS * CHUNK
N_HEADS_A = 8
HEAD_DIM_A = 64
WIDTH_A = 512
REL_CLIP = 256
N_HEADS_B = 4
KEY_DIM_B = 64
VAL_DIM_B = 128
WIDTH_BK = 256
WIDTH_BV = 512
GATE_RANK = 16
GATE_TAU = 16.0
N_MEM = 256
N_HEADS_MEM = 4
HEAD_DIM_MEM = 256
D_FF = 2816
EPS = 1e-6
NEG_BIG = -1e30
NT_DIMS = (((1,), (1,)), ((), ()))

LANES = 128
BIAS_W = 640
FF_BLOCK = 256
CAST_ROWS = 512
VMEM_LIMIT = 56 * 1024 * 1024


class _TilePlan(NamedTuple):
    proj_rows: int = 1024
    stack_rows: int = 512
    sample_attn_requests: int = 4
    sample_gla_requests: int = 8
    sample_stack_requests: int = 4


TILES = _TilePlan()

OFF_QA, OFF_KA, OFF_VA, OFF_QB, OFF_KB, OFF_VB, OFF_R = 0, 512, 1024, 1536, 1792, 2048, 2560
IN_WIDTH = OFF_R + GATE_RANK + WIDTH_BV


def _rms(x, g):
    return x * lax.rsqrt(jnp.mean(x * x, axis=-1, keepdims=True) + EPS) * g


def _silu(x):
    return x / (1.0 + jnp.exp(-x))


def _log_sigmoid(z):
    return jnp.minimum(z, 0.0) - jnp.log(1.0 + jnp.exp(-jnp.abs(z)))


def _const_spec(shape):
    nd = len(shape)
    return pl.BlockSpec(shape, lambda *_: (0,) * nd, pipeline_mode=pl.Buffered(1))


def _params(n_axes, vmem=None):
    return pltpu.CompilerParams(dimension_semantics=("arbitrary",) * n_axes, vmem_limit_bytes=vmem)


def _cast_rows(src_ref, dst_ref):
    n = src_ref.shape[0]
    for lo in range(0, n, CAST_ROWS):
        rows = slice(lo, min(lo + CAST_ROWS, n))
        dst_ref[rows, :] = src_ref[rows, :].astype(dst_ref.dtype)


def _proj_kernel(tiles_per_seq, tail_rows, tail_t, x_ref, g_ref, w32_ref, wa2_ref, ba_ref,
                 qa_ref, ka_ref, va_ref, qb_ref, kb_ref, vb_ref, r_ref, la_ref, kt_ref, vt_ref, w_ref):
    @pl.when(pl.program_id(0) == 0)
    def _():
        _cast_rows(w32_ref, w_ref)

    h = _rms(x_ref[...], g_ref[...]).astype(BF16)

    def proj(lo, hi):
        return lax.dot_general(h, w_ref[lo:hi, :], NT_DIMS, preferred_element_type=F32)

    g_low = proj(OFF_R, OFF_R + LANES).astype(BF16)
    z = jnp.dot(g_low, wa2_ref[...], preferred_element_type=F32) + ba_ref[...]
    la_ref[...] = _log_sigmoid(z) * (1.0 / GATE_TAU)
    qa_ref[...] = (proj(OFF_QA, OFF_KA) * (HEAD_DIM_A ** -0.5)).astype(BF16)
    ka = proj(OFF_KA, OFF_VA)
    va = proj(OFF_VA, OFF_QB)
    ka_ref[...] = ka.astype(BF16)
    va_ref[...] = va.astype(BF16)
    qb_ref[...] = proj(OFF_QB, OFF_KB) * (KEY_DIM_B ** -0.5)
    kb_ref[...] = proj(OFF_KB, OFF_VB)
    vb_ref[...] = proj(OFF_VB, OFF_R).astype(BF16)
    r_ref[...] = proj(OFF_R + GATE_RANK, IN_WIDTH)

    @pl.when(pl.program_id(0) % tiles_per_seq == tiles_per_seq - 1)
    def _():
        k_keep, v_keep = ka[ka.shape[0] - tail_rows:], va[va.shape[0] - tail_rows:]
        kt_ref[...] = k_keep.T if tail_t else k_keep
        vt_ref[...] = v_keep.T if tail_t else v_keep


def _proj(x, g, w_t, wa2, ba, tm, tiles_per_seq, tail_rows, tail_t):
    ntok = x.shape[0]
    nt = ntok // tm
    nseq = nt // tiles_per_seq
    row = lambda w: pl.BlockSpec((tm, w), lambda i: (i, 0))
    assert tail_rows <= tm
    tail_shape = (WIDTH_A, tail_rows) if tail_t else (tail_rows, WIDTH_A)
    tail = pl.BlockSpec(tail_shape, lambda i: (i // tiles_per_seq, 0))
    out_shape = (
        jax.ShapeDtypeStruct((ntok, WIDTH_A), BF16),
        jax.ShapeDtypeStruct((ntok, WIDTH_A), BF16),
        jax.ShapeDtypeStruct((ntok, WIDTH_A), BF16),
        jax.ShapeDtypeStruct((ntok, WIDTH_BK), F32),
        jax.ShapeDtypeStruct((ntok, WIDTH_BK), F32),
        jax.ShapeDtypeStruct((ntok, WIDTH_BV), BF16),
        jax.ShapeDtypeStruct((ntok, WIDTH_BV), F32),
        jax.ShapeDtypeStruct((ntok, WIDTH_BK), F32),
        jax.ShapeDtypeStruct((nseq * tail_shape[0], tail_shape[1]), F32),
        jax.ShapeDtypeStruct((nseq * tail_shape[0], tail_shape[1]), F32),
    )
    return pl.pallas_call(
        functools.partial(_proj_kernel, tiles_per_seq, tail_rows, tail_t),
        grid=(nt,),
        in_specs=[row(D_MODEL), _const_spec((1, D_MODEL)), _const_spec((IN_WIDTH, D_MODEL)),
                  _const_spec((LANES, WIDTH_BK)), _const_spec((1, WIDTH_BK))],
        out_specs=(row(WIDTH_A), row(WIDTH_A), row(WIDTH_A), row(WIDTH_BK), row(WIDTH_BK),
                   row(WIDTH_BV), row(WIDTH_BV), row(WIDTH_BK), tail, tail),
        out_shape=out_shape,
        scratch_shapes=[pltpu.VMEM((IN_WIDTH, D_MODEL), BF16)],
        compiler_params=_params(1, VMEM_LIMIT),
        name="proj",
    )(x, g, w_t, wa2, ba)


def _stack_heads(x_pair):
    low_half = lax.broadcasted_iota(jnp.int32, x_pair.shape, 1) < HEAD_DIM_A
    zero = jnp.zeros_like(x_pair)
    return jnp.concatenate([jnp.where(low_half, x_pair, zero), jnp.where(low_half, zero, x_pair)], axis=0)


TQ = 4 * CHUNK
KEYS = 3 * TQ
SOFT_W = 5 * LANES
assert 2 * TQ == A_WINDOW and SOFT_W >= (BAND_CHUNKS + 1) * CHUNK + CHUNK


def _build_band_bias(base_ref, bm_ref):
    n_band = (BAND_CHUNKS + 1) * CHUNK
    col = lax.broadcasted_iota(jnp.int32, (CHUNK, KEYS), 1)
    for h in range(N_HEADS_A):
        rows = jnp.broadcast_to(base_ref[h:h + 1, :], (CHUNK, KEYS))
        for qc in range(TQ // CHUNK):
            toeplitz = pltpu.roll(rows, qc * CHUNK, 1, stride=1, stride_axis=0)
            in_band = (col >= qc * CHUNK) & (col < qc * CHUNK + n_band)
            r0 = (qc * 2 + h % 2) * CHUNK
            bm_ref[h // 2, r0:r0 + CHUNK, :] = jnp.where(in_band, toeplitz, NEG_BIG)


def _band_tile(q_ref, k_refs, v_refs, bm_ref, o_ref, start_penalty):
    n_qc, n_pairs = TQ // CHUNK, N_HEADS_A // 2
    low_half = lax.broadcasted_iota(jnp.int32, (CHUNK, LANES), 1) < HEAD_DIM_A
    pair_cols = lambda hp: slice(hp * LANES, (hp + 1) * LANES)

    def scores(hp):
        cols = pair_cols(hp)
        qs = jnp.concatenate([_stack_heads(q_ref[qc * CHUNK:(qc + 1) * CHUNK, cols]) for qc in range(n_qc)],
                             axis=0)
        return jnp.concatenate([lax.dot_general(qs, k[:, cols], NT_DIMS, preferred_element_type=F32)
                                for k in k_refs], axis=1)

    def softmax(hp, s):
        probs, inv_l = [], []
        for qc in range(n_qc):
            rows = slice(qc * 2 * CHUNK, (qc + 1) * 2 * CHUNK)
            c0 = 0 if (qc + 1) * CHUNK + A_WINDOW <= SOFT_W else KEYS - SOFT_W
            sq = s[rows, c0:c0 + SOFT_W] + bm_ref[hp, rows, c0:c0 + SOFT_W]
            if start_penalty is not None:
                sq = sq + start_penalty[:, c0:c0 + SOFT_W]
            p = jnp.exp(sq - jnp.max(sq, axis=-1, keepdims=True))
            inv_l.append(1.0 / jnp.sum(p, axis=-1, keepdims=True))
            pad = jnp.zeros((2 * CHUNK, KEYS - SOFT_W), BF16)
            probs.append(jnp.concatenate([p.astype(BF16), pad] if c0 == 0 else [pad, p.astype(BF16)], axis=1))
        return jnp.concatenate(probs, axis=0), inv_l

    def values(hp, pm, inv_l):
        cols = pair_cols(hp)
        out = sum(jnp.dot(pm[:, i * TQ:(i + 1) * TQ], v[:, cols], preferred_element_type=F32)
                  for i, v in enumerate(v_refs))
        for qc in range(n_qc):
            r0 = qc * 2 * CHUNK
            o0 = out[r0:r0 + CHUNK] * inv_l[qc][0:CHUNK]
            o1 = out[r0 + CHUNK:r0 + 2 * CHUNK] * inv_l[qc][CHUNK:2 * CHUNK]
            o_ref[qc * CHUNK:(qc + 1) * CHUNK, cols] = jnp.where(low_half, o0, o1).astype(o_ref.dtype)

    for hp in range(n_pairs):
        s = scores(hp)
        yield
        pm, inv_l = softmax(hp, s)
        yield
        values(hp, pm, inv_l)
        yield


def _attn_prompt_kernel(q_ref, k2_ref, k1_ref, k0_ref, v2_ref, v1_ref, v0_ref, base_ref, o_ref, bm):
    t = pl.program_id(1)

    @pl.when((pl.program_id(0) == 0) & (t == 0))
    def _():
        _build_band_bias(base_ref, bm)

    def tiles(penalty):
        gens = [_band_tile(q_ref.at[i], (k2_ref.at[i], k1_ref.at[i], k0_ref.at[i]),
                           (v2_ref.at[i], v1_ref.at[i], v0_ref.at[i]), bm, o_ref.at[i], penalty)
                for i in range(q_ref.shape[0])]
        for _ in itertools.zip_longest(*gens):
            pass

    @pl.when(t >= 2)
    def _():
        tiles(None)

    @pl.when(t < 2)
    def _():
        col = lax.broadcasted_iota(jnp.int32, (1, KEYS), 1)
        tiles(jnp.where(col < (2 - t) * TQ, NEG_BIG, 0.0))


def _attn_prompt(q, k, v, base, n_batch, seq):
    nt = seq // TQ
    n_par = n_batch
    blk = lambda back: pl.BlockSpec((n_par, TQ, WIDTH_A), lambda g, t: (g, jnp.maximum(t - back, 0), 0))
    per_seq = lambda z: z.reshape(n_batch, seq, WIDTH_A)
    q, k, v = per_seq(q), per_seq(k), per_seq(v)
    return pl.pallas_call(
        _attn_prompt_kernel,
        grid=(n_batch // n_par, nt),
        in_specs=[blk(0), blk(2), blk(1), blk(0), blk(2), blk(1), blk(0), _const_spec((N_HEADS_A, KEYS))],
        out_specs=blk(0),
        out_shape=jax.ShapeDtypeStruct((n_batch, seq, WIDTH_A), BF16),
        scratch_shapes=[pltpu.VMEM((N_HEADS_A // 2, 2 * TQ, KEYS), F32)],
        compiler_params=_params(2, VMEM_LIMIT),
        name="band_attn_prompt",
    )(q, k, k, k, v, v, v, base).reshape(n_batch * seq, WIDTH_A)


def _attn_sample_one(rows, q_ref, kn_ref, vn_ref, kc_ref, vc_ref, sel_ref, dup_ref, o_ref, bias_c, bias_n):
    n_h = N_HEADS_A
    n_new = rows.stop - rows.start
    low_half = lax.broadcasted_iota(jnp.int32, (n_new, LANES), 1) < HEAD_DIM_A
    pairs = [slice(hp * LANES, (hp + 1) * LANES) for hp in range(n_h // 2)]
    q_heads = [jnp.dot(q_ref[rows, cols], sel_ref[e], preferred_element_type=F32).astype(BF16)
               for cols in pairs for e in range(2)]
    yield
    s_c = jnp.concatenate([jnp.dot(q_heads[h], kc_ref[h].astype(BF16), preferred_element_type=F32)
                           for h in range(n_h)], axis=0) + bias_c[...]
    s_n = jnp.concatenate([lax.dot_general(_stack_heads(q_ref[rows, cols]), kn_ref[rows, cols], NT_DIMS,
                                           preferred_element_type=F32) for cols in pairs], axis=0) + bias_n[...]
    yield
    m = jnp.maximum(jnp.max(s_c, axis=-1, keepdims=True), jnp.max(s_n, axis=-1, keepdims=True))
    p_c, p_n = jnp.exp(s_c - m), jnp.exp(s_n - m)
    inv_l = 1.0 / (jnp.sum(p_c, axis=-1, keepdims=True) + jnp.sum(p_n, axis=-1, keepdims=True))
    p_c, p_n = p_c.astype(BF16), p_n.astype(BF16)
    yield
    out_c = jnp.concatenate([lax.dot_general(p_c[h * n_new:(h + 1) * n_new], vc_ref[h].astype(BF16), NT_DIMS,
                                             preferred_element_type=F32) for h in range(n_h)], axis=0) * inv_l
    yield
    hi = out_c.astype(BF16)
    lo = (out_c - hi.astype(F32)).astype(BF16)
    out_c = (jnp.dot(hi, dup_ref[...], preferred_element_type=F32)
             + jnp.dot(lo, dup_ref[...], preferred_element_type=F32))
    for hp, cols in enumerate(pairs):
        pr = slice(hp * 2 * n_new, (hp + 1) * 2 * n_new)
        out = out_c[pr] + jnp.dot(p_n[pr], vn_ref[rows, cols], preferred_element_type=F32) * inv_l[pr]
        o_ref[rows, cols] = jnp.where(low_half, out[0:n_new], out[n_new:]).astype(o_ref.dtype)


def _attn_sample_kernel(n_new, q_ref, kn_ref, vn_ref, kc_ref, vc_ref, base_ref, sel_ref, dup_ref,
                        o_ref, bias_c, bias_n):
    n_cache = kc_ref.shape[3]

    @pl.when(pl.program_id(0) == 0)
    def _():
        for h in range(N_HEADS_A):
            rows = jnp.broadcast_to(base_ref[h:h + 1, :], (n_new, BIAS_W))
            toeplitz = pltpu.roll(rows, 0, 1, stride=1, stride_axis=0)
            bias_c[h * n_new:(h + 1) * n_new, :] = toeplitz[:, 0:n_cache]
            bias_n[h * n_new:(h + 1) * n_new, :] = toeplitz[:, n_cache:n_cache + n_new]

    requests = [_attn_sample_one(slice(i * n_new, (i + 1) * n_new), q_ref, kn_ref, vn_ref, kc_ref.at[i],
                                 vc_ref.at[i], sel_ref, dup_ref, o_ref, bias_c, bias_n)
                for i in range(kc_ref.shape[0])]
    for _ in itertools.zip_longest(*requests):
        pass


def _attn_sample(q, k_new, v_new, k_cache_t, v_cache_t, table, n_new, n_par):
    n_batch, n_h, _, n_cache = k_cache_t.shape
    assert n_cache == A_WINDOW and n_new <= CHUNK and n_h == N_HEADS_A
    lane = np.arange(LANES)
    sel = np.stack([lane[:, None] == np.arange(HEAD_DIM_A)[None, :] + e * HEAD_DIM_A for e in range(2)])
    dup = np.arange(HEAD_DIM_A)[:, None] == lane[None, :] % HEAD_DIM_A
    new = pl.BlockSpec((n_par * n_new, WIDTH_A), lambda g: (g, 0))
    old = pl.BlockSpec((n_par, n_h, HEAD_DIM_A, n_cache), lambda g: (g, 0, 0, 0))
    return pl.pallas_call(
        functools.partial(_attn_sample_kernel, n_new),
        grid=(n_batch // n_par,),
        in_specs=[new, new, new, old, old, _const_spec((n_h, BIAS_W)),
                  _const_spec((2, LANES, HEAD_DIM_A)), _const_spec((HEAD_DIM_A, LANES))],
        out_specs=new,
        out_shape=jax.ShapeDtypeStruct((n_batch * n_new, WIDTH_A), BF16),
        scratch_shapes=[pltpu.VMEM((n_h * n_new, n_cache), F32), pltpu.VMEM((n_h * n_new, n_new), F32)],
        compiler_params=_params(1, VMEM_LIMIT),
        name="band_attn_sample",
    )(q, k_new, v_new, k_cache_t, v_cache_t, _bias_base(table, BIAS_W), jnp.asarray(sel, BF16),
      jnp.asarray(dup, BF16))


def _gla_levels(chunk):
    return [chunk >> i for i in range(int(math.log2(chunk)) + 1)]


def _segment_matrix(chunk):
    i = np.arange(chunk)[:, None]
    t = np.arange(chunk)[None, :]
    blocks = []
    for li, s in enumerate(_gla_levels(chunk)):
        start = (i // s) * s
        f_rows = (t >= start) & (t <= i)
        r_rows = (t > i) & (t <= start + s - 1)
        if li == 0:
            blocks += [f_rows, r_rows]
        else:
            blocks.append(np.where((i // s) % 2 == 1, f_rows, r_rows))
    seg = np.concatenate(blocks, axis=0).astype(np.float32)
    return np.concatenate([seg, seg], axis=1)


def _level_masks(group_tokens):
    tg = group_tokens
    i = (np.arange(2 * tg) % tg)[:, None]
    j = np.arange(tg)[None, :]
    masks = [i == j]
    s = tg // 2
    while s >= 1:
        masks.append(((i // s) % 2 == 1) & (j // s == i // s - 1))
        s //= 2
    return np.stack(masks).astype(np.float32)


def _gla_tile(q, k, v, la, seg_ref, mask_ref, state_ref, o_ref, chunk, n_chunks):
    n, tg = n_chunks, chunk * n_chunks
    levels = _gla_levels(chunk)
    la_hi = la.astype(BF16)
    la_lo = (la - la_hi.astype(F32)).astype(BF16)
    split = jnp.concatenate(
        [jnp.concatenate([la_hi[c * chunk:(c + 1) * chunk], la_lo[c * chunk:(c + 1) * chunk]], axis=0)
         for c in range(n)], axis=1)
    sums = jnp.dot(seg_ref[...], split, preferred_element_type=F32)

    def seg_sum(block):
        rows = slice(block * chunk, (block + 1) * chunk)
        return jnp.minimum(jnp.concatenate(
            [sums[rows, c * WIDTH_BK:(c + 1) * WIDTH_BK] for c in range(n)], axis=0), 0.0)

    fwd0, rev0 = seg_sum(0), seg_sum(1)
    from_start, to_end = jnp.exp(fwd0), jnp.exp(rev0)
    total = [fwd0[(c + 1) * chunk - 1:(c + 1) * chunk] for c in range(n)]

    def chunk_sum(cs):
        cs = list(cs)
        return (sum(total[c] for c in cs[1:]) + total[cs[0]]) if cs else None

    def extend(base, offsets):
        pieces = []
        for c in range(n):
            x = base[c * chunk:(c + 1) * chunk]
            pieces.append(x if offsets[c] is None else x * jnp.exp(offsets[c]))
        return jnp.concatenate(pieces, axis=0) if n > 1 else pieces[0]

    q_state = (q * extend(from_start, [chunk_sum(range(0, c)) for c in range(n)])).astype(BF16)
    k_state = (k * extend(to_end, [chunk_sum(range(c + 1, n)) for c in range(n)])).astype(BF16)
    decay_tile = jnp.exp(chunk_sum(range(n)))
    ops = {}
    s = tg // 2
    while s >= chunk and n > 1:
        per = s // chunk
        offsets, bases = [], []
        for c in range(n):
            sg = c // per
            offsets.append(chunk_sum(range(sg * per, c)) if sg % 2 else chunk_sum(range(c + 1, (sg + 1) * per)))
            bases.append((from_start if sg % 2 else to_end)[c * chunk:(c + 1) * chunk])
        w = extend(jnp.concatenate(bases, axis=0), offsets)
        ops[s] = ((q * w).astype(BF16), (k * w).astype(BF16))
        s //= 2
    for li in range(1, len(levels)):
        w = jnp.exp(seg_sum(li + 1))
        ops[levels[li]] = ((q * w).astype(BF16), (k * w).astype(BF16))
    q16, k16 = q.astype(BF16), k.astype(BF16)
    yield

    gt = mask_ref.shape[2]
    nt_dims = (((1,), (1,)), ((), ()))
    tn_dims = (((0,), (0,)), ((), ()))
    n_pairs, n_groups = N_HEADS_B // 2, tg // gt
    pair_cols = [slice(p * LANES, (p + 1) * LANES) for p in range(n_pairs)]
    pair_v = [v[:, p * 2 * VAL_DIM_B:(p + 1) * 2 * VAL_DIM_B] for p in range(n_pairs)]
    head_v = lambda e: slice(e * VAL_DIM_B, (e + 1) * VAL_DIM_B)
    o_state = []
    for p, cols in enumerate(pair_cols):
        s0, s1 = state_ref[2 * p], state_ref[2 * p + 1]
        zero = jnp.zeros_like(s0)
        st = jnp.concatenate([jnp.concatenate([s0, zero], axis=1),
                              jnp.concatenate([zero, s1], axis=1)], axis=0)
        o_state.append(jnp.dot(q_state[:, cols], st.astype(BF16), preferred_element_type=F32))
        upd = lax.dot_general(k_state[:, cols], pair_v[p], tn_dims, preferred_element_type=F32)
        decay_rows = jnp.broadcast_to(decay_tile[:, cols], (LANES, LANES)).T
        state_ref[2 * p] = s0 * decay_rows[0:KEY_DIM_B] + upd[0:KEY_DIM_B, 0:VAL_DIM_B]
        state_ref[2 * p + 1] = s1 * decay_rows[KEY_DIM_B:] + upd[KEY_DIM_B:, VAL_DIM_B:]
    yield
    blocks = []
    for p, cols in enumerate(pair_cols):
        for g in range(n_groups):
            rows = slice(g * gt, (g + 1) * gt)
            a = mask_ref[0] * lax.dot_general(_stack_heads(q16[rows, cols]), k16[rows, cols], nt_dims,
                                              preferred_element_type=F32)
            m, s = 1, gt // 2
            while s >= 1:
                qd, kd = ops[s]
                a = a + mask_ref[m] * lax.dot_general(_stack_heads(qd[rows, cols]), kd[rows, cols], nt_dims,
                                                      preferred_element_type=F32)
                m, s = m + 1, s // 2
            blocks.append((p, g * gt, g * gt, a.astype(BF16)))
            yield
        s = tg // 2
        while s >= gt:
            qd, kd = ops[s]
            for blk in range(tg // (2 * s)):
                k0, q0 = blk * 2 * s, blk * 2 * s + s
                ab = lax.dot_general(_stack_heads(qd[q0:q0 + s, cols]), kd[k0:k0 + s, cols], nt_dims,
                                     preferred_element_type=F32)
                blocks.append((p, q0, k0, ab.astype(BF16)))
                yield
            s //= 2
    piece = [[[o_state[p][g * gt:(g + 1) * gt, head_v(e)] for g in range(n_groups)] for e in range(2)]
             for p in range(n_pairs)]
    for p, q0, k0, a in blocks:
        nq, nk = a.shape[0] // 2, a.shape[1]
        for e in range(2):
            contrib = jnp.dot(a[e * nq:(e + 1) * nq], pair_v[p][k0:k0 + nk, head_v(e)], preferred_element_type=F32)
            for gi in range(nq // gt):
                g = q0 // gt + gi
                piece[p][e][g] = piece[p][e][g] + contrib[gi * gt:(gi + 1) * gt]
        yield
    outs = [jnp.concatenate(piece[p][e], axis=0) if n_groups > 1 else piece[p][e][0]
            for p in range(n_pairs) for e in range(2)]
    o_ref[...] = jnp.concatenate(outs, axis=1)


def _gla_kernel(chunk, chunks_per_tile, has_init, *refs):
    if has_init:
        q_ref, k_ref, v_ref, la_ref, seg_ref, mask_ref, s0_ref, o_ref, sout_ref, state = refs
    else:
        q_ref, k_ref, v_ref, la_ref, seg_ref, mask_ref, o_ref, sout_ref, state = refs
    t = pl.program_id(1)

    @pl.when(t == 0)
    def _():
        state[...] = s0_ref[...] if has_init else jnp.zeros_like(state)

    tiles = [_gla_tile(q_ref[i], k_ref[i], v_ref[i], la_ref[i], seg_ref, mask_ref, state.at[i], o_ref.at[i],
                       chunk, chunks_per_tile) for i in range(q_ref.shape[0])]
    for _ in itertools.zip_longest(*tiles):
        pass

    @pl.when(t == pl.num_programs(1) - 1)
    def _():
        sout_ref[...] = state[...]


def _gla(q, k, v, la, s0, n_batch, seq, chunk, chunks_per_tile, n_par):
    tg = chunk * chunks_per_tile
    nt = seq // tg
    seg = jnp.asarray(_segment_matrix(chunk), BF16)
    masks = jnp.asarray(_level_masks(min(tg, LANES)), F32)
    row = lambda w: pl.BlockSpec((n_par, tg, w), lambda g, t: (g, t, 0))
    st = pl.BlockSpec((n_par, N_HEADS_B, KEY_DIM_B, VAL_DIM_B), lambda g, t: (g, 0, 0, 0))
    per_seq = lambda z: z.reshape(n_batch, seq, z.shape[-1])
    in_specs = [row(WIDTH_BK), row(WIDTH_BK), row(WIDTH_BV), row(WIDTH_BK), _const_spec(seg.shape),
                _const_spec(masks.shape)]
    args = [per_seq(q), per_seq(k), per_seq(v), per_seq(la), seg, masks]
    if s0 is not None:
        in_specs.append(st)
        args.append(s0)
    o, s_out = pl.pallas_call(
        functools.partial(_gla_kernel, chunk, chunks_per_tile, s0 is not None),
        grid=(n_batch // n_par, nt),
        in_specs=in_specs,
        out_specs=(row(WIDTH_BV), st),
        out_shape=(jax.ShapeDtypeStruct((n_batch, seq, WIDTH_BV), F32),
                   jax.ShapeDtypeStruct((n_batch, N_HEADS_B, KEY_DIM_B, VAL_DIM_B), F32)),
        scratch_shapes=[pltpu.VMEM((n_par, N_HEADS_B, KEY_DIM_B, VAL_DIM_B), F32)],
        compiler_params=_params(2, VMEM_LIMIT),
        name="gla_chunk%d" % chunk,
    )(*args)
    return o.reshape(n_batch * seq, WIDTH_BV), s_out


def _mem_kv_kernel(m_ref, g_ref, wk32_ref, wv32_ref, k_ref, v_ref, k16_ref, v16_ref, wk_ref, wv_ref):
    @pl.when(pl.program_id(0) == 0)
    def _():
        _cast_rows(wk32_ref, wk_ref)
        _cast_rows(wv32_ref, wv_ref)

    m = _rms(m_ref[...], g_ref[...]).astype(BF16)
    k = jnp.dot(m, wk_ref[...], preferred_element_type=F32)
    v = jnp.dot(m, wv_ref[...], preferred_element_type=F32)
    k_ref[...], v_ref[...] = k, v
    k16_ref[...], v16_ref[...] = k.astype(BF16), v.astype(BF16)


def _mem_kv(mem, g, wk, wv):
    n = mem.shape[0]
    tm = N_MEM
    row = pl.BlockSpec((tm, D_MODEL), lambda i: (i, 0))
    return pl.pallas_call(
        _mem_kv_kernel,
        grid=(n // tm,),
        in_specs=[row, _const_spec((1, D_MODEL)), _const_spec((D_MODEL, D_MODEL)), _const_spec((D_MODEL, D_MODEL))],
        out_specs=(row, row, row, row),
        out_shape=(jax.ShapeDtypeStruct((n, D_MODEL), F32), jax.ShapeDtypeStruct((n, D_MODEL), F32),
                   jax.ShapeDtypeStruct((n, D_MODEL), BF16), jax.ShapeDtypeStruct((n, D_MODEL), BF16)),
        scratch_shapes=[pltpu.VMEM((D_MODEL, D_MODEL), BF16), pltpu.VMEM((D_MODEL, D_MODEL), BF16)],
        compiler_params=_params(1, VMEM_LIMIT),
        name="mem_kv",
    )(mem, g, wk, wv)


def _stack_front(rows, segs, x_ref, oa_ref, ob_ref, r_ref, mk_ref, mv_ref, g_gla_ref, wo_ref, g_post_mix_ref,
                 g_pre_mem_ref, wq_ref, wmo_ref, g_post_mem_ref, result):
    n_rows = rows.stop - rows.start
    ob = ob_ref[rows, :]
    normed = []
    for h in range(N_HEADS_B):
        seg = ob[:, h * VAL_DIM_B:(h + 1) * VAL_DIM_B]
        normed.append(seg * lax.rsqrt(jnp.mean(seg * seg, axis=-1, keepdims=True) + EPS))
    yb = (jnp.concatenate(normed, axis=1) * g_gla_ref[...] * _silu(r_ref[rows, :])).astype(BF16)
    yield
    mix = (jnp.dot(oa_ref[rows, :], wo_ref[0:WIDTH_A, :], preferred_element_type=F32)
           + jnp.dot(yb, wo_ref[WIDTH_A:, :], preferred_element_type=F32))
    yield
    x1 = x_ref[rows, :] + _rms(mix, g_post_mix_ref[...])
    hq = _rms(x1, g_pre_mem_ref[...]).astype(BF16)
    yield
    q = (jnp.dot(hq, wq_ref[...], preferred_element_type=F32) * (HEAD_DIM_MEM ** -0.5)).astype(BF16)
    yield
    rows_per_seg = n_rows // len(segs)
    units = [(u, sg, h) for u, sg in enumerate(segs) for h in range(N_HEADS_MEM)]
    mem_rows = lambda sg: slice(sg * N_MEM, (sg + 1) * N_MEM)
    head_cols = lambda h: slice(h * HEAD_DIM_MEM, (h + 1) * HEAD_DIM_MEM)
    scores = [lax.dot_general(q[u * rows_per_seg:(u + 1) * rows_per_seg, head_cols(h)],
                              mk_ref[mem_rows(sg), head_cols(h)].astype(BF16), NT_DIMS,
                              preferred_element_type=F32) for u, sg, h in units]
    yield
    probs, inv_l = [], []
    for s in scores:
        p = jnp.exp(s - jnp.max(s, axis=-1, keepdims=True))
        inv_l.append(1.0 / jnp.sum(p, axis=-1, keepdims=True))
        probs.append(p.astype(BF16))
    yield
    outs = [jnp.dot(probs[i], mv_ref[mem_rows(sg), head_cols(h)].astype(BF16), preferred_element_type=F32) * inv_l[i]
            for i, (u, sg, h) in enumerate(units)]
    seg_outs = [jnp.concatenate(outs[u * N_HEADS_MEM:(u + 1) * N_HEADS_MEM], axis=1) for u in range(len(segs))]
    o = (jnp.concatenate(seg_outs, axis=0) if len(segs) > 1 else seg_outs[0]).astype(BF16)
    yield
    att = jnp.dot(o, wmo_ref[...], preferred_element_type=F32)
    yield
    result.append(x1 + _rms(att, g_post_mem_ref[...]))


def _stack_ffn(x2, g_pre_ffn_ref, wg_ref, wu_ref, wd_ref, g_post_ffn_ref, y_ref):
    hf = _rms(x2, g_pre_ffn_ref[...]).astype(BF16)
    f = jnp.zeros(x2.shape, F32)
    act, act_cols = None, None
    for c in range(D_FF // FF_BLOCK):
        cols = slice(c * FF_BLOCK, (c + 1) * FF_BLOCK)
        gate = jnp.dot(hf, wg_ref[:, cols], preferred_element_type=F32)
        up = jnp.dot(hf, wu_ref[:, cols], preferred_element_type=F32)
        if act is not None:
            f = f + jnp.dot(act, wd_ref[act_cols, :], preferred_element_type=F32)
        act, act_cols = (_silu(gate) * up).astype(BF16), cols
    f = f + jnp.dot(act, wd_ref[act_cols, :], preferred_element_type=F32)
    y_ref[...] = x2 + _rms(f, g_post_ffn_ref[...])


def _stack_kernel(n_seg, x_ref, oa_ref, ob_ref, r_ref, mk_ref, mv_ref,
                  g_gla_ref, wo_ref, g_post_mix_ref, g_pre_mem_ref, wq_ref, wmo_ref, g_post_mem_ref,
                  g_pre_ffn_ref, wg_ref, wu_ref, wd_ref, g_post_ffn_ref, y_ref):
    tm = x_ref.shape[0]
    half = tm // 2
    halves = []
    for i in range(2):
        segs = list(range(i * n_seg // 2, (i + 1) * n_seg // 2)) if n_seg > 1 else [0]
        result = []
        halves.append((result, _stack_front(slice(i * half, (i + 1) * half), segs, x_ref, oa_ref, ob_ref, r_ref,
                                            mk_ref, mv_ref, g_gla_ref, wo_ref, g_post_mix_ref, g_pre_mem_ref,
                                            wq_ref, wmo_ref, g_post_mem_ref, result)))
    for _ in itertools.zip_longest(*[gen for _, gen in halves]):
        pass
    x2 = jnp.concatenate([result[0] for result, _ in halves], axis=0)
    _stack_ffn(x2, g_pre_ffn_ref, wg_ref, wu_ref, wd_ref, g_post_ffn_ref, y_ref)


def _stack(x, oa, ob, r, mk, mv, weights, tm, n_seg, tiles_per_mem_block):
    ntok = x.shape[0]
    row = lambda w: pl.BlockSpec((tm, w), lambda i: (i, 0))
    mem = pl.BlockSpec((n_seg * N_MEM, D_MODEL), lambda i: (i // tiles_per_mem_block, 0))
    w_specs = [_const_spec(w.shape) for w in weights]
    return pl.pallas_call(
        functools.partial(_stack_kernel, n_seg),
        grid=(ntok // tm,),
        in_specs=[row(D_MODEL), row(WIDTH_A), row(WIDTH_BV), row(WIDTH_BV), mem, mem] + w_specs,
        out_specs=row(D_MODEL),
        out_shape=jax.ShapeDtypeStruct((ntok, D_MODEL), F32),
        compiler_params=_params(1, VMEM_LIMIT),
        name="token_stack",
    )(x, oa, ob, r, mk, mv, *weights)


def _bias_base(table, width):
    u = np.arange(width)
    idx = np.where(u < (BAND_CHUNKS + 1) * CHUNK, np.clip(A_WINDOW - u, -REL_CLIP, REL_CLIP) + REL_CLIP, 2 * REL_CLIP)
    return table[:, idx]


def kernel(x_prompt, x_sample, mem_prompt, cache_a_k, cache_a_v, state_gla, cache_mem_k, cache_mem_v,
           g_pre_mix, w_in, rel_bias, w_alpha2, b_alpha, g_gla_out, w_o, g_post_mix,
           g_pre_mem, g_mem, w_mq, w_mk, w_mv, w_mo, g_post_mem,
           g_pre_ffn, w_ffn_gate, w_ffn_up, w_ffn_down, g_post_ffn):
    depth = w_in.shape[0]
    assert depth == 1
    l = 0
    n_p, seq, _ = x_prompt.shape
    n_s, seq_s, _ = x_sample.shape
    vec = lambda g: g[l].reshape(1, -1)

    wi = w_in[l]
    w_t = jnp.swapaxes(wi, 0, 1)
    wa2 = jnp.pad(w_alpha2[l], ((0, LANES - GATE_RANK), (0, 0))).astype(BF16)
    stack_w = [vec(g_gla_out), w_o[l].astype(BF16), vec(g_post_mix), vec(g_pre_mem), w_mq[l].astype(BF16),
               w_mo[l].astype(BF16), vec(g_post_mem), vec(g_pre_ffn), w_ffn_gate[l].astype(BF16),
               w_ffn_up[l].astype(BF16), w_ffn_down[l].astype(BF16), vec(g_post_ffn)]

    xp = x_prompt.reshape(n_p * seq, D_MODEL)
    keep = min(A_WINDOW, seq)
    qa, ka, va, qb, kb, vb, r, la, k_tail, v_tail = _proj(xp, vec(g_pre_mix), w_t, wa2, vec(b_alpha), TILES.proj_rows,
                                                          seq // TILES.proj_rows, keep, True)
    oa = _attn_prompt(qa, ka, va, _bias_base(rel_bias[l], KEYS), n_p, seq)
    ob, sp = _gla(qb, kb, vb, la, None, n_p, seq, CHUNK, TQ // CHUNK, n_p)
    mk, mv, mk16, mv16 = _mem_kv(mem_prompt.reshape(n_p * N_MEM, D_MODEL), vec(g_mem), w_mk[l], w_mv[l])
    yp = _stack(xp, oa, ob, r, mk16, mv16, stack_w, TILES.stack_rows, 1,
                seq // TILES.stack_rows)

    xs = x_sample.reshape(n_s * seq_s, D_MODEL)
    ntok_s = n_s * seq_s
    qa_s, ka_s, va_s, qb_s, kb_s, vb_s, r_s, la_s, k_new, v_new = _proj(
        xs, vec(g_pre_mix), w_t, wa2, vec(b_alpha), ntok_s, 1, ntok_s, False)
    oa_s = _attn_sample(qa_s, ka_s, va_s, jnp.transpose(cache_a_k[l], (0, 2, 3, 1)),
                        jnp.transpose(cache_a_v[l], (0, 2, 3, 1)), rel_bias[l], seq_s, TILES.sample_attn_requests)
    ob_s, ss = _gla(qb_s, kb_s, vb_s, la_s, state_gla[l], n_s, seq_s, seq_s, 1,
                    TILES.sample_gla_requests)
    seg_s = TILES.sample_stack_requests
    ys = _stack(xs, oa_s, ob_s, r_s,
                cache_mem_k[l].reshape(n_s * N_MEM, D_MODEL),
                cache_mem_v[l].reshape(n_s * N_MEM, D_MODEL), stack_w, seg_s * seq_s, seg_s, 1)

    heads = lambda z, n, t: z.reshape(1, n, t, N_HEADS_A, HEAD_DIM_A)
    tails = lambda z: jnp.transpose(z.reshape(n_p, N_HEADS_A, HEAD_DIM_A, keep), (0, 3, 1, 2))[None]
    return (yp.reshape(n_p, seq, D_MODEL), ys.reshape(n_s, seq_s, D_MODEL),
            tails(k_tail), tails(v_tail), sp[None],
            mk.reshape(1, n_p, N_MEM, N_HEADS_MEM, HEAD_DIM_MEM), mv.reshape(1, n_p, N_MEM, N_HEADS_MEM, HEAD_DIM_MEM),
            heads(k_new, n_s, seq_s), heads(v_new, n_s, seq_s), ss[None])
```

```python
import functools
import itertools
import math
from typing import NamedTuple

import jax
import jax.numpy as jnp
import numpy as np
from jax import lax
from jax.experimental import pallas as pl
from jax.experimental.pallas import tpu as pltpu

F32 = jnp.float32
BF16 = jnp.bfloat16

D_MODEL = 1024
CHUNK = 64
BAND_CHUNKS = 8
A_WINDOW = BAND_CHUNKS * CHUNK
N_HEADS_A = 8
HEAD_DIM_A = 64
WIDTH_A = 512
REL_CLIP = 256
N_HEADS_B = 4
KEY_DIM_B = 64
VAL_DIM_B = 128
WIDTH_BK = 256
WIDTH_BV = 512
GATE_RANK = 16
GATE_TAU = 16.0
N_MEM = 256
N_HEADS_MEM = 4
HEAD_DIM_MEM = 256
D_FF = 2816
EPS = 1e-6
NEG_BIG = -1e30
NT_DIMS = (((1,), (1,)), ((), ()))

LANES = 128
BIAS_W = 640
FF_BLOCK = 256
CAST_ROWS = 512
VMEM_LIMIT = 56 * 1024 * 1024


class _TilePlan(NamedTuple):
    proj_rows: int = 1024
    stack_rows: int = 512
    sample_attn_requests: int = 4
    sample_gla_requests: int = 8
    sample_stack_requests: int = 4


TILES = _TilePlan()

OFF_QA, OFF_KA, OFF_VA, OFF_QB, OFF_KB, OFF_VB, OFF_R = 0, 512, 1024, 1536, 1792, 2048, 2560
IN_WIDTH = OFF_R + GATE_RANK + WIDTH_BV


def _rms(x, g):
    return x * lax.rsqrt(jnp.mean(x * x, axis=-1, keepdims=True) + EPS) * g


def _silu(x):
    return x / (1.0 + jnp.exp(-x))


def _log_sigmoid(z):
    return jnp.minimum(z, 0.0) - jnp.log(1.0 + jnp.exp(-jnp.abs(z)))


def _const_spec(shape):
    nd = len(shape)
    return pl.BlockSpec(shape, lambda *_: (0,) * nd, pipeline_mode=pl.Buffered(1))


def _params(n_axes, vmem=None):
    return pltpu.CompilerParams(dimension_semantics=("arbitrary",) * n_axes, vmem_limit_bytes=vmem)


def _cast_rows(src_ref, dst_ref):
    n = src_ref.shape[0]
    for lo in range(0, n, CAST_ROWS):
        rows = slice(lo, min(lo + CAST_ROWS, n))
        dst_ref[rows, :] = src_ref[rows, :].astype(dst_ref.dtype)


def _proj_kernel(tiles_per_seq, tail_rows, tail_t, x_ref, g_ref, w32_ref, wa2_ref, ba_ref,
                 qa_ref, ka_ref, va_ref, qb_ref, kb_ref, vb_ref, r_ref, la_ref, kt_ref, vt_ref, w_ref):
    @pl.when(pl.program_id(0) == 0)
    def _():
        _cast_rows(w32_ref, w_ref)

    h = _rms(x_ref[...], g_ref[...]).astype(BF16)

    def proj(lo, hi):
        return lax.dot_general(h, w_ref[lo:hi, :], NT_DIMS, preferred_element_type=F32)

    g_low = proj(OFF_R, OFF_R + LANES).astype(BF16)
    z = jnp.dot(g_low, wa2_ref[...], preferred_element_type=F32) + ba_ref[...]
    la_ref[...] = _log_sigmoid(z) * (1.0 / GATE_TAU)
    qa_ref[...] = (proj(OFF_QA, OFF_KA) * (HEAD_DIM_A ** -0.5)).astype(BF16)
    ka = proj(OFF_KA, OFF_VA)
    va = proj(OFF_VA, OFF_QB)
    ka_ref[...] = ka.astype(BF16)
    va_ref[...] = va.astype(BF16)
    qb_ref[...] = proj(OFF_QB, OFF_KB) * (KEY_DIM_B ** -0.5)
    kb_ref[...] = proj(OFF_KB, OFF_VB)
    vb_ref[...] = proj(OFF_VB, OFF_R).astype(BF16)
    r_ref[...] = proj(OFF_R + GATE_RANK, IN_WIDTH)

    @pl.when(pl.program_id(0) % tiles_per_seq == tiles_per_seq - 1)
    def _():
        k_keep, v_keep = ka[ka.shape[0] - tail_rows:], va[va.shape[0] - tail_rows:]
        kt_ref[...] = k_keep.T if tail_t else k_keep
        vt_ref[...] = v_keep.T if tail_t else v_keep


def _proj(x, g, w_t, wa2, ba, tm, tiles_per_seq, tail_rows, tail_t):
    ntok = x.shape[0]
    nt = ntok // tm
    nseq = nt // tiles_per_seq
    row = lambda w: pl.BlockSpec((tm, w), lambda i: (i, 0))
    assert tail_rows <= tm
    tail_shape = (WIDTH_A, tail_rows) if tail_t else (tail_rows, WIDTH_A)
    tail = pl.BlockSpec(tail_shape, lambda i: (i // tiles_per_seq, 0))
    out_shape = (
        jax.ShapeDtypeStruct((ntok, WIDTH_A), BF16),
        jax.ShapeDtypeStruct((ntok, WIDTH_A), BF16),
        jax.ShapeDtypeStruct((ntok, WIDTH_A), BF16),
        jax.ShapeDtypeStruct((ntok, WIDTH_BK), F32),
        jax.ShapeDtypeStruct((ntok, WIDTH_BK), F32),
        jax.ShapeDtypeStruct((ntok, WIDTH_BV), BF16),
        jax.ShapeDtypeStruct((ntok, WIDTH_BV), F32),
        jax.ShapeDtypeStruct((ntok, WIDTH_BK), F32),
        jax.ShapeDtypeStruct((nseq * tail_shape[0], tail_shape[1]), F32),
        jax.ShapeDtypeStruct((nseq * tail_shape[0], tail_shape[1]), F32),
    )
    return pl.pallas_call(
        functools.partial(_proj_kernel, tiles_per_seq, tail_rows, tail_t),
        grid=(nt,),
        in_specs=[row(D_MODEL), _const_spec((1, D_MODEL)), _const_spec((IN_WIDTH, D_MODEL)),
                  _const_spec((LANES, WIDTH_BK)), _const_spec((1, WIDTH_BK))],
        out_specs=(row(WIDTH_A), row(WIDTH_A), row(WIDTH_A), row(WIDTH_BK), row(WIDTH_BK),
                   row(WIDTH_BV), row(WIDTH_BV), row(WIDTH_BK), tail, tail),
        out_shape=out_shape,
        scratch_shapes=[pltpu.VMEM((IN_WIDTH, D_MODEL), BF16)],
        compiler_params=_params(1, VMEM_LIMIT),
        name="proj",
    )(x, g, w_t, wa2, ba)


def _stack_heads(x_pair):
    low_half = lax.broadcasted_iota(jnp.int32, x_pair.shape, 1) < HEAD_DIM_A
    zero = jnp.zeros_like(x_pair)
    return jnp.concatenate([jnp.where(low_half, x_pair, zero), jnp.where(low_half, zero, x_pair)], axis=0)


TQ = 4 * CHUNK
KEYS = 3 * TQ
SOFT_W = 5 * LANES
assert 2 * TQ == A_WINDOW and SOFT_W >= (BAND_CHUNKS + 1) * CHUNK + CHUNK


def _build_band_bias(base_ref, bm_ref):
    n_band = (BAND_CHUNKS + 1) * CHUNK
    col = lax.broadcasted_iota(jnp.int32, (CHUNK, KEYS), 1)
    for h in range(N_HEADS_A):
        rows = jnp.broadcast_to(base_ref[h:h + 1, :], (CHUNK, KEYS))
        for qc in range(TQ // CHUNK):
            toeplitz = pltpu.roll(rows, qc * CHUNK, 1, stride=1, stride_axis=0)
            in_band = (col >= qc * CHUNK) & (col < qc * CHUNK + n_band)
            r0 = (qc * 2 + h % 2) * CHUNK
            bm_ref[h // 2, r0:r0 + CHUNK, :] = jnp.where(in_band, toeplitz, NEG_BIG)


def _band_tile(q_ref, k_refs, v_refs, bm_ref, o_ref, start_penalty):
    n_qc, n_pairs = TQ // CHUNK, N_HEADS_A // 2
    low_half = lax.broadcasted_iota(jnp.int32, (CHUNK, LANES), 1) < HEAD_DIM_A
    pair_cols = lambda hp: slice(hp * LANES, (hp + 1) * LANES)

    def scores(hp):
        cols = pair_cols(hp)
        qs = jnp.concatenate([_stack_heads(q_ref[qc * CHUNK:(qc + 1) * CHUNK, cols]) for qc in range(n_qc)],
                             axis=0)
        return jnp.concatenate([lax.dot_general(qs, k[:, cols], NT_DIMS, preferred_element_type=F32)
                                for k in k_refs], axis=1)

    def softmax(hp, s):
        probs, inv_l = [], []
        for qc in range(n_qc):
            rows = slice(qc * 2 * CHUNK, (qc + 1) * 2 * CHUNK)
            c0 = 0 if (qc + 1) * CHUNK + A_WINDOW <= SOFT_W else KEYS - SOFT_W
            sq = s[rows, c0:c0 + SOFT_W] + bm_ref[hp, rows, c0:c0 + SOFT_W]
            if start_penalty is not None:
                sq = sq + start_penalty[:, c0:c0 + SOFT_W]
            p = jnp.exp(sq - jnp.max(sq, axis=-1, keepdims=True))
            inv_l.append(1.0 / jnp.sum(p, axis=-1, keepdims=True))
            pad = jnp.zeros((2 * CHUNK, KEYS - SOFT_W), BF16)
            probs.append(jnp.concatenate([p.astype(BF16), pad] if c0 == 0 else [pad, p.astype(BF16)], axis=1))
        return jnp.concatenate(probs, axis=0), inv_l

    def values(hp, pm, inv_l):
        cols = pair_cols(hp)
        out = sum(jnp.dot(pm[:, i * TQ:(i + 1) * TQ], v[:, cols], preferred_element_type=F32)
                  for i, v in enumerate(v_refs))
        for qc in range(n_qc):
            r0 = qc * 2 * CHUNK
            o0 = out[r0:r0 + CHUNK] * inv_l[qc][0:CHUNK]
            o1 = out[r0 + CHUNK:r0 + 2 * CHUNK] * inv_l[qc][CHUNK:2 * CHUNK]
            o_ref[qc * CHUNK:(qc + 1) * CHUNK, cols] = jnp.where(low_half, o0, o1).astype(o_ref.dtype)

    for hp in range(n_pairs):
        s = scores(hp)
        yield
        pm, inv_l = softmax(hp, s)
        yield
        values(hp, pm, inv_l)
        yield


def _attn_prompt_kernel(q_ref, k2_ref, k1_ref, k0_ref, v2_ref, v1_ref, v0_ref, base_ref, o_ref, bm):
    t = pl.program_id(1)

    @pl.when((pl.program_id(0) == 0) & (t == 0))
    def _():
        _build_band_bias(base_ref, bm)

    def tiles(penalty):
        gens = [_band_tile(q_ref.at[i], (k2_ref.at[i], k1_ref.at[i], k0_ref.at[i]),
                           (v2_ref.at[i], v1_ref.at[i], v0_ref.at[i]), bm, o_ref.at[i], penalty)
                for i in range(q_ref.shape[0])]
        for _ in itertools.zip_longest(*gens):
            pass

    @pl.when(t >= 2)
    def _():
        tiles(None)

    @pl.when(t < 2)
    def _():
        col = lax.broadcasted_iota(jnp.int32, (1, KEYS), 1)
        tiles(jnp.where(col < (2 - t) * TQ, NEG_BIG, 0.0))


def _attn_prompt(q, k, v, base, n_batch, seq):
    nt = seq // TQ
    n_par = n_batch
    blk = lambda back: pl.BlockSpec((n_par, TQ, WIDTH_A), lambda g, t: (g, jnp.maximum(t - back, 0), 0))
    per_seq = lambda z: z.reshape(n_batch, seq, WIDTH_A)
    q, k, v = per_seq(q), per_seq(k), per_seq(v)
    return pl.pallas_call(
        _attn_prompt_kernel,
        grid=(n_batch // n_par, nt),
        in_specs=[blk(0), blk(2), blk(1), blk(0), blk(2), blk(1), blk(0), _const_spec((N_HEADS_A, KEYS))],
        out_specs=blk(0),
        out_shape=jax.ShapeDtypeStruct((n_batch, seq, WIDTH_A), BF16),
        scratch_shapes=[pltpu.VMEM((N_HEADS_A // 2, 2 * TQ, KEYS), F32)],
        compiler_params=_params(2, VMEM_LIMIT),
        name="band_attn_prompt",
    )(q, k, k, k, v, v, v, base).reshape(n_batch * seq, WIDTH_A)


def _attn_sample_one(rows, q_ref, kn_ref, vn_ref, kc_ref, vc_ref, sel_ref, dup_ref, o_ref, bias_c, bias_n):
    n_h = N_HEADS_A
    n_new = rows.stop - rows.start
    low_half = lax.broadcasted_iota(jnp.int32, (n_new, LANES), 1) < HEAD_DIM_A
    pairs = [slice(hp * LANES, (hp + 1) * LANES) for hp in range(n_h // 2)]
    q_heads = [jnp.dot(q_ref[rows, cols], sel_ref[e], preferred_element_type=F32).astype(BF16)
               for cols in pairs for e in range(2)]
    yield
    s_c = jnp.concatenate([jnp.dot(q_heads[h], kc_ref[h].astype(BF16), preferred_element_type=F32)
                           for h in range(n_h)], axis=0) + bias_c[...]
    s_n = jnp.concatenate([lax.dot_general(_stack_heads(q_ref[rows, cols]), kn_ref[rows, cols], NT_DIMS,
                                           preferred_element_type=F32) for cols in pairs], axis=0) + bias_n[...]
    yield
    m = jnp.maximum(jnp.max(s_c, axis=-1, keepdims=True), jnp.max(s_n, axis=-1, keepdims=True))
    p_c, p_n = jnp.exp(s_c - m), jnp.exp(s_n - m)
    inv_l = 1.0 / (jnp.sum(p_c, axis=-1, keepdims=True) + jnp.sum(p_n, axis=-1, keepdims=True))
    p_c, p_n = p_c.astype(BF16), p_n.astype(BF16)
    yield
    out_c = jnp.concatenate([lax.dot_general(p_c[h * n_new:(h + 1) * n_new], vc_ref[h].astype(BF16), NT_DIMS,
                                             preferred_element_type=F32) for h in range(n_h)], axis=0) * inv_l
    yield
    hi = out_c.astype(BF16)
    lo = (out_c - hi.astype(F32)).astype(BF16)
    out_c = (jnp.dot(hi, dup_ref[...], preferred_element_type=F32)
             + jnp.dot(lo, dup_ref[...], preferred_element_type=F32))
    for hp, cols in enumerate(pairs):
        pr = slice(hp * 2 * n_new, (hp + 1) * 2 * n_new)
        out = out_c[pr] + jnp.dot(p_n[pr], vn_ref[rows, cols], preferred_element_type=F32) * inv_l[pr]
        o_ref[rows, cols] = jnp.where(low_half, out[0:n_new], out[n_new:]).astype(o_ref.dtype)


def _attn_sample_kernel(n_new, q_ref, kn_ref, vn_ref, kc_ref, vc_ref, base_ref, sel_ref, dup_ref,
                        o_ref, bias_c, bias_n):
    n_cache = kc_ref.shape[3]

    @pl.when(pl.program_id(0) == 0)
    def _():
        for h in range(N_HEADS_A):
            rows = jnp.broadcast_to(base_ref[h:h + 1, :], (n_new, BIAS_W))
            toeplitz = pltpu.roll(rows, 0, 1, stride=1, stride_axis=0)
            bias_c[h * n_new:(h + 1) * n_new, :] = toeplitz[:, 0:n_cache]
            bias_n[h * n_new:(h + 1) * n_new, :] = toeplitz[:, n_cache:n_cache + n_new]

    requests = [_attn_sample_one(slice(i * n_new, (i + 1) * n_new), q_ref, kn_ref, vn_ref, kc_ref.at[i],
                                 vc_ref.at[i], sel_ref, dup_ref, o_ref, bias_c, bias_n)
                for i in range(kc_ref.shape[0])]
    for _ in itertools.zip_longest(*requests):
        pass


def _attn_sample(q, k_new, v_new, k_cache_t, v_cache_t, table, n_new, n_par):
    n_batch, n_h, _, n_cache = k_cache_t.shape
    assert n_cache == A_WINDOW and n_new <= CHUNK and n_h == N_HEADS_A
    lane = np.arange(LANES)
    sel = np.stack([lane[:, None] == np.arange(HEAD_DIM_A)[None, :] + e * HEAD_DIM_A for e in range(2)])
    dup = np.arange(HEAD_DIM_A)[:, None] == lane[None, :] % HEAD_DIM_A
    new = pl.BlockSpec((n_par * n_new, WIDTH_A), lambda g: (g, 0))
    old = pl.BlockSpec((n_par, n_h, HEAD_DIM_A, n_cache), lambda g: (g, 0, 0, 0))
    return pl.pallas_call(
        functools.partial(_attn_sample_kernel, n_new),
        grid=(n_batch // n_par,),
        in_specs=[new, new, new, old, old, _const_spec((n_h, BIAS_W)),
                  _const_spec((2, LANES, HEAD_DIM_A)), _const_spec((HEAD_DIM_A, LANES))],
        out_specs=new,
        out_shape=jax.ShapeDtypeStruct((n_batch * n_new, WIDTH_A), BF16),
        scratch_shapes=[pltpu.VMEM((n_h * n_new, n_cache), F32), pltpu.VMEM((n_h * n_new, n_new), F32)],
        compiler_params=_params(1, VMEM_LIMIT),
        name="band_attn_sample",
    )(q, k_new, v_new, k_cache_t, v_cache_t, _bias_base(table, BIAS_W), jnp.asarray(sel, BF16),
      jnp.asarray(dup, BF16))


def _gla_levels(chunk):
    return [chunk >> i for i in range(int(math.log2(chunk)) + 1)]


def _segment_matrix(chunk):
    i = np.arange(chunk)[:, None]
    t = np.arange(chunk)[None, :]
    blocks = []
    for li, s in enumerate(_gla_levels(chunk)):
        start = (i // s) * s
        f_rows = (t >= start) & (t <= i)
        r_rows = (t > i) & (t <= start + s - 1)
        if li == 0:
            blocks += [f_rows, r_rows]
        else:
            blocks.append(np.where((i // s) % 2 == 1, f_rows, r_rows))
    seg = np.concatenate(blocks, axis=0).astype(np.float32)
    return np.concatenate([seg, seg], axis=1)


def _level_masks(group_tokens):
    tg = group_tokens
    i = (np.arange(2 * tg) % tg)[:, None]
    j = np.arange(tg)[None, :]
    masks = [i == j]
    s = tg // 2
    while s >= 1:
        masks.append(((i // s) % 2 == 1) & (j // s == i // s - 1))
        s //= 2
    return np.stack(masks).astype(np.float32)


def _gla_tile(q, k, v, la, seg_ref, mask_ref, state_ref, o_ref, chunk, n_chunks):
    n, tg = n_chunks, chunk * n_chunks
    levels = _gla_levels(chunk)
    la_hi = la.astype(BF16)
    la_lo = (la - la_hi.astype(F32)).astype(BF16)
    split = jnp.concatenate(
        [jnp.concatenate([la_hi[c * chunk:(c + 1) * chunk], la_lo[c * chunk:(c + 1) * chunk]], axis=0)
         for c in range(n)], axis=1)
    sums = jnp.dot(seg_ref[...], split, preferred_element_type=F32)

    def seg_sum(block):
        rows = slice(block * chunk, (block + 1) * chunk)
        return jnp.minimum(jnp.concatenate(
            [sums[rows, c * WIDTH_BK:(c + 1) * WIDTH_BK] for c in range(n)], axis=0), 0.0)

    fwd0, rev0 = seg_sum(0), seg_sum(1)
    from_start, to_end = jnp.exp(fwd0), jnp.exp(rev0)
    total = [fwd0[(c + 1) * chunk - 1:(c + 1) * chunk] for c in range(n)]

    def chunk_sum(cs):
        cs = list(cs)
        return (sum(total[c] for c in cs[1:]) + total[cs[0]]) if cs else None

    def extend(base, offsets):
        pieces = []
        for c in range(n):
            x = base[c * chunk:(c + 1) * chunk]
            pieces.append(x if offsets[c] is None else x * jnp.exp(offsets[c]))
        return jnp.concatenate(pieces, axis=0) if n > 1 else pieces[0]

    q_state = (q * extend(from_start, [chunk_sum(range(0, c)) for c in range(n)])).astype(BF16)
    k_state = (k * extend(to_end, [chunk_sum(range(c + 1, n)) for c in range(n)])).astype(BF16)
    decay_tile = jnp.exp(chunk_sum(range(n)))
    ops = {}
    s = tg // 2
    while s >= chunk and n > 1:
        per = s // chunk
        offsets, bases = [], []
        for c in range(n):
            sg = c // per
            offsets.append(chunk_sum(range(sg * per, c)) if sg % 2 else chunk_sum(range(c + 1, (sg + 1) * per)))
            bases.append((from_start if sg % 2 else to_end)[c * chunk:(c + 1) * chunk])
        w = extend(jnp.concatenate(bases, axis=0), offsets)
        ops[s] = ((q * w).astype(BF16), (k * w).astype(BF16))
        s //= 2
    for li in range(1, len(levels)):
        w = jnp.exp(seg_sum(li + 1))
        ops[levels[li]] = ((q * w).astype(BF16), (k * w).astype(BF16))
    q16, k16 = q.astype(BF16), k.astype(BF16)
    yield

    gt = mask_ref.shape[2]
    nt_dims = (((1,), (1,)), ((), ()))
    tn_dims = (((0,), (0,)), ((), ()))
    n_pairs, n_groups = N_HEADS_B // 2, tg // gt
    pair_cols = [slice(p * LANES, (p + 1) * LANES) for p in range(n_pairs)]
    pair_v = [v[:, p * 2 * VAL_DIM_B:(p + 1) * 2 * VAL_DIM_B] for p in range(n_pairs)]
    head_v = lambda e: slice(e * VAL_DIM_B, (e + 1) * VAL_DIM_B)
    o_state = []
    for p, cols in enumerate(pair_cols):
        s0, s1 = state_ref[2 * p], state_ref[2 * p + 1]
        zero = jnp.zeros_like(s0)
        st = jnp.concatenate([jnp.concatenate([s0, zero], axis=1),
                              jnp.concatenate([zero, s1], axis=1)], axis=0)
        o_state.append(jnp.dot(q_state[:, cols], st.astype(BF16), preferred_element_type=F32))
        upd = lax.dot_general(k_state[:, cols], pair_v[p], tn_dims, preferred_element_type=F32)
        decay_rows = jnp.broadcast_to(decay_tile[:, cols], (LANES, LANES)).T
        state_ref[2 * p] = s0 * decay_rows[0:KEY_DIM_B] + upd[0:KEY_DIM_B, 0:VAL_DIM_B]
        state_ref[2 * p + 1] = s1 * decay_rows[KEY_DIM_B:] + upd[KEY_DIM_B:, VAL_DIM_B:]
    yield
    blocks = []
    for p, cols in enumerate(pair_cols):
        for g in range(n_groups):
            rows = slice(g * gt, (g + 1) * gt)
            a = mask_ref[0] * lax.dot_general(_stack_heads(q16[rows, cols]), k16[rows, cols], nt_dims,
                                              preferred_element_type=F32)
            m, s = 1, gt // 2
            while s >= 1:
                qd, kd = ops[s]
                a = a + mask_ref[m] * lax.dot_general(_stack_heads(qd[rows, cols]), kd[rows, cols], nt_dims,
                                                      preferred_element_type=F32)
                m, s = m + 1, s // 2
            blocks.append((p, g * gt, g * gt, a.astype(BF16)))
            yield
        s = tg // 2
        while s >= gt:
            qd, kd = ops[s]
            for blk in range(tg // (2 * s)):
                k0, q0 = blk * 2 * s, blk * 2 * s + s
                ab = lax.dot_general(_stack_heads(qd[q0:q0 + s, cols]), kd[k0:k0 + s, cols], nt_dims,
                                     preferred_element_type=F32)
                blocks.append((p, q0, k0, ab.astype(BF16)))
                yield
            s //= 2
    piece = [[[o_state[p][g * gt:(g + 1) * gt, head_v(e)] for g in range(n_groups)] for e in range(2)]
             for p in range(n_pairs)]
    for p, q0, k0, a in blocks:
        nq, nk = a.shape[0] // 2, a.shape[1]
        for e in range(2):
            contrib = jnp.dot(a[e * nq:(e + 1) * nq], pair_v[p][k0:k0 + nk, head_v(e)], preferred_element_type=F32)
            for gi in range(nq // gt):
                g = q0 // gt + gi
                piece[p][e][g] = piece[p][e][g] + contrib[gi * gt:(gi + 1) * gt]
        yield
    outs = [jnp.concatenate(piece[p][e], axis=0) if n_groups > 1 else piece[p][e][0]
            for p in range(n_pairs) for e in range(2)]
    o_ref[...] = jnp.concatenate(outs, axis=1)


def _gla_kernel(chunk, chunks_per_tile, has_init, *refs):
    if has_init:
        q_ref, k_ref, v_ref, la_ref, seg_ref, mask_ref, s0_ref, o_ref, sout_ref, state = refs
    else:
        q_ref, k_ref, v_ref, la_ref, seg_ref, mask_ref, o_ref, sout_ref, state = refs
    t = pl.program_id(1)

    @pl.when(t == 0)
    def _():
        state[...] = s0_ref[...] if has_init else jnp.zeros_like(state)

    tiles = [_gla_tile(q_ref[i], k_ref[i], v_ref[i], la_ref[i], seg_ref, mask_ref, state.at[i], o_ref.at[i],
                       chunk, chunks_per_tile) for i in range(q_ref.shape[0])]
    for _ in itertools.zip_longest(*tiles):
        pass

    @pl.when(t == pl.num_programs(1) - 1)
    def _():
        sout_ref[...] = state[...]


def _gla(q, k, v, la, s0, n_batch, seq, chunk, chunks_per_tile, n_par):
    tg = chunk * chunks_per_tile
    nt = seq // tg
    seg = jnp.asarray(_segment_matrix(chunk), BF16)
    masks = jnp.asarray(_level_masks(min(tg, LANES)), F32)
    row = lambda w: pl.BlockSpec((n_par, tg, w), lambda g, t: (g, t, 0))
    st = pl.BlockSpec((n_par, N_HEADS_B, KEY_DIM_B, VAL_DIM_B), lambda g, t: (g, 0, 0, 0))
    per_seq = lambda z: z.reshape(n_batch, seq, z.shape[-1])
    in_specs = [row(WIDTH_BK), row(WIDTH_BK), row(WIDTH_BV), row(WIDTH_BK), _const_spec(seg.shape),
                _const_spec(masks.shape)]
    args = [per_seq(q), per_seq(k), per_seq(v), per_seq(la), seg, masks]
    if s0 is not None:
        in_specs.append(st)
        args.append(s0)
    o, s_out = pl.pallas_call(
        functools.partial(_gla_kernel, chunk, chunks_per_tile, s0 is not None),
        grid=(n_batch // n_par, nt),
        in_specs=in_specs,
        out_specs=(row(WIDTH_BV), st),
        out_shape=(jax.ShapeDtypeStruct((n_batch, seq, WIDTH_BV), F32),
                   jax.ShapeDtypeStruct((n_batch, N_HEADS_B, KEY_DIM_B, VAL_DIM_B), F32)),
        scratch_shapes=[pltpu.VMEM((n_par, N_HEADS_B, KEY_DIM_B, VAL_DIM_B), F32)],
        compiler_params=_params(2, VMEM_LIMIT),
        name="gla_chunk%d" % chunk,
    )(*args)
    return o.reshape(n_batch * seq, WIDTH_BV), s_out


def _mem_kv_kernel(m_ref, g_ref, wk32_ref, wv32_ref, k_ref, v_ref, k16_ref, v16_ref, wk_ref, wv_ref):
    @pl.when(pl.program_id(0) == 0)
    def _():
        _cast_rows(wk32_ref, wk_ref)
        _cast_rows(wv32_ref, wv_ref)

    m = _rms(m_ref[...], g_ref[...]).astype(BF16)
    k = jnp.dot(m, wk_ref[...], preferred_element_type=F32)
    v = jnp.dot(m, wv_ref[...], preferred_element_type=F32)
    k_ref[...], v_ref[...] = k, v
    k16_ref[...], v16_ref[...] = k.astype(BF16), v.astype(BF16)


def _mem_kv(mem, g, wk, wv):
    n = mem.shape[0]
    tm = N_MEM
    row = pl.BlockSpec((tm, D_MODEL), lambda i: (i, 0))
    return pl.pallas_call(
        _mem_kv_kernel,
        grid=(n // tm,),
        in_specs=[row, _const_spec((1, D_MODEL)), _const_spec((D_MODEL, D_MODEL)), _const_spec((D_MODEL, D_MODEL))],
        out_specs=(row, row, row, row),
        out_shape=(jax.ShapeDtypeStruct((n, D_MODEL), F32), jax.ShapeDtypeStruct((n, D_MODEL), F32),
                   jax.ShapeDtypeStruct((n, D_MODEL), BF16), jax.ShapeDtypeStruct((n, D_MODEL), BF16)),
        scratch_shapes=[pltpu.VMEM((D_MODEL, D_MODEL), BF16), pltpu.VMEM((D_MODEL, D_MODEL), BF16)],
        compiler_params=_params(1, VMEM_LIMIT),
        name="mem_kv",
    )(mem, g, wk, wv)


def _stack_front(rows, segs, x_ref, oa_ref, ob_ref, r_ref, mk_ref, mv_ref, g_gla_ref, wo_ref, g_post_mix_ref,
                 g_pre_mem_ref, wq_ref, wmo_ref, g_post_mem_ref, result):
    n_rows = rows.stop - rows.start
    ob = ob_ref[rows, :]
    normed = []
    for h in range(N_HEADS_B):
        seg = ob[:, h * VAL_DIM_B:(h + 1) * VAL_DIM_B]
        normed.append(seg * lax.rsqrt(jnp.mean(seg * seg, axis=-1, keepdims=True) + EPS))
    yb = (jnp.concatenate(normed, axis=1) * g_gla_ref[...] * _silu(r_ref[rows, :])).astype(BF16)
    yield
    mix = (jnp.dot(oa_ref[rows, :], wo_ref[0:WIDTH_A, :], preferred_element_type=F32)
           + jnp.dot(yb, wo_ref[WIDTH_A:, :], preferred_element_type=F32))
    yield
    x1 = x_ref[rows, :] + _rms(mix, g_post_mix_ref[...])
    hq = _rms(x1, g_pre_mem_ref[...]).astype(BF16)
    yield
    q = (jnp.dot(hq, wq_ref[...], preferred_element_type=F32) * (HEAD_DIM_MEM ** -0.5)).astype(BF16)
    yield
    rows_per_seg = n_rows // len(segs)
    units = [(u, sg, h) for u, sg in enumerate(segs) for h in range(N_HEADS_MEM)]
    mem_rows = lambda sg: slice(sg * N_MEM, (sg + 1) * N_MEM)
    head_cols = lambda h: slice(h * HEAD_DIM_MEM, (h + 1) * HEAD_DIM_MEM)
    scores = [lax.dot_general(q[u * rows_per_seg:(u + 1) * rows_per_seg, head_cols(h)],
                              mk_ref[mem_rows(sg), head_cols(h)].astype(BF16), NT_DIMS,
                              preferred_element_type=F32) for u, sg, h in units]
    yield
    probs, inv_l = [], []
    for s in scores:
        p = jnp.exp(s - jnp.max(s, axis=-1, keepdims=True))
        inv_l.append(1.0 / jnp.sum(p, axis=-1, keepdims=True))
        probs.append(p.astype(BF16))
    yield
    outs = [jnp.dot(probs[i], mv_ref[mem_rows(sg), head_cols(h)].astype(BF16), preferred_element_type=F32) * inv_l[i]
            for i, (u, sg, h) in enumerate(units)]
    seg_outs = [jnp.concatenate(outs[u * N_HEADS_MEM:(u + 1) * N_HEADS_MEM], axis=1) for u in range(len(segs))]
    o = (jnp.concatenate(seg_outs, axis=0) if len(segs) > 1 else seg_outs[0]).astype(BF16)
    yield
    att = jnp.dot(o, wmo_ref[...], preferred_element_type=F32)
    yield
    result.append(x1 + _rms(att, g_post_mem_ref[...]))


def _stack_ffn(x2, g_pre_ffn_ref, wg_ref, wu_ref, wd_ref, g_post_ffn_ref, y_ref):
    hf = _rms(x2, g_pre_ffn_ref[...]).astype(BF16)
    f = jnp.zeros(x2.shape, F32)
    act, act_cols = None, None
    for c in range(D_FF // FF_BLOCK):
        cols = slice(c * FF_BLOCK, (c + 1) * FF_BLOCK)
        gate = jnp.dot(hf, wg_ref[:, cols], preferred_element_type=F32)
        up = jnp.dot(hf, wu_ref[:, cols], preferred_element_type=F32)
        if act is not None:
            f = f + jnp.dot(act, wd_ref[act_cols, :], preferred_element_type=F32)
        act, act_cols = (_silu(gate) * up).astype(BF16), cols
        yield
    f = f + jnp.dot(act, wd_ref[act_cols, :], preferred_element_type=F32)
    yield
    y_ref[...] = x2 + _rms(f, g_post_ffn_ref[...])


def _stack_kernel(n_seg, x_ref, oa_ref, ob_ref, r_ref, mk_ref, mv_ref,
                  g_gla_ref, wo_ref, g_post_mix_ref, g_pre_mem_ref, wq_ref, wmo_ref, g_post_mem_ref,
                  g_pre_ffn_ref, wg_ref, wu_ref, wd_ref, g_post_ffn_ref, y_ref):
    tm = x_ref.shape[0]
    half = tm // 2
    halves = []
    for i in range(2):
        segs = list(range(i * n_seg // 2, (i + 1) * n_seg // 2)) if n_seg > 1 else [0]
        result = []
        halves.append((result, _stack_front(slice(i * half, (i + 1) * half), segs, x_ref, oa_ref, ob_ref, r_ref,
                                            mk_ref, mv_ref, g_gla_ref, wo_ref, g_post_mix_ref, g_pre_mem_ref,
                                            wq_ref, wmo_ref, g_post_mem_ref, result)))
    for _ in itertools.zip_longest(*[gen for _, gen in halves]):
        pass
    ffns = [_stack_ffn(result[0], g_pre_ffn_ref, wg_ref, wu_ref, wd_ref, g_post_ffn_ref,
                       y_ref.at[i * half:(i + 1) * half]) for i, (result, _) in enumerate(halves)]
    for _ in itertools.zip_longest(*ffns):
        pass


def _stack(x, oa, ob, r, mk, mv, weights, tm, n_seg, tiles_per_mem_block):
    ntok = x.shape[0]
    row = lambda w: pl.BlockSpec((tm, w), lambda i: (i, 0))
    mem = pl.BlockSpec((n_seg * N_MEM, D_MODEL), lambda i: (i // tiles_per_mem_block, 0))
    w_specs = [_const_spec(w.shape) for w in weights]
    return pl.pallas_call(
        functools.partial(_stack_kernel, n_seg),
        grid=(ntok // tm,),
        in_specs=[row(D_MODEL), row(WIDTH_A), row(WIDTH_BV), row(WIDTH_BV), mem, mem] + w_specs,
        out_specs=row(D_MODEL),
        out_shape=jax.ShapeDtypeStruct((ntok, D_MODEL), F32),
        compiler_params=_params(1, VMEM_LIMIT),
        name="token_stack",
    )(x, oa, ob, r, mk, mv, *weights)


def _bias_base(table, width):
    u = np.arange(width)
    idx = np.where(u < (BAND_CHUNKS + 1) * CHUNK, np.clip(A_WINDOW - u, -REL_CLIP, REL_CLIP) + REL_CLIP, 2 * REL_CLIP)
    return table[:, idx]


def kernel(x_prompt, x_sample, mem_prompt, cache_a_k, cache_a_v, state_gla, cache_mem_k, cache_mem_v,
           g_pre_mix, w_in, rel_bias, w_alpha2, b_alpha, g_gla_out, w_o, g_post_mix,
           g_pre_mem, g_mem, w_mq, w_mk, w_mv, w_mo, g_post_mem,
           g_pre_ffn, w_ffn_gate, w_ffn_up, w_ffn_down, g_post_ffn):
    depth = w_in.shape[0]
    assert depth == 1
    l = 0
    n_p, seq, _ = x_prompt.shape
    n_s, seq_s, _ = x_sample.shape
    vec = lambda g: g[l].reshape(1, -1)

    wi = w_in[l]
    w_t = jnp.swapaxes(wi, 0, 1)
    wa2 = jnp.pad(w_alpha2[l], ((0, LANES - GATE_RANK), (0, 0))).astype(BF16)
    stack_w = [vec(g_gla_out), w_o[l].astype(BF16), vec(g_post_mix), vec(g_pre_mem), w_mq[l].astype(BF16),
               w_mo[l].astype(BF16), vec(g_post_mem), vec(g_pre_ffn), w_ffn_gate[l].astype(BF16),
               w_ffn_up[l].astype(BF16), w_ffn_down[l].astype(BF16), vec(g_post_ffn)]

    xp = x_prompt.reshape(n_p * seq, D_MODEL)
    keep = min(A_WINDOW, seq)
    qa, ka, va, qb, kb, vb, r, la, k_tail, v_tail = _proj(xp, vec(g_pre_mix), w_t, wa2, vec(b_alpha), TILES.proj_rows,
                                                          seq // TILES.proj_rows, keep, True)
    oa = _attn_prompt(qa, ka, va, _bias_base(rel_bias[l], KEYS), n_p, seq)
    ob, sp = _gla(qb, kb, vb, la, None, n_p, seq, CHUNK, TQ // CHUNK, n_p)
    mk, mv, mk16, mv16 = _mem_kv(mem_prompt.reshape(n_p * N_MEM, D_MODEL), vec(g_mem), w_mk[l], w_mv[l])
    yp = _stack(xp, oa, ob, r, mk16, mv16, stack_w, TILES.stack_rows, 1,
                seq // TILES.stack_rows)

    xs = x_sample.reshape(n_s * seq_s, D_MODEL)
    ntok_s = n_s * seq_s
    qa_s, ka_s, va_s, qb_s, kb_s, vb_s, r_s, la_s, k_new, v_new = _proj(
        xs, vec(g_pre_mix), w_t, wa2, vec(b_alpha), ntok_s, 1, ntok_s, False)
    oa_s = _attn_sample(qa_s, ka_s, va_s, jnp.transpose(cache_a_k[l], (0, 2, 3, 1)),
                        jnp.transpose(cache_a_v[l], (0, 2, 3, 1)), rel_bias[l], seq_s, TILES.sample_attn_requests)
    ob_s, ss = _gla(qb_s, kb_s, vb_s, la_s, state_gla[l], n_s, seq_s, seq_s, 1,
                    TILES.sample_gla_requests)
    seg_s = TILES.sample_stack_requests
    ys = _stack(xs, oa_s, ob_s, r_s,
                cache_mem_k[l].reshape(n_s * N_MEM, D_MODEL),
                cache_mem_v[l].reshape(n_s * N_MEM, D_MODEL), stack_w, seg_s * seq_s, seg_s, 1)

    heads = lambda z, n, t: z.reshape(1, n, t, N_HEADS_A, HEAD_DIM_A)
    tails = lambda z: jnp.transpose(z.reshape(n_p, N_HEADS_A, HEAD_DIM_A, keep), (0, 3, 1, 2))[None]
    return (yp.reshape(n_p, seq, D_MODEL), ys.reshape(n_s, seq_s, D_MODEL),
            tails(k_tail), tails(v_tail), sp[None],
            mk.reshape(1, n_p, N_MEM, N_HEADS_MEM, HEAD_DIM_MEM), mv.reshape(1, n_p, N_MEM, N_HEADS_MEM, HEAD_DIM_MEM),
            heads(k_new, n_s, seq_s), heads(v_new, n_s, seq_s), ss[None])
```

```python
import functools
import itertools
import math
from typing import NamedTuple

import jax
import jax.numpy as jnp
import numpy as np
from jax import lax
from jax.experimental import pallas as pl
from jax.experimental.pallas import tpu as pltpu

F32 = jnp.float32
BF16 = jnp.bfloat16

D_MODEL = 1024
CHUNK = 64
BAND_CHUNKS = 8
A_WINDOW = BAND_CHUNKS * CHUNK
N_HEADS_A = 8
HEAD_DIM_A = 64
WIDTH_A = 512
REL_CLIP = 256
N_HEADS_B = 4
KEY_DIM_B = 64
VAL_DIM_B = 128
WIDTH_BK = 256
WIDTH_BV = 512
GATE_RANK = 16
GATE_TAU = 16.0
N_MEM = 256
N_HEADS_MEM = 4
HEAD_DIM_MEM = 256
D_FF = 2816
EPS = 1e-6
NEG_BIG = -1e30
NT_DIMS = (((1,), (1,)), ((), ()))

LANES = 128
BIAS_W = 640
FF_BLOCK = 256
CAST_ROWS = 512
MIN_SPLIT_ROWS = 256
VMEM_LIMIT = 56 * 1024 * 1024


class _TilePlan(NamedTuple):
    proj_rows: int = 1024
    stack_rows: int = 512
    sample_attn_requests: int = 4
    sample_gla_requests: int = 8
    sample_stack_requests: int = 4


TILES = _TilePlan()

OFF_QA, OFF_KA, OFF_VA, OFF_QB, OFF_KB, OFF_VB, OFF_R = 0, 512, 1024, 1536, 1792, 2048, 2560
IN_WIDTH = OFF_R + GATE_RANK + WIDTH_BV


def _rms(x, g):
    return x * lax.rsqrt(jnp.mean(x * x, axis=-1, keepdims=True) + EPS) * g


def _silu(x):
    return x / (1.0 + jnp.exp(-x))


def _log_sigmoid(z):
    return jnp.minimum(z, 0.0) - jnp.log(1.0 + jnp.exp(-jnp.abs(z)))


def _const_spec(shape):
    nd = len(shape)
    return pl.BlockSpec(shape, lambda *_: (0,) * nd, pipeline_mode=pl.Buffered(1))


def _params(n_axes, vmem=None):
    return pltpu.CompilerParams(dimension_semantics=("arbitrary",) * n_axes, vmem_limit_bytes=vmem)


def _cast_rows(src_ref, dst_ref):
    n = src_ref.shape[0]
    for lo in range(0, n, CAST_ROWS):
        rows = slice(lo, min(lo + CAST_ROWS, n))
        dst_ref[rows, :] = src_ref[rows, :].astype(dst_ref.dtype)


def _proj_kernel(tiles_per_seq, tail_rows, tail_t, x_ref, g_ref, w32_ref, wa2_ref, ba_ref,
                 qa_ref, ka_ref, va_ref, qb_ref, kb_ref, vb_ref, r_ref, la_ref, kt_ref, vt_ref, w_ref):
    @pl.when(pl.program_id(0) == 0)
    def _():
        _cast_rows(w32_ref, w_ref)

    h = _rms(x_ref[...], g_ref[...]).astype(BF16)

    def proj(lo, hi):
        return lax.dot_general(h, w_ref[lo:hi, :], NT_DIMS, preferred_element_type=F32)

    g_low = proj(OFF_R, OFF_R + LANES).astype(BF16)
    z = jnp.dot(g_low, wa2_ref[...], preferred_element_type=F32) + ba_ref[...]
    la_ref[...] = _log_sigmoid(z) * (1.0 / GATE_TAU)
    qa_ref[...] = (proj(OFF_QA, OFF_KA) * (HEAD_DIM_A ** -0.5)).astype(BF16)
    ka = proj(OFF_KA, OFF_VA)
    va = proj(OFF_VA, OFF_QB)
    ka_ref[...] = ka.astype(BF16)
    va_ref[...] = va.astype(BF16)
    qb_ref[...] = proj(OFF_QB, OFF_KB) * (KEY_DIM_B ** -0.5)
    kb_ref[...] = proj(OFF_KB, OFF_VB)
    vb_ref[...] = proj(OFF_VB, OFF_R).astype(BF16)
    r_ref[...] = proj(OFF_R + GATE_RANK, IN_WIDTH)

    @pl.when(pl.program_id(0) % tiles_per_seq == tiles_per_seq - 1)
    def _():
        k_keep, v_keep = ka[ka.shape[0] - tail_rows:], va[va.shape[0] - tail_rows:]
        kt_ref[...] = k_keep.T if tail_t else k_keep
        vt_ref[...] = v_keep.T if tail_t else v_keep


def _proj(x, g, w_t, wa2, ba, tm, tiles_per_seq, tail_rows, tail_t):
    ntok = x.shape[0]
    nt = ntok // tm
    nseq = nt // tiles_per_seq
    row = lambda w: pl.BlockSpec((tm, w), lambda i: (i, 0))
    assert tail_rows <= tm
    tail_shape = (WIDTH_A, tail_rows) if tail_t else (tail_rows, WIDTH_A)
    tail = pl.BlockSpec(tail_shape, lambda i: (i // tiles_per_seq, 0))
    out_shape = (
        jax.ShapeDtypeStruct((ntok, WIDTH_A), BF16),
        jax.ShapeDtypeStruct((ntok, WIDTH_A), BF16),
        jax.ShapeDtypeStruct((ntok, WIDTH_A), BF16),
        jax.ShapeDtypeStruct((ntok, WIDTH_BK), F32),
        jax.ShapeDtypeStruct((ntok, WIDTH_BK), F32),
        jax.ShapeDtypeStruct((ntok, WIDTH_BV), BF16),
        jax.ShapeDtypeStruct((ntok, WIDTH_BV), F32),
        jax.ShapeDtypeStruct((ntok, WIDTH_BK), F32),
        jax.ShapeDtypeStruct((nseq * tail_shape[0], tail_shape[1]), F32),
        jax.ShapeDtypeStruct((nseq * tail_shape[0], tail_shape[1]), F32),
    )
    return pl.pallas_call(
        functools.partial(_proj_kernel, tiles_per_seq, tail_rows, tail_t),
        grid=(nt,),
        in_specs=[row(D_MODEL), _const_spec((1, D_MODEL)), _const_spec((IN_WIDTH, D_MODEL)),
                  _const_spec((LANES, WIDTH_BK)), _const_spec((1, WIDTH_BK))],
        out_specs=(row(WIDTH_A), row(WIDTH_A), row(WIDTH_A), row(WIDTH_BK), row(WIDTH_BK),
                   row(WIDTH_BV), row(WIDTH_BV), row(WIDTH_BK), tail, tail),
        out_shape=out_shape,
        scratch_shapes=[pltpu.VMEM((IN_WIDTH, D_MODEL), BF16)],
        compiler_params=_params(1, VMEM_LIMIT),
        name="proj",
    )(x, g, w_t, wa2, ba)


def _stack_heads(x_pair):
    low_half = lax.broadcasted_iota(jnp.int32, x_pair.shape, 1) < HEAD_DIM_A
    zero = jnp.zeros_like(x_pair)
    return jnp.concatenate([jnp.where(low_half, x_pair, zero), jnp.where(low_half, zero, x_pair)], axis=0)


TQ = 4 * CHUNK
KEYS = 3 * TQ
SOFT_W = 5 * LANES
assert 2 * TQ == A_WINDOW and SOFT_W >= (BAND_CHUNKS + 1) * CHUNK + CHUNK


def _build_band_bias(base_ref, bm_ref):
    n_band = (BAND_CHUNKS + 1) * CHUNK
    col = lax.broadcasted_iota(jnp.int32, (CHUNK, KEYS), 1)
    for h in range(N_HEADS_A):
        rows = jnp.broadcast_to(base_ref[h:h + 1, :], (CHUNK, KEYS))
        for qc in range(TQ // CHUNK):
            toeplitz = pltpu.roll(rows, qc * CHUNK, 1, stride=1, stride_axis=0)
            in_band = (col >= qc * CHUNK) & (col < qc * CHUNK + n_band)
            r0 = (qc * 2 + h % 2) * CHUNK
            bm_ref[h // 2, r0:r0 + CHUNK, :] = jnp.where(in_band, toeplitz, NEG_BIG)


def _band_tile(q_ref, k_refs, v_refs, bm_ref, o_ref, start_penalty):
    n_qc, n_pairs = TQ // CHUNK, N_HEADS_A // 2
    low_half = lax.broadcasted_iota(jnp.int32, (CHUNK, LANES), 1) < HEAD_DIM_A
    pair_cols = lambda hp: slice(hp * LANES, (hp + 1) * LANES)

    def scores(hp):
        cols = pair_cols(hp)
        qs = jnp.concatenate([_stack_heads(q_ref[qc * CHUNK:(qc + 1) * CHUNK, cols]) for qc in range(n_qc)],
                             axis=0)
        return jnp.concatenate([lax.dot_general(qs, k[:, cols], NT_DIMS, preferred_element_type=F32)
                                for k in k_refs], axis=1)

    def softmax(hp, s):
        probs, inv_l = [], []
        for qc in range(n_qc):
            rows = slice(qc * 2 * CHUNK, (qc + 1) * 2 * CHUNK)
            c0 = 0 if (qc + 1) * CHUNK + A_WINDOW <= SOFT_W else KEYS - SOFT_W
            sq = s[rows, c0:c0 + SOFT_W] + bm_ref[hp, rows, c0:c0 + SOFT_W]
            if start_penalty is not None:
                sq = sq + start_penalty[:, c0:c0 + SOFT_W]
            p = jnp.exp(sq - jnp.max(sq, axis=-1, keepdims=True))
            inv_l.append(1.0 / jnp.sum(p, axis=-1, keepdims=True))
            pad = jnp.zeros((2 * CHUNK, KEYS - SOFT_W), BF16)
            probs.append(jnp.concatenate([p.astype(BF16), pad] if c0 == 0 else [pad, p.astype(BF16)], axis=1))
        return jnp.concatenate(probs, axis=0), inv_l

    def values(hp, pm, inv_l):
        cols = pair_cols(hp)
        out = sum(jnp.dot(pm[:, i * TQ:(i + 1) * TQ], v[:, cols], preferred_element_type=F32)
                  for i, v in enumerate(v_refs))
        for qc in range(n_qc):
            r0 = qc * 2 * CHUNK
            o0 = out[r0:r0 + CHUNK] * inv_l[qc][0:CHUNK]
            o1 = out[r0 + CHUNK:r0 + 2 * CHUNK] * inv_l[qc][CHUNK:2 * CHUNK]
            o_ref[qc * CHUNK:(qc + 1) * CHUNK, cols] = jnp.where(low_half, o0, o1).astype(o_ref.dtype)

    for hp in range(n_pairs):
        s = scores(hp)
        yield
        pm, inv_l = softmax(hp, s)
        yield
        values(hp, pm, inv_l)
        yield


def _attn_prompt_kernel(q_ref, k2_ref, k1_ref, k0_ref, v2_ref, v1_ref, v0_ref, base_ref, o_ref, bm):
    t = pl.program_id(1)

    @pl.when((pl.program_id(0) == 0) & (t == 0))
    def _():
        _build_band_bias(base_ref, bm)

    def tiles(penalty):
        gens = [_band_tile(q_ref.at[i], (k2_ref.at[i], k1_ref.at[i], k0_ref.at[i]),
                           (v2_ref.at[i], v1_ref.at[i], v0_ref.at[i]), bm, o_ref.at[i], penalty)
                for i in range(q_ref.shape[0])]
        for _ in itertools.zip_longest(*gens):
            pass

    @pl.when(t >= 2)
    def _():
        tiles(None)

    @pl.when(t < 2)
    def _():
        col = lax.broadcasted_iota(jnp.int32, (1, KEYS), 1)
        tiles(jnp.where(col < (2 - t) * TQ, NEG_BIG, 0.0))


def _attn_prompt(q, k, v, base, n_batch, seq):
    nt = seq // TQ
    n_par = n_batch
    blk = lambda back: pl.BlockSpec((n_par, TQ, WIDTH_A), lambda g, t: (g, jnp.maximum(t - back, 0), 0))
    per_seq = lambda z: z.reshape(n_batch, seq, WIDTH_A)
    q, k, v = per_seq(q), per_seq(k), per_seq(v)
    return pl.pallas_call(
        _attn_prompt_kernel,
        grid=(n_batch // n_par, nt),
        in_specs=[blk(0), blk(2), blk(1), blk(0), blk(2), blk(1), blk(0), _const_spec((N_HEADS_A, KEYS))],
        out_specs=blk(0),
        out_shape=jax.ShapeDtypeStruct((n_batch, seq, WIDTH_A), BF16),
        scratch_shapes=[pltpu.VMEM((N_HEADS_A // 2, 2 * TQ, KEYS), F32)],
        compiler_params=_params(2, VMEM_LIMIT),
        name="band_attn_prompt",
    )(q, k, k, k, v, v, v, base).reshape(n_batch * seq, WIDTH_A)


def _attn_sample_one(rows, q_ref, kn_ref, vn_ref, kc_ref, vc_ref, sel_ref, dup_ref, o_ref, bias_c, bias_n):
    n_h = N_HEADS_A
    n_new = rows.stop - rows.start
    low_half = lax.broadcasted_iota(jnp.int32, (n_new, LANES), 1) < HEAD_DIM_A
    pairs = [slice(hp * LANES, (hp + 1) * LANES) for hp in range(n_h // 2)]
    q_heads = [jnp.dot(q_ref[rows, cols], sel_ref[e], preferred_element_type=F32).astype(BF16)
               for cols in pairs for e in range(2)]
    yield
    s_c = jnp.concatenate([jnp.dot(q_heads[h], kc_ref[h].astype(BF16), preferred_element_type=F32)
                           for h in range(n_h)], axis=0) + bias_c[...]
    s_n = jnp.concatenate([lax.dot_general(_stack_heads(q_ref[rows, cols]), kn_ref[rows, cols], NT_DIMS,
                                           preferred_element_type=F32) for cols in pairs], axis=0) + bias_n[...]
    yield
    m = jnp.maximum(jnp.max(s_c, axis=-1, keepdims=True), jnp.max(s_n, axis=-1, keepdims=True))
    p_c, p_n = jnp.exp(s_c - m), jnp.exp(s_n - m)
    inv_l = 1.0 / (jnp.sum(p_c, axis=-1, keepdims=True) + jnp.sum(p_n, axis=-1, keepdims=True))
    p_c, p_n = p_c.astype(BF16), p_n.astype(BF16)
    yield
    out_c = jnp.concatenate([lax.dot_general(p_c[h * n_new:(h + 1) * n_new], vc_ref[h].astype(BF16), NT_DIMS,
                                             preferred_element_type=F32) for h in range(n_h)], axis=0) * inv_l
    yield
    hi = out_c.astype(BF16)
    lo = (out_c - hi.astype(F32)).astype(BF16)
    out_c = (jnp.dot(hi, dup_ref[...], preferred_element_type=F32)
             + jnp.dot(lo, dup_ref[...], preferred_element_type=F32))
    for hp, cols in enumerate(pairs):
        pr = slice(hp * 2 * n_new, (hp + 1) * 2 * n_new)
        out = out_c[pr] + jnp.dot(p_n[pr], vn_ref[rows, cols], preferred_element_type=F32) * inv_l[pr]
        o_ref[rows, cols] = jnp.where(low_half, out[0:n_new], out[n_new:]).astype(o_ref.dtype)


def _attn_sample_kernel(n_new, q_ref, kn_ref, vn_ref, kc_ref, vc_ref, base_ref, sel_ref, dup_ref,
                        o_ref, bias_c, bias_n):
    n_cache = kc_ref.shape[3]

    @pl.when(pl.program_id(0) == 0)
    def _():
        for h in range(N_HEADS_A):
            rows = jnp.broadcast_to(base_ref[h:h + 1, :], (n_new, BIAS_W))
            toeplitz = pltpu.roll(rows, 0, 1, stride=1, stride_axis=0)
            bias_c[h * n_new:(h + 1) * n_new, :] = toeplitz[:, 0:n_cache]
            bias_n[h * n_new:(h + 1) * n_new, :] = toeplitz[:, n_cache:n_cache + n_new]

    requests = [_attn_sample_one(slice(i * n_new, (i + 1) * n_new), q_ref, kn_ref, vn_ref, kc_ref.at[i],
                                 vc_ref.at[i], sel_ref, dup_ref, o_ref, bias_c, bias_n)
                for i in range(kc_ref.shape[0])]
    for _ in itertools.zip_longest(*requests):
        pass


def _attn_sample(q, k_new, v_new, k_cache_t, v_cache_t, table, n_new, n_par):
    n_batch, n_h, _, n_cache = k_cache_t.shape
    assert n_cache == A_WINDOW and n_new <= CHUNK and n_h == N_HEADS_A
    lane = np.arange(LANES)
    sel = np.stack([lane[:, None] == np.arange(HEAD_DIM_A)[None, :] + e * HEAD_DIM_A for e in range(2)])
    dup = np.arange(HEAD_DIM_A)[:, None] == lane[None, :] % HEAD_DIM_A
    new = pl.BlockSpec((n_par * n_new, WIDTH_A), lambda g: (g, 0))
    old = pl.BlockSpec((n_par, n_h, HEAD_DIM_A, n_cache), lambda g: (g, 0, 0, 0))
    return pl.pallas_call(
        functools.partial(_attn_sample_kernel, n_new),
        grid=(n_batch // n_par,),
        in_specs=[new, new, new, old, old, _const_spec((n_h, BIAS_W)),
                  _const_spec((2, LANES, HEAD_DIM_A)), _const_spec((HEAD_DIM_A, LANES))],
        out_specs=new,
        out_shape=jax.ShapeDtypeStruct((n_batch * n_new, WIDTH_A), BF16),
        scratch_shapes=[pltpu.VMEM((n_h * n_new, n_cache), F32), pltpu.VMEM((n_h * n_new, n_new), F32)],
        compiler_params=_params(1, VMEM_LIMIT),
        name="band_attn_sample",
    )(q, k_new, v_new, k_cache_t, v_cache_t, _bias_base(table, BIAS_W), jnp.asarray(sel, BF16),
      jnp.asarray(dup, BF16))


def _gla_levels(chunk):
    return [chunk >> i for i in range(int(math.log2(chunk)) + 1)]


def _segment_matrix(chunk):
    i = np.arange(chunk)[:, None]
    t = np.arange(chunk)[None, :]
    blocks = []
    for li, s in enumerate(_gla_levels(chunk)):
        start = (i // s) * s
        f_rows = (t >= start) & (t <= i)
        r_rows = (t > i) & (t <= start + s - 1)
        if li == 0:
            blocks += [f_rows, r_rows]
        else:
            blocks.append(np.where((i // s) % 2 == 1, f_rows, r_rows))
    seg = np.concatenate(blocks, axis=0).astype(np.float32)
    return np.concatenate([seg, seg], axis=1)


def _level_masks(group_tokens):
    tg = group_tokens
    i = (np.arange(2 * tg) % tg)[:, None]
    j = np.arange(tg)[None, :]
    masks = [i == j]
    s = tg // 2
    while s >= 1:
        masks.append(((i // s) % 2 == 1) & (j // s == i // s - 1))
        s //= 2
    return np.stack(masks).astype(np.float32)


def _gla_tile(q, k, v, la, seg_ref, mask_ref, state_ref, o_ref, chunk, n_chunks):
    n, tg = n_chunks, chunk * n_chunks
    levels = _gla_levels(chunk)
    la_hi = la.astype(BF16)
    la_lo = (la - la_hi.astype(F32)).astype(BF16)
    split = jnp.concatenate(
        [jnp.concatenate([la_hi[c * chunk:(c + 1) * chunk], la_lo[c * chunk:(c + 1) * chunk]], axis=0)
         for c in range(n)], axis=1)
    sums = jnp.dot(seg_ref[...], split, preferred_element_type=F32)

    def seg_sum(block):
        rows = slice(block * chunk, (block + 1) * chunk)
        return jnp.minimum(jnp.concatenate(
            [sums[rows, c * WIDTH_BK:(c + 1) * WIDTH_BK] for c in range(n)], axis=0), 0.0)

    fwd0, rev0 = seg_sum(0), seg_sum(1)
    from_start, to_end = jnp.exp(fwd0), jnp.exp(rev0)
    total = [fwd0[(c + 1) * chunk - 1:(c + 1) * chunk] for c in range(n)]

    def chunk_sum(cs):
        cs = list(cs)
        return (sum(total[c] for c in cs[1:]) + total[cs[0]]) if cs else None

    def extend(base, offsets):
        pieces = []
        for c in range(n):
            x = base[c * chunk:(c + 1) * chunk]
            pieces.append(x if offsets[c] is None else x * jnp.exp(offsets[c]))
        return jnp.concatenate(pieces, axis=0) if n > 1 else pieces[0]

    q_state = (q * extend(from_start, [chunk_sum(range(0, c)) for c in range(n)])).astype(BF16)
    k_state = (k * extend(to_end, [chunk_sum(range(c + 1, n)) for c in range(n)])).astype(BF16)
    decay_tile = jnp.exp(chunk_sum(range(n)))
    ops = {}
    s = tg // 2
    while s >= chunk and n > 1:
        per = s // chunk
        offsets, bases = [], []
        for c in range(n):
            sg = c // per
            offsets.append(chunk_sum(range(sg * per, c)) if sg % 2 else chunk_sum(range(c + 1, (sg + 1) * per)))
            bases.append((from_start if sg % 2 else to_end)[c * chunk:(c + 1) * chunk])
        w = extend(jnp.concatenate(bases, axis=0), offsets)
        ops[s] = ((q * w).astype(BF16), (k * w).astype(BF16))
        s //= 2
    for li in range(1, len(levels)):
        w = jnp.exp(seg_sum(li + 1))
        ops[levels[li]] = ((q * w).astype(BF16), (k * w).astype(BF16))
    q16, k16 = q.astype(BF16), k.astype(BF16)
    yield

    gt = mask_ref.shape[2]
    nt_dims = (((1,), (1,)), ((), ()))
    tn_dims = (((0,), (0,)), ((), ()))
    n_pairs, n_groups = N_HEADS_B // 2, tg // gt
    pair_cols = [slice(p * LANES, (p + 1) * LANES) for p in range(n_pairs)]
    pair_v = [v[:, p * 2 * VAL_DIM_B:(p + 1) * 2 * VAL_DIM_B] for p in range(n_pairs)]
    head_v = lambda e: slice(e * VAL_DIM_B, (e + 1) * VAL_DIM_B)
    o_state = []
    for p, cols in enumerate(pair_cols):
        s0, s1 = state_ref[2 * p], state_ref[2 * p + 1]
        zero = jnp.zeros_like(s0)
        st = jnp.concatenate([jnp.concatenate([s0, zero], axis=1),
                              jnp.concatenate([zero, s1], axis=1)], axis=0)
        o_state.append(jnp.dot(q_state[:, cols], st.astype(BF16), preferred_element_type=F32))
        upd = lax.dot_general(k_state[:, cols], pair_v[p], tn_dims, preferred_element_type=F32)
        decay_rows = jnp.broadcast_to(decay_tile[:, cols], (LANES, LANES)).T
        state_ref[2 * p] = s0 * decay_rows[0:KEY_DIM_B] + upd[0:KEY_DIM_B, 0:VAL_DIM_B]
        state_ref[2 * p + 1] = s1 * decay_rows[KEY_DIM_B:] + upd[KEY_DIM_B:, VAL_DIM_B:]
    yield
    blocks = []
    for p, cols in enumerate(pair_cols):
        for g in range(n_groups):
            rows = slice(g * gt, (g + 1) * gt)
            a = mask_ref[0] * lax.dot_general(_stack_heads(q16[rows, cols]), k16[rows, cols], nt_dims,
                                              preferred_element_type=F32)
            m, s = 1, gt // 2
            while s >= 1:
                qd, kd = ops[s]
                a = a + mask_ref[m] * lax.dot_general(_stack_heads(qd[rows, cols]), kd[rows, cols], nt_dims,
                                                      preferred_element_type=F32)
                m, s = m + 1, s // 2
            blocks.append((p, g * gt, g * gt, a.astype(BF16)))
            yield
        s = tg // 2
        while s >= gt:
            qd, kd = ops[s]
            for blk in range(tg // (2 * s)):
                k0, q0 = blk * 2 * s, blk * 2 * s + s
                ab = lax.dot_general(_stack_heads(qd[q0:q0 + s, cols]), kd[k0:k0 + s, cols], nt_dims,
                                     preferred_element_type=F32)
                blocks.append((p, q0, k0, ab.astype(BF16)))
                yield
            s //= 2
    piece = [[[o_state[p][g * gt:(g + 1) * gt, head_v(e)] for g in range(n_groups)] for e in range(2)]
             for p in range(n_pairs)]
    for p, q0, k0, a in blocks:
        nq, nk = a.shape[0] // 2, a.shape[1]
        for e in range(2):
            contrib = jnp.dot(a[e * nq:(e + 1) * nq], pair_v[p][k0:k0 + nk, head_v(e)], preferred_element_type=F32)
            for gi in range(nq // gt):
                g = q0 // gt + gi
                piece[p][e][g] = piece[p][e][g] + contrib[gi * gt:(gi + 1) * gt]
        yield
    outs = [jnp.concatenate(piece[p][e], axis=0) if n_groups > 1 else piece[p][e][0]
            for p in range(n_pairs) for e in range(2)]
    o_ref[...] = jnp.concatenate(outs, axis=1)


def _gla_kernel(chunk, chunks_per_tile, has_init, *refs):
    if has_init:
        q_ref, k_ref, v_ref, la_ref, seg_ref, mask_ref, s0_ref, o_ref, sout_ref, state = refs
    else:
        q_ref, k_ref, v_ref, la_ref, seg_ref, mask_ref, o_ref, sout_ref, state = refs
    t = pl.program_id(1)

    @pl.when(t == 0)
    def _():
        state[...] = s0_ref[...] if has_init else jnp.zeros_like(state)

    tiles = [_gla_tile(q_ref[i], k_ref[i], v_ref[i], la_ref[i], seg_ref, mask_ref, state.at[i], o_ref.at[i],
                       chunk, chunks_per_tile) for i in range(q_ref.shape[0])]
    for _ in itertools.zip_longest(*tiles):
        pass

    @pl.when(t == pl.num_programs(1) - 1)
    def _():
        sout_ref[...] = state[...]


def _gla(q, k, v, la, s0, n_batch, seq, chunk, chunks_per_tile, n_par):
    tg = chunk * chunks_per_tile
    nt = seq // tg
    seg = jnp.asarray(_segment_matrix(chunk), BF16)
    masks = jnp.asarray(_level_masks(min(tg, LANES)), F32)
    row = lambda w: pl.BlockSpec((n_par, tg, w), lambda g, t: (g, t, 0))
    st = pl.BlockSpec((n_par, N_HEADS_B, KEY_DIM_B, VAL_DIM_B), lambda g, t: (g, 0, 0, 0))
    per_seq = lambda z: z.reshape(n_batch, seq, z.shape[-1])
    in_specs = [row(WIDTH_BK), row(WIDTH_BK), row(WIDTH_BV), row(WIDTH_BK), _const_spec(seg.shape),
                _const_spec(masks.shape)]
    args = [per_seq(q), per_seq(k), per_seq(v), per_seq(la), seg, masks]
    if s0 is not None:
        in_specs.append(st)
        args.append(s0)
    o, s_out = pl.pallas_call(
        functools.partial(_gla_kernel, chunk, chunks_per_tile, s0 is not None),
        grid=(n_batch // n_par, nt),
        in_specs=in_specs,
        out_specs=(row(WIDTH_BV), st),
        out_shape=(jax.ShapeDtypeStruct((n_batch, seq, WIDTH_BV), F32),
                   jax.ShapeDtypeStruct((n_batch, N_HEADS_B, KEY_DIM_B, VAL_DIM_B), F32)),
        scratch_shapes=[pltpu.VMEM((n_par, N_HEADS_B, KEY_DIM_B, VAL_DIM_B), F32)],
        compiler_params=_params(2, VMEM_LIMIT),
        name="gla_chunk%d" % chunk,
    )(*args)
    return o.reshape(n_batch * seq, WIDTH_BV), s_out


def _mem_kv_kernel(m_ref, g_ref, wk32_ref, wv32_ref, k_ref, v_ref, k16_ref, v16_ref, wk_ref, wv_ref):
    @pl.when(pl.program_id(0) == 0)
    def _():
        _cast_rows(wk32_ref, wk_ref)
        _cast_rows(wv32_ref, wv_ref)

    m = _rms(m_ref[...], g_ref[...]).astype(BF16)
    k = jnp.dot(m, wk_ref[...], preferred_element_type=F32)
    v = jnp.dot(m, wv_ref[...], preferred_element_type=F32)
    k_ref[...], v_ref[...] = k, v
    k16_ref[...], v16_ref[...] = k.astype(BF16), v.astype(BF16)


def _mem_kv(mem, g, wk, wv):
    n = mem.shape[0]
    tm = N_MEM
    row = pl.BlockSpec((tm, D_MODEL), lambda i: (i, 0))
    return pl.pallas_call(
        _mem_kv_kernel,
        grid=(n // tm,),
        in_specs=[row, _const_spec((1, D_MODEL)), _const_spec((D_MODEL, D_MODEL)), _const_spec((D_MODEL, D_MODEL))],
        out_specs=(row, row, row, row),
        out_shape=(jax.ShapeDtypeStruct((n, D_MODEL), F32), jax.ShapeDtypeStruct((n, D_MODEL), F32),
                   jax.ShapeDtypeStruct((n, D_MODEL), BF16), jax.ShapeDtypeStruct((n, D_MODEL), BF16)),
        scratch_shapes=[pltpu.VMEM((D_MODEL, D_MODEL), BF16), pltpu.VMEM((D_MODEL, D_MODEL), BF16)],
        compiler_params=_params(1, VMEM_LIMIT),
        name="mem_kv",
    )(mem, g, wk, wv)


def _stack_front(rows, segs, x_ref, oa_ref, ob_ref, r_ref, mk_ref, mv_ref, g_gla_ref, wo_ref, g_post_mix_ref,
                 g_pre_mem_ref, wq_ref, wmo_ref, g_post_mem_ref, result):
    n_rows = rows.stop - rows.start
    ob = ob_ref[rows, :]
    normed = []
    for h in range(N_HEADS_B):
        seg = ob[:, h * VAL_DIM_B:(h + 1) * VAL_DIM_B]
        normed.append(seg * lax.rsqrt(jnp.mean(seg * seg, axis=-1, keepdims=True) + EPS))
    yb = (jnp.concatenate(normed, axis=1) * g_gla_ref[...] * _silu(r_ref[rows, :])).astype(BF16)
    yield
    mix = (jnp.dot(oa_ref[rows, :], wo_ref[0:WIDTH_A, :], preferred_element_type=F32)
           + jnp.dot(yb, wo_ref[WIDTH_A:, :], preferred_element_type=F32))
    yield
    x1 = x_ref[rows, :] + _rms(mix, g_post_mix_ref[...])
    hq = _rms(x1, g_pre_mem_ref[...]).astype(BF16)
    yield
    q = (jnp.dot(hq, wq_ref[...], preferred_element_type=F32) * (HEAD_DIM_MEM ** -0.5)).astype(BF16)
    yield
    rows_per_seg = n_rows // len(segs)
    units = [(u, sg, h) for u, sg in enumerate(segs) for h in range(N_HEADS_MEM)]
    mem_rows = lambda sg: slice(sg * N_MEM, (sg + 1) * N_MEM)
    head_cols = lambda h: slice(h * HEAD_DIM_MEM, (h + 1) * HEAD_DIM_MEM)
    scores = [lax.dot_general(q[u * rows_per_seg:(u + 1) * rows_per_seg, head_cols(h)],
                              mk_ref[mem_rows(sg), head_cols(h)].astype(BF16), NT_DIMS,
                              preferred_element_type=F32) for u, sg, h in units]
    yield
    probs, inv_l = [], []
    for s in scores:
        p = jnp.exp(s - jnp.max(s, axis=-1, keepdims=True))
        inv_l.append(1.0 / jnp.sum(p, axis=-1, keepdims=True))
        probs.append(p.astype(BF16))
    yield
    outs = [jnp.dot(probs[i], mv_ref[mem_rows(sg), head_cols(h)].astype(BF16), preferred_element_type=F32) * inv_l[i]
            for i, (u, sg, h) in enumerate(units)]
    seg_outs = [jnp.concatenate(outs[u * N_HEADS_MEM:(u + 1) * N_HEADS_MEM], axis=1) for u in range(len(segs))]
    o = (jnp.concatenate(seg_outs, axis=0) if len(segs) > 1 else seg_outs[0]).astype(BF16)
    yield
    att = jnp.dot(o, wmo_ref[...], preferred_element_type=F32)
    yield
    result.append(x1 + _rms(att, g_post_mem_ref[...]))


def _stack_ffn(x2, g_pre_ffn_ref, wg_ref, wu_ref, wd_ref, g_post_ffn_ref, y_ref):
    hf = _rms(x2, g_pre_ffn_ref[...]).astype(BF16)
    f = jnp.zeros(x2.shape, F32)
    act, act_cols = None, None
    for c in range(D_FF // FF_BLOCK):
        cols = slice(c * FF_BLOCK, (c + 1) * FF_BLOCK)
        gate = jnp.dot(hf, wg_ref[:, cols], preferred_element_type=F32)
        up = jnp.dot(hf, wu_ref[:, cols], preferred_element_type=F32)
        if act is not None:
            f = f + jnp.dot(act, wd_ref[act_cols, :], preferred_element_type=F32)
        act, act_cols = (_silu(gate) * up).astype(BF16), cols
        yield
    f = f + jnp.dot(act, wd_ref[act_cols, :], preferred_element_type=F32)
    yield
    y_ref[...] = x2 + _rms(f, g_post_ffn_ref[...])


def _stack_kernel(n_seg, x_ref, oa_ref, ob_ref, r_ref, mk_ref, mv_ref,
                  g_gla_ref, wo_ref, g_post_mix_ref, g_pre_mem_ref, wq_ref, wmo_ref, g_post_mem_ref,
                  g_pre_ffn_ref, wg_ref, wu_ref, wd_ref, g_post_ffn_ref, y_ref):
    tm = x_ref.shape[0]
    half = tm // 2
    halves = []
    for i in range(2):
        segs = list(range(i * n_seg // 2, (i + 1) * n_seg // 2)) if n_seg > 1 else [0]
        result = []
        halves.append((result, _stack_front(slice(i * half, (i + 1) * half), segs, x_ref, oa_ref, ob_ref, r_ref,
                                            mk_ref, mv_ref, g_gla_ref, wo_ref, g_post_mix_ref, g_pre_mem_ref,
                                            wq_ref, wmo_ref, g_post_mem_ref, result)))
    for _ in itertools.zip_longest(*[gen for _, gen in halves]):
        pass
    ffn_refs = (g_pre_ffn_ref, wg_ref, wu_ref, wd_ref, g_post_ffn_ref)
    if half >= MIN_SPLIT_ROWS:
        ffns = [_stack_ffn(result[0], *ffn_refs, y_ref.at[i * half:(i + 1) * half])
                for i, (result, _) in enumerate(halves)]
    else:
        ffns = [_stack_ffn(jnp.concatenate([result[0] for result, _ in halves], axis=0), *ffn_refs, y_ref)]
    for _ in itertools.zip_longest(*ffns):
        pass


def _stack(x, oa, ob, r, mk, mv, weights, tm, n_seg, tiles_per_mem_block):
    ntok = x.shape[0]
    row = lambda w: pl.BlockSpec((tm, w), lambda i: (i, 0))
    mem = pl.BlockSpec((n_seg * N_MEM, D_MODEL), lambda i: (i // tiles_per_mem_block, 0))
    w_specs = [_const_spec(w.shape) for w in weights]
    return pl.pallas_call(
        functools.partial(_stack_kernel, n_seg),
        grid=(ntok // tm,),
        in_specs=[row(D_MODEL), row(WIDTH_A), row(WIDTH_BV), row(WIDTH_BV), mem, mem] + w_specs,
        out_specs=row(D_MODEL),
        out_shape=jax.ShapeDtypeStruct((ntok, D_MODEL), F32),
        compiler_params=_params(1, VMEM_LIMIT),
        name="token_stack",
    )(x, oa, ob, r, mk, mv, *weights)


def _bias_base(table, width):
    u = np.arange(width)
    idx = np.where(u < (BAND_CHUNKS + 1) * CHUNK, np.clip(A_WINDOW - u, -REL_CLIP, REL_CLIP) + REL_CLIP, 2 * REL_CLIP)
    return table[:, idx]


def kernel(x_prompt, x_sample, mem_prompt, cache_a_k, cache_a_v, state_gla, cache_mem_k, cache_mem_v,
           g_pre_mix, w_in, rel_bias, w_alpha2, b_alpha, g_gla_out, w_o, g_post_mix,
           g_pre_mem, g_mem, w_mq, w_mk, w_mv, w_mo, g_post_mem,
           g_pre_ffn, w_ffn_gate, w_ffn_up, w_ffn_down, g_post_ffn):
    depth = w_in.shape[0]
    assert depth == 1
    l = 0
    n_p, seq, _ = x_prompt.shape
    n_s, seq_s, _ = x_sample.shape
    vec = lambda g: g[l].reshape(1, -1)

    wi = w_in[l]
    w_t = jnp.swapaxes(wi, 0, 1)
    wa2 = jnp.pad(w_alpha2[l], ((0, LANES - GATE_RANK), (0, 0))).astype(BF16)
    stack_w = [vec(g_gla_out), w_o[l].astype(BF16), vec(g_post_mix), vec(g_pre_mem), w_mq[l].astype(BF16),
               w_mo[l].astype(BF16), vec(g_post_mem), vec(g_pre_ffn), w_ffn_gate[l].astype(BF16),
               w_ffn_up[l].astype(BF16), w_ffn_down[l].astype(BF16), vec(g_post_ffn)]

    xp = x_prompt.reshape(n_p * seq, D_MODEL)
    keep = min(A_WINDOW, seq)
    qa, ka, va, qb, kb, vb, r, la, k_tail, v_tail = _proj(xp, vec(g_pre_mix), w_t, wa2, vec(b_alpha), TILES.proj_rows,
                                                          seq // TILES.proj_rows, keep, True)
    oa = _attn_prompt(qa, ka, va, _bias_base(rel_bias[l], KEYS), n_p, seq)
    ob, sp = _gla(qb, kb, vb, la, None, n_p, seq, CHUNK, TQ // CHUNK, n_p)
    mk, mv, mk16, mv16 = _mem_kv(mem_prompt.reshape(n_p * N_MEM, D_MODEL), vec(g_mem), w_mk[l], w_mv[l])
    yp = _stack(xp, oa, ob, r, mk16, mv16, stack_w, TILES.stack_rows, 1,
                seq // TILES.stack_rows)

    xs = x_sample.reshape(n_s * seq_s, D_MODEL)
    ntok_s = n_s * seq_s
    qa_s, ka_s, va_s, qb_s, kb_s, vb_s, r_s, la_s, k_new, v_new = _proj(
        xs, vec(g_pre_mix), w_t, wa2, vec(b_alpha), ntok_s, 1, ntok_s, False)
    oa_s = _attn_sample(qa_s, ka_s, va_s, jnp.transpose(cache_a_k[l], (0, 2, 3, 1)),
                        jnp.transpose(cache_a_v[l], (0, 2, 3, 1)), rel_bias[l], seq_s, TILES.sample_attn_requests)
    ob_s, ss = _gla(qb_s, kb_s, vb_s, la_s, state_gla[l], n_s, seq_s, seq_s, 1,
                    TILES.sample_gla_requests)
    seg_s = TILES.sample_stack_requests
    ys = _stack(xs, oa_s, ob_s, r_s,
                cache_mem_k[l].reshape(n_s * N_MEM, D_MODEL),
                cache_mem_v[l].reshape(n_s * N_MEM, D_MODEL), stack_w, seg_s * seq_s, seg_s, 1)

    heads = lambda z, n, t: z.reshape(1, n, t, N_HEADS_A, HEAD_DIM_A)
    tails = lambda z: jnp.transpose(z.reshape(n_p, N_HEADS_A, HEAD_DIM_A, keep), (0, 3, 1, 2))[None]
    return (yp.reshape(n_p, seq, D_MODEL), ys.reshape(n_s, seq_s, D_MODEL),
            tails(k_tail), tails(v_tail), sp[None],
            mk.reshape(1, n_p, N_MEM, N_HEADS_MEM, HEAD_DIM_MEM), mv.reshape(1, n_p, N_MEM, N_HEADS_MEM, HEAD_DIM_MEM),
            heads(k_new, n_s, seq_s), heads(v_new, n_s, seq_s), ss[None])
```

```python
import functools
import itertools
import math
from typing import NamedTuple

import jax
import jax.numpy as jnp
import numpy as np
from jax import lax
from jax.experimental import pallas as pl
from jax.experimental.pallas import tpu as pltpu

F32 = jnp.float32
BF16 = jnp.bfloat16

D_MODEL = 1024
CHUNK = 64
BAND_CHUNKS = 8
A_WINDOW = BAND_CHUNKS * CHUNK
N_HEADS_A = 8
HEAD_DIM_A = 64
WIDTH_A = 512
REL_CLIP = 256
N_HEADS_B = 4
KEY_DIM_B = 64
VAL_DIM_B = 128
WIDTH_BK = 256
WIDTH_BV = 512
GATE_RANK = 16
GATE_TAU = 16.0
N_MEM = 256
N_HEADS_MEM = 4
HEAD_DIM_MEM = 256
D_FF = 2816
EPS = 1e-6
NEG_BIG = -1e30
NT_DIMS = (((1,), (1,)), ((), ()))

LANES = 128
BIAS_W = 640
FF_BLOCK = 256
CAST_ROWS = 512
MIN_SPLIT_ROWS = 256
VMEM_LIMIT = 56 * 1024 * 1024


class _TilePlan(NamedTuple):
    proj_rows: int = 1024
    stack_rows: int = 512
    sample_attn_requests: int = 4
    sample_gla_requests: int = 8
    sample_stack_requests: int = 4


TILES = _TilePlan()

OFF_QA, OFF_KA, OFF_VA, OFF_QB, OFF_KB, OFF_VB, OFF_R = 0, 512, 1024, 1536, 1792, 2048, 2560
IN_WIDTH = OFF_R + GATE_RANK + WIDTH_BV


def _rms(x, g):
    return x * lax.rsqrt(jnp.mean(x * x, axis=-1, keepdims=True) + EPS) * g


def _silu(x):
    return x / (1.0 + jnp.exp(-x))


def _log_sigmoid(z):
    return jnp.minimum(z, 0.0) - jnp.log(1.0 + jnp.exp(-jnp.abs(z)))


def _const_spec(shape):
    nd = len(shape)
    return pl.BlockSpec(shape, lambda *_: (0,) * nd, pipeline_mode=pl.Buffered(1))


def _params(n_axes, vmem=None):
    return pltpu.CompilerParams(dimension_semantics=("arbitrary",) * n_axes, vmem_limit_bytes=vmem)


def _cast_rows(src_ref, dst_ref):
    n = src_ref.shape[0]
    for lo in range(0, n, CAST_ROWS):
        rows = slice(lo, min(lo + CAST_ROWS, n))
        dst_ref[rows, :] = src_ref[rows, :].astype(dst_ref.dtype)


def _proj_kernel(tiles_per_seq, tail_rows, tail_t, x_ref, g_ref, w32_ref, wa2_ref, ba_ref,
                 qa_ref, ka_ref, va_ref, qb_ref, kb_ref, vb_ref, r_ref, la_ref, kt_ref, vt_ref, w_ref):
    @pl.when(pl.program_id(0) == 0)
    def _():
        _cast_rows(w32_ref, w_ref)

    h = _rms(x_ref[...], g_ref[...]).astype(BF16)

    def proj(lo, hi):
        return lax.dot_general(h, w_ref[lo:hi, :], NT_DIMS, preferred_element_type=F32)

    g_low = proj(OFF_R, OFF_R + LANES).astype(BF16)
    z = jnp.dot(g_low, wa2_ref[...], preferred_element_type=F32) + ba_ref[...]
    la_ref[...] = _log_sigmoid(z) * (1.0 / GATE_TAU)
    qa_ref[...] = (proj(OFF_QA, OFF_KA) * (HEAD_DIM_A ** -0.5)).astype(BF16)
    ka = proj(OFF_KA, OFF_VA)
    va = proj(OFF_VA, OFF_QB)
    ka_ref[...] = ka.astype(BF16)
    va_ref[...] = va.astype(BF16)
    qb_ref[...] = proj(OFF_QB, OFF_KB) * (KEY_DIM_B ** -0.5)
    kb_ref[...] = proj(OFF_KB, OFF_VB)
    vb_ref[...] = proj(OFF_VB, OFF_R).astype(BF16)
    r_ref[...] = proj(OFF_R + GATE_RANK, IN_WIDTH)

    @pl.when(pl.program_id(0) % tiles_per_seq == tiles_per_seq - 1)
    def _():
        k_keep, v_keep = ka[ka.shape[0] - tail_rows:], va[va.shape[0] - tail_rows:]
        kt_ref[...] = k_keep.T if tail_t else k_keep
        vt_ref[...] = v_keep.T if tail_t else v_keep


def _proj(x, g, w_t, wa2, ba, tm, tiles_per_seq, tail_rows, tail_t):
    ntok = x.shape[0]
    nt = ntok // tm
    nseq = nt // tiles_per_seq
    row = lambda w: pl.BlockSpec((tm, w), lambda i: (i, 0))
    assert tail_rows <= tm
    tail_shape = (WIDTH_A, tail_rows) if tail_t else (tail_rows, WIDTH_A)
    tail = pl.BlockSpec(tail_shape, lambda i: (i // tiles_per_seq, 0))
    out_shape = (
        jax.ShapeDtypeStruct((ntok, WIDTH_A), BF16),
        jax.ShapeDtypeStruct((ntok, WIDTH_A), BF16),
        jax.ShapeDtypeStruct((ntok, WIDTH_A), BF16),
        jax.ShapeDtypeStruct((ntok, WIDTH_BK), F32),
        jax.ShapeDtypeStruct((ntok, WIDTH_BK), F32),
        jax.ShapeDtypeStruct((ntok, WIDTH_BV), BF16),
        jax.ShapeDtypeStruct((ntok, WIDTH_BV), F32),
        jax.ShapeDtypeStruct((ntok, WIDTH_BK), F32),
        jax.ShapeDtypeStruct((nseq * tail_shape[0], tail_shape[1]), F32),
        jax.ShapeDtypeStruct((nseq * tail_shape[0], tail_shape[1]), F32),
    )
    return pl.pallas_call(
        functools.partial(_proj_kernel, tiles_per_seq, tail_rows, tail_t),
        grid=(nt,),
        in_specs=[row(D_MODEL), _const_spec((1, D_MODEL)), _const_spec((IN_WIDTH, D_MODEL)),
                  _const_spec((LANES, WIDTH_BK)), _const_spec((1, WIDTH_BK))],
        out_specs=(row(WIDTH_A), row(WIDTH_A), row(WIDTH_A), row(WIDTH_BK), row(WIDTH_BK),
                   row(WIDTH_BV), row(WIDTH_BV), row(WIDTH_BK), tail, tail),
        out_shape=out_shape,
        scratch_shapes=[pltpu.VMEM((IN_WIDTH, D_MODEL), BF16)],
        compiler_params=_params(1, VMEM_LIMIT),
        name="proj",
    )(x, g, w_t, wa2, ba)


def _stack_heads(x_pair):
    low_half = lax.broadcasted_iota(jnp.int32, x_pair.shape, 1) < HEAD_DIM_A
    zero = jnp.zeros_like(x_pair)
    return jnp.concatenate([jnp.where(low_half, x_pair, zero), jnp.where(low_half, zero, x_pair)], axis=0)


TQ = 4 * CHUNK
KEYS = 3 * TQ
SOFT_W = 5 * LANES
assert 2 * TQ == A_WINDOW and SOFT_W >= (BAND_CHUNKS + 1) * CHUNK + CHUNK


def _build_band_bias(base_ref, bm_ref):
    n_band = (BAND_CHUNKS + 1) * CHUNK
    col = lax.broadcasted_iota(jnp.int32, (CHUNK, KEYS), 1)
    for h in range(N_HEADS_A):
        rows = jnp.broadcast_to(base_ref[h:h + 1, :], (CHUNK, KEYS))
        for qc in range(TQ // CHUNK):
            toeplitz = pltpu.roll(rows, qc * CHUNK, 1, stride=1, stride_axis=0)
            in_band = (col >= qc * CHUNK) & (col < qc * CHUNK + n_band)
            r0 = (qc * 2 + h % 2) * CHUNK
            bm_ref[h // 2, r0:r0 + CHUNK, :] = jnp.where(in_band, toeplitz, NEG_BIG)


def _band_tile(q_ref, k_refs, v_refs, bm_ref, o_ref, start_penalty):
    n_qc, n_pairs = TQ // CHUNK, N_HEADS_A // 2
    low_half = lax.broadcasted_iota(jnp.int32, (CHUNK, LANES), 1) < HEAD_DIM_A
    pair_cols = lambda hp: slice(hp * LANES, (hp + 1) * LANES)

    def scores(hp):
        cols = pair_cols(hp)
        qs = jnp.concatenate([_stack_heads(q_ref[qc * CHUNK:(qc + 1) * CHUNK, cols]) for qc in range(n_qc)],
                             axis=0)
        return jnp.concatenate([lax.dot_general(qs, k[:, cols], NT_DIMS, preferred_element_type=F32)
                                for k in k_refs], axis=1)

    def softmax(hp, s):
        probs, inv_l = [], []
        for qc in range(n_qc):
            rows = slice(qc * 2 * CHUNK, (qc + 1) * 2 * CHUNK)
            c0 = 0 if (qc + 1) * CHUNK + A_WINDOW <= SOFT_W else KEYS - SOFT_W
            sq = s[rows, c0:c0 + SOFT_W] + bm_ref[hp, rows, c0:c0 + SOFT_W]
            if start_penalty is not None:
                sq = sq + start_penalty[:, c0:c0 + SOFT_W]
            p = jnp.exp(sq - jnp.max(sq, axis=-1, keepdims=True))
            inv_l.append(1.0 / jnp.sum(p, axis=-1, keepdims=True))
            pad = jnp.zeros((2 * CHUNK, KEYS - SOFT_W), BF16)
            probs.append(jnp.concatenate([p.astype(BF16), pad] if c0 == 0 else [pad, p.astype(BF16)], axis=1))
        return jnp.concatenate(probs, axis=0), inv_l

    def values(hp, pm, inv_l):
        cols = pair_cols(hp)
        out = sum(jnp.dot(pm[:, i * TQ:(i + 1) * TQ], v[:, cols], preferred_element_type=F32)
                  for i, v in enumerate(v_refs))
        for qc in range(n_qc):
            r0 = qc * 2 * CHUNK
            o0 = out[r0:r0 + CHUNK] * inv_l[qc][0:CHUNK]
            o1 = out[r0 + CHUNK:r0 + 2 * CHUNK] * inv_l[qc][CHUNK:2 * CHUNK]
            o_ref[qc * CHUNK:(qc + 1) * CHUNK, cols] = jnp.where(low_half, o0, o1).astype(o_ref.dtype)

    for hp in range(n_pairs):
        s = scores(hp)
        yield
        pm, inv_l = softmax(hp, s)
        yield
        values(hp, pm, inv_l)
        yield


def _attn_prompt_kernel(q_ref, k2_ref, k1_ref, k0_ref, v2_ref, v1_ref, v0_ref, base_ref, o_ref, bm):
    t = pl.program_id(1)

    @pl.when((pl.program_id(0) == 0) & (t == 0))
    def _():
        _build_band_bias(base_ref, bm)

    def tiles(penalty):
        gens = [_band_tile(q_ref.at[i], (k2_ref.at[i], k1_ref.at[i], k0_ref.at[i]),
                           (v2_ref.at[i], v1_ref.at[i], v0_ref.at[i]), bm, o_ref.at[i], penalty)
                for i in range(q_ref.shape[0])]
        for _ in itertools.zip_longest(*gens):
            pass

    @pl.when(t >= 2)
    def _():
        tiles(None)

    @pl.when(t < 2)
    def _():
        col = lax.broadcasted_iota(jnp.int32, (1, KEYS), 1)
        tiles(jnp.where(col < (2 - t) * TQ, NEG_BIG, 0.0))


def _attn_prompt(q, k, v, base, n_batch, seq):
    nt = seq // TQ
    n_par = n_batch
    blk = lambda back: pl.BlockSpec((n_par, TQ, WIDTH_A), lambda g, t: (g, jnp.maximum(t - back, 0), 0))
    per_seq = lambda z: z.reshape(n_batch, seq, WIDTH_A)
    q, k, v = per_seq(q), per_seq(k), per_seq(v)
    return pl.pallas_call(
        _attn_prompt_kernel,
        grid=(n_batch // n_par, nt),
        in_specs=[blk(0), blk(2), blk(1), blk(0), blk(2), blk(1), blk(0), _const_spec((N_HEADS_A, KEYS))],
        out_specs=blk(0),
        out_shape=jax.ShapeDtypeStruct((n_batch, seq, WIDTH_A), BF16),
        scratch_shapes=[pltpu.VMEM((N_HEADS_A // 2, 2 * TQ, KEYS), F32)],
        compiler_params=_params(2, VMEM_LIMIT),
        name="band_attn_prompt",
    )(q, k, k, k, v, v, v, base).reshape(n_batch * seq, WIDTH_A)


def _attn_sample_one(rows, q_ref, kn_ref, vn_ref, kc_ref, vc_ref, sel_ref, dup_ref, o_ref, bias_c, bias_n):
    n_h = N_HEADS_A
    n_new = rows.stop - rows.start
    low_half = lax.broadcasted_iota(jnp.int32, (n_new, LANES), 1) < HEAD_DIM_A
    pairs = [slice(hp * LANES, (hp + 1) * LANES) for hp in range(n_h // 2)]
    q_heads = [jnp.dot(q_ref[rows, cols], sel_ref[e], preferred_element_type=F32).astype(BF16)
               for cols in pairs for e in range(2)]
    yield
    s_c = jnp.concatenate([jnp.dot(q_heads[h], kc_ref[h].astype(BF16), preferred_element_type=F32)
                           for h in range(n_h)], axis=0) + bias_c[...]
    s_n = jnp.concatenate([lax.dot_general(_stack_heads(q_ref[rows, cols]), kn_ref[rows, cols], NT_DIMS,
                                           preferred_element_type=F32) for cols in pairs], axis=0) + bias_n[...]
    yield
    m = jnp.maximum(jnp.max(s_c, axis=-1, keepdims=True), jnp.max(s_n, axis=-1, keepdims=True))
    p_c, p_n = jnp.exp(s_c - m), jnp.exp(s_n - m)
    inv_l = 1.0 / (jnp.sum(p_c, axis=-1, keepdims=True) + jnp.sum(p_n, axis=-1, keepdims=True))
    p_c, p_n = p_c.astype(BF16), p_n.astype(BF16)
    yield
    out_c = jnp.concatenate([lax.dot_general(p_c[h * n_new:(h + 1) * n_new], vc_ref[h].astype(BF16), NT_DIMS,
                                             preferred_element_type=F32) for h in range(n_h)], axis=0) * inv_l
    yield
    hi = out_c.astype(BF16)
    lo = (out_c - hi.astype(F32)).astype(BF16)
    out_c = (jnp.dot(hi, dup_ref[...], preferred_element_type=F32)
             + jnp.dot(lo, dup_ref[...], preferred_element_type=F32))
    for hp, cols in enumerate(pairs):
        pr = slice(hp * 2 * n_new, (hp + 1) * 2 * n_new)
        out = out_c[pr] + jnp.dot(p_n[pr], vn_ref[rows, cols], preferred_element_type=F32) * inv_l[pr]
        o_ref[rows, cols] = jnp.where(low_half, out[0:n_new], out[n_new:]).astype(o_ref.dtype)


def _attn_sample_kernel(n_new, q_ref, kn_ref, vn_ref, kc_ref, vc_ref, base_ref, sel_ref, dup_ref,
                        o_ref, bias_c, bias_n):
    n_cache = kc_ref.shape[3]

    @pl.when(pl.program_id(0) == 0)
    def _():
        for h in range(N_HEADS_A):
            rows = jnp.broadcast_to(base_ref[h:h + 1, :], (n_new, BIAS_W))
            toeplitz = pltpu.roll(rows, 0, 1, stride=1, stride_axis=0)
            bias_c[h * n_new:(h + 1) * n_new, :] = toeplitz[:, 0:n_cache]
            bias_n[h * n_new:(h + 1) * n_new, :] = toeplitz[:, n_cache:n_cache + n_new]

    requests = [_attn_sample_one(slice(i * n_new, (i + 1) * n_new), q_ref, kn_ref, vn_ref, kc_ref.at[i],
                                 vc_ref.at[i], sel_ref, dup_ref, o_ref, bias_c, bias_n)
                for i in range(kc_ref.shape[0])]
    for _ in itertools.zip_longest(*requests):
        pass


def _attn_sample(q, k_new, v_new, k_cache_t, v_cache_t, table, n_new, n_par):
    n_batch, n_h, _, n_cache = k_cache_t.shape
    assert n_cache == A_WINDOW and n_new <= CHUNK and n_h == N_HEADS_A
    lane = np.arange(LANES)
    sel = np.stack([lane[:, None] == np.arange(HEAD_DIM_A)[None, :] + e * HEAD_DIM_A for e in range(2)])
    dup = np.arange(HEAD_DIM_A)[:, None] == lane[None, :] % HEAD_DIM_A
    new = pl.BlockSpec((n_par * n_new, WIDTH_A), lambda g: (g, 0))
    old = pl.BlockSpec((n_par, n_h, HEAD_DIM_A, n_cache), lambda g: (g, 0, 0, 0))
    return pl.pallas_call(
        functools.partial(_attn_sample_kernel, n_new),
        grid=(n_batch // n_par,),
        in_specs=[new, new, new, old, old, _const_spec((n_h, BIAS_W)),
                  _const_spec((2, LANES, HEAD_DIM_A)), _const_spec((HEAD_DIM_A, LANES))],
        out_specs=new,
        out_shape=jax.ShapeDtypeStruct((n_batch * n_new, WIDTH_A), BF16),
        scratch_shapes=[pltpu.VMEM((n_h * n_new, n_cache), F32), pltpu.VMEM((n_h * n_new, n_new), F32)],
        compiler_params=_params(1, VMEM_LIMIT),
        name="band_attn_sample",
    )(q, k_new, v_new, k_cache_t, v_cache_t, _bias_base(table, BIAS_W), jnp.asarray(sel, BF16),
      jnp.asarray(dup, BF16))


def _gla_levels(chunk):
    return [chunk >> i for i in range(int(math.log2(chunk)) + 1)]


def _segment_matrix(chunk):
    i = np.arange(chunk)[:, None]
    t = np.arange(chunk)[None, :]
    blocks = []
    for li, s in enumerate(_gla_levels(chunk)):
        start = (i // s) * s
        f_rows = (t >= start) & (t <= i)
        r_rows = (t > i) & (t <= start + s - 1)
        if li == 0:
            blocks += [f_rows, r_rows]
        else:
            blocks.append(np.where((i // s) % 2 == 1, f_rows, r_rows))
    seg = np.concatenate(blocks, axis=0).astype(np.float32)
    return np.concatenate([seg, seg], axis=1)


def _level_masks(group_tokens):
    tg = group_tokens
    i = (np.arange(2 * tg) % tg)[:, None]
    j = np.arange(tg)[None, :]
    masks = [i == j]
    s = tg // 2
    while s >= 1:
        masks.append(((i // s) % 2 == 1) & (j // s == i // s - 1))
        s //= 2
    return np.stack(masks).astype(np.float32)


def _gla_tile(q, k, v, la, seg_ref, mask_ref, state_ref, o_ref, chunk, n_chunks):
    n, tg = n_chunks, chunk * n_chunks
    levels = _gla_levels(chunk)
    la_hi = la.astype(BF16)
    la_lo = (la - la_hi.astype(F32)).astype(BF16)
    split = jnp.concatenate(
        [jnp.concatenate([la_hi[c * chunk:(c + 1) * chunk], la_lo[c * chunk:(c + 1) * chunk]], axis=0)
         for c in range(n)], axis=1)
    sums = jnp.dot(seg_ref[...], split, preferred_element_type=F32)

    def seg_sum(block):
        rows = slice(block * chunk, (block + 1) * chunk)
        return jnp.minimum(jnp.concatenate(
            [sums[rows, c * WIDTH_BK:(c + 1) * WIDTH_BK] for c in range(n)], axis=0), 0.0)

    fwd0, rev0 = seg_sum(0), seg_sum(1)
    from_start, to_end = jnp.exp(fwd0), jnp.exp(rev0)
    total = [fwd0[(c + 1) * chunk - 1:(c + 1) * chunk] for c in range(n)]

    def chunk_sum(cs):
        cs = list(cs)
        return (sum(total[c] for c in cs[1:]) + total[cs[0]]) if cs else None

    def extend(base, offsets):
        pieces = []
        for c in range(n):
            x = base[c * chunk:(c + 1) * chunk]
            pieces.append(x if offsets[c] is None else x * jnp.exp(offsets[c]))
        return jnp.concatenate(pieces, axis=0) if n > 1 else pieces[0]

    q_state = (q * extend(from_start, [chunk_sum(range(0, c)) for c in range(n)])).astype(BF16)
    k_state = (k * extend(to_end, [chunk_sum(range(c + 1, n)) for c in range(n)])).astype(BF16)
    decay_tile = jnp.exp(chunk_sum(range(n)))
    ops = {}
    s = tg // 2
    while s >= chunk and n > 1:
        per = s // chunk
        offsets, bases = [], []
        for c in range(n):
            sg = c // per
            offsets.append(chunk_sum(range(sg * per, c)) if sg % 2 else chunk_sum(range(c + 1, (sg + 1) * per)))
            bases.append((from_start if sg % 2 else to_end)[c * chunk:(c + 1) * chunk])
        w = extend(jnp.concatenate(bases, axis=0), offsets)
        ops[s] = ((q * w).astype(BF16), (k * w).astype(BF16))
        s //= 2
    for li in range(1, len(levels)):
        w = jnp.exp(seg_sum(li + 1))
        ops[levels[li]] = ((q * w).astype(BF16), (k * w).astype(BF16))
    q16, k16 = q.astype(BF16), k.astype(BF16)
    yield

    gt = mask_ref.shape[2]
    nt_dims = (((1,), (1,)), ((), ()))
    tn_dims = (((0,), (0,)), ((), ()))
    n_pairs, n_groups = N_HEADS_B // 2, tg // gt
    pair_cols = [slice(p * LANES, (p + 1) * LANES) for p in range(n_pairs)]
    pair_v = [v[:, p * 2 * VAL_DIM_B:(p + 1) * 2 * VAL_DIM_B] for p in range(n_pairs)]
    head_v = lambda e: slice(e * VAL_DIM_B, (e + 1) * VAL_DIM_B)
    o_state = []
    for p, cols in enumerate(pair_cols):
        s0, s1 = state_ref[2 * p], state_ref[2 * p + 1]
        zero = jnp.zeros_like(s0)
        st = jnp.concatenate([jnp.concatenate([s0, zero], axis=1),
                              jnp.concatenate([zero, s1], axis=1)], axis=0)
        o_state.append(jnp.dot(q_state[:, cols], st.astype(BF16), preferred_element_type=F32))
        upd = lax.dot_general(k_state[:, cols], pair_v[p], tn_dims, preferred_element_type=F32)
        decay_rows = jnp.broadcast_to(decay_tile[:, cols], (LANES, LANES)).T
        state_ref[2 * p] = s0 * decay_rows[0:KEY_DIM_B] + upd[0:KEY_DIM_B, 0:VAL_DIM_B]
        state_ref[2 * p + 1] = s1 * decay_rows[KEY_DIM_B:] + upd[KEY_DIM_B:, VAL_DIM_B:]
    yield
    blocks = []
    for p, cols in enumerate(pair_cols):
        for g in range(n_groups):
            rows = slice(g * gt, (g + 1) * gt)
            a = mask_ref[0] * lax.dot_general(_stack_heads(q16[rows, cols]), k16[rows, cols], nt_dims,
                                              preferred_element_type=F32)
            m, s = 1, gt // 2
            while s >= 1:
                qd, kd = ops[s]
                a = a + mask_ref[m] * lax.dot_general(_stack_heads(qd[rows, cols]), kd[rows, cols], nt_dims,
                                                      preferred_element_type=F32)
                m, s = m + 1, s // 2
            blocks.append((p, g * gt, g * gt, a.astype(BF16)))
            yield
        s = tg // 2
        while s >= gt:
            qd, kd = ops[s]
            for blk in range(tg // (2 * s)):
                k0, q0 = blk * 2 * s, blk * 2 * s + s
                ab = lax.dot_general(_stack_heads(qd[q0:q0 + s, cols]), kd[k0:k0 + s, cols], nt_dims,
                                     preferred_element_type=F32)
                blocks.append((p, q0, k0, ab.astype(BF16)))
                yield
            s //= 2
    piece = [[[o_state[p][g * gt:(g + 1) * gt, head_v(e)] for g in range(n_groups)] for e in range(2)]
             for p in range(n_pairs)]
    for p, q0, k0, a in blocks:
        nq, nk = a.shape[0] // 2, a.shape[1]
        for e in range(2):
            contrib = jnp.dot(a[e * nq:(e + 1) * nq], pair_v[p][k0:k0 + nk, head_v(e)], preferred_element_type=F32)
            for gi in range(nq // gt):
                g = q0 // gt + gi
                piece[p][e][g] = piece[p][e][g] + contrib[gi * gt:(gi + 1) * gt]
        yield
    outs = [jnp.concatenate(piece[p][e], axis=0) if n_groups > 1 else piece[p][e][0]
            for p in range(n_pairs) for e in range(2)]
    o_ref[...] = jnp.concatenate(outs, axis=1)


def _gla_kernel(chunk, chunks_per_tile, has_init, *refs):
    if has_init:
        q_ref, k_ref, v_ref, la_ref, seg_ref, mask_ref, s0_ref, o_ref, sout_ref, state = refs
    else:
        q_ref, k_ref, v_ref, la_ref, seg_ref, mask_ref, o_ref, sout_ref, state = refs
    t = pl.program_id(1)

    @pl.when(t == 0)
    def _():
        state[...] = s0_ref[...] if has_init else jnp.zeros_like(state)

    tiles = [_gla_tile(q_ref[i], k_ref[i], v_ref[i], la_ref[i], seg_ref, mask_ref, state.at[i], o_ref.at[i],
                       chunk, chunks_per_tile) for i in range(q_ref.shape[0])]
    for _ in itertools.zip_longest(*tiles):
        pass

    @pl.when(t == pl.num_programs(1) - 1)
    def _():
        sout_ref[...] = state[...]


def _gla(q, k, v, la, s0, n_batch, seq, chunk, chunks_per_tile, n_par):
    tg = chunk * chunks_per_tile
    nt = seq // tg
    seg = jnp.asarray(_segment_matrix(chunk), BF16)
    masks = jnp.asarray(_level_masks(min(tg, LANES)), F32)
    row = lambda w: pl.BlockSpec((n_par, tg, w), lambda g, t: (g, t, 0))
    st = pl.BlockSpec((n_par, N_HEADS_B, KEY_DIM_B, VAL_DIM_B), lambda g, t: (g, 0, 0, 0))
    per_seq = lambda z: z.reshape(n_batch, seq, z.shape[-1])
    in_specs = [row(WIDTH_BK), row(WIDTH_BK), row(WIDTH_BV), row(WIDTH_BK), _const_spec(seg.shape),
                _const_spec(masks.shape)]
    args = [per_seq(q), per_seq(k), per_seq(v), per_seq(la), seg, masks]
    if s0 is not None:
        in_specs.append(st)
        args.append(s0)
    o, s_out = pl.pallas_call(
        functools.partial(_gla_kernel, chunk, chunks_per_tile, s0 is not None),
        grid=(n_batch // n_par, nt),
        in_specs=in_specs,
        out_specs=(row(WIDTH_BV), st),
        out_shape=(jax.ShapeDtypeStruct((n_batch, seq, WIDTH_BV), F32),
                   jax.ShapeDtypeStruct((n_batch, N_HEADS_B, KEY_DIM_B, VAL_DIM_B), F32)),
        scratch_shapes=[pltpu.VMEM((n_par, N_HEADS_B, KEY_DIM_B, VAL_DIM_B), F32)],
        compiler_params=_params(2, VMEM_LIMIT),
        name="gla_chunk%d" % chunk,
    )(*args)
    return o.reshape(n_batch * seq, WIDTH_BV), s_out


def _mem_kv_kernel(m_ref, g_ref, wk32_ref, wv32_ref, k_ref, v_ref, k16_ref, v16_ref, wk_ref, wv_ref):
    @pl.when(pl.program_id(0) == 0)
    def _():
        _cast_rows(wk32_ref, wk_ref)
        _cast_rows(wv32_ref, wv_ref)

    m = _rms(m_ref[...], g_ref[...]).astype(BF16)
    k = jnp.dot(m, wk_ref[...], preferred_element_type=F32)
    v = jnp.dot(m, wv_ref[...], preferred_element_type=F32)
    k_ref[...], v_ref[...] = k, v
    k16_ref[...], v16_ref[...] = k.astype(BF16), v.astype(BF16)


def _mem_kv(mem, g, wk, wv):
    n = mem.shape[0]
    tm = N_MEM
    row = pl.BlockSpec((tm, D_MODEL), lambda i: (i, 0))
    return pl.pallas_call(
        _mem_kv_kernel,
        grid=(n // tm,),
        in_specs=[row, _const_spec((1, D_MODEL)), _const_spec((D_MODEL, D_MODEL)), _const_spec((D_MODEL, D_MODEL))],
        out_specs=(row, row, row, row),
        out_shape=(jax.ShapeDtypeStruct((n, D_MODEL), F32), jax.ShapeDtypeStruct((n, D_MODEL), F32),
                   jax.ShapeDtypeStruct((n, D_MODEL), BF16), jax.ShapeDtypeStruct((n, D_MODEL), BF16)),
        scratch_shapes=[pltpu.VMEM((D_MODEL, D_MODEL), BF16), pltpu.VMEM((D_MODEL, D_MODEL), BF16)],
        compiler_params=_params(1, VMEM_LIMIT),
        name="mem_kv",
    )(mem, g, wk, wv)


def _stack_front(rows, segs, x_ref, oa_ref, ob_ref, r_ref, mk_ref, mv_ref, g_gla_ref, wo_ref, g_post_mix_ref,
                 g_pre_mem_ref, wq_ref, wmo_ref, g_post_mem_ref, result):
    n_rows = rows.stop - rows.start
    ob = ob_ref[rows, :]
    normed = []
    for h in range(N_HEADS_B):
        seg = ob[:, h * VAL_DIM_B:(h + 1) * VAL_DIM_B]
        normed.append(seg * lax.rsqrt(jnp.mean(seg * seg, axis=-1, keepdims=True) + EPS))
    yb = (jnp.concatenate(normed, axis=1) * g_gla_ref[...] * _silu(r_ref[rows, :])).astype(BF16)
    yield
    mix = (jnp.dot(oa_ref[rows, :], wo_ref[0:WIDTH_A, :], preferred_element_type=F32)
           + jnp.dot(yb, wo_ref[WIDTH_A:, :], preferred_element_type=F32))
    yield
    x1 = x_ref[rows, :] + _rms(mix, g_post_mix_ref[...])
    hq = _rms(x1, g_pre_mem_ref[...]).astype(BF16)
    yield
    q = (jnp.dot(hq, wq_ref[...], preferred_element_type=F32) * (HEAD_DIM_MEM ** -0.5)).astype(BF16)
    yield
    rows_per_seg = n_rows // len(segs)
    units = [(u, sg, h) for u, sg in enumerate(segs) for h in range(N_HEADS_MEM)]
    mem_rows = lambda sg: slice(sg * N_MEM, (sg + 1) * N_MEM)
    head_cols = lambda h: slice(h * HEAD_DIM_MEM, (h + 1) * HEAD_DIM_MEM)
    scores = [lax.dot_general(q[u * rows_per_seg:(u + 1) * rows_per_seg, head_cols(h)],
                              mk_ref[mem_rows(sg), head_cols(h)].astype(BF16), NT_DIMS,
                              preferred_element_type=F32) for u, sg, h in units]
    yield
    probs, inv_l = [], []
    for s in scores:
        p = jnp.exp(s - jnp.max(s, axis=-1, keepdims=True))
        inv_l.append(1.0 / jnp.sum(p, axis=-1, keepdims=True))
        probs.append(p.astype(BF16))
    yield
    outs = [jnp.dot(probs[i], mv_ref[mem_rows(sg), head_cols(h)].astype(BF16), preferred_element_type=F32) * inv_l[i]
            for i, (u, sg, h) in enumerate(units)]
    seg_outs = [jnp.concatenate(outs[u * N_HEADS_MEM:(u + 1) * N_HEADS_MEM], axis=1) for u in range(len(segs))]
    o = (jnp.concatenate(seg_outs, axis=0) if len(segs) > 1 else seg_outs[0]).astype(BF16)
    yield
    att = jnp.dot(o, wmo_ref[...], preferred_element_type=F32)
    yield
    result.append(x1 + _rms(att, g_post_mem_ref[...]))


def _stack_ffn(x2, g_pre_ffn_ref, wg_ref, wu_ref, wd_ref, g_post_ffn_ref, y_ref):
    hf = _rms(x2, g_pre_ffn_ref[...]).astype(BF16)
    f = jnp.zeros(x2.shape, F32)
    act, act_cols = None, None
    for c in range(D_FF // FF_BLOCK):
        cols = slice(c * FF_BLOCK, (c + 1) * FF_BLOCK)
        gate = jnp.dot(hf, wg_ref[:, cols], preferred_element_type=F32)
        up = jnp.dot(hf, wu_ref[:, cols], preferred_element_type=F32)
        if act is not None:
            f = f + jnp.dot(act, wd_ref[act_cols, :], preferred_element_type=F32)
        act, act_cols = (_silu(gate) * up).astype(BF16), cols
        yield
    f = f + jnp.dot(act, wd_ref[act_cols, :], preferred_element_type=F32)
    yield
    y_ref[...] = x2 + _rms(f, g_post_ffn_ref[...])


def _stack_kernel(n_seg, x_ref, oa_ref, ob_ref, r_ref, mk_ref, mv_ref,
                  g_gla_ref, wo_ref, g_post_mix_ref, g_pre_mem_ref, wq_ref, wmo_ref, g_post_mem_ref,
                  g_pre_ffn_ref, wg_ref, wu_ref, wd_ref, g_post_ffn_ref, y_ref):
    tm = x_ref.shape[0]
    n_parts = 2 if tm // 2 >= MIN_SPLIT_ROWS else 1
    part = tm // n_parts
    parts = []
    for i in range(n_parts):
        segs = list(range(i * n_seg // n_parts, (i + 1) * n_seg // n_parts)) if n_seg >= n_parts else [0]
        result = []
        parts.append((result, _stack_front(slice(i * part, (i + 1) * part), segs, x_ref, oa_ref, ob_ref, r_ref,
                                           mk_ref, mv_ref, g_gla_ref, wo_ref, g_post_mix_ref, g_pre_mem_ref,
                                           wq_ref, wmo_ref, g_post_mem_ref, result)))
    for _ in itertools.zip_longest(*[gen for _, gen in parts]):
        pass
    ffns = [_stack_ffn(result[0], g_pre_ffn_ref, wg_ref, wu_ref, wd_ref, g_post_ffn_ref,
                       y_ref.at[i * part:(i + 1) * part]) for i, (result, _) in enumerate(parts)]
    for _ in itertools.zip_longest(*ffns):
        pass


def _stack(x, oa, ob, r, mk, mv, weights, tm, n_seg, tiles_per_mem_block):
    ntok = x.shape[0]
    row = lambda w: pl.BlockSpec((tm, w), lambda i: (i, 0))
    mem = pl.BlockSpec((n_seg * N_MEM, D_MODEL), lambda i: (i // tiles_per_mem_block, 0))
    w_specs = [_const_spec(w.shape) for w in weights]
    return pl.pallas_call(
        functools.partial(_stack_kernel, n_seg),
        grid=(ntok // tm,),
        in_specs=[row(D_MODEL), row(WIDTH_A), row(WIDTH_BV), row(WIDTH_BV), mem, mem] + w_specs,
        out_specs=row(D_MODEL),
        out_shape=jax.ShapeDtypeStruct((ntok, D_MODEL), F32),
        compiler_params=_params(1, VMEM_LIMIT),
        name="token_stack",
    )(x, oa, ob, r, mk, mv, *weights)


def _bias_base(table, width):
    u = np.arange(width)
    idx = np.where(u < (BAND_CHUNKS + 1) * CHUNK, np.clip(A_WINDOW - u, -REL_CLIP, REL_CLIP) + REL_CLIP, 2 * REL_CLIP)
    return table[:, idx]


def kernel(x_prompt, x_sample, mem_prompt, cache_a_k, cache_a_v, state_gla, cache_mem_k, cache_mem_v,
           g_pre_mix, w_in, rel_bias, w_alpha2, b_alpha, g_gla_out, w_o, g_post_mix,
           g_pre_mem, g_mem, w_mq, w_mk, w_mv, w_mo, g_post_mem,
           g_pre_ffn, w_ffn_gate, w_ffn_up, w_ffn_down, g_post_ffn):
    depth = w_in.shape[0]
    assert depth == 1
    l = 0
    n_p, seq, _ = x_prompt.shape
    n_s, seq_s, _ = x_sample.shape
    vec = lambda g: g[l].reshape(1, -1)

    wi = w_in[l]
    w_t = jnp.swapaxes(wi, 0, 1)
    wa2 = jnp.pad(w_alpha2[l], ((0, LANES - GATE_RANK), (0, 0))).astype(BF16)
    stack_w = [vec(g_gla_out), w_o[l].astype(BF16), vec(g_post_mix), vec(g_pre_mem), w_mq[l].astype(BF16),
               w_mo[l].astype(BF16), vec(g_post_mem), vec(g_pre_ffn), w_ffn_gate[l].astype(BF16),
               w_ffn_up[l].astype(BF16), w_ffn_down[l].astype(BF16), vec(g_post_ffn)]

    xp = x_prompt.reshape(n_p * seq, D_MODEL)
    keep = min(A_WINDOW, seq)
    qa, ka, va, qb, kb, vb, r, la, k_tail, v_tail = _proj(xp, vec(g_pre_mix), w_t, wa2, vec(b_alpha), TILES.proj_rows,
                                                          seq // TILES.proj_rows, keep, True)
    oa = _attn_prompt(qa, ka, va, _bias_base(rel_bias[l], KEYS), n_p, seq)
    ob, sp = _gla(qb, kb, vb, la, None, n_p, seq, CHUNK, TQ // CHUNK, n_p)
    mk, mv, mk16, mv16 = _mem_kv(mem_prompt.reshape(n_p * N_MEM, D_MODEL), vec(g_mem), w_mk[l], w_mv[l])
    yp = _stack(xp, oa, ob, r, mk16, mv16, stack_w, TILES.stack_rows, 1,
                seq // TILES.stack_rows)

    xs = x_sample.reshape(n_s * seq_s, D_MODEL)
    ntok_s = n_s * seq_s
    qa_s, ka_s, va_s, qb_s, kb_s, vb_s, r_s, la_s, k_new, v_new = _proj(
        xs, vec(g_pre_mix), w_t, wa2, vec(b_alpha), ntok_s, 1, ntok_s, False)
    oa_s = _attn_sample(qa_s, ka_s, va_s, jnp.transpose(cache_a_k[l], (0, 2, 3, 1)),
                        jnp.transpose(cache_a_v[l], (0, 2, 3, 1)), rel_bias[l], seq_s, TILES.sample_attn_requests)
    ob_s, ss = _gla(qb_s, kb_s, vb_s, la_s, state_gla[l], n_s, seq_s, seq_s, 1,
                    TILES.sample_gla_requests)
    seg_s = TILES.sample_stack_requests
    ys = _stack(xs, oa_s, ob_s, r_s,
                cache_mem_k[l].reshape(n_s * N_MEM, D_MODEL),
                cache_mem_v[l].reshape(n_s * N_MEM, D_MODEL), stack_w, seg_s * seq_s, seg_s, 1)

    heads = lambda z, n, t: z.reshape(1, n, t, N_HEADS_A, HEAD_DIM_A)
    tails = lambda z: jnp.transpose(z.reshape(n_p, N_HEADS_A, HEAD_DIM_A, keep), (0, 3, 1, 2))[None]
    return (yp.reshape(n_p, seq, D_MODEL), ys.reshape(n_s, seq_s, D_MODEL),
            tails(k_tail), tails(v_tail), sp[None],
            mk.reshape(1, n_p, N_MEM, N_HEADS_MEM, HEAD_DIM_MEM), mv.reshape(1, n_p, N_MEM, N_HEADS_MEM, HEAD_DIM_MEM),
            heads(k_new, n_s, seq_s), heads(v_new, n_s, seq_s), ss[None])
```

```python
import functools
import itertools
import math
from typing import NamedTuple

import jax
import jax.numpy as jnp
import numpy as np
from jax import lax
from jax.experimental import pallas as pl
from jax.experimental.pallas import tpu as pltpu

F32 = jnp.float32
BF16 = jnp.bfloat16

D_MODEL = 1024
CHUNK = 64
BAND_CHUNKS = 8
A_WINDOW = BAND_CHUNKS * CHUNK
N_HEADS_A = 8
HEAD_DIM_A = 64
WIDTH_A = 512
REL_CLIP = 256
N_HEADS_B = 4
KEY_DIM_B = 64
VAL_DIM_B = 128
WIDTH_BK = 256
WIDTH_BV = 512
GATE_RANK = 16
GATE_TAU = 16.0
N_MEM = 256
N_HEADS_MEM = 4
HEAD_DIM_MEM = 256
D_FF = 2816
EPS = 1e-6
NEG_BIG = -1e30
NT_DIMS = (((1,), (1,)), ((), ()))

LANES = 128
BIAS_W = 640
FF_BLOCK = 256
CAST_ROWS = 512
MIN_SPLIT_ROWS = 256
VMEM_LIMIT = 56 * 1024 * 1024


class _TilePlan(NamedTuple):
    proj_rows: int = 1024
    stack_rows: int = 512
    sample_attn_requests: int = 4
    sample_gla_requests: int = 8
    sample_stack_requests: int = 4


TILES = _TilePlan()

OFF_QA, OFF_KA, OFF_VA, OFF_QB, OFF_KB, OFF_VB, OFF_R = 0, 512, 1024, 1536, 1792, 2048, 2560
IN_WIDTH = OFF_R + GATE_RANK + WIDTH_BV


def _rms(x, g):
    return x * lax.rsqrt(jnp.mean(x * x, axis=-1, keepdims=True) + EPS) * g


def _silu(x):
    return x / (1.0 + jnp.exp(-x))


def _log_sigmoid(z):
    return jnp.minimum(z, 0.0) - jnp.log(1.0 + jnp.exp(-jnp.abs(z)))


def _const_spec(shape):
    nd = len(shape)
    return pl.BlockSpec(shape, lambda *_: (0,) * nd, pipeline_mode=pl.Buffered(1))


def _params(n_axes, vmem=None):
    return pltpu.CompilerParams(dimension_semantics=("arbitrary",) * n_axes, vmem_limit_bytes=vmem)


def _cast_rows(src_ref, dst_ref):
    n = src_ref.shape[0]
    for lo in range(0, n, CAST_ROWS):
        rows = slice(lo, min(lo + CAST_ROWS, n))
        dst_ref[rows, :] = src_ref[rows, :].astype(dst_ref.dtype)


def _proj_kernel(tiles_per_seq, tail_rows, tail_t, x_ref, g_ref, w32_ref, wa2_ref, ba_ref,
                 qa_ref, ka_ref, va_ref, qb_ref, kb_ref, vb_ref, r_ref, la_ref, kt_ref, vt_ref, w_ref):
    @pl.when(pl.program_id(0) == 0)
    def _():
        _cast_rows(w32_ref, w_ref)

    h = _rms(x_ref[...], g_ref[...]).astype(BF16)

    def proj(lo, hi):
        return lax.dot_general(h, w_ref[lo:hi, :], NT_DIMS, preferred_element_type=F32)

    g_low = proj(OFF_R, OFF_R + LANES).astype(BF16)
    z = jnp.dot(g_low, wa2_ref[...], preferred_element_type=F32) + ba_ref[...]
    la_ref[...] = _log_sigmoid(z) * (1.0 / GATE_TAU)
    qa_ref[...] = (proj(OFF_QA, OFF_KA) * (HEAD_DIM_A ** -0.5)).astype(BF16)
    ka = proj(OFF_KA, OFF_VA)
    va = proj(OFF_VA, OFF_QB)
    ka_ref[...] = ka.astype(BF16)
    va_ref[...] = va.astype(BF16)
    qb_ref[...] = proj(OFF_QB, OFF_KB) * (KEY_DIM_B ** -0.5)
    kb_ref[...] = proj(OFF_KB, OFF_VB)
    vb_ref[...] = proj(OFF_VB, OFF_R).astype(BF16)
    r_ref[...] = proj(OFF_R + GATE_RANK, IN_WIDTH)

    @pl.when(pl.program_id(0) % tiles_per_seq == tiles_per_seq - 1)
    def _():
        k_keep, v_keep = ka[ka.shape[0] - tail_rows:], va[va.shape[0] - tail_rows:]
        kt_ref[...] = k_keep.T if tail_t else k_keep
        vt_ref[...] = v_keep.T if tail_t else v_keep


def _proj(x, g, w_t, wa2, ba, tm, tiles_per_seq, tail_rows, tail_t):
    ntok = x.shape[0]
    nt = ntok // tm
    nseq = nt // tiles_per_seq
    row = lambda w: pl.BlockSpec((tm, w), lambda i: (i, 0))
    assert tail_rows <= tm
    tail_shape = (WIDTH_A, tail_rows) if tail_t else (tail_rows, WIDTH_A)
    tail = pl.BlockSpec(tail_shape, lambda i: (i // tiles_per_seq, 0))
    out_shape = (
        jax.ShapeDtypeStruct((ntok, WIDTH_A), BF16),
        jax.ShapeDtypeStruct((ntok, WIDTH_A), BF16),
        jax.ShapeDtypeStruct((ntok, WIDTH_A), BF16),
        jax.ShapeDtypeStruct((ntok, WIDTH_BK), F32),
        jax.ShapeDtypeStruct((ntok, WIDTH_BK), F32),
        jax.ShapeDtypeStruct((ntok, WIDTH_BV), BF16),
        jax.ShapeDtypeStruct((ntok, WIDTH_BV), F32),
        jax.ShapeDtypeStruct((ntok, WIDTH_BK), F32),
        jax.ShapeDtypeStruct((nseq * tail_shape[0], tail_shape[1]), F32),
        jax.ShapeDtypeStruct((nseq * tail_shape[0], tail_shape[1]), F32),
    )
    return pl.pallas_call(
        functools.partial(_proj_kernel, tiles_per_seq, tail_rows, tail_t),
        grid=(nt,),
        in_specs=[row(D_MODEL), _const_spec((1, D_MODEL)), _const_spec((IN_WIDTH, D_MODEL)),
                  _const_spec((LANES, WIDTH_BK)), _const_spec((1, WIDTH_BK))],
        out_specs=(row(WIDTH_A), row(WIDTH_A), row(WIDTH_A), row(WIDTH_BK), row(WIDTH_BK),
                   row(WIDTH_BV), row(WIDTH_BV), row(WIDTH_BK), tail, tail),
        out_shape=out_shape,
        scratch_shapes=[pltpu.VMEM((IN_WIDTH, D_MODEL), BF16)],
        compiler_params=_params(1, VMEM_LIMIT),
        name="proj",
    )(x, g, w_t, wa2, ba)


def _stack_heads(x_pair):
    low_half = lax.broadcasted_iota(jnp.int32, x_pair.shape, 1) < HEAD_DIM_A
    zero = jnp.zeros_like(x_pair)
    return jnp.concatenate([jnp.where(low_half, x_pair, zero), jnp.where(low_half, zero, x_pair)], axis=0)


TQ = 4 * CHUNK
KEYS = 3 * TQ
SOFT_W = 5 * LANES
assert 2 * TQ == A_WINDOW and SOFT_W >= (BAND_CHUNKS + 1) * CHUNK + CHUNK


def _build_band_bias(base_ref, bm_ref):
    n_band = (BAND_CHUNKS + 1) * CHUNK
    col = lax.broadcasted_iota(jnp.int32, (CHUNK, KEYS), 1)
    for h in range(N_HEADS_A):
        rows = jnp.broadcast_to(base_ref[h:h + 1, :], (CHUNK, KEYS))
        for qc in range(TQ // CHUNK):
            toeplitz = pltpu.roll(rows, qc * CHUNK, 1, stride=1, stride_axis=0)
            in_band = (col >= qc * CHUNK) & (col < qc * CHUNK + n_band)
            r0 = (qc * 2 + h % 2) * CHUNK
            bm_ref[h // 2, r0:r0 + CHUNK, :] = jnp.where(in_band, toeplitz, NEG_BIG)


def _band_tile(q_ref, k_refs, v_refs, bm_ref, o_ref, start_penalty):
    n_qc, n_pairs = TQ // CHUNK, N_HEADS_A // 2
    low_half = lax.broadcasted_iota(jnp.int32, (CHUNK, LANES), 1) < HEAD_DIM_A
    pair_cols = lambda hp: slice(hp * LANES, (hp + 1) * LANES)

    def scores(hp):
        cols = pair_cols(hp)
        qs = jnp.concatenate([_stack_heads(q_ref[qc * CHUNK:(qc + 1) * CHUNK, cols]) for qc in range(n_qc)],
                             axis=0)
        return jnp.concatenate([lax.dot_general(qs, k[:, cols], NT_DIMS, preferred_element_type=F32)
                                for k in k_refs], axis=1)

    def softmax(hp, s):
        probs, inv_l = [], []
        for qc in range(n_qc):
            rows = slice(qc * 2 * CHUNK, (qc + 1) * 2 * CHUNK)
            c0 = 0 if (qc + 1) * CHUNK + A_WINDOW <= SOFT_W else KEYS - SOFT_W
            sq = s[rows, c0:c0 + SOFT_W] + bm_ref[hp, rows, c0:c0 + SOFT_W]
            if start_penalty is not None:
                sq = sq + start_penalty[:, c0:c0 + SOFT_W]
            p = jnp.exp(sq - jnp.max(sq, axis=-1, keepdims=True))
            inv_l.append(1.0 / jnp.sum(p, axis=-1, keepdims=True))
            pad = jnp.zeros((2 * CHUNK, KEYS - SOFT_W), BF16)
            probs.append(jnp.concatenate([p.astype(BF16), pad] if c0 == 0 else [pad, p.astype(BF16)], axis=1))
        return jnp.concatenate(probs, axis=0), inv_l

    def values(hp, pm, inv_l):
        cols = pair_cols(hp)
        out = sum(jnp.dot(pm[:, i * TQ:(i + 1) * TQ], v[:, cols], preferred_element_type=F32)
                  for i, v in enumerate(v_refs))
        for qc in range(n_qc):
            r0 = qc * 2 * CHUNK
            o0 = out[r0:r0 + CHUNK] * inv_l[qc][0:CHUNK]
            o1 = out[r0 + CHUNK:r0 + 2 * CHUNK] * inv_l[qc][CHUNK:2 * CHUNK]
            o_ref[qc * CHUNK:(qc + 1) * CHUNK, cols] = jnp.where(low_half, o0, o1).astype(o_ref.dtype)

    for hp in range(n_pairs):
        s = scores(hp)
        yield
        pm, inv_l = softmax(hp, s)
        yield
        values(hp, pm, inv_l)
        yield


def _attn_prompt_kernel(q_ref, k2_ref, k1_ref, k0_ref, v2_ref, v1_ref, v0_ref, base_ref, o_ref, bm):
    t = pl.program_id(1)

    @pl.when((pl.program_id(0) == 0) & (t == 0))
    def _():
        _build_band_bias(base_ref, bm)

    def tiles(penalty):
        gens = [_band_tile(q_ref.at[i], (k2_ref.at[i], k1_ref.at[i], k0_ref.at[i]),
                           (v2_ref.at[i], v1_ref.at[i], v0_ref.at[i]), bm, o_ref.at[i], penalty)
                for i in range(q_ref.shape[0])]
        for _ in itertools.zip_longest(*gens):
            pass

    @pl.when(t >= 2)
    def _():
        tiles(None)

    @pl.when(t < 2)
    def _():
        col = lax.broadcasted_iota(jnp.int32, (1, KEYS), 1)
        tiles(jnp.where(col < (2 - t) * TQ, NEG_BIG, 0.0))


def _attn_prompt(q, k, v, base, n_batch, seq):
    nt = seq // TQ
    n_par = n_batch
    blk = lambda back: pl.BlockSpec((n_par, TQ, WIDTH_A), lambda g, t: (g, jnp.maximum(t - back, 0), 0))
    per_seq = lambda z: z.reshape(n_batch, seq, WIDTH_A)
    q, k, v = per_seq(q), per_seq(k), per_seq(v)
    return pl.pallas_call(
        _attn_prompt_kernel,
        grid=(n_batch // n_par, nt),
        in_specs=[blk(0), blk(2), blk(1), blk(0), blk(2), blk(1), blk(0), _const_spec((N_HEADS_A, KEYS))],
        out_specs=blk(0),
        out_shape=jax.ShapeDtypeStruct((n_batch, seq, WIDTH_A), BF16),
        scratch_shapes=[pltpu.VMEM((N_HEADS_A // 2, 2 * TQ, KEYS), F32)],
        compiler_params=_params(2, VMEM_LIMIT),
        name="band_attn_prompt",
    )(q, k, k, k, v, v, v, base).reshape(n_batch * seq, WIDTH_A)


def _attn_sample_one(rows, q_ref, kn_ref, vn_ref, kc_ref, vc_ref, sel_ref, dup_ref, o_ref, bias_c, bias_n):
    n_h = N_HEADS_A
    n_new = rows.stop - rows.start
    low_half = lax.broadcasted_iota(jnp.int32, (n_new, LANES), 1) < HEAD_DIM_A
    pairs = [slice(hp * LANES, (hp + 1) * LANES) for hp in range(n_h // 2)]
    q_heads = [jnp.dot(q_ref[rows, cols], sel_ref[e], preferred_element_type=F32).astype(BF16)
               for cols in pairs for e in range(2)]
    yield
    s_c = jnp.concatenate([jnp.dot(q_heads[h], kc_ref[h].astype(BF16), preferred_element_type=F32)
                           for h in range(n_h)], axis=0) + bias_c[...]
    s_n = jnp.concatenate([lax.dot_general(_stack_heads(q_ref[rows, cols]), kn_ref[rows, cols], NT_DIMS,
                                           preferred_element_type=F32) for cols in pairs], axis=0) + bias_n[...]
    yield
    m = jnp.maximum(jnp.max(s_c, axis=-1, keepdims=True), jnp.max(s_n, axis=-1, keepdims=True))
    p_c, p_n = jnp.exp(s_c - m), jnp.exp(s_n - m)
    inv_l = 1.0 / (jnp.sum(p_c, axis=-1, keepdims=True) + jnp.sum(p_n, axis=-1, keepdims=True))
    p_c, p_n = p_c.astype(BF16), p_n.astype(BF16)
    yield
    out_c = jnp.concatenate([lax.dot_general(p_c[h * n_new:(h + 1) * n_new], vc_ref[h].astype(BF16), NT_DIMS,
                                             preferred_element_type=F32) for h in range(n_h)], axis=0) * inv_l
    yield
    hi = out_c.astype(BF16)
    lo = (out_c - hi.astype(F32)).astype(BF16)
    out_c = (jnp.dot(hi, dup_ref[...], preferred_element_type=F32)
             + jnp.dot(lo, dup_ref[...], preferred_element_type=F32))
    for hp, cols in enumerate(pairs):
        pr = slice(hp * 2 * n_new, (hp + 1) * 2 * n_new)
        out = out_c[pr] + jnp.dot(p_n[pr], vn_ref[rows, cols], preferred_element_type=F32) * inv_l[pr]
        o_ref[rows, cols] = jnp.where(low_half, out[0:n_new], out[n_new:]).astype(o_ref.dtype)


def _attn_sample_kernel(n_new, q_ref, kn_ref, vn_ref, kc_ref, vc_ref, base_ref, sel_ref, dup_ref,
                        o_ref, bias_c, bias_n):
    n_cache = kc_ref.shape[3]

    @pl.when(pl.program_id(0) == 0)
    def _():
        for h in range(N_HEADS_A):
            rows = jnp.broadcast_to(base_ref[h:h + 1, :], (n_new, BIAS_W))
            toeplitz = pltpu.roll(rows, 0, 1, stride=1, stride_axis=0)
            bias_c[h * n_new:(h + 1) * n_new, :] = toeplitz[:, 0:n_cache]
            bias_n[h * n_new:(h + 1) * n_new, :] = toeplitz[:, n_cache:n_cache + n_new]

    requests = [_attn_sample_one(slice(i * n_new, (i + 1) * n_new), q_ref, kn_ref, vn_ref, kc_ref.at[i],
                                 vc_ref.at[i], sel_ref, dup_ref, o_ref, bias_c, bias_n)
                for i in range(kc_ref.shape[0])]
    for _ in itertools.zip_longest(*requests):
        pass


def _attn_sample(q, k_new, v_new, k_cache_t, v_cache_t, table, n_new, n_par):
    n_batch, n_h, _, n_cache = k_cache_t.shape
    assert n_cache == A_WINDOW and n_new <= CHUNK and n_h == N_HEADS_A
    lane = np.arange(LANES)
    sel = np.stack([lane[:, None] == np.arange(HEAD_DIM_A)[None, :] + e * HEAD_DIM_A for e in range(2)])
    dup = np.arange(HEAD_DIM_A)[:, None] == lane[None, :] % HEAD_DIM_A
    new = pl.BlockSpec((n_par * n_new, WIDTH_A), lambda g: (g, 0))
    old = pl.BlockSpec((n_par, n_h, HEAD_DIM_A, n_cache), lambda g: (g, 0, 0, 0))
    return pl.pallas_call(
        functools.partial(_attn_sample_kernel, n_new),
        grid=(n_batch // n_par,),
        in_specs=[new, new, new, old, old, _const_spec((n_h, BIAS_W)),
                  _const_spec((2, LANES, HEAD_DIM_A)), _const_spec((HEAD_DIM_A, LANES))],
        out_specs=new,
        out_shape=jax.ShapeDtypeStruct((n_batch * n_new, WIDTH_A), BF16),
        scratch_shapes=[pltpu.VMEM((n_h * n_new, n_cache), F32), pltpu.VMEM((n_h * n_new, n_new), F32)],
        compiler_params=_params(1, VMEM_LIMIT),
        name="band_attn_sample",
    )(q, k_new, v_new, k_cache_t, v_cache_t, _bias_base(table, BIAS_W), jnp.asarray(sel, BF16),
      jnp.asarray(dup, BF16))


def _gla_levels(chunk):
    return [chunk >> i for i in range(int(math.log2(chunk)) + 1)]


def _segment_matrix(chunk):
    i = np.arange(chunk)[:, None]
    t = np.arange(chunk)[None, :]
    blocks = []
    for li, s in enumerate(_gla_levels(chunk)):
        start = (i // s) * s
        f_rows = (t >= start) & (t <= i)
        r_rows = (t > i) & (t <= start + s - 1)
        if li == 0:
            blocks += [f_rows, r_rows]
        else:
            blocks.append(np.where((i // s) % 2 == 1, f_rows, r_rows))
    seg = np.concatenate(blocks, axis=0).astype(np.float32)
    return np.concatenate([seg, seg], axis=1)


def _level_masks(group_tokens):
    tg = group_tokens
    i = (np.arange(2 * tg) % tg)[:, None]
    j = np.arange(tg)[None, :]
    masks = [i == j]
    s = tg // 2
    while s >= 1:
        masks.append(((i // s) % 2 == 1) & (j // s == i // s - 1))
        s //= 2
    return np.stack(masks).astype(np.float32)


def _gla_tile(q, k, v, la, seg_ref, mask_ref, state_ref, o_ref, chunk, n_chunks):
    n, tg = n_chunks, chunk * n_chunks
    levels = _gla_levels(chunk)
    la_hi = la.astype(BF16)
    la_lo = (la - la_hi.astype(F32)).astype(BF16)
    split = jnp.concatenate(
        [jnp.concatenate([la_hi[c * chunk:(c + 1) * chunk], la_lo[c * chunk:(c + 1) * chunk]], axis=0)
         for c in range(n)], axis=1)
    sums = jnp.dot(seg_ref[...], split, preferred_element_type=F32)

    def seg_sum(block):
        rows = slice(block * chunk, (block + 1) * chunk)
        return jnp.minimum(jnp.concatenate(
            [sums[rows, c * WIDTH_BK:(c + 1) * WIDTH_BK] for c in range(n)], axis=0), 0.0)

    fwd0, rev0 = seg_sum(0), seg_sum(1)
    from_start, to_end = jnp.exp(fwd0), jnp.exp(rev0)
    total = [fwd0[(c + 1) * chunk - 1:(c + 1) * chunk] for c in range(n)]

    def chunk_sum(cs):
        cs = list(cs)
        return (sum(total[c] for c in cs[1:]) + total[cs[0]]) if cs else None

    def extend(base, offsets):
        pieces = []
        for c in range(n):
            x = base[c * chunk:(c + 1) * chunk]
            pieces.append(x if offsets[c] is None else x * jnp.exp(offsets[c]))
        return jnp.concatenate(pieces, axis=0) if n > 1 else pieces[0]

    q_state = (q * extend(from_start, [chunk_sum(range(0, c)) for c in range(n)])).astype(BF16)
    k_state = (k * extend(to_end, [chunk_sum(range(c + 1, n)) for c in range(n)])).astype(BF16)
    decay_tile = jnp.exp(chunk_sum(range(n)))
    ops = {}
    s = tg // 2
    while s >= chunk and n > 1:
        per = s // chunk
        offsets, bases = [], []
        for c in range(n):
            sg = c // per
            offsets.append(chunk_sum(range(sg * per, c)) if sg % 2 else chunk_sum(range(c + 1, (sg + 1) * per)))
            bases.append((from_start if sg % 2 else to_end)[c * chunk:(c + 1) * chunk])
        w = extend(jnp.concatenate(bases, axis=0), offsets)
        ops[s] = ((q * w).astype(BF16), (k * w).astype(BF16))
        s //= 2
    for li in range(1, len(levels)):
        w = jnp.exp(seg_sum(li + 1))
        ops[levels[li]] = ((q * w).astype(BF16), (k * w).astype(BF16))
    q16, k16 = q.astype(BF16), k.astype(BF16)
    yield

    gt = mask_ref.shape[2]
    nt_dims = (((1,), (1,)), ((), ()))
    tn_dims = (((0,), (0,)), ((), ()))
    n_pairs, n_groups = N_HEADS_B // 2, tg // gt
    pair_cols = [slice(p * LANES, (p + 1) * LANES) for p in range(n_pairs)]
    pair_v = [v[:, p * 2 * VAL_DIM_B:(p + 1) * 2 * VAL_DIM_B] for p in range(n_pairs)]
    head_v = lambda e: slice(e * VAL_DIM_B, (e + 1) * VAL_DIM_B)
    o_state = []
    for p, cols in enumerate(pair_cols):
        s0, s1 = state_ref[2 * p], state_ref[2 * p + 1]
        zero = jnp.zeros_like(s0)
        st = jnp.concatenate([jnp.concatenate([s0, zero], axis=1),
                              jnp.concatenate([zero, s1], axis=1)], axis=0)
        o_state.append(jnp.dot(q_state[:, cols], st.astype(BF16), preferred_element_type=F32))
        upd = lax.dot_general(k_state[:, cols], pair_v[p], tn_dims, preferred_element_type=F32)
        decay_rows = jnp.broadcast_to(decay_tile[:, cols], (LANES, LANES)).T
        state_ref[2 * p] = s0 * decay_rows[0:KEY_DIM_B] + upd[0:KEY_DIM_B, 0:VAL_DIM_B]
        state_ref[2 * p + 1] = s1 * decay_rows[KEY_DIM_B:] + upd[KEY_DIM_B:, VAL_DIM_B:]
    yield
    blocks = []
    for p, cols in enumerate(pair_cols):
        for g in range(n_groups):
            rows = slice(g * gt, (g + 1) * gt)
            a = mask_ref[0] * lax.dot_general(_stack_heads(q16[rows, cols]), k16[rows, cols], nt_dims,
                                              preferred_element_type=F32)
            m, s = 1, gt // 2
            while s >= 1:
                qd, kd = ops[s]
                a = a + mask_ref[m] * lax.dot_general(_stack_heads(qd[rows, cols]), kd[rows, cols], nt_dims,
                                                      preferred_element_type=F32)
                m, s = m + 1, s // 2
            blocks.append((p, g * gt, g * gt, a.astype(BF16)))
            yield
        s = tg // 2
        while s >= gt:
            qd, kd = ops[s]
            for blk in range(tg // (2 * s)):
                k0, q0 = blk * 2 * s, blk * 2 * s + s
                ab = lax.dot_general(_stack_heads(qd[q0:q0 + s, cols]), kd[k0:k0 + s, cols], nt_dims,
                                     preferred_element_type=F32)
                blocks.append((p, q0, k0, ab.astype(BF16)))
                yield
            s //= 2
    piece = [[[o_state[p][g * gt:(g + 1) * gt, head_v(e)] for g in range(n_groups)] for e in range(2)]
             for p in range(n_pairs)]
    for p, q0, k0, a in blocks:
        nq, nk = a.shape[0] // 2, a.shape[1]
        for e in range(2):
            contrib = jnp.dot(a[e * nq:(e + 1) * nq], pair_v[p][k0:k0 + nk, head_v(e)], preferred_element_type=F32)
            for gi in range(nq // gt):
                g = q0 // gt + gi
                piece[p][e][g] = piece[p][e][g] + contrib[gi * gt:(gi + 1) * gt]
        yield
    outs = [jnp.concatenate(piece[p][e], axis=0) if n_groups > 1 else piece[p][e][0]
            for p in range(n_pairs) for e in range(2)]
    o_ref[...] = jnp.concatenate(outs, axis=1)


def _gla_kernel(chunk, chunks_per_tile, has_init, *refs):
    if has_init:
        q_ref, k_ref, v_ref, la_ref, seg_ref, mask_ref, s0_ref, o_ref, sout_ref, state = refs
    else:
        q_ref, k_ref, v_ref, la_ref, seg_ref, mask_ref, o_ref, sout_ref, state = refs
    t = pl.program_id(1)

    @pl.when(t == 0)
    def _():
        state[...] = s0_ref[...] if has_init else jnp.zeros_like(state)

    tiles = [_gla_tile(q_ref[i], k_ref[i], v_ref[i], la_ref[i], seg_ref, mask_ref, state.at[i], o_ref.at[i],
                       chunk, chunks_per_tile) for i in range(q_ref.shape[0])]
    for _ in itertools.zip_longest(*tiles):
        pass

    @pl.when(t == pl.num_programs(1) - 1)
    def _():
        sout_ref[...] = state[...]


def _gla(q, k, v, la, s0, n_batch, seq, chunk, chunks_per_tile, n_par):
    tg = chunk * chunks_per_tile
    nt = seq // tg
    seg = jnp.asarray(_segment_matrix(chunk), BF16)
    masks = jnp.asarray(_level_masks(min(tg, LANES)), F32)
    row = lambda w: pl.BlockSpec((n_par, tg, w), lambda g, t: (g, t, 0))
    st = pl.BlockSpec((n_par, N_HEADS_B, KEY_DIM_B, VAL_DIM_B), lambda g, t: (g, 0, 0, 0))
    per_seq = lambda z: z.reshape(n_batch, seq, z.shape[-1])
    in_specs = [row(WIDTH_BK), row(WIDTH_BK), row(WIDTH_BV), row(WIDTH_BK), _const_spec(seg.shape),
                _const_spec(masks.shape)]
    args = [per_seq(q), per_seq(k), per_seq(v), per_seq(la), seg, masks]
    if s0 is not None:
        in_specs.append(st)
        args.append(s0)
    o, s_out = pl.pallas_call(
        functools.partial(_gla_kernel, chunk, chunks_per_tile, s0 is not None),
        grid=(n_batch // n_par, nt),
        in_specs=in_specs,
        out_specs=(row(WIDTH_BV), st),
        out_shape=(jax.ShapeDtypeStruct((n_batch, seq, WIDTH_BV), F32),
                   jax.ShapeDtypeStruct((n_batch, N_HEADS_B, KEY_DIM_B, VAL_DIM_B), F32)),
        scratch_shapes=[pltpu.VMEM((n_par, N_HEADS_B, KEY_DIM_B, VAL_DIM_B), F32)],
        compiler_params=_params(2, VMEM_LIMIT),
        name="gla_chunk%d" % chunk,
    )(*args)
    return o.reshape(n_batch * seq, WIDTH_BV), s_out


def _mixers_kernel(chunks_per_tile, q_ref, k2_ref, k1_ref, k0_ref, v2_ref, v1_ref, v0_ref, base_ref,
                   gq_ref, gk_ref, gv_ref, la_ref, seg_ref, mask_ref, oa_ref, ob_ref, sout_ref, bm, state):
    t = pl.program_id(0)

    @pl.when(t == 0)
    def _():
        _build_band_bias(base_ref, bm)
        state[...] = jnp.zeros_like(state)

    def tiles(penalty):
        n_par = q_ref.shape[0]
        gens = [_band_tile(q_ref.at[i], (k2_ref.at[i], k1_ref.at[i], k0_ref.at[i]),
                           (v2_ref.at[i], v1_ref.at[i], v0_ref.at[i]), bm, oa_ref.at[i], penalty)
                for i in range(n_par)]
        gens += [_gla_tile(gq_ref[i], gk_ref[i], gv_ref[i], la_ref[i], seg_ref, mask_ref, state.at[i],
                           ob_ref.at[i], CHUNK, chunks_per_tile) for i in range(n_par)]
        for _ in itertools.zip_longest(*gens):
            pass

    @pl.when(t >= 2)
    def _():
        tiles(None)

    @pl.when(t < 2)
    def _():
        col = lax.broadcasted_iota(jnp.int32, (1, KEYS), 1)
        tiles(jnp.where(col < (2 - t) * TQ, NEG_BIG, 0.0))

    @pl.when(t == pl.num_programs(0) - 1)
    def _():
        sout_ref[...] = state[...]


def _mixers_prompt(qa, ka, va, base, qb, kb, vb, la, n_batch, seq):
    nt = seq // TQ
    chunks_per_tile = TQ // CHUNK
    seg = jnp.asarray(_segment_matrix(CHUNK), BF16)
    masks = jnp.asarray(_level_masks(min(TQ, LANES)), F32)
    per_seq = lambda z: z.reshape(n_batch, seq, z.shape[-1])
    blk = lambda back: pl.BlockSpec((n_batch, TQ, WIDTH_A), lambda t: (0, jnp.maximum(t - back, 0), 0))
    row = lambda w: pl.BlockSpec((n_batch, TQ, w), lambda t: (0, t, 0))
    st_shape = (n_batch, N_HEADS_B, KEY_DIM_B, VAL_DIM_B)
    qa, ka, va = per_seq(qa), per_seq(ka), per_seq(va)
    oa, ob, s_out = pl.pallas_call(
        functools.partial(_mixers_kernel, chunks_per_tile),
        grid=(nt,),
        in_specs=[blk(0), blk(2), blk(1), blk(0), blk(2), blk(1), blk(0), _const_spec((N_HEADS_A, KEYS)),
                  row(WIDTH_BK), row(WIDTH_BK), row(WIDTH_BV), row(WIDTH_BK), _const_spec(seg.shape),
                  _const_spec(masks.shape)],
        out_specs=(blk(0), row(WIDTH_BV), pl.BlockSpec(st_shape, lambda t: (0, 0, 0, 0))),
        out_shape=(jax.ShapeDtypeStruct((n_batch, seq, WIDTH_A), BF16),
                   jax.ShapeDtypeStruct((n_batch, seq, WIDTH_BV), F32),
                   jax.ShapeDtypeStruct(st_shape, F32)),
        scratch_shapes=[pltpu.VMEM((N_HEADS_A // 2, 2 * TQ, KEYS), F32), pltpu.VMEM(st_shape, F32)],
        compiler_params=_params(1, VMEM_LIMIT),
        name="mixers_prompt",
    )(qa, ka, ka, ka, va, va, va, base, per_seq(qb), per_seq(kb), per_seq(vb), per_seq(la), seg, masks)
    return oa.reshape(n_batch * seq, WIDTH_A), ob.reshape(n_batch * seq, WIDTH_BV), s_out


def _mem_kv_kernel(m_ref, g_ref, wk32_ref, wv32_ref, k_ref, v_ref, k16_ref, v16_ref, wk_ref, wv_ref):
    @pl.when(pl.program_id(0) == 0)
    def _():
        _cast_rows(wk32_ref, wk_ref)
        _cast_rows(wv32_ref, wv_ref)

    m = _rms(m_ref[...], g_ref[...]).astype(BF16)
    k = jnp.dot(m, wk_ref[...], preferred_element_type=F32)
    v = jnp.dot(m, wv_ref[...], preferred_element_type=F32)
    k_ref[...], v_ref[...] = k, v
    k16_ref[...], v16_ref[...] = k.astype(BF16), v.astype(BF16)


def _mem_kv(mem, g, wk, wv):
    n = mem.shape[0]
    tm = N_MEM
    row = pl.BlockSpec((tm, D_MODEL), lambda i: (i, 0))
    return pl.pallas_call(
        _mem_kv_kernel,
        grid=(n // tm,),
        in_specs=[row, _const_spec((1, D_MODEL)), _const_spec((D_MODEL, D_MODEL)), _const_spec((D_MODEL, D_MODEL))],
        out_specs=(row, row, row, row),
        out_shape=(jax.ShapeDtypeStruct((n, D_MODEL), F32), jax.ShapeDtypeStruct((n, D_MODEL), F32),
                   jax.ShapeDtypeStruct((n, D_MODEL), BF16), jax.ShapeDtypeStruct((n, D_MODEL), BF16)),
        scratch_shapes=[pltpu.VMEM((D_MODEL, D_MODEL), BF16), pltpu.VMEM((D_MODEL, D_MODEL), BF16)],
        compiler_params=_params(1, VMEM_LIMIT),
        name="mem_kv",
    )(mem, g, wk, wv)


def _stack_front(rows, segs, x_ref, oa_ref, ob_ref, r_ref, mk_ref, mv_ref, g_gla_ref, wo_ref, g_post_mix_ref,
                 g_pre_mem_ref, wq_ref, wmo_ref, g_post_mem_ref, result):
    n_rows = rows.stop - rows.start
    ob = ob_ref[rows, :]
    normed = []
    for h in range(N_HEADS_B):
        seg = ob[:, h * VAL_DIM_B:(h + 1) * VAL_DIM_B]
        normed.append(seg * lax.rsqrt(jnp.mean(seg * seg, axis=-1, keepdims=True) + EPS))
    yb = (jnp.concatenate(normed, axis=1) * g_gla_ref[...] * _silu(r_ref[rows, :])).astype(BF16)
    yield
    mix = (jnp.dot(oa_ref[rows, :], wo_ref[0:WIDTH_A, :], preferred_element_type=F32)
           + jnp.dot(yb, wo_ref[WIDTH_A:, :], preferred_element_type=F32))
    yield
    x1 = x_ref[rows, :] + _rms(mix, g_post_mix_ref[...])
    hq = _rms(x1, g_pre_mem_ref[...]).astype(BF16)
    yield
    q = (jnp.dot(hq, wq_ref[...], preferred_element_type=F32) * (HEAD_DIM_MEM ** -0.5)).astype(BF16)
    yield
    rows_per_seg = n_rows // len(segs)
    units = [(u, sg, h) for u, sg in enumerate(segs) for h in range(N_HEADS_MEM)]
    mem_rows = lambda sg: slice(sg * N_MEM, (sg + 1) * N_MEM)
    head_cols = lambda h: slice(h * HEAD_DIM_MEM, (h + 1) * HEAD_DIM_MEM)
    scores = [lax.dot_general(q[u * rows_per_seg:(u + 1) * rows_per_seg, head_cols(h)],
                              mk_ref[mem_rows(sg), head_cols(h)].astype(BF16), NT_DIMS,
                              preferred_element_type=F32) for u, sg, h in units]
    yield
    probs, inv_l = [], []
    for s in scores:
        p = jnp.exp(s - jnp.max(s, axis=-1, keepdims=True))
        inv_l.append(1.0 / jnp.sum(p, axis=-1, keepdims=True))
        probs.append(p.astype(BF16))
    yield
    outs = [jnp.dot(probs[i], mv_ref[mem_rows(sg), head_cols(h)].astype(BF16), preferred_element_type=F32) * inv_l[i]
            for i, (u, sg, h) in enumerate(units)]
    seg_outs = [jnp.concatenate(outs[u * N_HEADS_MEM:(u + 1) * N_HEADS_MEM], axis=1) for u in range(len(segs))]
    o = (jnp.concatenate(seg_outs, axis=0) if len(segs) > 1 else seg_outs[0]).astype(BF16)
    yield
    att = jnp.dot(o, wmo_ref[...], preferred_element_type=F32)
    yield
    result.append(x1 + _rms(att, g_post_mem_ref[...]))


def _stack_ffn(x2, g_pre_ffn_ref, wg_ref, wu_ref, wd_ref, g_post_ffn_ref, y_ref):
    hf = _rms(x2, g_pre_ffn_ref[...]).astype(BF16)
    f = jnp.zeros(x2.shape, F32)
    act, act_cols = None, None
    for c in range(D_FF // FF_BLOCK):
        cols = slice(c * FF_BLOCK, (c + 1) * FF_BLOCK)
        gate = jnp.dot(hf, wg_ref[:, cols], preferred_element_type=F32)
        up = jnp.dot(hf, wu_ref[:, cols], preferred_element_type=F32)
        if act is not None:
            f = f + jnp.dot(act, wd_ref[act_cols, :], preferred_element_type=F32)
        act, act_cols = (_silu(gate) * up).astype(BF16), cols
        yield
    f = f + jnp.dot(act, wd_ref[act_cols, :], preferred_element_type=F32)
    yield
    y_ref[...] = x2 + _rms(f, g_post_ffn_ref[...])


def _stack_kernel(n_seg, x_ref, oa_ref, ob_ref, r_ref, mk_ref, mv_ref,
                  g_gla_ref, wo_ref, g_post_mix_ref, g_pre_mem_ref, wq_ref, wmo_ref, g_post_mem_ref,
                  g_pre_ffn_ref, wg_ref, wu_ref, wd_ref, g_post_ffn_ref, y_ref):
    tm = x_ref.shape[0]
    n_parts = 2 if tm // 2 >= MIN_SPLIT_ROWS else 1
    part = tm // n_parts
    parts = []
    for i in range(n_parts):
        segs = list(range(i * n_seg // n_parts, (i + 1) * n_seg // n_parts)) if n_seg >= n_parts else [0]
        result = []
        parts.append((result, _stack_front(slice(i * part, (i + 1) * part), segs, x_ref, oa_ref, ob_ref, r_ref,
                                           mk_ref, mv_ref, g_gla_ref, wo_ref, g_post_mix_ref, g_pre_mem_ref,
                                           wq_ref, wmo_ref, g_post_mem_ref, result)))
    for _ in itertools.zip_longest(*[gen for _, gen in parts]):
        pass
    ffns = [_stack_ffn(result[0], g_pre_ffn_ref, wg_ref, wu_ref, wd_ref, g_post_ffn_ref,
                       y_ref.at[i * part:(i + 1) * part]) for i, (result, _) in enumerate(parts)]
    for _ in itertools.zip_longest(*ffns):
        pass


def _stack(x, oa, ob, r, mk, mv, weights, tm, n_seg, tiles_per_mem_block):
    ntok = x.shape[0]
    row = lambda w: pl.BlockSpec((tm, w), lambda i: (i, 0))
    mem = pl.BlockSpec((n_seg * N_MEM, D_MODEL), lambda i: (i // tiles_per_mem_block, 0))
    w_specs = [_const_spec(w.shape) for w in weights]
    return pl.pallas_call(
        functools.partial(_stack_kernel, n_seg),
        grid=(ntok // tm,),
        in_specs=[row(D_MODEL), row(WIDTH_A), row(WIDTH_BV), row(WIDTH_BV), mem, mem] + w_specs,
        out_specs=row(D_MODEL),
        out_shape=jax.ShapeDtypeStruct((ntok, D_MODEL), F32),
        compiler_params=_params(1, VMEM_LIMIT),
        name="token_stack",
    )(x, oa, ob, r, mk, mv, *weights)


def _bias_base(table, width):
    u = np.arange(width)
    idx = np.where(u < (BAND_CHUNKS + 1) * CHUNK, np.clip(A_WINDOW - u, -REL_CLIP, REL_CLIP) + REL_CLIP, 2 * REL_CLIP)
    return table[:, idx]


def kernel(x_prompt, x_sample, mem_prompt, cache_a_k, cache_a_v, state_gla, cache_mem_k, cache_mem_v,
           g_pre_mix, w_in, rel_bias, w_alpha2, b_alpha, g_gla_out, w_o, g_post_mix,
           g_pre_mem, g_mem, w_mq, w_mk, w_mv, w_mo, g_post_mem,
           g_pre_ffn, w_ffn_gate, w_ffn_up, w_ffn_down, g_post_ffn):
    depth = w_in.shape[0]
    assert depth == 1
    l = 0
    n_p, seq, _ = x_prompt.shape
    n_s, seq_s, _ = x_sample.shape
    vec = lambda g: g[l].reshape(1, -1)

    wi = w_in[l]
    w_t = jnp.swapaxes(wi, 0, 1)
    wa2 = jnp.pad(w_alpha2[l], ((0, LANES - GATE_RANK), (0, 0))).astype(BF16)
    stack_w = [vec(g_gla_out), w_o[l].astype(BF16), vec(g_post_mix), vec(g_pre_mem), w_mq[l].astype(BF16),
               w_mo[l].astype(BF16), vec(g_post_mem), vec(g_pre_ffn), w_ffn_gate[l].astype(BF16),
               w_ffn_up[l].astype(BF16), w_ffn_down[l].astype(BF16), vec(g_post_ffn)]

    xp = x_prompt.reshape(n_p * seq, D_MODEL)
    keep = min(A_WINDOW, seq)
    qa, ka, va, qb, kb, vb, r, la, k_tail, v_tail = _proj(xp, vec(g_pre_mix), w_t, wa2, vec(b_alpha), TILES.proj_rows,
                                                          seq // TILES.proj_rows, keep, True)
    oa, ob, sp = _mixers_prompt(qa, ka, va, _bias_base(rel_bias[l], KEYS), qb, kb, vb, la, n_p, seq)
    mk, mv, mk16, mv16 = _mem_kv(mem_prompt.reshape(n_p * N_MEM, D_MODEL), vec(g_mem), w_mk[l], w_mv[l])
    yp = _stack(xp, oa, ob, r, mk16, mv16, stack_w, TILES.stack_rows, 1,
                seq // TILES.stack_rows)

    xs = x_sample.reshape(n_s * seq_s, D_MODEL)
    ntok_s = n_s * seq_s
    qa_s, ka_s, va_s, qb_s, kb_s, vb_s, r_s, la_s, k_new, v_new = _proj(
        xs, vec(g_pre_mix), w_t, wa2, vec(b_alpha), ntok_s, 1, ntok_s, False)
    oa_s = _attn_sample(qa_s, ka_s, va_s, jnp.transpose(cache_a_k[l], (0, 2, 3, 1)),
                        jnp.transpose(cache_a_v[l], (0, 2, 3, 1)), rel_bias[l], seq_s, TILES.sample_attn_requests)
    ob_s, ss = _gla(qb_s, kb_s, vb_s, la_s, state_gla[l], n_s, seq_s, seq_s, 1,
                    TILES.sample_gla_requests)
    seg_s = TILES.sample_stack_requests
    ys = _stack(xs, oa_s, ob_s, r_s,
                cache_mem_k[l].reshape(n_s * N_MEM, D_MODEL),
                cache_mem_v[l].reshape(n_s * N_MEM, D_MODEL), stack_w, seg_s * seq_s, seg_s, 1)

    heads = lambda z, n, t: z.reshape(1, n, t, N_HEADS_A, HEAD_DIM_A)
    tails = lambda z: jnp.transpose(z.reshape(n_p, N_HEADS_A, HEAD_DIM_A, keep), (0, 3, 1, 2))[None]
    return (yp.reshape(n_p, seq, D_MODEL), ys.reshape(n_s, seq_s, D_MODEL),
            tails(k_tail), tails(v_tail), sp[None],
            mk.reshape(1, n_p, N_MEM, N_HEADS_MEM, HEAD_DIM_MEM), mv.reshape(1, n_p, N_MEM, N_HEADS_MEM, HEAD_DIM_MEM),
            heads(k_new, n_s, seq_s), heads(v_new, n_s, seq_s), ss[None])
```
